```python
import math
import jax, jax.numpy as jnp
from jax import lax
import numpy as np

D_MODEL = 1024
BATCH = 16
SEQ = 256
DEPTH = 2
DEC_BATCH = 4
DEC_SEQ = 1024
PAST_LEN = 512

GRID_W = 64
MIX_WIDTH = D_MODEL
DA_HEADS = 4
DA_QK = 64
DA_V = 2 * DA_QK
DA_WIDTH = DA_HEADS * DA_V
NA_HEADS = 4
NA_DIM = 64
NA_WIDTH = NA_HEADS * NA_DIM
NA_ROWS = 8
NA_COLS = 16
NA_QBLOCK = 16
NA_KBAND = 2 * NA_COLS
POOL_WINDOWS = (2, 4, 8, 16)
POOL_CH = 64
POOL_WIDTH = len(POOL_WINDOWS) * POOL_CH
IN_COLS = 3 * DA_WIDTH + 3 * NA_WIDTH + POOL_WIDTH
ROPE_THETA = 10000.0
Q_BLOCK = 128
N_EXPERTS = 64
TOP_K = 8
N_GROUPS = 8
TOPK_GROUPS = 4
EXPERT_DIM = 256
SHARED_DIM = 256
ROUTED_SCALE = 2.5
EPS = 1e-6
NEG = -1e30

kernel_name = 'hybrid_diffattn_natten_pool_moe_prefix_step'


def rmsnorm(x, g):
    xf = x.astype(jnp.float32)
    y = xf * lax.rsqrt(jnp.mean(xf * xf, axis=-1, keepdims=True) + EPS)
    return (y * g.astype(jnp.float32)).astype(x.dtype)


def modulation(cvec, w, b):
    m = jax.nn.silu(cvec) @ w + b
    return jnp.split(m[:, None, :], 6, axis=-1)


def adaln(h, shift, scale):
    return h * (1 + scale) + shift


def axial_rope(x):
    b, n, h, dim = x.shape
    nf = dim // 4
    t = jnp.arange(n, dtype=jnp.int32)
    pos = jnp.stack([t // GRID_W, t % GRID_W], axis=-1).astype(jnp.float32)
    inv = jnp.power(ROPE_THETA, -jnp.arange(nf, dtype=jnp.float32) / nf)
    ang = pos[:, :, None] * inv
    cos = jnp.cos(ang)[None, :, None]
    sin = jnp.sin(ang)[None, :, None]
    xr = x.astype(jnp.float32).reshape(b, n, h, 2, 2, nf)
    x1, x2 = xr[..., 0, :], xr[..., 1, :]
    out = jnp.stack([x1 * cos - x2 * sin, x2 * cos + x1 * sin], axis=-2)
    return out.reshape(b, n, h, dim).astype(x.dtype)


def _query_blocks(nq):
    blk = Q_BLOCK if nq % Q_BLOCK == 0 else nq
    return blk, nq // blk


def diff_attention(q, k, v, lam):
    b, nq, h, _, dq = q.shape
    blk, nblk = _query_blocks(nq)
    qs = jnp.moveaxis(q.reshape(b, nblk, blk, h, 2, dq), 1, 0)
    scale = dq ** -0.5

    def one_block(qi):
        s = jnp.einsum('bqhmd,bkhmd->bhmqk', qi, k).astype(jnp.float32) * scale
        p = jax.nn.softmax(s, axis=-1)
        w = p[:, :, 0] - lam * p[:, :, 1]
        return jnp.einsum('bhqk,bkhd->bqhd', w.astype(v.dtype), v)

    o = lax.map(one_block, qs)
    return jnp.moveaxis(o, 0, 1).reshape(b, nq, h, v.shape[-1])


def dense_attention(q, k, v):
    b, nq, h, d = q.shape
    blk, nblk = _query_blocks(nq)
    qs = jnp.moveaxis(q.reshape(b, nblk, blk, h, d), 1, 0)
    scale = d ** -0.5

    def one_block(qi):
        s = jnp.einsum('bqhd,bkhd->bhqk', qi, k).astype(jnp.float32) * scale
        p = jax.nn.softmax(s, axis=-1)
        return jnp.einsum('bhqk,bkhd->bqhd', p.astype(v.dtype), v)

    o = lax.map(one_block, qs)
    return jnp.moveaxis(o, 0, 1).reshape(b, nq, h, v.shape[-1])


def neighbourhood_attention(q, k, v, k_ctx, v_ctx, bias_table):
    b, n, h, d = q.shape
    rows = n // GRID_W
    kr = min(NA_ROWS, rows)
    ncb = GRID_W // NA_QBLOCK
    r = jnp.arange(rows)
    row_start = jnp.clip(r - kr // 2, 0, rows - kr)
    row_idx = row_start[:, None] + jnp.arange(kr)[None, :]
    cpos = jnp.arange(GRID_W)
    col_start = jnp.clip(cpos - NA_COLS // 2, 0, GRID_W - NA_COLS)
    band_start = jnp.clip(col_start[::NA_QBLOCK], 0, GRID_W - NA_KBAND)
    col_idx = band_start[:, None] + jnp.arange(NA_KBAND)[None, :]
    kg = k.reshape(b, rows, GRID_W, h, d)
    vg = v.reshape(b, rows, GRID_W, h, d)
    ridx = row_idx[:, None, :, None]
    cidx = col_idx[None, :, None, :]
    k_blk = kg[:, ridx, cidx]
    v_blk = vg[:, ridx, cidx]
    qg = q.reshape(b, rows, ncb, NA_QBLOCK, h, d)
    scale = d ** -0.5
    s_loc = jnp.einsum('brnqhd,brnikhd->bhrnqik', qg, k_blk).astype(jnp.float32) * scale
    qcol = cpos.reshape(ncb, NA_QBLOCK)
    qcs = col_start[qcol]
    kcol = col_idx[:, None, :]
    valid = (kcol >= qcs[:, :, None]) & (kcol < qcs[:, :, None] + NA_COLS)
    off_r = (row_idx - r[:, None] + NA_ROWS - 1)[:, None, None, :, None]
    off_c = jnp.clip(kcol - qcol[:, :, None] + NA_COLS - 1, 0, 2 * NA_COLS - 2)[None, :, :, None, :]
    bias = bias_table.astype(jnp.float32)[:, off_r, off_c]
    s_loc = jnp.where(valid[None, None, None, :, :, None, :], s_loc + bias[None], NEG)
    s_loc = s_loc.reshape(b, h, rows, ncb, NA_QBLOCK, kr * NA_KBAND)
    s_ctx = jnp.einsum('brnqhd,blhd->bhrnql', qg, k_ctx).astype(jnp.float32) * scale
    p = jax.nn.softmax(jnp.concatenate([s_loc, s_ctx], axis=-1), axis=-1)
    p_loc = p[..., :kr * NA_KBAND].reshape(b, h, rows, ncb, NA_QBLOCK, kr, NA_KBAND)
    p_ctx = p[..., kr * NA_KBAND:]
    o = jnp.einsum('bhrnqik,brnikhd->brnqhd', p_loc.astype(v.dtype), v_blk)
    o = o + jnp.einsum('bhrnql,blhd->brnqhd', p_ctx.astype(v.dtype), v_ctx)
    return o.reshape(b, n, h, d)


def multiscale_pool(u, w_pool, scale):
    b, n, _ = u.shape
    uf = u.astype(jnp.float32)
    csum = jnp.concatenate([jnp.zeros((b, 1, POOL_WIDTH), jnp.float32), jnp.cumsum(uf, axis=1)], axis=1)
    t = jnp.arange(n)
    outs = []
    for g, w in enumerate(POOL_WINDOWS):
        lo = jnp.clip(t - w // 2, 0, n)
        hi = jnp.clip(t + w // 2, 0, n)
        seg = csum[:, :, g * POOL_CH:(g + 1) * POOL_CH]
        mean = (seg[:, hi] - seg[:, lo]) / (hi - lo).astype(jnp.float32)[None, :, None]
        outs.append(mean - uf[:, :, g * POOL_CH:(g + 1) * POOL_CH])
    pooled = jnp.stack(outs, axis=2)
    mixed = jnp.einsum('bngc,gcd->bngd', pooled, w_pool.astype(jnp.float32)).reshape(b, n, POOL_WIDTH)
    return (mixed * scale.astype(jnp.float32)).astype(u.dtype)


def split_projection(u):
    b, n, _ = u.shape
    o1 = DA_WIDTH
    o2 = 2 * DA_WIDTH
    o3 = 3 * DA_WIDTH
    o4 = o3 + NA_WIDTH
    o5 = o4 + NA_WIDTH
    o6 = o5 + NA_WIDTH
    qa = u[..., :o1].reshape(b, n, DA_HEADS, 2, DA_QK)
    ka = u[..., o1:o2].reshape(b, n, DA_HEADS, 2, DA_QK)
    va = u[..., o2:o3].reshape(b, n, DA_HEADS, DA_V)
    qb = u[..., o3:o4].reshape(b, n, NA_HEADS, NA_DIM)
    kb = u[..., o4:o5].reshape(b, n, NA_HEADS, NA_DIM)
    vb = u[..., o5:o6].reshape(b, n, NA_HEADS, NA_DIM)
    up = u[..., o6:]
    return qa, ka, va, qb, kb, vb, up


def merge_heads(o_a, o_b, o_c, w_out):
    b, n = o_c.shape[:2]
    cat = jnp.concatenate([o_a.reshape(b, n, DA_WIDTH), o_b.reshape(b, n, NA_WIDTH), o_c], axis=-1)
    return cat @ w_out


def diff_lambda_value(lp, lam_init):
    lp = lp.astype(jnp.float32)
    return jnp.exp(jnp.sum(lp[0] * lp[1])) - jnp.exp(jnp.sum(lp[2] * lp[3])) + lam_init


def moe_ffn(h, w_router, b_router, w_gate, w_up, w_down, ws_gate, ws_up, ws_down):
    shp = h.shape
    t = h.reshape(-1, shp[-1])
    T = t.shape[0]
    s = jax.nn.sigmoid((t @ w_router).astype(jnp.float32))
    sel = s + b_router.astype(jnp.float32)
    per_group = N_EXPERTS // N_GROUPS
    grp_score = lax.top_k(sel.reshape(T, N_GROUPS, per_group), 2)[0].sum(-1)
    _, gidx = lax.top_k(grp_score, TOPK_GROUPS)
    gmask = jnp.any(gidx[:, :, None] == jnp.arange(N_GROUPS)[None, None, :], axis=1)
    emask = jnp.repeat(gmask, per_group, axis=1)
    _, eidx = lax.top_k(jnp.where(emask, sel, -jnp.inf), TOP_K)
    w_sel = jnp.take_along_axis(s, eidx, axis=1)
    w_sel = w_sel / jnp.sum(w_sel, axis=-1, keepdims=True) * ROUTED_SCALE
    gates = jnp.zeros((T, N_EXPERTS), jnp.float32).at[jnp.arange(T)[:, None], eidx].set(w_sel)
    hid = jax.nn.silu(jnp.einsum('td,edf->tef', t, w_gate)) * jnp.einsum('td,edf->tef', t, w_up)
    routed = jnp.einsum('tef,efd->td', hid * gates[:, :, None].astype(hid.dtype), w_down)
    shared = (jax.nn.silu(t @ ws_gate) * (t @ ws_up)) @ ws_down
    return (routed + shared).reshape(shp)


def setup_inputs(seed: int = 0) -> dict:
    key = jax.random.key(seed)
    ks = iter(jax.random.split(key, 40))
    D = D_MODEL

    def nrm(shape, s):
        return jax.random.normal(next(ks), shape, jnp.float32) * s

    return {
        'x_prompt': nrm((BATCH, SEQ, D), 1.0),
        'x_sample': nrm((DEC_BATCH, DEC_SEQ, D), 1.0),
        'cache_diff_k': nrm((DEC_BATCH, DEPTH, PAST_LEN, DA_HEADS, 2 * DA_QK), 1.0),
        'cache_diff_v': nrm((DEC_BATCH, DEPTH, PAST_LEN, DA_HEADS, DA_V), 1.0),
        'cache_na_k': nrm((DEC_BATCH, DEPTH, PAST_LEN, NA_HEADS, NA_DIM), 1.0),
        'cache_na_v': nrm((DEC_BATCH, DEPTH, PAST_LEN, NA_HEADS, NA_DIM), 1.0),
        'c': nrm((DEC_BATCH, D), 1.0),
        'c_ctx': nrm((D,), 1.0),
        'w_ada': nrm((DEPTH, D, 6 * D), 0.5 * D ** -0.5),
        'b_ada': nrm((DEPTH, 6 * D), 0.02),
        'g_pre_mix': 1.0 + nrm((DEPTH, D), 0.02),
        'g_post_mix': 1.0 + nrm((DEPTH, D), 0.02),
        'g_pre_ffn': 1.0 + nrm((DEPTH, D), 0.02),
        'g_post_ffn': 1.0 + nrm((DEPTH, D), 0.02),
        'w_in': nrm((DEPTH, D, IN_COLS), D ** -0.5),
        'w_out': nrm((DEPTH, MIX_WIDTH, D), MIX_WIDTH ** -0.5),
        'diff_lambda': nrm((DEPTH, 4, DA_QK), 0.1),
        'g_diff': 1.0 + nrm((DEPTH, DA_V), 0.02),
        'na_bias': nrm((DEPTH, NA_HEADS, 2 * NA_ROWS - 1, 2 * NA_COLS - 1), 0.1),
        'pool_w': nrm((DEPTH, len(POOL_WINDOWS), POOL_CH, POOL_CH), POOL_CH ** -0.5),
        'pool_scale': 1.0 + nrm((DEPTH, POOL_WIDTH), 0.02),
        'w_router': nrm((DEPTH, D, N_EXPERTS), D ** -0.5),
        'b_router': nrm((DEPTH, N_EXPERTS), 0.01),
        'w_gate': nrm((DEPTH, N_EXPERTS, D, EXPERT_DIM), D ** -0.5),
        'w_up': nrm((DEPTH, N_EXPERTS, D, EXPERT_DIM), D ** -0.5),
        'w_down': nrm((DEPTH, N_EXPERTS, EXPERT_DIM, D), EXPERT_DIM ** -0.5),
        'ws_gate': nrm((DEPTH, D, SHARED_DIM), D ** -0.5),
        'ws_up': nrm((DEPTH, D, SHARED_DIM), D ** -0.5),
        'ws_down': nrm((DEPTH, SHARED_DIM, D), SHARED_DIM ** -0.5),
    }


def reference(x_prompt, x_sample, cache_diff_k, cache_diff_v, cache_na_k, cache_na_v, c, c_ctx,
              w_ada, b_ada, g_pre_mix, g_post_mix, g_pre_ffn, g_post_ffn, w_in, w_out,
              diff_lambda, g_diff, na_bias, pool_w, pool_scale, w_router, b_router,
              w_gate, w_up, w_down, ws_gate, ws_up, ws_down):
    yp = x_prompt
    ys = x_sample
    bp, lp = yp.shape[:2]
    bs, ns = ys.shape[:2]
    new_dk, new_dv, new_nk, new_nv = [], [], [], []
    for li in range(DEPTH):
        lam_init = 0.8 - 0.6 * math.exp(-0.3 * li)
        lam = diff_lambda_value(diff_lambda[li], lam_init)
        m_ctx = modulation(c_ctx[None, :], w_ada[li], b_ada[li])
        m_lat = modulation(c, w_ada[li], b_ada[li])

        h = adaln(rmsnorm(yp, g_pre_mix[li]), m_ctx[0], m_ctx[1])
        qa, ka, va, qb, kb, vb, up = split_projection(h @ w_in[li])
        o_a = rmsnorm(diff_attention(qa, ka, va, lam), g_diff[li]) * (1.0 - lam_init)
        o_b = dense_attention(qb, kb, vb)
        o_c = multiscale_pool(up, pool_w[li], pool_scale[li])
        yp = yp + m_ctx[2] * rmsnorm(merge_heads(o_a, o_b, o_c, w_out[li]), g_post_mix[li])
        h = adaln(rmsnorm(yp, g_pre_ffn[li]), m_ctx[3], m_ctx[4])
        f = moe_ffn(h, w_router[li], b_router[li], w_gate[li], w_up[li], w_down[li],
                    ws_gate[li], ws_up[li], ws_down[li])
        yp = yp + m_ctx[5] * rmsnorm(f, g_post_ffn[li])
        new_dk.append(ka.reshape(bp, lp, DA_HEADS, 2 * DA_QK))
        new_dv.append(va)
        new_nk.append(kb)
        new_nv.append(vb)

        lc = cache_diff_k.shape[2]
        h = adaln(rmsnorm(ys, g_pre_mix[li]), m_lat[0], m_lat[1])
        qa, ka, va, qb, kb, vb, up = split_projection(h @ w_in[li])
        qa = axial_rope(qa.reshape(bs, ns, 2 * DA_HEADS, DA_QK)).reshape(bs, ns, DA_HEADS, 2, DA_QK)
        ka = axial_rope(ka.reshape(bs, ns, 2 * DA_HEADS, DA_QK)).reshape(bs, ns, DA_HEADS, 2, DA_QK)
        ck = cache_diff_k[:, li].reshape(bs, lc, DA_HEADS, 2, DA_QK)
        k_all = jnp.concatenate([ka, ck], axis=1)
        v_all = jnp.concatenate([va, cache_diff_v[:, li]], axis=1)
        o_a = rmsnorm(diff_attention(qa, k_all, v_all, lam), g_diff[li]) * (1.0 - lam_init)
        o_b = neighbourhood_attention(qb, kb, vb, cache_na_k[:, li], cache_na_v[:, li], na_bias[li])
        o_c = multiscale_pool(up, pool_w[li], pool_scale[li])
        ys = ys + m_lat[2] * rmsnorm(merge_heads(o_a, o_b, o_c, w_out[li]), g_post_mix[li])
        h = adaln(rmsnorm(ys, g_pre_ffn[li]), m_lat[3], m_lat[4])
        f = moe_ffn(h, w_router[li], b_router[li], w_gate[li], w_up[li], w_down[li],
                    ws_gate[li], ws_up[li], ws_down[li])
        ys = ys + m_lat[5] * rmsnorm(f, g_post_ffn[li])

    new_diff_k = jnp.stack(new_dk, axis=1)
    new_diff_v = jnp.stack(new_dv, axis=1)
    new_na_k = jnp.stack(new_nk, axis=1)
    new_na_v = jnp.stack(new_nv, axis=1)
    return (yp, ys, new_diff_k, new_diff_v, new_na_k, new_na_v)
```

```python
import functools
import math

import numpy as np
import jax
import jax.numpy as jnp
from jax import lax
from jax.experimental import pallas as pl
from jax.experimental.pallas import tpu as pltpu

F32 = jnp.float32
BF16 = jnp.bfloat16

D = 1024
DEPTH = 2
NB_P, N_P = 16, 256
NB_S, N_S = 4, 1024
T_P = NB_P * N_P
T_S = NB_S * N_S
T = T_P + T_S
PAST = 512
GRID_W = 64
DA_H, DA_QK, DA_V = 4, 64, 128
DA_W = DA_H * DA_V
NA_H, NA_D = 4, 64
NA_W = NA_H * NA_D
NA_ROWS, NA_COLS = 8, 16
POOL_WINDOWS = (2, 4, 8, 16)
POOL_CH = 64
POOL_W = 256
IN_COLS = 3 * DA_W + 3 * NA_W + POOL_W
AB_W = DA_W + NA_W
N_EXP, TOP_K, N_GRP, TOPK_GRP = 64, 8, 8, 4
EXP_DIM = 256
ROUTED_SCALE = 2.5
EPS = 1e-6
NEG = -1e30
ROPE_THETA = 10000.0

VMEM_LIMIT = 50 * 1024 * 1024

TM_IN = 512
TM_MOE = 1024
TQ = 256


def _cparams(sem):
    return pltpu.CompilerParams(dimension_semantics=sem, vmem_limit_bytes=VMEM_LIMIT)


def _rms(x, g):
    return x * lax.rsqrt(jnp.mean(x * x, axis=-1, keepdims=True) + EPS) * g


def _dot(a, b):
    return jnp.dot(a, b, preferred_element_type=F32)


def _dot_nt(a, b):
    return lax.dot_general(a, b, (((1,), (1,)), ((), ())), preferred_element_type=F32)


def _silu(x):
    return x / (1.0 + jnp.exp(-x))


def _mod_row(i, tm):
    off = i * tm - T_P
    return jnp.where(off >= 0, 1 + jnp.maximum(off, 0) // N_S, 0)


def _mod_body(c_ref, w_ref, b_ref, o_ref):
    c = c_ref[...]
    o_ref[...] = jnp.dot(_silu(c), w_ref[...], precision=lax.Precision.HIGHEST,
                         preferred_element_type=F32) + b_ref[...]


def _modulation(cvec, w_ada, b_ada):
    tn = 1536
    return pl.pallas_call(
        _mod_body,
        grid=(DEPTH, 6 * D // tn),
        in_specs=[pl.BlockSpec((8, D), lambda l, j: (0, 0)),
                  pl.BlockSpec((None, D, tn), lambda l, j: (l, 0, j)),
                  pl.BlockSpec((None, 1, tn), lambda l, j: (l, 0, j))],
        out_specs=pl.BlockSpec((None, 8, tn), lambda l, j: (l, 0, j)),
        out_shape=jax.ShapeDtypeStruct((DEPTH, 8, 6 * D), F32),
        compiler_params=_cparams(("parallel", "parallel")),
        name="modulation",
    )(cvec, w_ada, b_ada.reshape(DEPTH, 1, 6 * D))


def _rope_tables():
    nf = DA_QK // 4
    t = np.arange(N_S)
    pos = np.stack([t // GRID_W, t % GRID_W], axis=-1).astype(np.float32)
    inv = np.power(np.float32(ROPE_THETA), -np.arange(nf, dtype=np.float32) / nf)
    ang = pos[:, :, None] * inv
    cos = np.cos(ang)
    sin = np.sin(ang)
    cos64 = np.concatenate([cos[:, 0], cos[:, 0], cos[:, 1], cos[:, 1]], axis=-1)
    sin64 = np.concatenate([-sin[:, 0], sin[:, 0], -sin[:, 1], sin[:, 1]], axis=-1)
    reps = DA_W // DA_QK
    cos_t = np.concatenate([np.tile(cos64, (1, reps)), np.ones((TM_IN, DA_W), np.float32)], axis=0)
    sin_t = np.concatenate([np.tile(sin64, (1, reps)), np.zeros((TM_IN, DA_W), np.float32)], axis=0)
    return jnp.asarray(cos_t, F32), jnp.asarray(sin_t, F32)


def _in_proj_body(x_ref, mod_ref, g_ref, w_ref, cos_ref, sin_ref,
                  qa_ref, ka_ref, va_ref, qb_ref, kb_ref, vb_ref, up_ref):
    x = x_ref[...]
    h = (_rms(x, g_ref[...]) * (1.0 + mod_ref[1:2, :]) + mod_ref[0:1, :]).astype(BF16)
    cos = cos_ref[...]
    sin = sin_ref[...]
    lane = lax.broadcasted_iota(jnp.int32, (TM_IN, DA_W), 1)
    first = (lane % 32) < 16

    def proj(lo, hi):
        return _dot(h, w_ref[:, lo:hi])

    def rope(t):
        swapped = jnp.where(first, pltpu.roll(t, DA_W - 16, 1), pltpu.roll(t, 16, 1))
        return t * cos + swapped * sin

    o = 0
    qa_ref[...] = (rope(proj(o, o + DA_W)) * (DA_QK ** -0.5)).astype(BF16)
    o += DA_W
    ka_ref[...] = rope(proj(o, o + DA_W))
    o += DA_W
    va_ref[...] = proj(o, o + DA_W)
    o += DA_W
    qb_ref[...] = (proj(o, o + NA_W) * (NA_D ** -0.5)).astype(BF16)
    o += NA_W
    kb_ref[...] = proj(o, o + NA_W)
    o += NA_W
    vb_ref[...] = proj(o, o + NA_W)
    o += NA_W
    up_ref[...] = proj(o, o + POOL_W)


def _in_proj(li, x, modv, g_pre, w_in_bf, cos_t, sin_t):
    nt = T // TM_IN
    n_pos_blk = N_S // TM_IN

    def pos_blk(i):
        off = i * TM_IN - T_P
        return jnp.where(off >= 0, (jnp.maximum(off, 0) // TM_IN) % n_pos_blk, n_pos_blk)

    def tok(w):
        return pl.BlockSpec((TM_IN, w), lambda i: (i, 0))

    return pl.pallas_call(
        _in_proj_body,
        grid=(nt,),
        in_specs=[tok(D),
                  pl.BlockSpec((None, None, 6, D), lambda i: (li, _mod_row(i, TM_IN), 0, 0)),
                  pl.BlockSpec((None, 1, D), lambda i: (li, 0, 0)),
                  pl.BlockSpec((None, D, IN_COLS), lambda i: (li, 0, 0)),
                  pl.BlockSpec((TM_IN, DA_W), lambda i: (pos_blk(i), 0)),
                  pl.BlockSpec((TM_IN, DA_W), lambda i: (pos_blk(i), 0))],
        out_specs=[tok(DA_W), tok(DA_W), tok(DA_W), tok(NA_W), tok(NA_W), tok(NA_W), tok(POOL_W)],
        out_shape=[jax.ShapeDtypeStruct((T, DA_W), BF16),
                   jax.ShapeDtypeStruct((T, DA_W), F32),
                   jax.ShapeDtypeStruct((T, DA_W), F32),
                   jax.ShapeDtypeStruct((T, NA_W), BF16),
                   jax.ShapeDtypeStruct((T, NA_W), F32),
                   jax.ShapeDtypeStruct((T, NA_W), F32),
                   jax.ShapeDtypeStruct((T, POOL_W), F32)],
        compiler_params=_cparams(("parallel",)),
        name=f"in_proj_l{li}",
    )(x, modv, g_pre, w_in_bf, cos_t, sin_t)


def _pool_body(u_ref, w_ref, s_ref, o_ref, *, n):
    u = u_ref[...]
    row = lax.broadcasted_iota(jnp.int32, (n, POOL_W), 0)
    lane = lax.broadcasted_iota(jnp.int32, (n, POOL_W), 1)

    def shift_dn(a, k):
        return jnp.where(row >= k, pltpu.roll(a, k, 0), 0.0)

    def shift_up(a, k):
        return jnp.where(row < n - k, pltpu.roll(a, n - k, 0), 0.0)

    fwd = u
    bwd = shift_dn(u, 1)
    mean = jnp.zeros_like(u)
    k = 1
    for gi, w in enumerate(POOL_WINDOWS):
        while k < w // 2:
            fwd = fwd + shift_up(fwd, k)
            bwd = bwd + shift_dn(bwd, k)
            k *= 2
        cnt = (jnp.minimum(row + w // 2, n) - jnp.maximum(row - w // 2, 0)).astype(F32)
        in_group = (lane >= gi * POOL_CH) & (lane < (gi + 1) * POOL_CH)
        mean = jnp.where(in_group, (fwd + bwd) / cnt, mean)
    pooled = mean - u
    mixed = jnp.dot(pooled, w_ref[...], precision=lax.Precision.HIGHEST, preferred_element_type=F32)
    o_ref[...] = (mixed * s_ref[...]).astype(BF16)


def _pool_call(li, up, w_bd, scale, n, nseg, blk0, name):
    return pl.pallas_call(
        functools.partial(_pool_body, n=n),
        grid=(nseg,),
        in_specs=[pl.BlockSpec((n, POOL_W), lambda i: (blk0 + i, 0)),
                  pl.BlockSpec((None, POOL_W, POOL_W), lambda i: (li, 0, 0)),
                  pl.BlockSpec((None, 1, POOL_W), lambda i: (li, 0, 0))],
        out_specs=pl.BlockSpec((n, POOL_W), lambda i: (i, 0)),
        out_shape=jax.ShapeDtypeStruct((nseg * n, POOL_W), BF16),
        compiler_params=_cparams(("parallel",)),
        name=name,
    )(up, w_bd, scale)


def _lambda(lp, lam_init):
    a = jnp.sum(lp[0:1, :] * lp[1:2, :], axis=1, keepdims=True)
    b = jnp.sum(lp[2:3, :] * lp[3:4, :], axis=1, keepdims=True)
    return jnp.exp(a) - jnp.exp(b) + lam_init


def _softmax_parts(parts):
    m = functools.reduce(jnp.maximum, [jnp.max(s, axis=-1, keepdims=True) for s in parts])
    es = [jnp.exp(s - m) for s in parts]
    inv = 1.0 / functools.reduce(jnp.add, [jnp.sum(e, axis=-1, keepdims=True) for e in es])
    return [e * inv for e in es]


def _diff_head(q, ks, vs, lam, gd, lam_init):
    p1 = _softmax_parts([_dot_nt(q[:, :DA_QK], k[:, :DA_QK]) for k in ks])
    p2 = _softmax_parts([_dot_nt(q[:, DA_QK:], k[:, DA_QK:]) for k in ks])
    o = None
    for a, b, v in zip(p1, p2, vs):
        t = _dot((a - lam * b).astype(BF16), v)
        o = t if o is None else o + t
    return _rms(o, gd) * (1.0 - lam_init)


def _plain_head(q, ks, vs, biases):
    scores = []
    for k, bias in zip(ks, biases):
        s = _dot_nt(q, k)
        scores.append(s if bias is None else s + bias)
    ps = _softmax_parts(scores)
    o = None
    for p, v in zip(ps, vs):
        t = _dot(p.astype(BF16), v)
        o = t if o is None else o + t
    return o


def _attn_ctx_body(lam_ref, gd_ref, qa_ref, ka_ref, va_ref, qb_ref, kb_ref, vb_ref, o_ref, *, lam_init):
    lam = _lambda(lam_ref[...], lam_init)
    gd = gd_ref[...]
    for h in range(DA_H):
        sl = slice(h * DA_V, (h + 1) * DA_V)
        o = _diff_head(qa_ref[:, sl], [ka_ref[:, sl].astype(BF16)], [va_ref[:, sl].astype(BF16)],
                       lam, gd, lam_init)
        o_ref[:, sl] = o.astype(BF16)
    for h in range(NA_H):
        sl = slice(h * NA_D, (h + 1) * NA_D)
        o = _plain_head(qb_ref[:, sl], [kb_ref[:, sl].astype(BF16)], [vb_ref[:, sl].astype(BF16)], [None])
        o_ref[:, DA_W + h * NA_D:DA_W + (h + 1) * NA_D] = o.astype(BF16)


def _attn_ctx(li, lam_p, g_diff, qa, ka, va, qb, kb, vb, lam_init):
    def blk(w):
        return pl.BlockSpec((N_P, w), lambda b: (b, 0))

    return pl.pallas_call(
        functools.partial(_attn_ctx_body, lam_init=lam_init),
        grid=(NB_P,),
        in_specs=[pl.BlockSpec((None, 4, DA_QK), lambda b: (li, 0, 0)),
                  pl.BlockSpec((None, 1, DA_V), lambda b: (li, 0, 0)),
                  blk(DA_W), blk(DA_W), blk(DA_W), blk(NA_W), blk(NA_W), blk(NA_W)],
        out_specs=pl.BlockSpec((N_P, AB_W), lambda b: (b, 0)),
        out_shape=jax.ShapeDtypeStruct((T_P, AB_W), BF16),
        compiler_params=_cparams(("parallel",)),
        name=f"attn_ctx_l{li}",
    )(lam_p, g_diff, qa, ka, va, qb, kb, vb)


def _attn_lat_body(lam_ref, gd_ref, qa_ref, ka_ref, va_ref, ck_ref, cv_ref,
                   qb_ref, kb_ref, vb_ref, cnk_ref, cnv_ref, bias_ref, o_ref, *, lam_init):
    lam = _lambda(lam_ref[...], lam_init)
    gd = gd_ref[...]
    for h in range(DA_H):
        sl = slice(h * DA_V, (h + 1) * DA_V)
        ks = [ka_ref[:, sl].astype(BF16), ck_ref[:, sl].astype(BF16)]
        vs = [va_ref[:, sl].astype(BF16), cv_ref[:, sl].astype(BF16)]
        o_ref[:, sl] = _diff_head(qa_ref[:, sl], ks, vs, lam, gd, lam_init).astype(BF16)
    for h in range(NA_H):
        sl = slice(h * NA_D, (h + 1) * NA_D)
        ks = [kb_ref[:, sl].astype(BF16), cnk_ref[:, sl].astype(BF16)]
        vs = [vb_ref[:, sl].astype(BF16), cnv_ref[:, sl].astype(BF16)]
        o = _plain_head(qb_ref[:, sl], ks, vs, [bias_ref[h], None])
        o_ref[:, DA_W + h * NA_D:DA_W + (h + 1) * NA_D] = o.astype(BF16)


def _attn_lat(li, lam_p, g_diff, qa, ka, va, ck, cv, qb, kb, vb, cnk, cnv, bias, lam_init):
    nq = N_S // TQ
    q0 = T_P // TQ
    b0 = T_P // N_S

    def qblk(w):
        return pl.BlockSpec((TQ, w), lambda b, j: (q0 + b * nq + j, 0))

    def kvblk(w):
        return pl.BlockSpec((N_S, w), lambda b, j: (b0 + b, 0))

    def cblk(w):
        return pl.BlockSpec((None, None, PAST, w), lambda b, j: (b, li, 0, 0))

    return pl.pallas_call(
        functools.partial(_attn_lat_body, lam_init=lam_init),
        grid=(NB_S, nq),
        in_specs=[pl.BlockSpec((None, 4, DA_QK), lambda b, j: (li, 0, 0)),
                  pl.BlockSpec((None, 1, DA_V), lambda b, j: (li, 0, 0)),
                  qblk(DA_W), kvblk(DA_W), kvblk(DA_W), cblk(DA_W), cblk(DA_W),
                  qblk(NA_W), kvblk(NA_W), kvblk(NA_W), cblk(NA_W), cblk(NA_W),
                  pl.BlockSpec((NA_H, TQ, N_S), lambda b, j: (0, j, 0))],
        out_specs=pl.BlockSpec((TQ, AB_W), lambda b, j: (b * nq + j, 0)),
        out_shape=jax.ShapeDtypeStruct((T_S, AB_W), BF16),
        compiler_params=_cparams(("parallel", "parallel")),
        name=f"attn_lat_l{li}",
    )(lam_p, g_diff, qa, ka, va, ck, cv, qb, kb, vb, cnk, cnv, bias)


def _na_bias_mask(table):
    rows = N_S // GRID_W
    t = np.arange(N_S)
    r, c = t // GRID_W, t % GRID_W
    rs = np.clip(r - NA_ROWS // 2, 0, rows - NA_ROWS)
    cs = np.clip(c - NA_COLS // 2, 0, GRID_W - NA_COLS)
    kr, kc = r[None, :], c[None, :]
    valid = ((kr >= rs[:, None]) & (kr < rs[:, None] + NA_ROWS)
             & (kc >= cs[:, None]) & (kc < cs[:, None] + NA_COLS))
    off_r = np.clip(kr - r[:, None] + NA_ROWS - 1, 0, 2 * NA_ROWS - 2)
    off_c = np.clip(kc - c[:, None] + NA_COLS - 1, 0, 2 * NA_COLS - 2)
    bias = table.astype(F32)[:, off_r, off_c]
    return jnp.where(jnp.asarray(valid)[None], bias, NEG)


def _route(logits, b_router, tm):
    s = 1.0 / (1.0 + jnp.exp(-logits))
    sel = s + b_router
    lane = lax.broadcasted_iota(jnp.int32, (tm, N_EXP), 1)
    per = N_EXP // N_GRP
    grp = lane // per
    ninf = -jnp.inf

    def first_max(x):
        m = jnp.max(x, axis=-1, keepdims=True)
        idx = jnp.min(jnp.where(x == m, lane, N_EXP), axis=-1, keepdims=True)
        return m, idx

    scores = []
    gscore = jnp.zeros((tm, N_EXP), F32)
    for g in range(N_GRP):
        mk = jnp.where(grp == g, sel, ninf)
        m1, i1 = first_max(mk)
        m2 = jnp.max(jnp.where(lane == i1, ninf, mk), axis=-1, keepdims=True)
        sc = m1 + m2
        scores.append(sc)
        gscore = jnp.where(grp == g, sc, gscore)
    cnt = jnp.zeros((tm, N_EXP), jnp.int32)
    for g in range(N_GRP):
        sc = scores[g]
        beats = (sc > gscore) | ((sc == gscore) & (g < grp))
        cnt = cnt + beats.astype(jnp.int32)
    x = jnp.where(cnt < TOPK_GRP, sel, ninf)
    chosen = jnp.zeros((tm, N_EXP), jnp.bool_)
    for _ in range(TOP_K):
        _, idx = first_max(x)
        hit = lane == idx
        chosen = chosen | hit
        x = jnp.where(hit, ninf, x)
    w = jnp.where(chosen, s, 0.0)
    return w / jnp.sum(w, axis=-1, keepdims=True) * ROUTED_SCALE


def _merge_body(oab_ref, oc_ref, x_ref, mod_ref, gpost_ref, gpre_ref, wout_ref, wr_ref, br_ref,
                y_ref, h_ref, gates_ref):
    mix = _dot(oab_ref[...], wout_ref[0:AB_W, :]) + _dot(oc_ref[...], wout_ref[AB_W:D, :])
    y = x_ref[...] + mod_ref[2:3, :] * _rms(mix, gpost_ref[...])
    y_ref[...] = y
    h = _rms(y, gpre_ref[...]) * (1.0 + mod_ref[4:5, :]) + mod_ref[3:4, :]
    h_ref[...] = h.astype(BF16)
    logits = jnp.dot(h, wr_ref[...], precision=lax.Precision.HIGHEST, preferred_element_type=F32)
    gates_ref[...] = _route(logits, br_ref[...], TM_IN)


def _merge(li, oab, oc, x, modv, g_post, g_pre, w_out_bf, w_router, b_router):
    def tok(w):
        return pl.BlockSpec((TM_IN, w), lambda i: (i, 0))

    def par(*shape):
        return pl.BlockSpec((None,) + shape, lambda i: (li,) + (0,) * len(shape))

    return pl.pallas_call(
        _merge_body,
        grid=(T // TM_IN,),
        in_specs=[tok(AB_W), tok(POOL_W), tok(D),
                  pl.BlockSpec((None, None, 6, D), lambda i: (li, _mod_row(i, TM_IN), 0, 0)),
                  par(1, D), par(1, D), par(D, D), par(D, N_EXP), par(1, N_EXP)],
        out_specs=[tok(D), tok(D), tok(N_EXP)],
        out_shape=[jax.ShapeDtypeStruct((T, D), F32),
                   jax.ShapeDtypeStruct((T, D), BF16),
                   jax.ShapeDtypeStruct((T, N_EXP), F32)],
        compiler_params=_cparams(("parallel",)),
        name=f"merge_l{li}",
    )(oab, oc, x, modv, g_post, g_pre, w_out_bf, w_router, b_router)


def _moe_body(h_ref, gates_ref, y_ref, mod_ref, gpost_ref, wsg_ref, wsu_ref, wsd_ref,
              wg_ref, wu_ref, wd_ref, o_ref, acc_ref):
    e = pl.program_id(1)
    h = h_ref[...]

    @pl.when(e == 0)
    def _():
        hid = _silu(_dot(h, wsg_ref[...])) * _dot(h, wsu_ref[...])
        acc_ref[...] = _dot(hid.astype(BF16), wsd_ref[...])

    lane = lax.broadcasted_iota(jnp.int32, (TM_MOE, N_EXP), 1)
    gate = jnp.sum(jnp.where(lane == e, gates_ref[...], 0.0), axis=-1, keepdims=True)
    hid = _silu(_dot(h, wg_ref[...].astype(BF16))) * _dot(h, wu_ref[...].astype(BF16))
    acc_ref[...] += _dot((hid * gate).astype(BF16), wd_ref[...].astype(BF16))

    @pl.when(e == N_EXP - 1)
    def _():
        o_ref[...] = y_ref[...] + mod_ref[5:6, :] * _rms(acc_ref[...], gpost_ref[...])


def _moe(li, h, gates, y, modv, g_post, wsg_bf, wsu_bf, wsd_bf, w_gate, w_up, w_down):
    def tok(w):
        return pl.BlockSpec((TM_MOE, w), lambda i, e: (i, 0))

    def par(*shape):
        return pl.BlockSpec((None,) + shape, lambda i, e: (li,) + (0,) * len(shape))

    def exp(*shape):
        return pl.BlockSpec((None, None) + shape, lambda i, e: (li, e) + (0,) * len(shape))

    return pl.pallas_call(
        _moe_body,
        grid=(T // TM_MOE, N_EXP),
        in_specs=[tok(D), tok(N_EXP), tok(D),
                  pl.BlockSpec((None, None, 6, D), lambda i, e: (li, _mod_row(i, TM_MOE), 0, 0)),
                  par(1, D), par(D, EXP_DIM), par(D, EXP_DIM), par(EXP_DIM, D),
                  exp(D, EXP_DIM), exp(D, EXP_DIM), exp(EXP_DIM, D)],
        out_specs=tok(D),
        out_shape=jax.ShapeDtypeStruct((T, D), F32),
        scratch_shapes=[pltpu.VMEM((TM_MOE, D), F32)],
        compiler_params=_cparams(("parallel", "arbitrary")),
        name=f"moe_l{li}",
    )(h, gates, y, modv, g_post, wsg_bf, wsu_bf, wsd_bf, w_gate, w_up, w_down)


def _block_diag(w):
    out = jnp.zeros((DEPTH, POOL_W, POOL_W), F32)
    for g in range(len(POOL_WINDOWS)):
        out = out.at[:, g * POOL_CH:(g + 1) * POOL_CH, g * POOL_CH:(g + 1) * POOL_CH].set(w[:, g])
    return out


def kernel(x_prompt, x_sample, cache_diff_k, cache_diff_v, cache_na_k, cache_na_v, c, c_ctx,
           w_ada, b_ada, g_pre_mix, g_post_mix, g_pre_ffn, g_post_ffn, w_in, w_out,
           diff_lambda, g_diff, na_bias, pool_w, pool_scale, w_router, b_router,
           w_gate, w_up, w_down, ws_gate, ws_up, ws_down):
    x = jnp.concatenate([x_prompt.reshape(T_P, D), x_sample.reshape(T_S, D)], axis=0)
    cvec = jnp.concatenate([c_ctx[None, :], c, jnp.zeros((3, D), F32)], axis=0)
    modv = _modulation(cvec, w_ada, b_ada)[:, :1 + NB_S].reshape(DEPTH, 1 + NB_S, 6, D)

    cos_t, sin_t = _rope_tables()
    w_in_bf = w_in.astype(BF16)
    w_out_bf = w_out.astype(BF16)
    wsg_bf, wsu_bf, wsd_bf = ws_gate.astype(BF16), ws_up.astype(BF16), ws_down.astype(BF16)
    pool_bd = _block_diag(pool_w)
    row = lambda a: a.reshape(DEPTH, 1, a.shape[-1])
    g_pre_mix, g_post_mix, g_pre_ffn, g_post_ffn = map(row, (g_pre_mix, g_post_mix, g_pre_ffn, g_post_ffn))
    g_diff, pool_scale, b_router = row(g_diff), row(pool_scale), row(b_router)
    ck = cache_diff_k.reshape(NB_S, DEPTH, PAST, DA_W)
    cv = cache_diff_v.reshape(NB_S, DEPTH, PAST, DA_W)
    cnk = cache_na_k.reshape(NB_S, DEPTH, PAST, NA_W)
    cnv = cache_na_v.reshape(NB_S, DEPTH, PAST, NA_W)

    new_dk, new_dv, new_nk, new_nv = [], [], [], []
    for li in range(DEPTH):
        lam_init = 0.8 - 0.6 * math.exp(-0.3 * li)
        qa, ka, va, qb, kb, vb, up = _in_proj(li, x, modv, g_pre_mix, w_in_bf, cos_t, sin_t)
        oc = jnp.concatenate([
            _pool_call(li, up, pool_bd, pool_scale, N_P, NB_P, 0, f"pool_ctx_l{li}"),
            _pool_call(li, up, pool_bd, pool_scale, N_S, NB_S, T_P // N_S, f"pool_lat_l{li}")], axis=0)
        bias = _na_bias_mask(na_bias[li])
        oab = jnp.concatenate([
            _attn_ctx(li, diff_lambda, g_diff, qa, ka, va, qb, kb, vb, lam_init),
            _attn_lat(li, diff_lambda, g_diff, qa, ka, va, ck, cv, qb, kb, vb, cnk, cnv, bias, lam_init)],
            axis=0)
        y, h, gates = _merge(li, oab, oc, x, modv, g_post_mix, g_pre_ffn, w_out_bf, w_router, b_router)
        x = _moe(li, h, gates, y, modv, g_post_ffn, wsg_bf, wsu_bf, wsd_bf, w_gate, w_up, w_down)
        new_dk.append(ka[:T_P].reshape(NB_P, N_P, DA_H, 2 * DA_QK))
        new_dv.append(va[:T_P].reshape(NB_P, N_P, DA_H, DA_V))
        new_nk.append(kb[:T_P].reshape(NB_P, N_P, NA_H, NA_D))
        new_nv.append(vb[:T_P].reshape(NB_P, N_P, NA_H, NA_D))

    yp = x[:T_P].reshape(NB_P, N_P, D)
    ys = x[T_P:].reshape(NB_S, N_S, D)
    return (yp, ys, jnp.stack(new_dk, axis=1), jnp.stack(new_dv, axis=1),
            jnp.stack(new_nk, axis=1), jnp.stack(new_nv, axis=1))
```

```python
import functools
import math

import numpy as np
import jax
import jax.numpy as jnp
from jax import lax
from jax.experimental import pallas as pl
from jax.experimental.pallas import tpu as pltpu

F32 = jnp.float32
BF16 = jnp.bfloat16

D = 1024
DEPTH = 2
NB_P, N_P = 16, 256
NB_S, N_S = 4, 1024
T_P = NB_P * N_P
T_S = NB_S * N_S
T = T_P + T_S
PAST = 512
GRID_W = 64
DA_H, DA_QK, DA_V = 4, 64, 128
DA_W = DA_H * DA_V
NA_H, NA_D = 4, 64
NA_W = NA_H * NA_D
NA_ROWS, NA_COLS = 8, 16
POOL_WINDOWS = (2, 4, 8, 16)
POOL_CH = 64
POOL_W = 256
IN_COLS = 3 * DA_W + 3 * NA_W + POOL_W
AB_W = DA_W + NA_W
N_EXP, TOP_K, N_GRP, TOPK_GRP = 64, 8, 8, 4
EXP_DIM = 256
ROUTED_SCALE = 2.5
EPS = 1e-6
NEG = -1e30
ROPE_THETA = 10000.0

VMEM_LIMIT = 50 * 1024 * 1024

TM_IN = 512
TM_MOE = 1024
TQ = 256


def _cparams(sem):
    return pltpu.CompilerParams(dimension_semantics=sem, vmem_limit_bytes=VMEM_LIMIT)


def _rms(x, g):
    return x * lax.rsqrt(jnp.mean(x * x, axis=-1, keepdims=True) + EPS) * g


def _dot(a, b):
    return jnp.dot(a, b, preferred_element_type=F32)


def _dot_nt(a, b):
    return lax.dot_general(a, b, (((1,), (1,)), ((), ())), preferred_element_type=F32)


def _silu(x):
    return x / (1.0 + jnp.exp(-x))


def _mod_row(i, tm):
    off = i * tm - T_P
    return jnp.where(off >= 0, 1 + jnp.maximum(off, 0) // N_S, 0)


def _mod_body(c_ref, w_ref, b_ref, o_ref):
    c = c_ref[...]
    o_ref[...] = jnp.dot(_silu(c), w_ref[...], precision=lax.Precision.HIGHEST,
                         preferred_element_type=F32) + b_ref[...]


def _modulation(cvec, w_ada, b_ada):
    tn = 1536
    return pl.pallas_call(
        _mod_body,
        grid=(DEPTH, 6 * D // tn),
        in_specs=[pl.BlockSpec((8, D), lambda l, j: (0, 0)),
                  pl.BlockSpec((None, D, tn), lambda l, j: (l, 0, j)),
                  pl.BlockSpec((None, 1, tn), lambda l, j: (l, 0, j))],
        out_specs=pl.BlockSpec((None, 8, tn), lambda l, j: (l, 0, j)),
        out_shape=jax.ShapeDtypeStruct((DEPTH, 8, 6 * D), F32),
        compiler_params=_cparams(("parallel", "parallel")),
        name="modulation",
    )(cvec, w_ada, b_ada.reshape(DEPTH, 1, 6 * D))


def _rope_tables():
    nf = DA_QK // 4
    t = np.arange(N_S)
    pos = np.stack([t // GRID_W, t % GRID_W], axis=-1).astype(np.float32)
    inv = np.power(np.float32(ROPE_THETA), -np.arange(nf, dtype=np.float32) / nf)
    ang = pos[:, :, None] * inv
    cos = np.cos(ang)
    sin = np.sin(ang)
    cos64 = np.concatenate([cos[:, 0], cos[:, 0], cos[:, 1], cos[:, 1]], axis=-1)
    sin64 = np.concatenate([-sin[:, 0], sin[:, 0], -sin[:, 1], sin[:, 1]], axis=-1)
    reps = DA_W // DA_QK
    cos_t = np.concatenate([np.tile(cos64, (1, reps)), np.ones((TM_IN, DA_W), np.float32)], axis=0)
    sin_t = np.concatenate([np.tile(sin64, (1, reps)), np.zeros((TM_IN, DA_W), np.float32)], axis=0)
    return jnp.asarray(cos_t, F32), jnp.asarray(sin_t, F32)


def _in_proj_body(x_ref, mod_ref, g_ref, w_ref, cos_ref, sin_ref,
                  qa_ref, ka_ref, va_ref, qb_ref, kb_ref, vb_ref, up_ref):
    x = x_ref[...]
    h = (_rms(x, g_ref[...]) * (1.0 + mod_ref[1:2, :]) + mod_ref[0:1, :]).astype(BF16)
    cos = cos_ref[...]
    sin = sin_ref[...]
    lane = lax.broadcasted_iota(jnp.int32, (TM_IN, DA_W), 1)
    first = (lane % 32) < 16

    def proj(lo, hi):
        return _dot(h, w_ref[:, lo:hi])

    def rope(t):
        swapped = jnp.where(first, pltpu.roll(t, DA_W - 16, 1), pltpu.roll(t, 16, 1))
        return t * cos + swapped * sin

    o = 0
    qa_ref[...] = (rope(proj(o, o + DA_W)) * (DA_QK ** -0.5)).astype(BF16)
    o += DA_W
    ka_ref[...] = rope(proj(o, o + DA_W))
    o += DA_W
    va_ref[...] = proj(o, o + DA_W)
    o += DA_W
    qb_ref[...] = (proj(o, o + NA_W) * (NA_D ** -0.5)).astype(BF16)
    o += NA_W
    kb_ref[...] = proj(o, o + NA_W)
    o += NA_W
    vb_ref[...] = proj(o, o + NA_W)
    o += NA_W
    up_ref[...] = proj(o, o + POOL_W)


def _in_proj(li, x, modv, g_pre, w_in_bf, cos_t, sin_t):
    nt = T // TM_IN
    n_pos_blk = N_S // TM_IN

    def pos_blk(i):
        off = i * TM_IN - T_P
        return jnp.where(off >= 0, (jnp.maximum(off, 0) // TM_IN) % n_pos_blk, n_pos_blk)

    def tok(w):
        return pl.BlockSpec((TM_IN, w), lambda i: (i, 0))

    return pl.pallas_call(
        _in_proj_body,
        grid=(nt,),
        in_specs=[tok(D),
                  pl.BlockSpec((None, None, 6, D), lambda i: (li, _mod_row(i, TM_IN), 0, 0)),
                  pl.BlockSpec((None, 1, D), lambda i: (li, 0, 0)),
                  pl.BlockSpec((None, D, IN_COLS), lambda i: (li, 0, 0)),
                  pl.BlockSpec((TM_IN, DA_W), lambda i: (pos_blk(i), 0)),
                  pl.BlockSpec((TM_IN, DA_W), lambda i: (pos_blk(i), 0))],
        out_specs=[tok(DA_W), tok(DA_W), tok(DA_W), tok(NA_W), tok(NA_W), tok(NA_W), tok(POOL_W)],
        out_shape=[jax.ShapeDtypeStruct((T, DA_W), BF16),
                   jax.ShapeDtypeStruct((T, DA_W), F32),
                   jax.ShapeDtypeStruct((T, DA_W), F32),
                   jax.ShapeDtypeStruct((T, NA_W), BF16),
                   jax.ShapeDtypeStruct((T, NA_W), F32),
                   jax.ShapeDtypeStruct((T, NA_W), F32),
                   jax.ShapeDtypeStruct((T, POOL_W), F32)],
        compiler_params=_cparams(("parallel",)),
        name=f"in_proj_l{li}",
    )(x, modv, g_pre, w_in_bf, cos_t, sin_t)


def _pool_body(u_ref, w_ref, s_ref, o_ref, *, n):
    u = u_ref[...]
    row = lax.broadcasted_iota(jnp.int32, (n, POOL_W), 0)
    lane = lax.broadcasted_iota(jnp.int32, (n, POOL_W), 1)

    def shift_dn(a, k):
        return jnp.where(row >= k, pltpu.roll(a, k, 0), 0.0)

    def shift_up(a, k):
        return jnp.where(row < n - k, pltpu.roll(a, n - k, 0), 0.0)

    fwd = u
    bwd = shift_dn(u, 1)
    mean = jnp.zeros_like(u)
    k = 1
    for gi, w in enumerate(POOL_WINDOWS):
        while k < w // 2:
            fwd = fwd + shift_up(fwd, k)
            bwd = bwd + shift_dn(bwd, k)
            k *= 2
        cnt = (jnp.minimum(row + w // 2, n) - jnp.maximum(row - w // 2, 0)).astype(F32)
        in_group = (lane >= gi * POOL_CH) & (lane < (gi + 1) * POOL_CH)
        mean = jnp.where(in_group, (fwd + bwd) / cnt, mean)
    pooled = mean - u
    mixed = jnp.dot(pooled, w_ref[...], precision=lax.Precision.HIGHEST, preferred_element_type=F32)
    o_ref[...] = (mixed * s_ref[...]).astype(BF16)


def _pool_call(li, up, w_bd, scale, n, nseg, blk0, name):
    return pl.pallas_call(
        functools.partial(_pool_body, n=n),
        grid=(nseg,),
        in_specs=[pl.BlockSpec((n, POOL_W), lambda i: (blk0 + i, 0)),
                  pl.BlockSpec((None, POOL_W, POOL_W), lambda i: (li, 0, 0)),
                  pl.BlockSpec((None, 1, POOL_W), lambda i: (li, 0, 0))],
        out_specs=pl.BlockSpec((n, POOL_W), lambda i: (i, 0)),
        out_shape=jax.ShapeDtypeStruct((nseg * n, POOL_W), BF16),
        compiler_params=_cparams(("parallel",)),
        name=name,
    )(up, w_bd, scale)


def _lambda(lp, lam_init):
    a = jnp.sum(lp[0:1, :] * lp[1:2, :], axis=1, keepdims=True)
    b = jnp.sum(lp[2:3, :] * lp[3:4, :], axis=1, keepdims=True)
    return jnp.exp(a) - jnp.exp(b) + lam_init


def _softmax_parts(parts):
    m = functools.reduce(jnp.maximum, [jnp.max(s, axis=-1, keepdims=True) for s in parts])
    es = [jnp.exp(s - m) for s in parts]
    inv = 1.0 / functools.reduce(jnp.add, [jnp.sum(e, axis=-1, keepdims=True) for e in es])
    return [e * inv for e in es]


def _diff_head(q, ks, vs, lam, gd, lam_init):
    p1 = _softmax_parts([_dot_nt(q[:, :DA_QK], k[:, :DA_QK]) for k in ks])
    p2 = _softmax_parts([_dot_nt(q[:, DA_QK:], k[:, DA_QK:]) for k in ks])
    o = None
    for a, b, v in zip(p1, p2, vs):
        t = _dot((a - lam * b).astype(BF16), v)
        o = t if o is None else o + t
    return _rms(o, gd) * (1.0 - lam_init)


def _plain_head(q, ks, vs, biases):
    scores = []
    for k, bias in zip(ks, biases):
        s = _dot_nt(q, k)
        scores.append(s if bias is None else s + bias)
    ps = _softmax_parts(scores)
    o = None
    for p, v in zip(ps, vs):
        t = _dot(p.astype(BF16), v)
        o = t if o is None else o + t
    return o


def _attn_ctx_body(lam_ref, gd_ref, qa_ref, ka_ref, va_ref, qb_ref, kb_ref, vb_ref, o_ref, *, lam_init):
    lam = _lambda(lam_ref[...], lam_init)
    gd = gd_ref[...]
    for h in range(DA_H):
        sl = slice(h * DA_V, (h + 1) * DA_V)
        o = _diff_head(qa_ref[:, sl], [ka_ref[:, sl].astype(BF16)], [va_ref[:, sl].astype(BF16)],
                       lam, gd, lam_init)
        o_ref[:, sl] = o.astype(BF16)
    for h in range(NA_H):
        sl = slice(h * NA_D, (h + 1) * NA_D)
        o = _plain_head(qb_ref[:, sl], [kb_ref[:, sl].astype(BF16)], [vb_ref[:, sl].astype(BF16)], [None])
        o_ref[:, DA_W + h * NA_D:DA_W + (h + 1) * NA_D] = o.astype(BF16)


def _attn_ctx(li, lam_p, g_diff, qa, ka, va, qb, kb, vb, lam_init):
    def blk(w):
        return pl.BlockSpec((N_P, w), lambda b: (b, 0))

    return pl.pallas_call(
        functools.partial(_attn_ctx_body, lam_init=lam_init),
        grid=(NB_P,),
        in_specs=[pl.BlockSpec((None, 4, DA_QK), lambda b: (li, 0, 0)),
                  pl.BlockSpec((None, 1, DA_V), lambda b: (li, 0, 0)),
                  blk(DA_W), blk(DA_W), blk(DA_W), blk(NA_W), blk(NA_W), blk(NA_W)],
        out_specs=pl.BlockSpec((N_P, AB_W), lambda b: (b, 0)),
        out_shape=jax.ShapeDtypeStruct((T_P, AB_W), BF16),
        compiler_params=_cparams(("parallel",)),
        name=f"attn_ctx_l{li}",
    )(lam_p, g_diff, qa, ka, va, qb, kb, vb)


def _attn_lat_body(lam_ref, gd_ref, qa_ref, ka_ref, va_ref, ck_ref, cv_ref,
                   qb_ref, kb_ref, vb_ref, cnk_ref, cnv_ref, bias_ref, o_ref, *, lam_init):
    lam = _lambda(lam_ref[...], lam_init)
    gd = gd_ref[...]
    for h in range(DA_H):
        sl = slice(h * DA_V, (h + 1) * DA_V)
        ks = [ka_ref[:, sl].astype(BF16), ck_ref[:, sl].astype(BF16)]
        vs = [va_ref[:, sl].astype(BF16), cv_ref[:, sl].astype(BF16)]
        o_ref[:, sl] = _diff_head(qa_ref[:, sl], ks, vs, lam, gd, lam_init).astype(BF16)
    for h in range(NA_H):
        sl = slice(h * NA_D, (h + 1) * NA_D)
        ks = [kb_ref[:, sl].astype(BF16), cnk_ref[:, sl].astype(BF16)]
        vs = [vb_ref[:, sl].astype(BF16), cnv_ref[:, sl].astype(BF16)]
        o = _plain_head(qb_ref[:, sl], ks, vs, [bias_ref[h], None])
        o_ref[:, DA_W + h * NA_D:DA_W + (h + 1) * NA_D] = o.astype(BF16)


def _attn_lat(li, lam_p, g_diff, qa, ka, va, ck, cv, qb, kb, vb, cnk, cnv, bias, lam_init):
    nq = N_S // TQ
    q0 = T_P // TQ
    b0 = T_P // N_S

    def qblk(w):
        return pl.BlockSpec((TQ, w), lambda b, j: (q0 + b * nq + j, 0))

    def kvblk(w):
        return pl.BlockSpec((N_S, w), lambda b, j: (b0 + b, 0))

    def cblk(w):
        return pl.BlockSpec((None, None, PAST, w), lambda b, j: (b, li, 0, 0))

    return pl.pallas_call(
        functools.partial(_attn_lat_body, lam_init=lam_init),
        grid=(NB_S, nq),
        in_specs=[pl.BlockSpec((None, 4, DA_QK), lambda b, j: (li, 0, 0)),
                  pl.BlockSpec((None, 1, DA_V), lambda b, j: (li, 0, 0)),
                  qblk(DA_W), kvblk(DA_W), kvblk(DA_W), cblk(DA_W), cblk(DA_W),
                  qblk(NA_W), kvblk(NA_W), kvblk(NA_W), cblk(NA_W), cblk(NA_W),
                  pl.BlockSpec((NA_H, TQ, N_S), lambda b, j: (0, j, 0))],
        out_specs=pl.BlockSpec((TQ, AB_W), lambda b, j: (b * nq + j, 0)),
        out_shape=jax.ShapeDtypeStruct((T_S, AB_W), BF16),
        compiler_params=_cparams(("parallel", "parallel")),
        name=f"attn_lat_l{li}",
    )(lam_p, g_diff, qa, ka, va, ck, cv, qb, kb, vb, cnk, cnv, bias)


def _na_bias_mask(table):
    rows = N_S // GRID_W
    t = np.arange(N_S)
    r, c = t // GRID_W, t % GRID_W
    rs = np.clip(r - NA_ROWS // 2, 0, rows - NA_ROWS)
    cs = np.clip(c - NA_COLS // 2, 0, GRID_W - NA_COLS)
    kr, kc = r[None, :], c[None, :]
    valid = ((kr >= rs[:, None]) & (kr < rs[:, None] + NA_ROWS)
             & (kc >= cs[:, None]) & (kc < cs[:, None] + NA_COLS))
    g = np.arange(rows)
    sel_r = (g[None, :, None] - g[:, None, None] + NA_ROWS - 1 == np.arange(2 * NA_ROWS - 1)).astype(np.float32)
    w = np.arange(GRID_W)
    sel_c = (np.clip(w[None, :, None] - w[:, None, None] + NA_COLS - 1, 0, 2 * NA_COLS - 2)
             == np.arange(2 * NA_COLS - 1)).astype(np.float32)
    hp = lax.Precision.HIGHEST
    t1 = jnp.einsum('hab,cdb->hacd', table.astype(F32), jnp.asarray(sel_c), precision=hp)
    bias = jnp.einsum('rsa,hacd->hrcsd', jnp.asarray(sel_r), t1, precision=hp).reshape(NA_H, N_S, N_S)
    return jnp.where(jnp.asarray(valid)[None], bias, NEG)


def _route(logits, b_router, tm):
    s = 1.0 / (1.0 + jnp.exp(-logits))
    sel = s + b_router
    lane = lax.broadcasted_iota(jnp.int32, (tm, N_EXP), 1)
    per = N_EXP // N_GRP
    grp = lane // per
    ninf = -jnp.inf

    def first_max(x):
        m = jnp.max(x, axis=-1, keepdims=True)
        idx = jnp.min(jnp.where(x == m, lane, N_EXP), axis=-1, keepdims=True)
        return m, idx

    scores = []
    gscore = jnp.zeros((tm, N_EXP), F32)
    for g in range(N_GRP):
        mk = jnp.where(grp == g, sel, ninf)
        m1, i1 = first_max(mk)
        m2 = jnp.max(jnp.where(lane == i1, ninf, mk), axis=-1, keepdims=True)
        sc = m1 + m2
        scores.append(sc)
        gscore = jnp.where(grp == g, sc, gscore)
    cnt = jnp.zeros((tm, N_EXP), jnp.int32)
    for g in range(N_GRP):
        sc = scores[g]
        beats = (sc > gscore) | ((sc == gscore) & (g < grp))
        cnt = cnt + beats.astype(jnp.int32)
    x = jnp.where(cnt < TOPK_GRP, sel, ninf)
    chosen = jnp.zeros((tm, N_EXP), jnp.bool_)
    for _ in range(TOP_K):
        _, idx = first_max(x)
        hit = lane == idx
        chosen = chosen | hit
        x = jnp.where(hit, ninf, x)
    w = jnp.where(chosen, s, 0.0)
    return w / jnp.sum(w, axis=-1, keepdims=True) * ROUTED_SCALE


def _merge_body(oab_ref, oc_ref, x_ref, mod_ref, gpost_ref, gpre_ref, wout_ref, wr_ref, br_ref,
                y_ref, h_ref, gates_ref):
    mix = _dot(oab_ref[...], wout_ref[0:AB_W, :]) + _dot(oc_ref[...], wout_ref[AB_W:D, :])
    y = x_ref[...] + mod_ref[2:3, :] * _rms(mix, gpost_ref[...])
    y_ref[...] = y
    h = _rms(y, gpre_ref[...]) * (1.0 + mod_ref[4:5, :]) + mod_ref[3:4, :]
    h_ref[...] = h.astype(BF16)
    logits = jnp.dot(h, wr_ref[...], precision=lax.Precision.HIGHEST, preferred_element_type=F32)
    gates_ref[...] = _route(logits, br_ref[...], TM_IN)


def _merge(li, oab, oc, x, modv, g_post, g_pre, w_out_bf, w_router, b_router):
    def tok(w):
        return pl.BlockSpec((TM_IN, w), lambda i: (i, 0))

    def par(*shape):
        return pl.BlockSpec((None,) + shape, lambda i: (li,) + (0,) * len(shape))

    return pl.pallas_call(
        _merge_body,
        grid=(T // TM_IN,),
        in_specs=[tok(AB_W), tok(POOL_W), tok(D),
                  pl.BlockSpec((None, None, 6, D), lambda i: (li, _mod_row(i, TM_IN), 0, 0)),
                  par(1, D), par(1, D), par(D, D), par(D, N_EXP), par(1, N_EXP)],
        out_specs=[tok(D), tok(D), tok(N_EXP)],
        out_shape=[jax.ShapeDtypeStruct((T, D), F32),
                   jax.ShapeDtypeStruct((T, D), BF16),
                   jax.ShapeDtypeStruct((T, N_EXP), F32)],
        compiler_params=_cparams(("parallel",)),
        name=f"merge_l{li}",
    )(oab, oc, x, modv, g_post, g_pre, w_out_bf, w_router, b_router)


def _moe_body(h_ref, gates_ref, y_ref, mod_ref, gpost_ref, wsg_ref, wsu_ref, wsd_ref,
              wg_ref, wu_ref, wd_ref, o_ref, acc_ref):
    e = pl.program_id(1)
    h = h_ref[...]

    @pl.when(e == 0)
    def _():
        hid = _silu(_dot(h, wsg_ref[...])) * _dot(h, wsu_ref[...])
        acc_ref[...] = _dot(hid.astype(BF16), wsd_ref[...])

    lane = lax.broadcasted_iota(jnp.int32, (TM_MOE, N_EXP), 1)
    gate = jnp.sum(jnp.where(lane == e, gates_ref[...], 0.0), axis=-1, keepdims=True)
    hid = _silu(_dot(h, wg_ref[...].astype(BF16))) * _dot(h, wu_ref[...].astype(BF16))
    acc_ref[...] += _dot((hid * gate).astype(BF16), wd_ref[...].astype(BF16))

    @pl.when(e == N_EXP - 1)
    def _():
        o_ref[...] = y_ref[...] + mod_ref[5:6, :] * _rms(acc_ref[...], gpost_ref[...])


def _moe(li, h, gates, y, modv, g_post, wsg_bf, wsu_bf, wsd_bf, w_gate, w_up, w_down):
    def tok(w):
        return pl.BlockSpec((TM_MOE, w), lambda i, e: (i, 0))

    def par(*shape):
        return pl.BlockSpec((None,) + shape, lambda i, e: (li,) + (0,) * len(shape))

    def exp(*shape):
        return pl.BlockSpec((None, None) + shape, lambda i, e: (li, e) + (0,) * len(shape))

    return pl.pallas_call(
        _moe_body,
        grid=(T // TM_MOE, N_EXP),
        in_specs=[tok(D), tok(N_EXP), tok(D),
                  pl.BlockSpec((None, None, 6, D), lambda i, e: (li, _mod_row(i, TM_MOE), 0, 0)),
                  par(1, D), par(D, EXP_DIM), par(D, EXP_DIM), par(EXP_DIM, D),
                  exp(D, EXP_DIM), exp(D, EXP_DIM), exp(EXP_DIM, D)],
        out_specs=tok(D),
        out_shape=jax.ShapeDtypeStruct((T, D), F32),
        scratch_shapes=[pltpu.VMEM((TM_MOE, D), F32)],
        compiler_params=_cparams(("parallel", "arbitrary")),
        name=f"moe_l{li}",
    )(h, gates, y, modv, g_post, wsg_bf, wsu_bf, wsd_bf, w_gate, w_up, w_down)


def _block_diag(w):
    out = jnp.zeros((DEPTH, POOL_W, POOL_W), F32)
    for g in range(len(POOL_WINDOWS)):
        out = out.at[:, g * POOL_CH:(g + 1) * POOL_CH, g * POOL_CH:(g + 1) * POOL_CH].set(w[:, g])
    return out


def kernel(x_prompt, x_sample, cache_diff_k, cache_diff_v, cache_na_k, cache_na_v, c, c_ctx,
           w_ada, b_ada, g_pre_mix, g_post_mix, g_pre_ffn, g_post_ffn, w_in, w_out,
           diff_lambda, g_diff, na_bias, pool_w, pool_scale, w_router, b_router,
           w_gate, w_up, w_down, ws_gate, ws_up, ws_down):
    x = jnp.concatenate([x_prompt.reshape(T_P, D), x_sample.reshape(T_S, D)], axis=0)
    cvec = jnp.concatenate([c_ctx[None, :], c, jnp.zeros((3, D), F32)], axis=0)
    modv = _modulation(cvec, w_ada, b_ada)[:, :1 + NB_S].reshape(DEPTH, 1 + NB_S, 6, D)

    cos_t, sin_t = _rope_tables()
    w_in_bf = w_in.astype(BF16)
    w_out_bf = w_out.astype(BF16)
    wsg_bf, wsu_bf, wsd_bf = ws_gate.astype(BF16), ws_up.astype(BF16), ws_down.astype(BF16)
    pool_bd = _block_diag(pool_w)
    row = lambda a: a.reshape(DEPTH, 1, a.shape[-1])
    g_pre_mix, g_post_mix, g_pre_ffn, g_post_ffn = map(row, (g_pre_mix, g_post_mix, g_pre_ffn, g_post_ffn))
    g_diff, pool_scale, b_router = row(g_diff), row(pool_scale), row(b_router)
    ck = cache_diff_k.reshape(NB_S, DEPTH, PAST, DA_W)
    cv = cache_diff_v.reshape(NB_S, DEPTH, PAST, DA_W)
    cnk = cache_na_k.reshape(NB_S, DEPTH, PAST, NA_W)
    cnv = cache_na_v.reshape(NB_S, DEPTH, PAST, NA_W)

    new_dk, new_dv, new_nk, new_nv = [], [], [], []
    for li in range(DEPTH):
        lam_init = 0.8 - 0.6 * math.exp(-0.3 * li)
        qa, ka, va, qb, kb, vb, up = _in_proj(li, x, modv, g_pre_mix, w_in_bf, cos_t, sin_t)
        oc = jnp.concatenate([
            _pool_call(li, up, pool_bd, pool_scale, N_P, NB_P, 0, f"pool_ctx_l{li}"),
            _pool_call(li, up, pool_bd, pool_scale, N_S, NB_S, T_P // N_S, f"pool_lat_l{li}")], axis=0)
        bias = _na_bias_mask(na_bias[li])
        oab = jnp.concatenate([
            _attn_ctx(li, diff_lambda, g_diff, qa, ka, va, qb, kb, vb, lam_init),
            _attn_lat(li, diff_lambda, g_diff, qa, ka, va, ck, cv, qb, kb, vb, cnk, cnv, bias, lam_init)],
            axis=0)
        y, h, gates = _merge(li, oab, oc, x, modv, g_post_mix, g_pre_ffn, w_out_bf, w_router, b_router)
        x = _moe(li, h, gates, y, modv, g_post_ffn, wsg_bf, wsu_bf, wsd_bf, w_gate, w_up, w_down)
        new_dk.append(ka[:T_P].reshape(NB_P, N_P, DA_H, 2 * DA_QK))
        new_dv.append(va[:T_P].reshape(NB_P, N_P, DA_H, DA_V))
        new_nk.append(kb[:T_P].reshape(NB_P, N_P, NA_H, NA_D))
        new_nv.append(vb[:T_P].reshape(NB_P, N_P, NA_H, NA_D))

    yp = x[:T_P].reshape(NB_P, N_P, D)
    ys = x[T_P:].reshape(NB_S, N_S, D)
    return (yp, ys, jnp.stack(new_dk, axis=1), jnp.stack(new_dv, axis=1),
            jnp.stack(new_nk, axis=1), jnp.stack(new_nv, axis=1))
```

```python
import functools
import math

import numpy as np
import jax
import jax.numpy as jnp
from jax import lax
from jax.experimental import pallas as pl
from jax.experimental.pallas import tpu as pltpu

F32 = jnp.float32
BF16 = jnp.bfloat16

D = 1024
DEPTH = 2
NB_P, N_P = 16, 256
NB_S, N_S = 4, 1024
T_P = NB_P * N_P
T_S = NB_S * N_S
T = T_P + T_S
PAST = 512
GRID_W = 64
DA_H, DA_QK, DA_V = 4, 64, 128
DA_W = DA_H * DA_V
NA_H, NA_D = 4, 64
NA_W = NA_H * NA_D
NA_ROWS, NA_COLS = 8, 16
POOL_WINDOWS = (2, 4, 8, 16)
POOL_CH = 64
POOL_W = 256
IN_COLS = 3 * DA_W + 3 * NA_W + POOL_W
AB_W = DA_W + NA_W
N_EXP, TOP_K, N_GRP, TOPK_GRP = 64, 8, 8, 4
EXP_DIM = 256
ROUTED_SCALE = 2.5
EPS = 1e-6
NEG = -1e30
ROPE_THETA = 10000.0

VMEM_LIMIT = 50 * 1024 * 1024

TM_IN = 512
TQ = 256

B_SORT = 256
UNIT = 16
N_BLK = T // B_SORT
RUN_PAD_ROWS = TOP_K * B_SORT + N_EXP * (UNIT - 1)
RB = -(-RUN_PAD_ROWS // 256) * 256
UPB = RB // UNIT
TMG = 256
UPT = TMG // UNIT
NT_MAX = (N_BLK * RUN_PAD_ROWS) // TMG + N_EXP
R_PAD = NT_MAX * TMG
XS_W = D + 128
EXP_PER_BLK = N_EXP // N_BLK
ZPB = EXP_PER_BLK * (UPT - 1)


def _cparams(sem):
    return pltpu.CompilerParams(dimension_semantics=sem, vmem_limit_bytes=VMEM_LIMIT)


def _rms(x, g):
    return x * lax.rsqrt(jnp.mean(x * x, axis=-1, keepdims=True) + EPS) * g


def _dot(a, b):
    return jnp.dot(a, b, preferred_element_type=F32)


def _dot_nt(a, b):
    return lax.dot_general(a, b, (((1,), (1,)), ((), ())), preferred_element_type=F32)


def _silu(x):
    return x / (1.0 + jnp.exp(-x))


def _mod_row(i, tm):
    off = i * tm - T_P
    return jnp.where(off >= 0, 1 + jnp.maximum(off, 0) // N_S, 0)


def _mod_body(c_ref, w_ref, b_ref, o_ref):
    c = c_ref[...]
    o_ref[...] = jnp.dot(_silu(c), w_ref[...], precision=lax.Precision.HIGHEST,
                         preferred_element_type=F32) + b_ref[...]


def _modulation(cvec, w_ada, b_ada):
    tn = 1536
    return pl.pallas_call(
        _mod_body,
        grid=(DEPTH, 6 * D // tn),
        in_specs=[pl.BlockSpec((8, D), lambda l, j: (0, 0)),
                  pl.BlockSpec((None, D, tn), lambda l, j: (l, 0, j)),
                  pl.BlockSpec((None, 1, tn), lambda l, j: (l, 0, j))],
        out_specs=pl.BlockSpec((None, 8, tn), lambda l, j: (l, 0, j)),
        out_shape=jax.ShapeDtypeStruct((DEPTH, 8, 6 * D), F32),
        compiler_params=_cparams(("parallel", "parallel")),
        name="modulation",
    )(cvec, w_ada, b_ada.reshape(DEPTH, 1, 6 * D))


def _rope_tables():
    nf = DA_QK // 4
    t = np.arange(N_S)
    pos = np.stack([t // GRID_W, t % GRID_W], axis=-1).astype(np.float32)
    inv = np.power(np.float32(ROPE_THETA), -np.arange(nf, dtype=np.float32) / nf)
    ang = pos[:, :, None] * inv
    cos = np.cos(ang)
    sin = np.sin(ang)
    cos64 = np.concatenate([cos[:, 0], cos[:, 0], cos[:, 1], cos[:, 1]], axis=-1)
    sin64 = np.concatenate([-sin[:, 0], sin[:, 0], -sin[:, 1], sin[:, 1]], axis=-1)
    reps = DA_W // DA_QK
    cos_t = np.concatenate([np.tile(cos64, (1, reps)), np.ones((TM_IN, DA_W), np.float32)], axis=0)
    sin_t = np.concatenate([np.tile(sin64, (1, reps)), np.zeros((TM_IN, DA_W), np.float32)], axis=0)
    return jnp.asarray(cos_t, F32), jnp.asarray(sin_t, F32)


def _in_proj_body(x_ref, mod_ref, g_ref, w_ref, cos_ref, sin_ref,
                  qa_ref, ka_ref, va_ref, qb_ref, kb_ref, vb_ref, up_ref):
    x = x_ref[...]
    h = (_rms(x, g_ref[...]) * (1.0 + mod_ref[1:2, :]) + mod_ref[0:1, :]).astype(BF16)
    cos = cos_ref[...]
    sin = sin_ref[...]
    lane = lax.broadcasted_iota(jnp.int32, (TM_IN, DA_W), 1)
    first = (lane % 32) < 16

    def proj(lo, hi):
        return _dot(h, w_ref[:, lo:hi])

    def rope(t):
        swapped = jnp.where(first, pltpu.roll(t, DA_W - 16, 1), pltpu.roll(t, 16, 1))
        return t * cos + swapped * sin

    o = 0
    qa_ref[...] = (rope(proj(o, o + DA_W)) * (DA_QK ** -0.5)).astype(BF16)
    o += DA_W
    ka_ref[...] = rope(proj(o, o + DA_W))
    o += DA_W
    va_ref[...] = proj(o, o + DA_W)
    o += DA_W
    qb_ref[...] = (proj(o, o + NA_W) * (NA_D ** -0.5)).astype(BF16)
    o += NA_W
    kb_ref[...] = proj(o, o + NA_W)
    o += NA_W
    vb_ref[...] = proj(o, o + NA_W)
    o += NA_W
    up_ref[...] = proj(o, o + POOL_W)


def _in_proj(li, x, modv, g_pre, w_in_bf, cos_t, sin_t):
    nt = T // TM_IN
    n_pos_blk = N_S // TM_IN

    def pos_blk(i):
        off = i * TM_IN - T_P
        return jnp.where(off >= 0, (jnp.maximum(off, 0) // TM_IN) % n_pos_blk, n_pos_blk)

    def tok(w):
        return pl.BlockSpec((TM_IN, w), lambda i: (i, 0))

    return pl.pallas_call(
        _in_proj_body,
        grid=(nt,),
        in_specs=[tok(D),
                  pl.BlockSpec((None, None, 6, D), lambda i: (li, _mod_row(i, TM_IN), 0, 0)),
                  pl.BlockSpec((None, 1, D), lambda i: (li, 0, 0)),
                  pl.BlockSpec((None, D, IN_COLS), lambda i: (li, 0, 0)),
                  pl.BlockSpec((TM_IN, DA_W), lambda i: (pos_blk(i), 0)),
                  pl.BlockSpec((TM_IN, DA_W), lambda i: (pos_blk(i), 0))],
        out_specs=[tok(DA_W), tok(DA_W), tok(DA_W), tok(NA_W), tok(NA_W), tok(NA_W), tok(POOL_W)],
        out_shape=[jax.ShapeDtypeStruct((T, DA_W), BF16),
                   jax.ShapeDtypeStruct((T, DA_W), F32),
                   jax.ShapeDtypeStruct((T, DA_W), F32),
                   jax.ShapeDtypeStruct((T, NA_W), BF16),
                   jax.ShapeDtypeStruct((T, NA_W), F32),
                   jax.ShapeDtypeStruct((T, NA_W), F32),
                   jax.ShapeDtypeStruct((T, POOL_W), F32)],
        compiler_params=_cparams(("parallel",)),
        name=f"in_proj_l{li}",
    )(x, modv, g_pre, w_in_bf, cos_t, sin_t)


def _pool_body(u_ref, w_ref, s_ref, o_ref, *, n):
    u = u_ref[...]
    row = lax.broadcasted_iota(jnp.int32, (n, POOL_W), 0)
    lane = lax.broadcasted_iota(jnp.int32, (n, POOL_W), 1)

    def shift_dn(a, k):
        return jnp.where(row >= k, pltpu.roll(a, k, 0), 0.0)

    def shift_up(a, k):
        return jnp.where(row < n - k, pltpu.roll(a, n - k, 0), 0.0)

    fwd = u
    bwd = shift_dn(u, 1)
    mean = jnp.zeros_like(u)
    k = 1
    for gi, w in enumerate(POOL_WINDOWS):
        while k < w // 2:
            fwd = fwd + shift_up(fwd, k)
            bwd = bwd + shift_dn(bwd, k)
            k *= 2
        cnt = (jnp.minimum(row + w // 2, n) - jnp.maximum(row - w // 2, 0)).astype(F32)
        in_group = (lane >= gi * POOL_CH) & (lane < (gi + 1) * POOL_CH)
        mean = jnp.where(in_group, (fwd + bwd) / cnt, mean)
    pooled = mean - u
    mixed = jnp.dot(pooled, w_ref[...], precision=lax.Precision.HIGHEST, preferred_element_type=F32)
    o_ref[...] = (mixed * s_ref[...]).astype(BF16)


def _pool_call(li, up, w_bd, scale, n, nseg, blk0, name):
    return pl.pallas_call(
        functools.partial(_pool_body, n=n),
        grid=(nseg,),
        in_specs=[pl.BlockSpec((n, POOL_W), lambda i: (blk0 + i, 0)),
                  pl.BlockSpec((None, POOL_W, POOL_W), lambda i: (li, 0, 0)),
                  pl.BlockSpec((None, 1, POOL_W), lambda i: (li, 0, 0))],
        out_specs=pl.BlockSpec((n, POOL_W), lambda i: (i, 0)),
        out_shape=jax.ShapeDtypeStruct((nseg * n, POOL_W), BF16),
        compiler_params=_cparams(("parallel",)),
        name=name,
    )(up, w_bd, scale)


def _lambda(lp, lam_init):
    a = jnp.sum(lp[0:1, :] * lp[1:2, :], axis=1, keepdims=True)
    b = jnp.sum(lp[2:3, :] * lp[3:4, :], axis=1, keepdims=True)
    return jnp.exp(a) - jnp.exp(b) + lam_init


def _softmax_parts(parts):
    m = functools.reduce(jnp.maximum, [jnp.max(s, axis=-1, keepdims=True) for s in parts])
    es = [jnp.exp(s - m) for s in parts]
    inv = 1.0 / functools.reduce(jnp.add, [jnp.sum(e, axis=-1, keepdims=True) for e in es])
    return [e * inv for e in es]


def _diff_head(q, ks, vs, lam, gd, lam_init):
    p1 = _softmax_parts([_dot_nt(q[:, :DA_QK], k[:, :DA_QK]) for k in ks])
    p2 = _softmax_parts([_dot_nt(q[:, DA_QK:], k[:, DA_QK:]) for k in ks])
    o = None
    for a, b, v in zip(p1, p2, vs):
        t = _dot((a - lam * b).astype(BF16), v)
        o = t if o is None else o + t
    return _rms(o, gd) * (1.0 - lam_init)


def _plain_head(q, ks, vs, biases):
    scores = []
    for k, bias in zip(ks, biases):
        s = _dot_nt(q, k)
        scores.append(s if bias is None else s + bias)
    ps = _softmax_parts(scores)
    o = None
    for p, v in zip(ps, vs):
        t = _dot(p.astype(BF16), v)
        o = t if o is None else o + t
    return o


def _attn_ctx_body(lam_ref, gd_ref, qa_ref, ka_ref, va_ref, qb_ref, kb_ref, vb_ref, o_ref, *, lam_init):
    lam = _lambda(lam_ref[...], lam_init)
    gd = gd_ref[...]
    for h in range(DA_H):
        sl = slice(h * DA_V, (h + 1) * DA_V)
        o = _diff_head(qa_ref[:, sl], [ka_ref[:, sl].astype(BF16)], [va_ref[:, sl].astype(BF16)],
                       lam, gd, lam_init)
        o_ref[:, sl] = o.astype(BF16)
    for h in range(NA_H):
        sl = slice(h * NA_D, (h + 1) * NA_D)
        o = _plain_head(qb_ref[:, sl], [kb_ref[:, sl].astype(BF16)], [vb_ref[:, sl].astype(BF16)], [None])
        o_ref[:, DA_W + h * NA_D:DA_W + (h + 1) * NA_D] = o.astype(BF16)


def _attn_ctx(li, lam_p, g_diff, qa, ka, va, qb, kb, vb, lam_init):
    def blk(w):
        return pl.BlockSpec((N_P, w), lambda b: (b, 0))

    return pl.pallas_call(
        functools.partial(_attn_ctx_body, lam_init=lam_init),
        grid=(NB_P,),
        in_specs=[pl.BlockSpec((None, 4, DA_QK), lambda b: (li, 0, 0)),
                  pl.BlockSpec((None, 1, DA_V), lambda b: (li, 0, 0)),
                  blk(DA_W), blk(DA_W), blk(DA_W), blk(NA_W), blk(NA_W), blk(NA_W)],
        out_specs=pl.BlockSpec((N_P, AB_W), lambda b: (b, 0)),
        out_shape=jax.ShapeDtypeStruct((T_P, AB_W), BF16),
        compiler_params=_cparams(("parallel",)),
        name=f"attn_ctx_l{li}",
    )(lam_p, g_diff, qa, ka, va, qb, kb, vb)


def _attn_lat_body(lam_ref, gd_ref, qa_ref, ka_ref, va_ref, ck_ref, cv_ref,
                   qb_ref, kb_ref, vb_ref, cnk_ref, cnv_ref, bias_ref, o_ref, *, lam_init):
    lam = _lambda(lam_ref[...], lam_init)
    gd = gd_ref[...]
    for h in range(DA_H):
        sl = slice(h * DA_V, (h + 1) * DA_V)
        ks = [ka_ref[:, sl].astype(BF16), ck_ref[:, sl].astype(BF16)]
        vs = [va_ref[:, sl].astype(BF16), cv_ref[:, sl].astype(BF16)]
        o_ref[:, sl] = _diff_head(qa_ref[:, sl], ks, vs, lam, gd, lam_init).astype(BF16)
    for h in range(NA_H):
        sl = slice(h * NA_D, (h + 1) * NA_D)
        ks = [kb_ref[:, sl].astype(BF16), cnk_ref[:, sl].astype(BF16)]
        vs = [vb_ref[:, sl].astype(BF16), cnv_ref[:, sl].astype(BF16)]
        o = _plain_head(qb_ref[:, sl], ks, vs, [bias_ref[h], None])
        o_ref[:, DA_W + h * NA_D:DA_W + (h + 1) * NA_D] = o.astype(BF16)


def _attn_lat(li, lam_p, g_diff, qa, ka, va, ck, cv, qb, kb, vb, cnk, cnv, bias, lam_init):
    nq = N_S // TQ
    q0 = T_P // TQ
    b0 = T_P // N_S

    def qblk(w):
        return pl.BlockSpec((TQ, w), lambda b, j: (q0 + b * nq + j, 0))

    def kvblk(w):
        return pl.BlockSpec((N_S, w), lambda b, j: (b0 + b, 0))

    def cblk(w):
        return pl.BlockSpec((None, None, PAST, w), lambda b, j: (b, li, 0, 0))

    return pl.pallas_call(
        functools.partial(_attn_lat_body, lam_init=lam_init),
        grid=(NB_S, nq),
        in_specs=[pl.BlockSpec((None, 4, DA_QK), lambda b, j: (li, 0, 0)),
                  pl.BlockSpec((None, 1, DA_V), lambda b, j: (li, 0, 0)),
                  qblk(DA_W), kvblk(DA_W), kvblk(DA_W), cblk(DA_W), cblk(DA_W),
                  qblk(NA_W), kvblk(NA_W), kvblk(NA_W), cblk(NA_W), cblk(NA_W),
                  pl.BlockSpec((NA_H, TQ, N_S), lambda b, j: (0, j, 0))],
        out_specs=pl.BlockSpec((TQ, AB_W), lambda b, j: (b * nq + j, 0)),
        out_shape=jax.ShapeDtypeStruct((T_S, AB_W), BF16),
        compiler_params=_cparams(("parallel", "parallel")),
        name=f"attn_lat_l{li}",
    )(lam_p, g_diff, qa, ka, va, ck, cv, qb, kb, vb, cnk, cnv, bias)


def _na_bias_mask(table):
    rows = N_S // GRID_W
    t = np.arange(N_S)
    r, c = t // GRID_W, t % GRID_W
    rs = np.clip(r - NA_ROWS // 2, 0, rows - NA_ROWS)
    cs = np.clip(c - NA_COLS // 2, 0, GRID_W - NA_COLS)
    kr, kc = r[None, :], c[None, :]
    valid = ((kr >= rs[:, None]) & (kr < rs[:, None] + NA_ROWS)
             & (kc >= cs[:, None]) & (kc < cs[:, None] + NA_COLS))
    g = np.arange(rows)
    sel_r = (g[None, :, None] - g[:, None, None] + NA_ROWS - 1 == np.arange(2 * NA_ROWS - 1)).astype(np.float32)
    w = np.arange(GRID_W)
    sel_c = (np.clip(w[None, :, None] - w[:, None, None] + NA_COLS - 1, 0, 2 * NA_COLS - 2)
             == np.arange(2 * NA_COLS - 1)).astype(np.float32)
    hp = lax.Precision.HIGHEST
    t1 = jnp.einsum('hab,cdb->hacd', table.astype(F32), jnp.asarray(sel_c), precision=hp)
    bias = jnp.einsum('rsa,hacd->hrcsd', jnp.asarray(sel_r), t1, precision=hp).reshape(NA_H, N_S, N_S)
    return jnp.where(jnp.asarray(valid)[None], bias, NEG)


def _route_t(logits_t, b_col, tm):
    per = N_EXP // N_GRP
    s = 1.0 / (1.0 + jnp.exp(-logits_t))
    sel3 = (s + b_col).reshape(N_GRP, per, tm)
    midx = lax.broadcasted_iota(jnp.int32, (N_GRP, per, tm), 1)
    gidx = lax.broadcasted_iota(jnp.int32, (N_GRP, per, tm), 0)
    ninf = -jnp.inf
    m1 = jnp.max(sel3, axis=1, keepdims=True)
    i1 = jnp.min(jnp.where(sel3 == m1, midx, per), axis=1, keepdims=True)
    m2 = jnp.max(jnp.where(midx == i1, ninf, sel3), axis=1, keepdims=True)
    gsc = jnp.broadcast_to(m1 + m2, (N_GRP, per, tm))
    cnt = jnp.zeros((N_GRP, per, tm), jnp.int32)
    for g in range(N_GRP):
        sg = gsc[g][None]
        cnt = cnt + ((sg > gsc) | ((sg == gsc) & (g < gidx))).astype(jnp.int32)
    x = jnp.where(cnt < TOPK_GRP, sel3, ninf).reshape(N_EXP, tm)
    eidx = lax.broadcasted_iota(jnp.int32, (N_EXP, tm), 0)
    cnt = jnp.zeros((N_EXP, tm), jnp.int32)
    for e in range(N_EXP):
        row = x[e:e + 1, :]
        cnt = cnt + ((row > x) | ((row == x) & (e < eidx))).astype(jnp.int32)
    chosen = cnt < TOP_K
    w = jnp.where(chosen, s, 0.0)
    return chosen, w / jnp.sum(w, axis=0, keepdims=True) * ROUTED_SCALE


def _merge_body(oab_ref, oc_ref, x_ref, mod_ref, gpost_ref, gpre_ref, wout_ref, wrt_ref, br_ref, tri_ref,
                y_ref, h_ref, gates_ref, rank_ref, pcu_ref):
    mix = _dot(oab_ref[...], wout_ref[0:AB_W, :]) + _dot(oc_ref[...], wout_ref[AB_W:D, :])
    y = x_ref[...] + mod_ref[2:3, :] * _rms(mix, gpost_ref[...])
    y_ref[...] = y
    h = _rms(y, gpre_ref[...]) * (1.0 + mod_ref[4:5, :]) + mod_ref[3:4, :]
    h_ref[...] = h.astype(BF16)
    logits_t = lax.dot_general(wrt_ref[...], h, (((1,), (1,)), ((), ())),
                               precision=lax.Precision.HIGHEST, preferred_element_type=F32)
    chosen, gates = _route_t(logits_t, br_ref[...], TM_IN)
    gates_ref[...] = gates
    ch = jnp.where(chosen, 1.0, 0.0)
    rank = _dot(ch.astype(BF16), tri_ref[...])
    rank_ref[...] = jnp.where(chosen, rank, -1.0)
    for j in range(TM_IN // B_SORT):
        cnt = jnp.sum(ch[:, j * B_SORT:(j + 1) * B_SORT], axis=1, keepdims=True)
        units = jnp.floor((cnt + (UNIT - 1)) * (1.0 / UNIT))
        pcu_ref[j] = jnp.broadcast_to(units, (N_EXP, 128))


def _merge(li, oab, oc, x, modv, g_post, g_pre, w_out_bf, w_router_t, b_router_col, tri):
    def tok(w):
        return pl.BlockSpec((TM_IN, w), lambda i: (i, 0))

    def par(*shape):
        return pl.BlockSpec((None,) + shape, lambda i: (li,) + (0,) * len(shape))

    tok_t = pl.BlockSpec((N_EXP, TM_IN), lambda i: (0, i))
    nb = TM_IN // B_SORT
    return pl.pallas_call(
        _merge_body,
        grid=(T // TM_IN,),
        in_specs=[tok(AB_W), tok(POOL_W), tok(D),
                  pl.BlockSpec((None, None, 6, D), lambda i: (li, _mod_row(i, TM_IN), 0, 0)),
                  par(1, D), par(1, D), par(D, D), par(N_EXP, D), par(N_EXP, 1),
                  pl.BlockSpec((TM_IN, TM_IN), lambda i: (0, 0))],
        out_specs=[tok(D), tok(D), tok_t, tok_t,
                   pl.BlockSpec((nb, N_EXP, 128), lambda i: (i, 0, 0))],
        out_shape=[jax.ShapeDtypeStruct((T, D), F32),
                   jax.ShapeDtypeStruct((T, D), BF16),
                   jax.ShapeDtypeStruct((N_EXP, T), F32),
                   jax.ShapeDtypeStruct((N_EXP, T), F32),
                   jax.ShapeDtypeStruct((N_BLK, N_EXP, 128), F32)],
        compiler_params=_cparams(("parallel",)),
        name=f"merge_l{li}",
    )(oab, oc, x, modv, g_post, g_pre, w_out_bf, w_router_t, b_router_col, tri)


def _rank_matrix():
    t = np.arange(TM_IN)
    m = (t[:, None] < t[None, :]) & (t[:, None] // B_SORT == t[None, :] // B_SORT)
    return jnp.asarray(m, BF16)


def _moe_tables(pcu_f):
    pcu = pcu_f[:, :, 0].astype(jnp.int32)
    incl = jnp.cumsum(pcu, axis=1)
    uo = incl - pcu
    tot = jnp.sum(pcu, axis=0)
    gt = (tot + UPT - 1) // UPT
    gincl = jnp.cumsum(gt)
    gstart_u = (gincl - gt) * UPT
    bstart = gstart_u[None, :] + jnp.cumsum(pcu, axis=0) - pcu
    u = jnp.arange(UPB, dtype=jnp.int32)
    eou = jnp.sum((u[None, :, None] >= incl[:, None, :]).astype(jnp.int32), axis=-1)
    valid = eou < N_EXP
    onehot = eou[:, :, None] == jnp.arange(N_EXP, dtype=jnp.int32)[None, None, :]
    run_u0 = jnp.sum(jnp.where(onehot, uo[:, None, :], 0), axis=-1)
    run_g0 = jnp.sum(jnp.where(onehot, bstart[:, None, :], 0), axis=-1)
    dst = jnp.where(valid, run_g0 + u[None, :] - run_u0, -1)
    loc = jnp.where(valid, (u[None, :] - run_u0) * UNIT, -(1 << 20))
    k = jnp.arange(UPT - 1, dtype=jnp.int32)
    ztail = gt * UPT - tot
    zdst = jnp.where(k[None, :] < ztail[:, None], (gstart_u + tot)[:, None] + k[None, :], -1)
    tile = jnp.arange(NT_MAX, dtype=jnp.int32)
    te = jnp.minimum(jnp.sum((tile[:, None] >= gincl[None, :]).astype(jnp.int32), axis=-1), N_EXP - 1)
    return (jnp.minimum(eou, N_EXP - 1), loc, dst, zdst.reshape(N_BLK, ZPB), te, gincl[-1:])


def _unit_rows(u):
    if isinstance(u, int):
        return pl.ds(u * UNIT, UNIT)
    return pl.ds(pl.multiple_of(u * UNIT, UNIT), UNIT)


def _build_onehot(b, eou_ref, loc_ref, rank_ref, p_scr, gate_fn=None):
    base = lax.broadcasted_iota(jnp.int32, (UNIT, B_SORT), 0)

    def unit(u, carry):
        e = eou_ref[b, u]
        match = rank_ref[pl.ds(e, 1), :] == (base + loc_ref[b, u]).astype(F32)
        p_scr[_unit_rows(u), :] = jnp.where(match, 1.0, 0.0).astype(BF16)
        if gate_fn is not None:
            gate_fn(u, e, match)
        return carry

    lax.fori_loop(0, UPB, unit, 0)


def _sort_body(eou_ref, loc_ref, dst_ref, zdst_ref, h_ref, rank_ref, gates_ref, xs_hbm,
               p_scr, xs_scr, zero_scr, sem):
    b = pl.program_id(0)
    zero_scr[...] = jnp.zeros((UNIT, XS_W), BF16)
    lane = lax.broadcasted_iota(jnp.int32, (UNIT, 128), 1)

    def gate_slab(u, e, match):
        g = jnp.sum(jnp.where(match, gates_ref[pl.ds(e, 1), :], 0.0), axis=1, keepdims=True)
        hi = g.astype(BF16).astype(F32)
        xs_scr[_unit_rows(u), D:XS_W] = jnp.where(lane < 64, hi, g - hi).astype(BF16)

    _build_onehot(b, eou_ref, loc_ref, rank_ref, p_scr, gate_slab)
    h = h_ref[...]
    for c in range(RB // 512):
        xs_scr[c * 512:(c + 1) * 512, 0:D] = _dot(p_scr[c * 512:(c + 1) * 512, :], h).astype(BF16)

    def copy(src, d):
        return pltpu.make_async_copy(src, xs_hbm.at[_unit_rows(d)], sem)

    def start_run(u, n):
        d = dst_ref[b, u]

        @pl.when(d >= 0)
        def _():
            copy(xs_scr.at[_unit_rows(u)], d).start()

        return n + (d >= 0).astype(jnp.int32)

    def start_zero(j, n):
        d = zdst_ref[b, j]

        @pl.when(d >= 0)
        def _():
            copy(zero_scr, d).start()

        return n + (d >= 0).astype(jnp.int32)

    n = lax.fori_loop(0, UPB, start_run, 0)
    n = lax.fori_loop(0, ZPB, start_zero, n)

    def wait_one(k, carry):
        copy(zero_scr, 0).wait()
        return carry

    lax.fori_loop(0, n, wait_one, 0)


def _moe_sort(li, eou, loc, dst, zdst, h, rank_t, gates_t):
    blk_t = pl.BlockSpec((N_EXP, B_SORT), lambda b, *_: (0, b))
    return pl.pallas_call(
        _sort_body,
        grid_spec=pltpu.PrefetchScalarGridSpec(
            num_scalar_prefetch=4,
            grid=(N_BLK,),
            in_specs=[pl.BlockSpec((B_SORT, D), lambda b, *_: (b, 0)), blk_t, blk_t],
            out_specs=pl.BlockSpec(memory_space=pl.ANY),
            scratch_shapes=[pltpu.VMEM((RB, B_SORT), BF16),
                            pltpu.VMEM((RB, XS_W), BF16),
                            pltpu.VMEM((UNIT, XS_W), BF16),
                            pltpu.SemaphoreType.DMA]),
        out_shape=jax.ShapeDtypeStruct((R_PAD, XS_W), BF16),
        compiler_params=_cparams(("arbitrary",)),
        name=f"moe_sort_l{li}",
    )(eou, loc, dst, zdst, h, rank_t, gates_t)


def _experts_body(te_ref, nt_ref, xs_ref, wg_ref, wu_ref, wd_ref, o_ref, wg_s, wu_s, wd_s):
    i = pl.program_id(0)
    valid = i < nt_ref[0]
    new_expert = (i == 0) | (te_ref[i] != te_ref[jnp.maximum(i - 1, 0)])

    @pl.when(valid & new_expert)
    def _():
        wg_s[...] = wg_ref[...].astype(BF16)
        wu_s[...] = wu_ref[...].astype(BF16)
        wd_s[...] = wd_ref[...].astype(BF16)

    @pl.when(valid)
    def _():
        x = xs_ref[:, 0:D]
        gate = xs_ref[:, D:D + 1].astype(F32) + xs_ref[:, D + 64:D + 65].astype(F32)
        hid = _silu(_dot(x, wg_s[...])) * _dot(x, wu_s[...]) * gate
        o_ref[...] = _dot(hid.astype(BF16), wd_s[...]).astype(BF16)


def _moe_experts(li, te, nt, xs, w_gate, w_up, w_down):
    def last(i, nt):
        return jnp.minimum(i, nt[0] - 1)

    def wspec(*shape):
        return pl.BlockSpec((None, None) + shape,
                            lambda i, te, nt: (li, te[last(i, nt)]) + (0,) * len(shape))

    return pl.pallas_call(
        _experts_body,
        grid_spec=pltpu.PrefetchScalarGridSpec(
            num_scalar_prefetch=2,
            grid=(NT_MAX,),
            in_specs=[pl.BlockSpec((TMG, XS_W), lambda i, te, nt: (last(i, nt), 0)),
                      wspec(D, EXP_DIM), wspec(D, EXP_DIM), wspec(EXP_DIM, D)],
            out_specs=pl.BlockSpec((TMG, D), lambda i, te, nt: (last(i, nt), 0)),
            scratch_shapes=[pltpu.VMEM((D, EXP_DIM), BF16),
                            pltpu.VMEM((D, EXP_DIM), BF16),
                            pltpu.VMEM((EXP_DIM, D), BF16)]),
        out_shape=jax.ShapeDtypeStruct((R_PAD, D), BF16),
        compiler_params=_cparams(("arbitrary",)),
        name=f"moe_experts_l{li}",
    )(te, nt, xs, w_gate, w_up, w_down)


def _combine_body(eou_ref, loc_ref, dst_ref, h_ref, rank_ref, y_ref, mod_ref, gpost_ref,
                  wsg_ref, wsu_ref, wsd_ref, os_hbm, o_ref, p_scr, os_scr, sem):
    b = pl.program_id(0)

    def copy(d, u):
        return pltpu.make_async_copy(os_hbm.at[_unit_rows(d)], os_scr.at[_unit_rows(u)], sem)

    def start_run(u, n):
        d = dst_ref[b, u]

        @pl.when(d >= 0)
        def _():
            copy(d, u).start()

        @pl.when(d < 0)
        def _():
            os_scr[_unit_rows(u), :] = jnp.zeros((UNIT, D), BF16)

        return n + (d >= 0).astype(jnp.int32)

    n = lax.fori_loop(0, UPB, start_run, 0)
    _build_onehot(b, eou_ref, loc_ref, rank_ref, p_scr)
    h = h_ref[...]
    shared = _dot((_silu(_dot(h, wsg_ref[...])) * _dot(h, wsu_ref[...])).astype(BF16), wsd_ref[...])

    def wait_one(k, carry):
        copy(0, 0).wait()
        return carry

    lax.fori_loop(0, n, wait_one, 0)
    routed = lax.dot_general(p_scr[...], os_scr[...], (((0,), (0,)), ((), ())),
                             preferred_element_type=F32)
    o_ref[...] = y_ref[...] + mod_ref[5:6, :] * _rms(routed + shared, gpost_ref[...])


def _moe_combine(li, eou, loc, dst, h, rank_t, y, modv, g_post, wsg_bf, wsu_bf, wsd_bf, os):
    def tok(w):
        return pl.BlockSpec((B_SORT, w), lambda b, *_: (b, 0))

    def par(*shape):
        return pl.BlockSpec((None,) + shape, lambda b, *_: (li,) + (0,) * len(shape))

    return pl.pallas_call(
        _combine_body,
        grid_spec=pltpu.PrefetchScalarGridSpec(
            num_scalar_prefetch=3,
            grid=(N_BLK,),
            in_specs=[tok(D), pl.BlockSpec((N_EXP, B_SORT), lambda b, *_: (0, b)), tok(D),
                      pl.BlockSpec((None, None, 6, D), lambda b, *_: (li, _mod_row(b, B_SORT), 0, 0)),
                      par(1, D), par(D, EXP_DIM), par(D, EXP_DIM), par(EXP_DIM, D),
                      pl.BlockSpec(memory_space=pl.ANY)],
            out_specs=tok(D),
            scratch_shapes=[pltpu.VMEM((RB, B_SORT), BF16),
                            pltpu.VMEM((RB, D), BF16),
                            pltpu.SemaphoreType.DMA]),
        out_shape=jax.ShapeDtypeStruct((T, D), F32),
        compiler_params=_cparams(("arbitrary",)),
        name=f"moe_combine_l{li}",
    )(eou, loc, dst, h, rank_t, y, modv, g_post, wsg_bf, wsu_bf, wsd_bf, os)


def _block_diag(w):
    out = jnp.zeros((DEPTH, POOL_W, POOL_W), F32)
    for g in range(len(POOL_WINDOWS)):
        out = out.at[:, g * POOL_CH:(g + 1) * POOL_CH, g * POOL_CH:(g + 1) * POOL_CH].set(w[:, g])
    return out


def kernel(x_prompt, x_sample, cache_diff_k, cache_diff_v, cache_na_k, cache_na_v, c, c_ctx,
           w_ada, b_ada, g_pre_mix, g_post_mix, g_pre_ffn, g_post_ffn, w_in, w_out,
           diff_lambda, g_diff, na_bias, pool_w, pool_scale, w_router, b_router,
           w_gate, w_up, w_down, ws_gate, ws_up, ws_down):
    x = jnp.concatenate([x_prompt.reshape(T_P, D), x_sample.reshape(T_S, D)], axis=0)
    cvec = jnp.concatenate([c_ctx[None, :], c, jnp.zeros((3, D), F32)], axis=0)
    modv = _modulation(cvec, w_ada, b_ada)[:, :1 + NB_S].reshape(DEPTH, 1 + NB_S, 6, D)

    cos_t, sin_t = _rope_tables()
    w_in_bf = w_in.astype(BF16)
    w_out_bf = w_out.astype(BF16)
    wsg_bf, wsu_bf, wsd_bf = ws_gate.astype(BF16), ws_up.astype(BF16), ws_down.astype(BF16)
    pool_bd = _block_diag(pool_w)
    row = lambda a: a.reshape(DEPTH, 1, a.shape[-1])
    g_pre_mix, g_post_mix, g_pre_ffn, g_post_ffn = map(row, (g_pre_mix, g_post_mix, g_pre_ffn, g_post_ffn))
    g_diff, pool_scale = row(g_diff), row(pool_scale)
    w_router_t = jnp.swapaxes(w_router, 1, 2)
    b_router_col = b_router.reshape(DEPTH, N_EXP, 1)
    tri = _rank_matrix()
    ck = cache_diff_k.reshape(NB_S, DEPTH, PAST, DA_W)
    cv = cache_diff_v.reshape(NB_S, DEPTH, PAST, DA_W)
    cnk = cache_na_k.reshape(NB_S, DEPTH, PAST, NA_W)
    cnv = cache_na_v.reshape(NB_S, DEPTH, PAST, NA_W)

    new_dk, new_dv, new_nk, new_nv = [], [], [], []
    for li in range(DEPTH):
        lam_init = 0.8 - 0.6 * math.exp(-0.3 * li)
        qa, ka, va, qb, kb, vb, up = _in_proj(li, x, modv, g_pre_mix, w_in_bf, cos_t, sin_t)
        oc = jnp.concatenate([
            _pool_call(li, up, pool_bd, pool_scale, N_P, NB_P, 0, f"pool_ctx_l{li}"),
            _pool_call(li, up, pool_bd, pool_scale, N_S, NB_S, T_P // N_S, f"pool_lat_l{li}")], axis=0)
        bias = _na_bias_mask(na_bias[li])
        oab = jnp.concatenate([
            _attn_ctx(li, diff_lambda, g_diff, qa, ka, va, qb, kb, vb, lam_init),
            _attn_lat(li, diff_lambda, g_diff, qa, ka, va, ck, cv, qb, kb, vb, cnk, cnv, bias, lam_init)],
            axis=0)
        y, h, gates_t, rank_t, pcu = _merge(li, oab, oc, x, modv, g_post_mix, g_pre_ffn, w_out_bf,
                                            w_router_t, b_router_col, tri)
        eou, loc, dst, zdst, te, nt = _moe_tables(pcu)
        xs = _moe_sort(li, eou, loc, dst, zdst, h, rank_t, gates_t)
        os = _moe_experts(li, te, nt, xs, w_gate, w_up, w_down)
        x = _moe_combine(li, eou, loc, dst, h, rank_t, y, modv, g_post_ffn, wsg_bf, wsu_bf, wsd_bf, os)
        new_dk.append(ka[:T_P].reshape(NB_P, N_P, DA_H, 2 * DA_QK))
        new_dv.append(va[:T_P].reshape(NB_P, N_P, DA_H, DA_V))
        new_nk.append(kb[:T_P].reshape(NB_P, N_P, NA_H, NA_D))
        new_nv.append(vb[:T_P].reshape(NB_P, N_P, NA_H, NA_D))

    yp = x[:T_P].reshape(NB_P, N_P, D)
    ys = x[T_P:].reshape(NB_S, N_S, D)
    return (yp, ys, jnp.stack(new_dk, axis=1), jnp.stack(new_dv, axis=1),
            jnp.stack(new_nk, axis=1), jnp.stack(new_nv, axis=1))
```

```python
import functools
import math

import numpy as np
import jax
import jax.numpy as jnp
from jax import lax
from jax.experimental import pallas as pl
from jax.experimental.pallas import tpu as pltpu

F32 = jnp.float32
BF16 = jnp.bfloat16

D = 1024
DEPTH = 2
NB_P, N_P = 16, 256
NB_S, N_S = 4, 1024
T_P = NB_P * N_P
T_S = NB_S * N_S
T = T_P + T_S
PAST = 512
GRID_W = 64
DA_H, DA_QK, DA_V = 4, 64, 128
DA_W = DA_H * DA_V
NA_H, NA_D = 4, 64
NA_W = NA_H * NA_D
NA_ROWS, NA_COLS = 8, 16
POOL_WINDOWS = (2, 4, 8, 16)
POOL_CH = 64
POOL_W = 256
IN_COLS = 3 * DA_W + 3 * NA_W + POOL_W
AB_W = DA_W + NA_W
N_EXP, TOP_K, N_GRP, TOPK_GRP = 64, 8, 8, 4
EXP_DIM = 256
ROUTED_SCALE = 2.5
EPS = 1e-6
NEG = -1e30
ROPE_THETA = 10000.0

VMEM_LIMIT = 50 * 1024 * 1024

TM_IN = 512
TQ = 256

B_SORT = 256
UNIT = 16
N_BLK = T // B_SORT
RUN_PAD_ROWS = TOP_K * B_SORT + N_EXP * (UNIT - 1)
RB = -(-RUN_PAD_ROWS // 256) * 256
UPB = RB // UNIT
TMG = 256
UPT = TMG // UNIT
NT_MAX = (N_BLK * RUN_PAD_ROWS) // TMG + N_EXP
XS_W = D + 128
EXP_PER_BLK = N_EXP // N_BLK
ZPB = EXP_PER_BLK * (UPT - 1)
ZERO_UNIT = NT_MAX * UPT
SPARE_UNIT0 = ZERO_UNIT + UPT
R_OUT = (NT_MAX + 1) * TMG
R_SORT = R_OUT + -(-(UPB + ZPB) // UPT) * TMG
UNROLL = 8


def _cparams(sem):
    return pltpu.CompilerParams(dimension_semantics=sem, vmem_limit_bytes=VMEM_LIMIT)


def _rms(x, g):
    return x * lax.rsqrt(jnp.mean(x * x, axis=-1, keepdims=True) + EPS) * g


def _dot(a, b):
    return jnp.dot(a, b, preferred_element_type=F32)


def _dot_nt(a, b):
    return lax.dot_general(a, b, (((1,), (1,)), ((), ())), preferred_element_type=F32)


def _silu(x):
    return x / (1.0 + jnp.exp(-x))


def _mod_row(i, tm):
    off = i * tm - T_P
    return jnp.where(off >= 0, 1 + jnp.maximum(off, 0) // N_S, 0)


def _mod_body(c_ref, w_ref, b_ref, o_ref):
    c = c_ref[...]
    o_ref[...] = jnp.dot(_silu(c), w_ref[...], precision=lax.Precision.HIGHEST,
                         preferred_element_type=F32) + b_ref[...]


def _modulation(cvec, w_ada, b_ada):
    tn = 1536
    return pl.pallas_call(
        _mod_body,
        grid=(DEPTH, 6 * D // tn),
        in_specs=[pl.BlockSpec((8, D), lambda l, j: (0, 0)),
                  pl.BlockSpec((None, D, tn), lambda l, j: (l, 0, j)),
                  pl.BlockSpec((None, 1, tn), lambda l, j: (l, 0, j))],
        out_specs=pl.BlockSpec((None, 8, tn), lambda l, j: (l, 0, j)),
        out_shape=jax.ShapeDtypeStruct((DEPTH, 8, 6 * D), F32),
        compiler_params=_cparams(("parallel", "parallel")),
        name="modulation",
    )(cvec, w_ada, b_ada.reshape(DEPTH, 1, 6 * D))


def _rope_tables():
    nf = DA_QK // 4
    t = np.arange(N_S)
    pos = np.stack([t // GRID_W, t % GRID_W], axis=-1).astype(np.float32)
    inv = np.power(np.float32(ROPE_THETA), -np.arange(nf, dtype=np.float32) / nf)
    ang = pos[:, :, None] * inv
    cos = np.cos(ang)
    sin = np.sin(ang)
    cos64 = np.concatenate([cos[:, 0], cos[:, 0], cos[:, 1], cos[:, 1]], axis=-1)
    sin64 = np.concatenate([-sin[:, 0], sin[:, 0], -sin[:, 1], sin[:, 1]], axis=-1)
    reps = DA_W // DA_QK
    cos_t = np.concatenate([np.tile(cos64, (1, reps)), np.ones((TM_IN, DA_W), np.float32)], axis=0)
    sin_t = np.concatenate([np.tile(sin64, (1, reps)), np.zeros((TM_IN, DA_W), np.float32)], axis=0)
    return jnp.asarray(cos_t, F32), jnp.asarray(sin_t, F32)


def _in_proj_body(x_ref, mod_ref, g_ref, w_ref, cos_ref, sin_ref,
                  qa_ref, ka_ref, va_ref, qb_ref, kb_ref, vb_ref, up_ref):
    x = x_ref[...]
    h = (_rms(x, g_ref[...]) * (1.0 + mod_ref[1:2, :]) + mod_ref[0:1, :]).astype(BF16)
    cos = cos_ref[...]
    sin = sin_ref[...]
    lane = lax.broadcasted_iota(jnp.int32, (TM_IN, DA_W), 1)
    first = (lane % 32) < 16

    def proj(lo, hi):
        return _dot(h, w_ref[:, lo:hi])

    def rope(t):
        swapped = jnp.where(first, pltpu.roll(t, DA_W - 16, 1), pltpu.roll(t, 16, 1))
        return t * cos + swapped * sin

    o = 0
    qa_ref[...] = (rope(proj(o, o + DA_W)) * (DA_QK ** -0.5)).astype(BF16)
    o += DA_W
    ka_ref[...] = rope(proj(o, o + DA_W))
    o += DA_W
    va_ref[...] = proj(o, o + DA_W)
    o += DA_W
    qb_ref[...] = (proj(o, o + NA_W) * (NA_D ** -0.5)).astype(BF16)
    o += NA_W
    kb_ref[...] = proj(o, o + NA_W)
    o += NA_W
    vb_ref[...] = proj(o, o + NA_W)
    o += NA_W
    up_ref[...] = proj(o, o + POOL_W)


def _in_proj(li, x, modv, g_pre, w_in_bf, cos_t, sin_t):
    nt = T // TM_IN
    n_pos_blk = N_S // TM_IN

    def pos_blk(i):
        off = i * TM_IN - T_P
        return jnp.where(off >= 0, (jnp.maximum(off, 0) // TM_IN) % n_pos_blk, n_pos_blk)

    def tok(w):
        return pl.BlockSpec((TM_IN, w), lambda i: (i, 0))

    return pl.pallas_call(
        _in_proj_body,
        grid=(nt,),
        in_specs=[tok(D),
                  pl.BlockSpec((None, None, 6, D), lambda i: (li, _mod_row(i, TM_IN), 0, 0)),
                  pl.BlockSpec((None, 1, D), lambda i: (li, 0, 0)),
                  pl.BlockSpec((None, D, IN_COLS), lambda i: (li, 0, 0)),
                  pl.BlockSpec((TM_IN, DA_W), lambda i: (pos_blk(i), 0)),
                  pl.BlockSpec((TM_IN, DA_W), lambda i: (pos_blk(i), 0))],
        out_specs=[tok(DA_W), tok(DA_W), tok(DA_W), tok(NA_W), tok(NA_W), tok(NA_W), tok(POOL_W)],
        out_shape=[jax.ShapeDtypeStruct((T, DA_W), BF16),
                   jax.ShapeDtypeStruct((T, DA_W), F32),
                   jax.ShapeDtypeStruct((T, DA_W), F32),
                   jax.ShapeDtypeStruct((T, NA_W), BF16),
                   jax.ShapeDtypeStruct((T, NA_W), F32),
                   jax.ShapeDtypeStruct((T, NA_W), F32),
                   jax.ShapeDtypeStruct((T, POOL_W), F32)],
        compiler_params=_cparams(("parallel",)),
        name=f"in_proj_l{li}",
    )(x, modv, g_pre, w_in_bf, cos_t, sin_t)


def _pool_body(u_ref, w_ref, s_ref, o_ref, *, n):
    u = u_ref[...]
    row = lax.broadcasted_iota(jnp.int32, (n, POOL_W), 0)
    lane = lax.broadcasted_iota(jnp.int32, (n, POOL_W), 1)

    def shift_dn(a, k):
        return jnp.where(row >= k, pltpu.roll(a, k, 0), 0.0)

    def shift_up(a, k):
        return jnp.where(row < n - k, pltpu.roll(a, n - k, 0), 0.0)

    fwd = u
    bwd = shift_dn(u, 1)
    mean = jnp.zeros_like(u)
    k = 1
    for gi, w in enumerate(POOL_WINDOWS):
        while k < w // 2:
            fwd = fwd + shift_up(fwd, k)
            bwd = bwd + shift_dn(bwd, k)
            k *= 2
        cnt = (jnp.minimum(row + w // 2, n) - jnp.maximum(row - w // 2, 0)).astype(F32)
        in_group = (lane >= gi * POOL_CH) & (lane < (gi + 1) * POOL_CH)
        mean = jnp.where(in_group, (fwd + bwd) / cnt, mean)
    pooled = mean - u
    mixed = jnp.dot(pooled, w_ref[...], precision=lax.Precision.HIGHEST, preferred_element_type=F32)
    o_ref[...] = (mixed * s_ref[...]).astype(BF16)


def _pool_call(li, up, w_bd, scale, n, nseg, blk0, name):
    return pl.pallas_call(
        functools.partial(_pool_body, n=n),
        grid=(nseg,),
        in_specs=[pl.BlockSpec((n, POOL_W), lambda i: (blk0 + i, 0)),
                  pl.BlockSpec((None, POOL_W, POOL_W), lambda i: (li, 0, 0)),
                  pl.BlockSpec((None, 1, POOL_W), lambda i: (li, 0, 0))],
        out_specs=pl.BlockSpec((n, POOL_W), lambda i: (i, 0)),
        out_shape=jax.ShapeDtypeStruct((nseg * n, POOL_W), BF16),
        compiler_params=_cparams(("parallel",)),
        name=name,
    )(up, w_bd, scale)


def _lambda(lp, lam_init):
    a = jnp.sum(lp[0:1, :] * lp[1:2, :], axis=1, keepdims=True)
    b = jnp.sum(lp[2:3, :] * lp[3:4, :], axis=1, keepdims=True)
    return jnp.exp(a) - jnp.exp(b) + lam_init


def _softmax_parts(parts):
    m = functools.reduce(jnp.maximum, [jnp.max(s, axis=-1, keepdims=True) for s in parts])
    es = [jnp.exp(s - m) for s in parts]
    inv = 1.0 / functools.reduce(jnp.add, [jnp.sum(e, axis=-1, keepdims=True) for e in es])
    return [e * inv for e in es]


def _diff_head(q, ks, vs, lam, gd, lam_init):
    p1 = _softmax_parts([_dot_nt(q[:, :DA_QK], k[:, :DA_QK]) for k in ks])
    p2 = _softmax_parts([_dot_nt(q[:, DA_QK:], k[:, DA_QK:]) for k in ks])
    o = None
    for a, b, v in zip(p1, p2, vs):
        t = _dot((a - lam * b).astype(BF16), v)
        o = t if o is None else o + t
    return _rms(o, gd) * (1.0 - lam_init)


def _plain_head(q, ks, vs, biases):
    scores = []
    for k, bias in zip(ks, biases):
        s = _dot_nt(q, k)
        scores.append(s if bias is None else s + bias)
    ps = _softmax_parts(scores)
    o = None
    for p, v in zip(ps, vs):
        t = _dot(p.astype(BF16), v)
        o = t if o is None else o + t
    return o


def _attn_ctx_body(lam_ref, gd_ref, qa_ref, ka_ref, va_ref, qb_ref, kb_ref, vb_ref, o_ref, *, lam_init):
    lam = _lambda(lam_ref[...], lam_init)
    gd = gd_ref[...]
    for h in range(DA_H):
        sl = slice(h * DA_V, (h + 1) * DA_V)
        o = _diff_head(qa_ref[:, sl], [ka_ref[:, sl].astype(BF16)], [va_ref[:, sl].astype(BF16)],
                       lam, gd, lam_init)
        o_ref[:, sl] = o.astype(BF16)
    for h in range(NA_H):
        sl = slice(h * NA_D, (h + 1) * NA_D)
        o = _plain_head(qb_ref[:, sl], [kb_ref[:, sl].astype(BF16)], [vb_ref[:, sl].astype(BF16)], [None])
        o_ref[:, DA_W + h * NA_D:DA_W + (h + 1) * NA_D] = o.astype(BF16)


def _attn_ctx(li, lam_p, g_diff, qa, ka, va, qb, kb, vb, lam_init):
    def blk(w):
        return pl.BlockSpec((N_P, w), lambda b: (b, 0))

    return pl.pallas_call(
        functools.partial(_attn_ctx_body, lam_init=lam_init),
        grid=(NB_P,),
        in_specs=[pl.BlockSpec((None, 4, DA_QK), lambda b: (li, 0, 0)),
                  pl.BlockSpec((None, 1, DA_V), lambda b: (li, 0, 0)),
                  blk(DA_W), blk(DA_W), blk(DA_W), blk(NA_W), blk(NA_W), blk(NA_W)],
        out_specs=pl.BlockSpec((N_P, AB_W), lambda b: (b, 0)),
        out_shape=jax.ShapeDtypeStruct((T_P, AB_W), BF16),
        compiler_params=_cparams(("parallel",)),
        name=f"attn_ctx_l{li}",
    )(lam_p, g_diff, qa, ka, va, qb, kb, vb)


def _attn_lat_body(lam_ref, gd_ref, qa_ref, ka_ref, va_ref, ck_ref, cv_ref,
                   qb_ref, kb_ref, vb_ref, cnk_ref, cnv_ref, bias_ref, o_ref, *, lam_init):
    lam = _lambda(lam_ref[...], lam_init)
    gd = gd_ref[...]
    for h in range(DA_H):
        sl = slice(h * DA_V, (h + 1) * DA_V)
        ks = [ka_ref[:, sl].astype(BF16), ck_ref[:, sl].astype(BF16)]
        vs = [va_ref[:, sl].astype(BF16), cv_ref[:, sl].astype(BF16)]
        o_ref[:, sl] = _diff_head(qa_ref[:, sl], ks, vs, lam, gd, lam_init).astype(BF16)
    for h in range(NA_H):
        sl = slice(h * NA_D, (h + 1) * NA_D)
        ks = [kb_ref[:, sl].astype(BF16), cnk_ref[:, sl].astype(BF16)]
        vs = [vb_ref[:, sl].astype(BF16), cnv_ref[:, sl].astype(BF16)]
        o = _plain_head(qb_ref[:, sl], ks, vs, [bias_ref[h], None])
        o_ref[:, DA_W + h * NA_D:DA_W + (h + 1) * NA_D] = o.astype(BF16)


def _attn_lat(li, lam_p, g_diff, qa, ka, va, ck, cv, qb, kb, vb, cnk, cnv, bias, lam_init):
    nq = N_S // TQ
    q0 = T_P // TQ
    b0 = T_P // N_S

    def qblk(w):
        return pl.BlockSpec((TQ, w), lambda b, j: (q0 + b * nq + j, 0))

    def kvblk(w):
        return pl.BlockSpec((N_S, w), lambda b, j: (b0 + b, 0))

    def cblk(w):
        return pl.BlockSpec((None, None, PAST, w), lambda b, j: (b, li, 0, 0))

    return pl.pallas_call(
        functools.partial(_attn_lat_body, lam_init=lam_init),
        grid=(NB_S, nq),
        in_specs=[pl.BlockSpec((None, 4, DA_QK), lambda b, j: (li, 0, 0)),
                  pl.BlockSpec((None, 1, DA_V), lambda b, j: (li, 0, 0)),
                  qblk(DA_W), kvblk(DA_W), kvblk(DA_W), cblk(DA_W), cblk(DA_W),
                  qblk(NA_W), kvblk(NA_W), kvblk(NA_W), cblk(NA_W), cblk(NA_W),
                  pl.BlockSpec((NA_H, TQ, N_S), lambda b, j: (0, j, 0))],
        out_specs=pl.BlockSpec((TQ, AB_W), lambda b, j: (b * nq + j, 0)),
        out_shape=jax.ShapeDtypeStruct((T_S, AB_W), BF16),
        compiler_params=_cparams(("parallel", "parallel")),
        name=f"attn_lat_l{li}",
    )(lam_p, g_diff, qa, ka, va, ck, cv, qb, kb, vb, cnk, cnv, bias)


def _na_bias_mask(table):
    rows = N_S // GRID_W
    t = np.arange(N_S)
    r, c = t // GRID_W, t % GRID_W
    rs = np.clip(r - NA_ROWS // 2, 0, rows - NA_ROWS)
    cs = np.clip(c - NA_COLS // 2, 0, GRID_W - NA_COLS)
    kr, kc = r[None, :], c[None, :]
    valid = ((kr >= rs[:, None]) & (kr < rs[:, None] + NA_ROWS)
             & (kc >= cs[:, None]) & (kc < cs[:, None] + NA_COLS))
    g = np.arange(rows)
    sel_r = (g[None, :, None] - g[:, None, None] + NA_ROWS - 1 == np.arange(2 * NA_ROWS - 1)).astype(np.float32)
    w = np.arange(GRID_W)
    sel_c = (np.clip(w[None, :, None] - w[:, None, None] + NA_COLS - 1, 0, 2 * NA_COLS - 2)
             == np.arange(2 * NA_COLS - 1)).astype(np.float32)
    hp = lax.Precision.HIGHEST
    t1 = jnp.einsum('hab,cdb->hacd', table.astype(F32), jnp.asarray(sel_c), precision=hp)
    bias = jnp.einsum('rsa,hacd->hrcsd', jnp.asarray(sel_r), t1, precision=hp).reshape(NA_H, N_S, N_S)
    return jnp.where(jnp.asarray(valid)[None], bias, NEG)


def _route_t(logits_t, b_col, tm):
    per = N_EXP // N_GRP
    s = 1.0 / (1.0 + jnp.exp(-logits_t))
    sel3 = (s + b_col).reshape(N_GRP, per, tm)
    midx = lax.broadcasted_iota(jnp.int32, (N_GRP, per, tm), 1)
    gidx = lax.broadcasted_iota(jnp.int32, (N_GRP, per, tm), 0)
    ninf = -jnp.inf
    m1 = jnp.max(sel3, axis=1, keepdims=True)
    i1 = jnp.min(jnp.where(sel3 == m1, midx, per), axis=1, keepdims=True)
    m2 = jnp.max(jnp.where(midx == i1, ninf, sel3), axis=1, keepdims=True)
    gsc = jnp.broadcast_to(m1 + m2, (N_GRP, per, tm))
    cnt = jnp.zeros((N_GRP, per, tm), jnp.int32)
    for g in range(N_GRP):
        sg = gsc[g][None]
        cnt = cnt + ((sg > gsc) | ((sg == gsc) & (g < gidx))).astype(jnp.int32)
    x = jnp.where(cnt < TOPK_GRP, sel3, ninf).reshape(N_EXP, tm)
    eidx = lax.broadcasted_iota(jnp.int32, (N_EXP, tm), 0)
    cnt = jnp.zeros((N_EXP, tm), jnp.int32)
    for e in range(N_EXP):
        row = x[e:e + 1, :]
        cnt = cnt + ((row > x) | ((row == x) & (e < eidx))).astype(jnp.int32)
    chosen = cnt < TOP_K
    w = jnp.where(chosen, s, 0.0)
    return chosen, w / jnp.sum(w, axis=0, keepdims=True) * ROUTED_SCALE


def _merge_body(oab_ref, oc_ref, x_ref, mod_ref, gpost_ref, gpre_ref, wout_ref, wrt_ref, br_ref, tri_ref,
                y_ref, h_ref, gates_ref, rank_ref, pcu_ref):
    mix = _dot(oab_ref[...], wout_ref[0:AB_W, :]) + _dot(oc_ref[...], wout_ref[AB_W:D, :])
    y = x_ref[...] + mod_ref[2:3, :] * _rms(mix, gpost_ref[...])
    y_ref[...] = y
    h = _rms(y, gpre_ref[...]) * (1.0 + mod_ref[4:5, :]) + mod_ref[3:4, :]
    h_ref[...] = h.astype(BF16)
    logits_t = lax.dot_general(wrt_ref[...], h, (((1,), (1,)), ((), ())),
                               precision=lax.Precision.HIGHEST, preferred_element_type=F32)
    chosen, gates = _route_t(logits_t, br_ref[...], TM_IN)
    gates_ref[...] = gates
    ch = jnp.where(chosen, 1.0, 0.0)
    rank = _dot(ch.astype(BF16), tri_ref[...])
    rank_ref[...] = jnp.where(chosen, rank, -1.0)
    for j in range(TM_IN // B_SORT):
        cnt = jnp.sum(ch[:, j * B_SORT:(j + 1) * B_SORT], axis=1, keepdims=True)
        units = jnp.floor((cnt + (UNIT - 1)) * (1.0 / UNIT))
        pcu_ref[j] = jnp.broadcast_to(units, (N_EXP, 128))


def _merge(li, oab, oc, x, modv, g_post, g_pre, w_out_bf, w_router_t, b_router_col, tri):
    def tok(w):
        return pl.BlockSpec((TM_IN, w), lambda i: (i, 0))

    def par(*shape):
        return pl.BlockSpec((None,) + shape, lambda i: (li,) + (0,) * len(shape))

    tok_t = pl.BlockSpec((N_EXP, TM_IN), lambda i: (0, i))
    nb = TM_IN // B_SORT
    return pl.pallas_call(
        _merge_body,
        grid=(T // TM_IN,),
        in_specs=[tok(AB_W), tok(POOL_W), tok(D),
                  pl.BlockSpec((None, None, 6, D), lambda i: (li, _mod_row(i, TM_IN), 0, 0)),
                  par(1, D), par(1, D), par(D, D), par(N_EXP, D), par(N_EXP, 1),
                  pl.BlockSpec((TM_IN, TM_IN), lambda i: (0, 0))],
        out_specs=[tok(D), tok(D), tok_t, tok_t,
                   pl.BlockSpec((nb, N_EXP, 128), lambda i: (i, 0, 0))],
        out_shape=[jax.ShapeDtypeStruct((T, D), F32),
                   jax.ShapeDtypeStruct((T, D), BF16),
                   jax.ShapeDtypeStruct((N_EXP, T), F32),
                   jax.ShapeDtypeStruct((N_EXP, T), F32),
                   jax.ShapeDtypeStruct((N_BLK, N_EXP, 128), F32)],
        compiler_params=_cparams(("parallel",)),
        name=f"merge_l{li}",
    )(oab, oc, x, modv, g_post, g_pre, w_out_bf, w_router_t, b_router_col, tri)


def _rank_matrix():
    t = np.arange(TM_IN)
    m = (t[:, None] < t[None, :]) & (t[:, None] // B_SORT == t[None, :] // B_SORT)
    return jnp.asarray(m, BF16)


def _moe_tables(pcu_f):
    pcu = pcu_f[:, :, 0].astype(jnp.int32)
    incl = jnp.cumsum(pcu, axis=1)
    uo = incl - pcu
    tot = jnp.sum(pcu, axis=0)
    gt = (tot + UPT - 1) // UPT
    gincl = jnp.cumsum(gt)
    gstart_u = (gincl - gt) * UPT
    bstart = gstart_u[None, :] + jnp.cumsum(pcu, axis=0) - pcu
    u = jnp.arange(UPB, dtype=jnp.int32)
    eou = jnp.sum((u[None, :, None] >= incl[:, None, :]).astype(jnp.int32), axis=-1)
    valid = eou < N_EXP
    onehot = eou[:, :, None] == jnp.arange(N_EXP, dtype=jnp.int32)[None, None, :]
    run_u0 = jnp.sum(jnp.where(onehot, uo[:, None, :], 0), axis=-1)
    run_g0 = jnp.sum(jnp.where(onehot, bstart[:, None, :], 0), axis=-1)
    dst = run_g0 + u[None, :] - run_u0
    dst_sort = jnp.where(valid, dst, SPARE_UNIT0 + u[None, :])
    dst_comb = jnp.where(valid, dst, ZERO_UNIT)
    loc = jnp.where(valid, (u[None, :] - run_u0) * UNIT, -(1 << 20))
    k = jnp.arange(UPT - 1, dtype=jnp.int32)
    ztail = gt * UPT - tot
    zdst = jnp.where(k[None, :] < ztail[:, None], (gstart_u + tot)[:, None] + k[None, :], -1).reshape(N_BLK, ZPB)
    zdst = jnp.where(zdst >= 0, zdst, SPARE_UNIT0 + UPB + jnp.arange(ZPB, dtype=jnp.int32)[None, :])
    tile = jnp.arange(NT_MAX, dtype=jnp.int32)
    te = jnp.minimum(jnp.sum((tile[:, None] >= gincl[None, :]).astype(jnp.int32), axis=-1), N_EXP - 1)
    return (jnp.minimum(eou, N_EXP - 1), loc, dst_sort, dst_comb, zdst, te, gincl[-1:])


def _unit_rows(u):
    if isinstance(u, int):
        return pl.ds(u * UNIT, UNIT)
    return pl.ds(pl.multiple_of(u * UNIT, UNIT), UNIT)


def _sort_body(eou_ref, loc_ref, dst_ref, zdst_ref, h_ref, rank_ref, gates_ref, xs_hbm, p_ref,
               xs_scr, zero_scr, sem):
    b = pl.program_id(0)
    zero_scr[...] = jnp.zeros((UNIT, XS_W), BF16)
    base = lax.broadcasted_iota(jnp.int32, (UNIT, B_SORT), 0)

    def onehot_unit(u, carry):
        match = rank_ref[pl.ds(eou_ref[b, u], 1), :] == (base + loc_ref[b, u]).astype(F32)
        p_ref[_unit_rows(u), :] = jnp.where(match, 1.0, 0.0).astype(BF16)
        return carry

    lax.fori_loop(0, UPB, onehot_unit, 0, unroll=UNROLL)
    g = gates_ref[...]
    g_hi = g.astype(BF16)
    g_hl = jnp.concatenate([g_hi, (g - g_hi.astype(F32)).astype(BF16)], axis=0)
    h = h_ref[...]
    for c in range(RB // 512):
        rows = slice(c * 512, (c + 1) * 512)
        xs_scr[rows, 0:D] = _dot(p_ref[rows, :], h).astype(BF16)
        xs_scr[rows, D:XS_W] = _dot_nt(p_ref[rows, :], g_hl).astype(BF16)

    def copy(src, d):
        return pltpu.make_async_copy(src, xs_hbm.at[_unit_rows(d)], sem)

    def start_run(u, carry):
        copy(xs_scr.at[_unit_rows(u)], dst_ref[b, u]).start()
        return carry

    def start_zero(j, carry):
        copy(zero_scr, zdst_ref[b, j]).start()
        return carry

    def wait_one(k, carry):
        copy(zero_scr, 0).wait()
        return carry

    lax.fori_loop(0, UPB, start_run, 0, unroll=UNROLL)
    lax.fori_loop(0, ZPB, start_zero, 0, unroll=True)
    lax.fori_loop(0, UPB + ZPB, wait_one, 0, unroll=True)


def _moe_sort(li, eou, loc, dst, zdst, h, rank_t, gates_t):
    blk_t = pl.BlockSpec((N_EXP, B_SORT), lambda b, *_: (0, b))
    return pl.pallas_call(
        _sort_body,
        grid_spec=pltpu.PrefetchScalarGridSpec(
            num_scalar_prefetch=4,
            grid=(N_BLK,),
            in_specs=[pl.BlockSpec((B_SORT, D), lambda b, *_: (b, 0)), blk_t, blk_t],
            out_specs=[pl.BlockSpec(memory_space=pl.ANY),
                       pl.BlockSpec((RB, B_SORT), lambda b, *_: (b, 0))],
            scratch_shapes=[pltpu.VMEM((RB, XS_W), BF16),
                            pltpu.VMEM((UNIT, XS_W), BF16),
                            pltpu.SemaphoreType.DMA]),
        out_shape=[jax.ShapeDtypeStruct((R_SORT, XS_W), BF16),
                   jax.ShapeDtypeStruct((N_BLK * RB, B_SORT), BF16)],
        compiler_params=_cparams(("arbitrary",)),
        name=f"moe_sort_l{li}",
    )(eou, loc, dst, zdst, h, rank_t, gates_t)


def _experts_body(te_ref, nt_ref, xs_ref, wg_ref, wu_ref, wd_ref, o_ref, wg_s, wu_s, wd_s):
    i = pl.program_id(0)
    valid = i < nt_ref[0]
    e = te_ref[jnp.minimum(i, NT_MAX - 1)]
    new_expert = (i == 0) | (e != te_ref[jnp.maximum(i - 1, 0)])

    @pl.when(valid & new_expert)
    def _():
        wg_s[...] = wg_ref[...].astype(BF16)
        wu_s[...] = wu_ref[...].astype(BF16)
        wd_s[...] = wd_ref[...].astype(BF16)

    @pl.when(valid)
    def _():
        x = xs_ref[:, 0:D]
        lane = lax.broadcasted_iota(jnp.int32, (TMG, 128), 1)
        mine = (lane == e) | (lane == e + N_EXP)
        gate = jnp.sum(jnp.where(mine, xs_ref[:, D:XS_W].astype(F32), 0.0), axis=1, keepdims=True)
        hid = _silu(_dot(x, wg_s[...])) * _dot(x, wu_s[...]) * gate
        o_ref[...] = _dot(hid.astype(BF16), wd_s[...]).astype(BF16)

    @pl.when(i == NT_MAX)
    def _():
        o_ref[...] = jnp.zeros((TMG, D), BF16)


def _moe_experts(li, te, nt, xs, w_gate, w_up, w_down):
    def last(i, nt):
        return jnp.minimum(i, nt[0] - 1)

    def out_tile(i, nt):
        return jnp.where(i == NT_MAX, NT_MAX, last(i, nt))

    def wspec(*shape):
        return pl.BlockSpec((None, None) + shape,
                            lambda i, te, nt: (li, te[last(i, nt)]) + (0,) * len(shape))

    return pl.pallas_call(
        _experts_body,
        grid_spec=pltpu.PrefetchScalarGridSpec(
            num_scalar_prefetch=2,
            grid=(NT_MAX + 1,),
            in_specs=[pl.BlockSpec((TMG, XS_W), lambda i, te, nt: (last(i, nt), 0)),
                      wspec(D, EXP_DIM), wspec(D, EXP_DIM), wspec(EXP_DIM, D)],
            out_specs=pl.BlockSpec((TMG, D), lambda i, te, nt: (out_tile(i, nt), 0)),
            scratch_shapes=[pltpu.VMEM((D, EXP_DIM), BF16),
                            pltpu.VMEM((D, EXP_DIM), BF16),
                            pltpu.VMEM((EXP_DIM, D), BF16)]),
        out_shape=jax.ShapeDtypeStruct((R_OUT, D), BF16),
        compiler_params=_cparams(("arbitrary",)),
        name=f"moe_experts_l{li}",
    )(te, nt, xs, w_gate, w_up, w_down)


def _combine_body(dst_ref, h_ref, p_ref, y_ref, mod_ref, gpost_ref,
                  wsg_ref, wsu_ref, wsd_ref, os_hbm, o_ref, os_scr, sem):
    b = pl.program_id(0)

    def copy(d, u):
        return pltpu.make_async_copy(os_hbm.at[_unit_rows(d)], os_scr.at[_unit_rows(u)], sem)

    def start_run(u, carry):
        copy(dst_ref[b, u], u).start()
        return carry

    def wait_one(k, carry):
        copy(0, 0).wait()
        return carry

    lax.fori_loop(0, UPB, start_run, 0, unroll=UNROLL)
    h = h_ref[...]
    shared = _dot((_silu(_dot(h, wsg_ref[...])) * _dot(h, wsu_ref[...])).astype(BF16), wsd_ref[...])
    lax.fori_loop(0, UPB, wait_one, 0, unroll=True)
    routed = lax.dot_general(p_ref[...], os_scr[...], (((0,), (0,)), ((), ())),
                             preferred_element_type=F32)
    o_ref[...] = y_ref[...] + mod_ref[5:6, :] * _rms(routed + shared, gpost_ref[...])


def _moe_combine(li, dst, h, p, y, modv, g_post, wsg_bf, wsu_bf, wsd_bf, os):
    def tok(w):
        return pl.BlockSpec((B_SORT, w), lambda b, *_: (b, 0))

    def par(*shape):
        return pl.BlockSpec((None,) + shape, lambda b, *_: (li,) + (0,) * len(shape))

    return pl.pallas_call(
        _combine_body,
        grid_spec=pltpu.PrefetchScalarGridSpec(
            num_scalar_prefetch=1,
            grid=(N_BLK,),
            in_specs=[tok(D), pl.BlockSpec((RB, B_SORT), lambda b, *_: (b, 0)), tok(D),
                      pl.BlockSpec((None, None, 6, D), lambda b, *_: (li, _mod_row(b, B_SORT), 0, 0)),
                      par(1, D), par(D, EXP_DIM), par(D, EXP_DIM), par(EXP_DIM, D),
                      pl.BlockSpec(memory_space=pl.ANY)],
            out_specs=tok(D),
            scratch_shapes=[pltpu.VMEM((RB, D), BF16),
                            pltpu.SemaphoreType.DMA]),
        out_shape=jax.ShapeDtypeStruct((T, D), F32),
        compiler_params=_cparams(("arbitrary",)),
        name=f"moe_combine_l{li}",
    )(dst, h, p, y, modv, g_post, wsg_bf, wsu_bf, wsd_bf, os)


def _block_diag(w):
    out = jnp.zeros((DEPTH, POOL_W, POOL_W), F32)
    for g in range(len(POOL_WINDOWS)):
        out = out.at[:, g * POOL_CH:(g + 1) * POOL_CH, g * POOL_CH:(g + 1) * POOL_CH].set(w[:, g])
    return out


def kernel(x_prompt, x_sample, cache_diff_k, cache_diff_v, cache_na_k, cache_na_v, c, c_ctx,
           w_ada, b_ada, g_pre_mix, g_post_mix, g_pre_ffn, g_post_ffn, w_in, w_out,
           diff_lambda, g_diff, na_bias, pool_w, pool_scale, w_router, b_router,
           w_gate, w_up, w_down, ws_gate, ws_up, ws_down):
    x = jnp.concatenate([x_prompt.reshape(T_P, D), x_sample.reshape(T_S, D)], axis=0)
    cvec = jnp.concatenate([c_ctx[None, :], c, jnp.zeros((3, D), F32)], axis=0)
    modv = _modulation(cvec, w_ada, b_ada)[:, :1 + NB_S].reshape(DEPTH, 1 + NB_S, 6, D)

    cos_t, sin_t = _rope_tables()
    w_in_bf = w_in.astype(BF16)
    w_out_bf = w_out.astype(BF16)
    wsg_bf, wsu_bf, wsd_bf = ws_gate.astype(BF16), ws_up.astype(BF16), ws_down.astype(BF16)
    pool_bd = _block_diag(pool_w)
    row = lambda a: a.reshape(DEPTH, 1, a.shape[-1])
    g_pre_mix, g_post_mix, g_pre_ffn, g_post_ffn = map(row, (g_pre_mix, g_post_mix, g_pre_ffn, g_post_ffn))
    g_diff, pool_scale = row(g_diff), row(pool_scale)
    w_router_t = jnp.swapaxes(w_router, 1, 2)
    b_router_col = b_router.reshape(DEPTH, N_EXP, 1)
    tri = _rank_matrix()
    ck = cache_diff_k.reshape(NB_S, DEPTH, PAST, DA_W)
    cv = cache_diff_v.reshape(NB_S, DEPTH, PAST, DA_W)
    cnk = cache_na_k.reshape(NB_S, DEPTH, PAST, NA_W)
    cnv = cache_na_v.reshape(NB_S, DEPTH, PAST, NA_W)

    new_dk, new_dv, new_nk, new_nv = [], [], [], []
    for li in range(DEPTH):
        lam_init = 0.8 - 0.6 * math.exp(-0.3 * li)
        qa, ka, va, qb, kb, vb, up = _in_proj(li, x, modv, g_pre_mix, w_in_bf, cos_t, sin_t)
        oc = jnp.concatenate([
            _pool_call(li, up, pool_bd, pool_scale, N_P, NB_P, 0, f"pool_ctx_l{li}"),
            _pool_call(li, up, pool_bd, pool_scale, N_S, NB_S, T_P // N_S, f"pool_lat_l{li}")], axis=0)
        bias = _na_bias_mask(na_bias[li])
        oab = jnp.concatenate([
            _attn_ctx(li, diff_lambda, g_diff, qa, ka, va, qb, kb, vb, lam_init),
            _attn_lat(li, diff_lambda, g_diff, qa, ka, va, ck, cv, qb, kb, vb, cnk, cnv, bias, lam_init)],
            axis=0)
        y, h, gates_t, rank_t, pcu = _merge(li, oab, oc, x, modv, g_post_mix, g_pre_ffn, w_out_bf,
                                            w_router_t, b_router_col, tri)
        eou, loc, dst_sort, dst_comb, zdst, te, nt = _moe_tables(pcu)
        xs, p = _moe_sort(li, eou, loc, dst_sort, zdst, h, rank_t, gates_t)
        os = _moe_experts(li, te, nt, xs, w_gate, w_up, w_down)
        x = _moe_combine(li, dst_comb, h, p, y, modv, g_post_ffn, wsg_bf, wsu_bf, wsd_bf, os)
        new_dk.append(ka[:T_P].reshape(NB_P, N_P, DA_H, 2 * DA_QK))
        new_dv.append(va[:T_P].reshape(NB_P, N_P, DA_H, DA_V))
        new_nk.append(kb[:T_P].reshape(NB_P, N_P, NA_H, NA_D))
        new_nv.append(vb[:T_P].reshape(NB_P, N_P, NA_H, NA_D))

    yp = x[:T_P].reshape(NB_P, N_P, D)
    ys = x[T_P:].reshape(NB_S, N_S, D)
    return (yp, ys, jnp.stack(new_dk, axis=1), jnp.stack(new_dv, axis=1),
            jnp.stack(new_nk, axis=1), jnp.stack(new_nv, axis=1))
```

```python
import functools
import math

import numpy as np
import jax
import jax.numpy as jnp
from jax import lax
from jax.experimental import pallas as pl
from jax.experimental.pallas import tpu as pltpu

F32 = jnp.float32
BF16 = jnp.bfloat16

D = 1024
DEPTH = 2
NB_P, N_P = 16, 256
NB_S, N_S = 4, 1024
T_P = NB_P * N_P
T_S = NB_S * N_S
T = T_P + T_S
PAST = 512
GRID_W = 64
DA_H, DA_QK, DA_V = 4, 64, 128
DA_W = DA_H * DA_V
NA_H, NA_D = 4, 64
NA_W = NA_H * NA_D
NA_ROWS, NA_COLS = 8, 16
POOL_WINDOWS = (2, 4, 8, 16)
POOL_CH = 64
POOL_W = 256
IN_COLS = 3 * DA_W + 3 * NA_W + POOL_W
AB_W = DA_W + NA_W
N_EXP, TOP_K, N_GRP, TOPK_GRP = 64, 8, 8, 4
EXP_DIM = 256
ROUTED_SCALE = 2.5
EPS = 1e-6
NEG = -1e30
ROPE_THETA = 10000.0

VMEM_LIMIT = 50 * 1024 * 1024

TM_IN = 512
NT_P = T_P // TM_IN
TQ = 256

B_SORT = 256
UNIT = 16
N_BLK = T // B_SORT
RUN_PAD_ROWS = TOP_K * B_SORT + N_EXP * (UNIT - 1)
RB = -(-RUN_PAD_ROWS // 256) * 256
UPB = RB // UNIT
TMG = 256
UPT = TMG // UNIT
NT_MAX = (N_BLK * RUN_PAD_ROWS) // TMG + N_EXP
XS_W = D + 128
EXP_PER_BLK = N_EXP // N_BLK
ZPB = EXP_PER_BLK * (UPT - 1)
TPS = 4
STEP_ROWS = TPS * TMG
UPS = STEP_ROWS // UNIT
NS_MAX = NT_MAX // TPS + (N_EXP * (TPS - 1)) // TPS
ZERO_UNIT = NS_MAX * UPS
SPARE_UNIT0 = ZERO_UNIT + UPS
R_OUT = (NS_MAX + 1) * STEP_ROWS
R_SORT = R_OUT + -(-2 * (UPB + ZPB) // UPS) * STEP_ROWS
UNROLL = 8


def _cparams(sem):
    return pltpu.CompilerParams(dimension_semantics=sem, vmem_limit_bytes=VMEM_LIMIT)


def _rms(x, g):
    return x * lax.rsqrt(jnp.mean(x * x, axis=-1, keepdims=True) + EPS) * g


def _dot(a, b):
    return jnp.dot(a, b, preferred_element_type=F32)


def _dot_nt(a, b):
    return lax.dot_general(a, b, (((1,), (1,)), ((), ())), preferred_element_type=F32)


def _silu(x):
    return x / (1.0 + jnp.exp(-x))


def _mod_row(i, tm):
    off = i * tm - T_P
    return jnp.where(off >= 0, 1 + jnp.maximum(off, 0) // N_S, 0)


def _mod_body(c_ref, w_ref, b_ref, o_ref):
    c = c_ref[...]
    o_ref[...] = jnp.dot(_silu(c), w_ref[...], precision=lax.Precision.HIGHEST,
                         preferred_element_type=F32) + b_ref[...]


def _modulation(cvec, w_ada, b_ada):
    tn = 1536
    return pl.pallas_call(
        _mod_body,
        grid=(DEPTH, 6 * D // tn),
        in_specs=[pl.BlockSpec((8, D), lambda l, j: (0, 0)),
                  pl.BlockSpec((None, D, tn), lambda l, j: (l, 0, j)),
                  pl.BlockSpec((None, 1, tn), lambda l, j: (l, 0, j))],
        out_specs=pl.BlockSpec((None, 8, tn), lambda l, j: (l, 0, j)),
        out_shape=jax.ShapeDtypeStruct((DEPTH, 8, 6 * D), F32),
        compiler_params=_cparams(("parallel", "parallel")),
        name="modulation",
    )(cvec, w_ada, b_ada.reshape(DEPTH, 1, 6 * D))


def _rope_tables():
    nf = DA_QK // 4
    t = np.arange(N_S)
    pos = np.stack([t // GRID_W, t % GRID_W], axis=-1).astype(np.float32)
    inv = np.power(np.float32(ROPE_THETA), -np.arange(nf, dtype=np.float32) / nf)
    ang = pos[:, :, None] * inv
    cos = np.cos(ang)
    sin = np.sin(ang)
    cos64 = np.concatenate([cos[:, 0], cos[:, 0], cos[:, 1], cos[:, 1]], axis=-1)
    sin64 = np.concatenate([-sin[:, 0], sin[:, 0], -sin[:, 1], sin[:, 1]], axis=-1)
    reps = DA_W // DA_QK
    cos_t = np.concatenate([np.tile(cos64, (1, reps)), np.ones((TM_IN, DA_W), np.float32)], axis=0)
    sin_t = np.concatenate([np.tile(sin64, (1, reps)), np.zeros((TM_IN, DA_W), np.float32)], axis=0)
    return jnp.asarray(cos_t, F32), jnp.asarray(sin_t, F32)


def _tok_select(i, ctx_ref, lat_ref):
    return jnp.where(i < NT_P, ctx_ref[...], lat_ref[...])


def _in_proj_body(*refs):
    (xp_ref, xs_ref, mod_ref, g_ref, w_ref, cos_ref, sin_ref) = refs[:7]
    (qa_ref, qb_ref, up_ref, ka_ref, va_ref, kb_ref, vb_ref, ck_ref, cv_ref, cnk_ref, cnv_ref) = refs[-11:]
    i = pl.program_id(0)
    x = _tok_select(i, xp_ref, xs_ref)
    h = (_rms(x, g_ref[...]) * (1.0 + mod_ref[1:2, :]) + mod_ref[0:1, :]).astype(BF16)
    cos = cos_ref[...]
    sin = sin_ref[...]
    lane = lax.broadcasted_iota(jnp.int32, (TM_IN, DA_W), 1)
    first = (lane % 32) < 16

    def proj(lo, hi):
        return _dot(h, w_ref[:, lo:hi])

    def rope(t):
        swapped = jnp.where(first, pltpu.roll(t, DA_W - 16, 1), pltpu.roll(t, 16, 1))
        return t * cos + swapped * sin

    o = 0
    qa_ref[...] = (rope(proj(o, o + DA_W)) * (DA_QK ** -0.5)).astype(BF16)
    o += DA_W
    ka = rope(proj(o, o + DA_W))
    o += DA_W
    va = proj(o, o + DA_W)
    o += DA_W
    qb_ref[...] = (proj(o, o + NA_W) * (NA_D ** -0.5)).astype(BF16)
    o += NA_W
    kb = proj(o, o + NA_W)
    o += NA_W
    vb = proj(o, o + NA_W)
    o += NA_W
    up_ref[...] = proj(o, o + POOL_W)

    @pl.when(i < NT_P)
    def _():
        for ref, val in ((ck_ref, ka), (cv_ref, va), (cnk_ref, kb), (cnv_ref, vb)):
            ref[...] = val.reshape(ref.shape)

    @pl.when(i >= NT_P)
    def _():
        for ref, val in ((ka_ref, ka), (va_ref, va), (kb_ref, kb), (vb_ref, vb)):
            ref[...] = val.astype(BF16)


def _ctx_tok(w):
    return pl.BlockSpec((TM_IN, w), lambda i: (jnp.minimum(i, NT_P - 1), 0))


def _lat_tok(w):
    return pl.BlockSpec((TM_IN, w), lambda i: (jnp.maximum(i - NT_P, 0), 0))


def _in_proj(li, xp, xs, modv, g_pre, w_in_bf, cos_t, sin_t, caches):
    n_pos_blk = N_S // TM_IN

    def pos_blk(i):
        return jnp.where(i >= NT_P, jnp.maximum(i - NT_P, 0) % n_pos_blk, n_pos_blk)

    def tok(w):
        return pl.BlockSpec((TM_IN, w), lambda i: (i, 0))

    def cache(w):
        return pl.BlockSpec((TM_IN // N_P, None, N_P, w), lambda i: (jnp.minimum(i, NT_P - 1), li, 0, 0))

    widths = (DA_W, DA_W, NA_W, NA_W)
    n_in = 7
    return pl.pallas_call(
        _in_proj_body,
        grid=(T // TM_IN,),
        in_specs=[_ctx_tok(D), _lat_tok(D),
                  pl.BlockSpec((None, None, 6, D), lambda i: (li, _mod_row(i, TM_IN), 0, 0)),
                  pl.BlockSpec((None, 1, D), lambda i: (li, 0, 0)),
                  pl.BlockSpec((None, D, IN_COLS), lambda i: (li, 0, 0)),
                  pl.BlockSpec((TM_IN, DA_W), lambda i: (pos_blk(i), 0)),
                  pl.BlockSpec((TM_IN, DA_W), lambda i: (pos_blk(i), 0))]
                 + [pl.BlockSpec(memory_space=pl.ANY)] * len(caches),
        out_specs=[tok(DA_W), tok(NA_W), tok(POOL_W)] + [_lat_tok(w) for w in widths]
                  + [cache(w) for w in widths],
        out_shape=[jax.ShapeDtypeStruct((T, DA_W), BF16),
                   jax.ShapeDtypeStruct((T, NA_W), BF16),
                   jax.ShapeDtypeStruct((T, POOL_W), F32)]
                  + [jax.ShapeDtypeStruct((T_S, w), BF16) for w in widths]
                  + [jax.ShapeDtypeStruct((NB_P, DEPTH, N_P, w), F32) for w in widths],
        input_output_aliases={n_in + k: 7 + k for k in range(len(caches))},
        compiler_params=_cparams(("arbitrary",)),
        name=f"in_proj_l{li}",
    )(xp, xs, modv, g_pre, w_in_bf, cos_t, sin_t, *caches)


def _pool_body(u_ref, w_ref, s_ref, o_ref, *, n):
    u = u_ref[...]
    row = lax.broadcasted_iota(jnp.int32, (n, POOL_W), 0)
    lane = lax.broadcasted_iota(jnp.int32, (n, POOL_W), 1)

    def shift_dn(a, k):
        return jnp.where(row >= k, pltpu.roll(a, k, 0), 0.0)

    def shift_up(a, k):
        return jnp.where(row < n - k, pltpu.roll(a, n - k, 0), 0.0)

    fwd = u
    bwd = shift_dn(u, 1)
    mean = jnp.zeros_like(u)
    k = 1
    for gi, w in enumerate(POOL_WINDOWS):
        while k < w // 2:
            fwd = fwd + shift_up(fwd, k)
            bwd = bwd + shift_dn(bwd, k)
            k *= 2
        cnt = (jnp.minimum(row + w // 2, n) - jnp.maximum(row - w // 2, 0)).astype(F32)
        in_group = (lane >= gi * POOL_CH) & (lane < (gi + 1) * POOL_CH)
        mean = jnp.where(in_group, (fwd + bwd) / cnt, mean)
    pooled = mean - u
    mixed = jnp.dot(pooled, w_ref[...], precision=lax.Precision.HIGHEST, preferred_element_type=F32)
    o_ref[...] = (mixed * s_ref[...]).astype(BF16)


def _pool_call(li, up, w_bd, scale, n, nseg, blk0, name):
    return pl.pallas_call(
        functools.partial(_pool_body, n=n),
        grid=(nseg,),
        in_specs=[pl.BlockSpec((n, POOL_W), lambda i: (blk0 + i, 0)),
                  pl.BlockSpec((None, POOL_W, POOL_W), lambda i: (li, 0, 0)),
                  pl.BlockSpec((None, 1, POOL_W), lambda i: (li, 0, 0))],
        out_specs=pl.BlockSpec((n, POOL_W), lambda i: (i, 0)),
        out_shape=jax.ShapeDtypeStruct((nseg * n, POOL_W), BF16),
        compiler_params=_cparams(("parallel",)),
        name=name,
    )(up, w_bd, scale)


def _lambda(lp, lam_init):
    a = jnp.sum(lp[0:1, :] * lp[1:2, :], axis=1, keepdims=True)
    b = jnp.sum(lp[2:3, :] * lp[3:4, :], axis=1, keepdims=True)
    return jnp.exp(a) - jnp.exp(b) + lam_init


def _softmax_parts(parts):
    m = functools.reduce(jnp.maximum, [jnp.max(s, axis=-1, keepdims=True) for s in parts])
    es = [jnp.exp(s - m) for s in parts]
    inv = 1.0 / functools.reduce(jnp.add, [jnp.sum(e, axis=-1, keepdims=True) for e in es])
    return [e * inv for e in es]


def _diff_head(q, ks, vs, lam, gd, lam_init):
    p1 = _softmax_parts([_dot_nt(q[:, :DA_QK], k[:, :DA_QK]) for k in ks])
    p2 = _softmax_parts([_dot_nt(q[:, DA_QK:], k[:, DA_QK:]) for k in ks])
    o = None
    for a, b, v in zip(p1, p2, vs):
        t = _dot((a - lam * b).astype(BF16), v)
        o = t if o is None else o + t
    return _rms(o, gd) * (1.0 - lam_init)


def _plain_head(q, ks, vs, biases):
    scores = []
    for k, bias in zip(ks, biases):
        s = _dot_nt(q, k)
        scores.append(s if bias is None else s + bias)
    ps = _softmax_parts(scores)
    o = None
    for p, v in zip(ps, vs):
        t = _dot(p.astype(BF16), v)
        o = t if o is None else o + t
    return o


def _attn_ctx_body(lam_ref, gd_ref, qa_ref, ka_ref, va_ref, qb_ref, kb_ref, vb_ref, o_ref, *, lam_init):
    lam = _lambda(lam_ref[...], lam_init)
    gd = gd_ref[...]
    for h in range(DA_H):
        sl = slice(h * DA_V, (h + 1) * DA_V)
        o = _diff_head(qa_ref[:, sl], [ka_ref[:, sl].astype(BF16)], [va_ref[:, sl].astype(BF16)],
                       lam, gd, lam_init)
        o_ref[:, sl] = o.astype(BF16)
    for h in range(NA_H):
        sl = slice(h * NA_D, (h + 1) * NA_D)
        o = _plain_head(qb_ref[:, sl], [kb_ref[:, sl].astype(BF16)], [vb_ref[:, sl].astype(BF16)], [None])
        o_ref[:, DA_W + h * NA_D:DA_W + (h + 1) * NA_D] = o.astype(BF16)


def _attn_ctx(li, lam_p, g_diff, qa, ka, va, qb, kb, vb, lam_init):
    def blk(w):
        return pl.BlockSpec((N_P, w), lambda b: (b, 0))

    def cblk(w):
        return pl.BlockSpec((None, None, N_P, w), lambda b: (b, li, 0, 0))

    return pl.pallas_call(
        functools.partial(_attn_ctx_body, lam_init=lam_init),
        grid=(NB_P,),
        in_specs=[pl.BlockSpec((None, 4, DA_QK), lambda b: (li, 0, 0)),
                  pl.BlockSpec((None, 1, DA_V), lambda b: (li, 0, 0)),
                  blk(DA_W), cblk(DA_W), cblk(DA_W), blk(NA_W), cblk(NA_W), cblk(NA_W)],
        out_specs=pl.BlockSpec((N_P, AB_W), lambda b: (b, 0)),
        out_shape=jax.ShapeDtypeStruct((T_P, AB_W), BF16),
        compiler_params=_cparams(("parallel",)),
        name=f"attn_ctx_l{li}",
    )(lam_p, g_diff, qa, ka, va, qb, kb, vb)


G_ROWS = N_S // GRID_W
QR_PER_BLK = TQ // GRID_W


def _fill_na_bias(j, strip_ref, bias_scr):
    for jj in range(N_S // TQ):
        @pl.when(j == jj)
        def _():
            for rr in range(QR_PER_BLK):
                qr = jj * QR_PER_BLK + rr
                rs = min(max(qr - NA_ROWS // 2, 0), G_ROWS - NA_ROWS)
                lo, hi = rs * GRID_W, (rs + NA_ROWS) * GRID_W
                a0 = (rs - qr + NA_ROWS - 1) * GRID_W
                q = slice(rr * GRID_W, (rr + 1) * GRID_W)
                for h in range(NA_H):
                    if lo > 0:
                        bias_scr[h, q, 0:lo] = jnp.full((GRID_W, lo), NEG, F32)
                    bias_scr[h, q, lo:hi] = strip_ref[h, :, a0:a0 + NA_ROWS * GRID_W]
                    if hi < N_S:
                        bias_scr[h, q, hi:N_S] = jnp.full((GRID_W, N_S - hi), NEG, F32)


def _attn_lat_body(lam_ref, gd_ref, qa_ref, ka_ref, va_ref, ck_ref, cv_ref,
                   qb_ref, kb_ref, vb_ref, cnk_ref, cnv_ref, strip_ref, o_ref, bias_scr, *, lam_init):
    _fill_na_bias(pl.program_id(1), strip_ref, bias_scr)
    lam = _lambda(lam_ref[...], lam_init)
    gd = gd_ref[...]
    for h in range(DA_H):
        sl = slice(h * DA_V, (h + 1) * DA_V)
        ks = [ka_ref[:, sl], ck_ref[:, sl].astype(BF16)]
        vs = [va_ref[:, sl], cv_ref[:, sl].astype(BF16)]
        o_ref[:, sl] = _diff_head(qa_ref[:, sl], ks, vs, lam, gd, lam_init).astype(BF16)
    for h in range(NA_H):
        sl = slice(h * NA_D, (h + 1) * NA_D)
        ks = [kb_ref[:, sl], cnk_ref[:, sl].astype(BF16)]
        vs = [vb_ref[:, sl], cnv_ref[:, sl].astype(BF16)]
        o = _plain_head(qb_ref[:, sl], ks, vs, [bias_scr[h], None])
        o_ref[:, DA_W + h * NA_D:DA_W + (h + 1) * NA_D] = o.astype(BF16)


def _attn_lat(li, lam_p, g_diff, qa, ka, va, ck, cv, qb, kb, vb, cnk, cnv, strip, lam_init):
    nq = N_S // TQ
    q0 = T_P // TQ

    def qblk(w):
        return pl.BlockSpec((TQ, w), lambda b, j: (q0 + b * nq + j, 0))

    def kvblk(w):
        return pl.BlockSpec((N_S, w), lambda b, j: (b, 0))

    def cblk(w):
        return pl.BlockSpec((None, None, PAST, w), lambda b, j: (b, li, 0, 0))

    return pl.pallas_call(
        functools.partial(_attn_lat_body, lam_init=lam_init),
        grid=(NB_S, nq),
        in_specs=[pl.BlockSpec((None, 4, DA_QK), lambda b, j: (li, 0, 0)),
                  pl.BlockSpec((None, 1, DA_V), lambda b, j: (li, 0, 0)),
                  qblk(DA_W), kvblk(DA_W), kvblk(DA_W), cblk(DA_W), cblk(DA_W),
                  qblk(NA_W), kvblk(NA_W), kvblk(NA_W), cblk(NA_W), cblk(NA_W),
                  pl.BlockSpec((None, NA_H, GRID_W, (2 * NA_ROWS - 1) * GRID_W), lambda b, j: (li, 0, 0, 0))],
        out_specs=pl.BlockSpec((TQ, AB_W), lambda b, j: (b * nq + j, 0)),
        out_shape=jax.ShapeDtypeStruct((T_S, AB_W), BF16),
        scratch_shapes=[pltpu.VMEM((NA_H, TQ, N_S), F32)],
        compiler_params=_cparams(("parallel", "parallel")),
        name=f"attn_lat_l{li}",
    )(lam_p, g_diff, qa, ka, va, ck, cv, qb, kb, vb, cnk, cnv, strip)


def _na_bias_strips(table):
    w = np.arange(GRID_W)
    cs = np.clip(w - NA_COLS // 2, 0, GRID_W - NA_COLS)
    col_ok = (w[None, :] >= cs[:, None]) & (w[None, :] < cs[:, None] + NA_COLS)
    sel_c = (np.clip(w[None, :, None] - w[:, None, None] + NA_COLS - 1, 0, 2 * NA_COLS - 2)
             == np.arange(2 * NA_COLS - 1)).astype(np.float32)
    t1 = jnp.einsum('lhab,cdb->lhcad', table.astype(F32), jnp.asarray(sel_c),
                    precision=lax.Precision.HIGHEST)
    t1 = jnp.where(jnp.asarray(col_ok)[None, None, :, None, :], t1, NEG)
    return t1.reshape(DEPTH, NA_H, GRID_W, (2 * NA_ROWS - 1) * GRID_W)


def _route_t(logits_t, b_col, tm):
    per = N_EXP // N_GRP
    s = 1.0 / (1.0 + jnp.exp(-logits_t))
    sel3 = (s + b_col).reshape(N_GRP, per, tm)
    midx = lax.broadcasted_iota(jnp.int32, (N_GRP, per, tm), 1)
    gidx = lax.broadcasted_iota(jnp.int32, (N_GRP, per, tm), 0)
    ninf = -jnp.inf
    m1 = jnp.max(sel3, axis=1, keepdims=True)
    i1 = jnp.min(jnp.where(sel3 == m1, midx, per), axis=1, keepdims=True)
    m2 = jnp.max(jnp.where(midx == i1, ninf, sel3), axis=1, keepdims=True)
    gsc = jnp.broadcast_to(m1 + m2, (N_GRP, per, tm))
    cnt = jnp.zeros((N_GRP, per, tm), jnp.int32)
    for g in range(N_GRP):
        sg = gsc[g][None]
        cnt = cnt + ((sg > gsc) | ((sg == gsc) & (g < gidx))).astype(jnp.int32)
    x = jnp.where(cnt < TOPK_GRP, sel3, ninf).reshape(N_EXP, tm)
    eidx = lax.broadcasted_iota(jnp.int32, (N_EXP, tm), 0)
    cnt = jnp.zeros((N_EXP, tm), jnp.int32)
    for e in range(N_EXP):
        row = x[e:e + 1, :]
        cnt = cnt + ((row > x) | ((row == x) & (e < eidx))).astype(jnp.int32)
    chosen = cnt < TOP_K
    w = jnp.where(chosen, s, 0.0)
    return chosen, w / jnp.sum(w, axis=0, keepdims=True) * ROUTED_SCALE


def _merge_body(oabp_ref, oabs_ref, ocp_ref, ocs_ref, xp_ref, xs_ref, mod_ref, gpost_ref, gpre_ref,
                wout_ref, wrt_ref, br_ref, tri_ref, y_ref, h_ref, gates_ref, rank_ref, pcu_ref):
    i = pl.program_id(0)
    mix = (_dot(_tok_select(i, oabp_ref, oabs_ref), wout_ref[0:AB_W, :])
           + _dot(_tok_select(i, ocp_ref, ocs_ref), wout_ref[AB_W:D, :]))
    y = _tok_select(i, xp_ref, xs_ref) + mod_ref[2:3, :] * _rms(mix, gpost_ref[...])
    y_ref[...] = y
    h = _rms(y, gpre_ref[...]) * (1.0 + mod_ref[4:5, :]) + mod_ref[3:4, :]
    h_ref[...] = h.astype(BF16)
    logits_t = lax.dot_general(wrt_ref[...], h, (((1,), (1,)), ((), ())),
                               precision=lax.Precision.HIGHEST, preferred_element_type=F32)
    chosen, gates = _route_t(logits_t, br_ref[...], TM_IN)
    gates_ref[...] = gates
    ch = jnp.where(chosen, 1.0, 0.0)
    rank = _dot(ch.astype(BF16), tri_ref[...])
    rank_ref[...] = jnp.where(chosen, rank, -1.0)
    for j in range(TM_IN // B_SORT):
        cnt = jnp.sum(ch[:, j * B_SORT:(j + 1) * B_SORT], axis=1, keepdims=True)
        units = jnp.floor((cnt + (UNIT - 1)) * (1.0 / UNIT))
        pcu_ref[j] = jnp.broadcast_to(units, (N_EXP, 128))


def _merge(li, oabp, oabs, ocp, ocs, xp, xs, modv, g_post, g_pre, w_out_bf, w_router_t, b_router_col, tri):
    def tok(w):
        return pl.BlockSpec((TM_IN, w), lambda i: (i, 0))

    def par(*shape):
        return pl.BlockSpec((None,) + shape, lambda i: (li,) + (0,) * len(shape))

    tok_t = pl.BlockSpec((N_EXP, TM_IN), lambda i: (0, i))
    nb = TM_IN // B_SORT
    return pl.pallas_call(
        _merge_body,
        grid=(T // TM_IN,),
        in_specs=[_ctx_tok(AB_W), _lat_tok(AB_W), _ctx_tok(POOL_W), _lat_tok(POOL_W), _ctx_tok(D), _lat_tok(D),
                  pl.BlockSpec((None, None, 6, D), lambda i: (li, _mod_row(i, TM_IN), 0, 0)),
                  par(1, D), par(1, D), par(D, D), par(N_EXP, D), par(N_EXP, 1),
                  pl.BlockSpec((TM_IN, TM_IN), lambda i: (0, 0))],
        out_specs=[tok(D), tok(D), tok_t, tok_t,
                   pl.BlockSpec((nb, N_EXP, 128), lambda i: (i, 0, 0))],
        out_shape=[jax.ShapeDtypeStruct((T, D), F32),
                   jax.ShapeDtypeStruct((T, D), BF16),
                   jax.ShapeDtypeStruct((N_EXP, T), F32),
                   jax.ShapeDtypeStruct((N_EXP, T), F32),
                   jax.ShapeDtypeStruct((N_BLK, N_EXP, 128), F32)],
        compiler_params=_cparams(("parallel",)),
        name=f"merge_l{li}",
    )(oabp, oabs, ocp, ocs, xp, xs, modv, g_post, g_pre, w_out_bf, w_router_t, b_router_col, tri)


def _rank_matrix():
    t = np.arange(TM_IN)
    m = (t[:, None] < t[None, :]) & (t[:, None] // B_SORT == t[None, :] // B_SORT)
    return jnp.asarray(m, BF16)


def _moe_tables(pcu_f):
    pcu = pcu_f[:, :, 0].astype(jnp.int32)
    incl = jnp.cumsum(pcu, axis=1)
    uo = incl - pcu
    tot = jnp.sum(pcu, axis=0)
    gt = (tot + UPT - 1) // UPT
    steps = (gt + TPS - 1) // TPS
    sincl = jnp.cumsum(steps)
    sstart = sincl - steps
    gstart_u = sstart * UPS
    bstart = gstart_u[None, :] + jnp.cumsum(pcu, axis=0) - pcu
    u = jnp.arange(UPB, dtype=jnp.int32)
    eou = jnp.sum((u[None, :, None] >= incl[:, None, :]).astype(jnp.int32), axis=-1)
    valid = eou < N_EXP
    onehot = eou[:, :, None] == jnp.arange(N_EXP, dtype=jnp.int32)[None, None, :]
    run_u0 = jnp.sum(jnp.where(onehot, uo[:, None, :], 0), axis=-1)
    run_g0 = jnp.sum(jnp.where(onehot, bstart[:, None, :], 0), axis=-1)
    dst = run_g0 + u[None, :] - run_u0
    spare0 = SPARE_UNIT0 + (jnp.arange(N_BLK, dtype=jnp.int32) % 2)[:, None] * (UPB + ZPB)
    dst_sort = jnp.where(valid, dst, spare0 + u[None, :])
    dst_comb = jnp.where(valid, dst, ZERO_UNIT)
    loc = jnp.where(valid, (u[None, :] - run_u0) * UNIT, -(1 << 20))
    k = jnp.arange(UPT - 1, dtype=jnp.int32)
    ztail = gt * UPT - tot
    zdst = jnp.where(k[None, :] < ztail[:, None], (gstart_u + tot)[:, None] + k[None, :], -1).reshape(N_BLK, ZPB)
    zdst = jnp.where(zdst >= 0, zdst, spare0 + UPB + jnp.arange(ZPB, dtype=jnp.int32)[None, :])
    step = jnp.arange(NS_MAX, dtype=jnp.int32)
    te = jnp.minimum(jnp.sum((step[:, None] >= sincl[None, :]).astype(jnp.int32), axis=-1), N_EXP - 1)
    mine = te[:, None] == jnp.arange(N_EXP, dtype=jnp.int32)[None, :]
    left = jnp.sum(jnp.where(mine, (gt - TPS * (step[:, None] - sstart[None, :])), 0), axis=-1)
    ntile = jnp.clip(left, 0, TPS)
    return (jnp.minimum(eou, N_EXP - 1), loc, dst_sort, dst_comb, zdst, te, ntile, sincl[-1:])


def _unit_rows(u):
    if isinstance(u, int):
        return pl.ds(u * UNIT, UNIT)
    return pl.ds(pl.multiple_of(u * UNIT, UNIT), UNIT)


def _sort_body(eou_ref, loc_ref, dst_ref, zdst_ref, h_ref, rank_ref, gates_ref, xs_hbm, p_ref,
               xs_scr, zero_scr, sem):
    b = pl.program_id(0)
    slot = b % 2

    def copy(src, d, s):
        return pltpu.make_async_copy(src, xs_hbm.at[_unit_rows(d)], sem.at[s])

    def wait_step(s):
        lax.fori_loop(0, UPB + ZPB, lambda k, c: (copy(zero_scr, 0, s).wait(), c)[1], 0, unroll=True)

    @pl.when(b == 0)
    def _():
        zero_scr[...] = jnp.zeros((UNIT, XS_W), BF16)

    @pl.when(b >= 2)
    def _():
        wait_step(slot)

    base = lax.broadcasted_iota(jnp.int32, (UNIT, B_SORT), 0)

    def onehot_unit(u, carry):
        match = rank_ref[pl.ds(eou_ref[b, u], 1), :] == (base + loc_ref[b, u]).astype(F32)
        p_ref[_unit_rows(u), :] = jnp.where(match, 1.0, 0.0).astype(BF16)
        return carry

    lax.fori_loop(0, UPB, onehot_unit, 0, unroll=UNROLL)
    g = gates_ref[...]
    g_hi = g.astype(BF16)
    g_hl = jnp.concatenate([g_hi, (g - g_hi.astype(F32)).astype(BF16)], axis=0)
    h = h_ref[...]
    for c in range(RB // 512):
        rows = slice(c * 512, (c + 1) * 512)
        xs_scr[slot, rows, 0:D] = _dot(p_ref[rows, :], h).astype(BF16)
        xs_scr[slot, rows, D:XS_W] = _dot_nt(p_ref[rows, :], g_hl).astype(BF16)

    def start_run(u, carry):
        copy(xs_scr.at[slot, _unit_rows(u)], dst_ref[b, u], slot).start()
        return carry

    def start_zero(j, carry):
        copy(zero_scr, zdst_ref[b, j], slot).start()
        return carry

    lax.fori_loop(0, UPB, start_run, 0, unroll=UNROLL)
    lax.fori_loop(0, ZPB, start_zero, 0, unroll=True)

    @pl.when(b == N_BLK - 1)
    def _():
        wait_step(1 - slot)
        wait_step(slot)


def _moe_sort(li, eou, loc, dst, zdst, h, rank_t, gates_t):
    blk_t = pl.BlockSpec((N_EXP, B_SORT), lambda b, *_: (0, b))
    return pl.pallas_call(
        _sort_body,
        grid_spec=pltpu.PrefetchScalarGridSpec(
            num_scalar_prefetch=4,
            grid=(N_BLK,),
            in_specs=[pl.BlockSpec((B_SORT, D), lambda b, *_: (b, 0)), blk_t, blk_t],
            out_specs=[pl.BlockSpec(memory_space=pl.ANY),
                       pl.BlockSpec((RB, B_SORT), lambda b, *_: (b, 0))],
            scratch_shapes=[pltpu.VMEM((2, RB, XS_W), BF16),
                            pltpu.VMEM((UNIT, XS_W), BF16),
                            pltpu.SemaphoreType.DMA((2,))]),
        out_shape=[jax.ShapeDtypeStruct((R_SORT, XS_W), BF16),
                   jax.ShapeDtypeStruct((N_BLK * RB, B_SORT), BF16)],
        compiler_params=_cparams(("arbitrary",)),
        name=f"moe_sort_l{li}",
    )(eou, loc, dst, zdst, h, rank_t, gates_t)


def _experts_body(te_ref, ntile_ref, ns_ref, xs_ref, wg_ref, wu_ref, wd_ref, o_ref, wg_s, wu_s, wd_s):
    i = pl.program_id(0)
    valid = i < ns_ref[0]
    idx = jnp.minimum(i, NS_MAX - 1)
    e = te_ref[idx]
    ntile = jnp.where(valid, ntile_ref[idx], 0)
    new_expert = (i == 0) | (e != te_ref[jnp.maximum(i - 1, 0)])

    @pl.when(valid & new_expert)
    def _():
        wg_s[...] = wg_ref[...].astype(BF16)
        wu_s[...] = wu_ref[...].astype(BF16)
        wd_s[...] = wd_ref[...].astype(BF16)

    for k in range(1, TPS + 1):
        rows = k * TMG

        @pl.when(ntile == k)
        def _():
            x = xs_ref[0:rows, 0:D]
            lane = lax.broadcasted_iota(jnp.int32, (rows, 128), 1)
            mine = (lane == e) | (lane == e + N_EXP)
            gate = jnp.sum(jnp.where(mine, xs_ref[0:rows, D:XS_W].astype(F32), 0.0), axis=1, keepdims=True)
            hid = _silu(_dot(x, wg_s[...])) * _dot(x, wu_s[...]) * gate
            o_ref[0:rows, :] = _dot(hid.astype(BF16), wd_s[...]).astype(BF16)
            if rows < STEP_ROWS:
                o_ref[rows:STEP_ROWS, :] = jnp.zeros((STEP_ROWS - rows, D), BF16)

    @pl.when(i == NS_MAX)
    def _():
        o_ref[...] = jnp.zeros((STEP_ROWS, D), BF16)


def _moe_experts(li, te, ntile, ns, xs, w_gate, w_up, w_down):
    def last(i, ns):
        return jnp.minimum(i, ns[0] - 1)

    def out_blk(i, ns):
        return jnp.where(i == NS_MAX, NS_MAX, last(i, ns))

    def wspec(*shape):
        return pl.BlockSpec((None, None) + shape,
                            lambda i, te, ntile, ns: (li, te[last(i, ns)]) + (0,) * len(shape))

    return pl.pallas_call(
        _experts_body,
        grid_spec=pltpu.PrefetchScalarGridSpec(
            num_scalar_prefetch=3,
            grid=(NS_MAX + 1,),
            in_specs=[pl.BlockSpec((STEP_ROWS, XS_W), lambda i, te, ntile, ns: (last(i, ns), 0)),
                      wspec(D, EXP_DIM), wspec(D, EXP_DIM), wspec(EXP_DIM, D)],
            out_specs=pl.BlockSpec((STEP_ROWS, D), lambda i, te, ntile, ns: (out_blk(i, ns), 0)),
            scratch_shapes=[pltpu.VMEM((D, EXP_DIM), BF16),
                            pltpu.VMEM((D, EXP_DIM), BF16),
                            pltpu.VMEM((EXP_DIM, D), BF16)]),
        out_shape=jax.ShapeDtypeStruct((R_OUT, D), BF16),
        compiler_params=_cparams(("arbitrary",)),
        name=f"moe_experts_l{li}",
    )(te, ntile, ns, xs, w_gate, w_up, w_down)


def _combine_body(dst_ref, h_ref, p_ref, y_ref, mod_ref, gpost_ref,
                  wsg_ref, wsu_ref, wsd_ref, os_hbm, op_ref, ol_ref, os_scr, sem):
    b = pl.program_id(0)
    slot = b % 2

    def copy(d, u, s):
        return pltpu.make_async_copy(os_hbm.at[_unit_rows(d)], os_scr.at[s, _unit_rows(u)], sem.at[s])

    def fetch(blk, s):
        lax.fori_loop(0, UPB, lambda u, c: (copy(dst_ref[blk, u], u, s).start(), c)[1], 0, unroll=UNROLL)

    @pl.when(b == 0)
    def _():
        fetch(0, 0)

    @pl.when(b + 1 < N_BLK)
    def _():
        fetch(b + 1, 1 - slot)

    h = h_ref[...]
    shared = _dot((_silu(_dot(h, wsg_ref[...])) * _dot(h, wsu_ref[...])).astype(BF16), wsd_ref[...])
    lax.fori_loop(0, UPB, lambda k, c: (copy(0, 0, slot).wait(), c)[1], 0, unroll=True)
    routed = lax.dot_general(p_ref[...], os_scr[slot], (((0,), (0,)), ((), ())),
                             preferred_element_type=F32)
    out = y_ref[...] + mod_ref[5:6, :] * _rms(routed + shared, gpost_ref[...])

    @pl.when(b < T_P // B_SORT)
    def _():
        op_ref[...] = out

    @pl.when(b >= T_P // B_SORT)
    def _():
        ol_ref[...] = out


def _moe_combine(li, dst, h, p, y, modv, g_post, wsg_bf, wsu_bf, wsd_bf, os):
    nbp = T_P // B_SORT
    def tok(w):
        return pl.BlockSpec((B_SORT, w), lambda b, *_: (b, 0))

    def par(*shape):
        return pl.BlockSpec((None,) + shape, lambda b, *_: (li,) + (0,) * len(shape))

    return pl.pallas_call(
        _combine_body,
        grid_spec=pltpu.PrefetchScalarGridSpec(
            num_scalar_prefetch=1,
            grid=(N_BLK,),
            in_specs=[tok(D), pl.BlockSpec((RB, B_SORT), lambda b, *_: (b, 0)), tok(D),
                      pl.BlockSpec((None, None, 6, D), lambda b, *_: (li, _mod_row(b, B_SORT), 0, 0)),
                      par(1, D), par(D, EXP_DIM), par(D, EXP_DIM), par(EXP_DIM, D),
                      pl.BlockSpec(memory_space=pl.ANY)],
            out_specs=[pl.BlockSpec((B_SORT, D), lambda b, *_: (jnp.minimum(b, nbp - 1), 0)),
                       pl.BlockSpec((B_SORT, D), lambda b, *_: (jnp.maximum(b - nbp, 0), 0))],
            scratch_shapes=[pltpu.VMEM((2, RB, D), BF16),
                            pltpu.SemaphoreType.DMA((2,))]),
        out_shape=[jax.ShapeDtypeStruct((T_P, D), F32), jax.ShapeDtypeStruct((T_S, D), F32)],
        compiler_params=_cparams(("arbitrary",)),
        name=f"moe_combine_l{li}",
    )(dst, h, p, y, modv, g_post, wsg_bf, wsu_bf, wsd_bf, os)


def _block_diag(w):
    out = jnp.zeros((DEPTH, POOL_W, POOL_W), F32)
    for g in range(len(POOL_WINDOWS)):
        out = out.at[:, g * POOL_CH:(g + 1) * POOL_CH, g * POOL_CH:(g + 1) * POOL_CH].set(w[:, g])
    return out


def kernel(x_prompt, x_sample, cache_diff_k, cache_diff_v, cache_na_k, cache_na_v, c, c_ctx,
           w_ada, b_ada, g_pre_mix, g_post_mix, g_pre_ffn, g_post_ffn, w_in, w_out,
           diff_lambda, g_diff, na_bias, pool_w, pool_scale, w_router, b_router,
           w_gate, w_up, w_down, ws_gate, ws_up, ws_down):
    xp, xs = x_prompt.reshape(T_P, D), x_sample.reshape(T_S, D)
    cvec = jnp.concatenate([c_ctx[None, :], c, jnp.zeros((3, D), F32)], axis=0)
    modv = _modulation(cvec, w_ada, b_ada)[:, :1 + NB_S].reshape(DEPTH, 1 + NB_S, 6, D)

    cos_t, sin_t = _rope_tables()
    w_in_bf = w_in.astype(BF16)
    w_out_bf = w_out.astype(BF16)
    wsg_bf, wsu_bf, wsd_bf = ws_gate.astype(BF16), ws_up.astype(BF16), ws_down.astype(BF16)
    pool_bd = _block_diag(pool_w)
    row = lambda a: a.reshape(DEPTH, 1, a.shape[-1])
    g_pre_mix, g_post_mix, g_pre_ffn, g_post_ffn = map(row, (g_pre_mix, g_post_mix, g_pre_ffn, g_post_ffn))
    g_diff, pool_scale = row(g_diff), row(pool_scale)
    w_router_t = jnp.swapaxes(w_router, 1, 2)
    b_router_col = b_router.reshape(DEPTH, N_EXP, 1)
    tri = _rank_matrix()
    ck = cache_diff_k.reshape(NB_S, DEPTH, PAST, DA_W)
    cv = cache_diff_v.reshape(NB_S, DEPTH, PAST, DA_W)
    cnk = cache_na_k.reshape(NB_S, DEPTH, PAST, NA_W)
    cnv = cache_na_v.reshape(NB_S, DEPTH, PAST, NA_W)

    strips = _na_bias_strips(na_bias)

    caches = ()
    for li in range(DEPTH):
        lam_init = 0.8 - 0.6 * math.exp(-0.3 * li)
        qa, qb, up, ka, va, kb, vb, *caches = _in_proj(li, xp, xs, modv, g_pre_mix, w_in_bf, cos_t, sin_t,
                                                      caches)
        ocp = _pool_call(li, up, pool_bd, pool_scale, N_P, NB_P, 0, f"pool_ctx_l{li}")
        ocs = _pool_call(li, up, pool_bd, pool_scale, N_S, NB_S, T_P // N_S, f"pool_lat_l{li}")
        oabp = _attn_ctx(li, diff_lambda, g_diff, qa, caches[0], caches[1], qb, caches[2], caches[3], lam_init)
        oabs = _attn_lat(li, diff_lambda, g_diff, qa, ka, va, ck, cv, qb, kb, vb, cnk, cnv, strips, lam_init)
        y, h, gates_t, rank_t, pcu = _merge(li, oabp, oabs, ocp, ocs, xp, xs, modv, g_post_mix, g_pre_ffn,
                                            w_out_bf, w_router_t, b_router_col, tri)
        eou, loc, dst_sort, dst_comb, zdst, te, ntile, ns = _moe_tables(pcu)
        srt, p = _moe_sort(li, eou, loc, dst_sort, zdst, h, rank_t, gates_t)
        os = _moe_experts(li, te, ntile, ns, srt, w_gate, w_up, w_down)
        xp, xs = _moe_combine(li, dst_comb, h, p, y, modv, g_post_ffn, wsg_bf, wsu_bf, wsd_bf, os)

    new_dk, new_dv, new_nk, new_nv = caches
    return (xp.reshape(NB_P, N_P, D), xs.reshape(NB_S, N_S, D),
            new_dk.reshape(NB_P, DEPTH, N_P, DA_H, 2 * DA_QK), new_dv.reshape(NB_P, DEPTH, N_P, DA_H, DA_V),
            new_nk.reshape(NB_P, DEPTH, N_P, NA_H, NA_D), new_nv.reshape(NB_P, DEPTH, N_P, NA_H, NA_D))
```

```python
import functools
import math

import numpy as np
import jax
import jax.numpy as jnp
from jax import lax
from jax.experimental import pallas as pl
from jax.experimental.pallas import tpu as pltpu

F32 = jnp.float32
BF16 = jnp.bfloat16

D = 1024
DEPTH = 2
NB_P, N_P = 16, 256
NB_S, N_S = 4, 1024
T_P = NB_P * N_P
T_S = NB_S * N_S
T = T_P + T_S
PAST = 512
GRID_W = 64
DA_H, DA_QK, DA_V = 4, 64, 128
DA_W = DA_H * DA_V
NA_H, NA_D = 4, 64
NA_W = NA_H * NA_D
NA_ROWS, NA_COLS = 8, 16
POOL_WINDOWS = (2, 4, 8, 16)
POOL_CH = 64
POOL_W = 256
IN_COLS = 3 * DA_W + 3 * NA_W + POOL_W
AB_W = DA_W + NA_W
N_EXP, TOP_K, N_GRP, TOPK_GRP = 64, 8, 8, 4
EXP_DIM = 256
ROUTED_SCALE = 2.5
EPS = 1e-6
NEG = -1e30
ROPE_THETA = 10000.0

VMEM_LIMIT = 50 * 1024 * 1024

TM_IN = 512
NT_P = T_P // TM_IN
TQ = 256

B_SORT = 256
UNIT = 16
N_BLK = T // B_SORT
RUN_PAD_ROWS = TOP_K * B_SORT + N_EXP * (UNIT - 1)
RB = -(-RUN_PAD_ROWS // 256) * 256
UPB = RB // UNIT
TMG = 256
UPT = TMG // UNIT
NT_MAX = (N_BLK * RUN_PAD_ROWS) // TMG + N_EXP
XS_W = D + 128
EXP_PER_BLK = N_EXP // N_BLK
ZPB = EXP_PER_BLK * (UPT - 1)
TPS = 4
STEP_ROWS = TPS * TMG
UPS = STEP_ROWS // UNIT
NS_MAX = NT_MAX // TPS + (N_EXP * (TPS - 1)) // TPS
ZERO_UNIT = NS_MAX * UPS
SPARE_UNIT0 = ZERO_UNIT + UPS
R_OUT = (NS_MAX + 1) * STEP_ROWS
R_SORT = R_OUT + -(-2 * (UPB + ZPB) // UPS) * STEP_ROWS
UNROLL = 8


def _cparams(sem):
    return pltpu.CompilerParams(dimension_semantics=sem, vmem_limit_bytes=VMEM_LIMIT)


def _rms(x, g):
    return x * lax.rsqrt(jnp.mean(x * x, axis=-1, keepdims=True) + EPS) * g


def _dot(a, b):
    return jnp.dot(a, b, preferred_element_type=F32)


def _dot_nt(a, b):
    return lax.dot_general(a, b, (((1,), (1,)), ((), ())), preferred_element_type=F32)


def _silu(x):
    return x / (1.0 + jnp.exp(-x))


def _mod_row(i, tm):
    off = i * tm - T_P
    return jnp.where(off >= 0, 1 + jnp.maximum(off, 0) // N_S, 0)


def _mod_body(c_ref, w_ref, b_ref, o_ref):
    c = c_ref[...]
    o_ref[...] = jnp.dot(_silu(c), w_ref[...], precision=lax.Precision.HIGHEST,
                         preferred_element_type=F32) + b_ref[...]


def _modulation(cvec, w_ada, b_ada):
    tn = 1536
    return pl.pallas_call(
        _mod_body,
        grid=(DEPTH, 6 * D // tn),
        in_specs=[pl.BlockSpec((8, D), lambda l, j: (0, 0)),
                  pl.BlockSpec((None, D, tn), lambda l, j: (l, 0, j)),
                  pl.BlockSpec((None, 1, tn), lambda l, j: (l, 0, j))],
        out_specs=pl.BlockSpec((None, 8, tn), lambda l, j: (l, 0, j)),
        out_shape=jax.ShapeDtypeStruct((DEPTH, 8, 6 * D), F32),
        compiler_params=_cparams(("parallel", "parallel")),
        name="modulation",
    )(cvec, w_ada, b_ada.reshape(DEPTH, 1, 6 * D))


def _rope_tables():
    nf = DA_QK // 4
    t = np.arange(N_S)
    pos = np.stack([t // GRID_W, t % GRID_W], axis=-1).astype(np.float32)
    inv = np.power(np.float32(ROPE_THETA), -np.arange(nf, dtype=np.float32) / nf)
    ang = pos[:, :, None] * inv
    cos = np.cos(ang)
    sin = np.sin(ang)
    cos64 = np.concatenate([cos[:, 0], cos[:, 0], cos[:, 1], cos[:, 1]], axis=-1)
    sin64 = np.concatenate([-sin[:, 0], sin[:, 0], -sin[:, 1], sin[:, 1]], axis=-1)
    reps = DA_W // DA_QK
    cos_t = np.concatenate([np.tile(cos64, (1, reps)), np.ones((TM_IN, DA_W), np.float32)], axis=0)
    sin_t = np.concatenate([np.tile(sin64, (1, reps)), np.zeros((TM_IN, DA_W), np.float32)], axis=0)
    return jnp.asarray(cos_t, F32), jnp.asarray(sin_t, F32)


def _tok_select(i, ctx_ref, lat_ref):
    return jnp.where(i < NT_P, ctx_ref[...], lat_ref[...])


def _in_proj_body(*refs):
    (xp_ref, xs_ref, mod_ref, g_ref, w_ref, cos_ref, sin_ref) = refs[:7]
    (qa_ref, qb_ref, up_ref, ka_ref, va_ref, kb_ref, vb_ref, ck_ref, cv_ref, cnk_ref, cnv_ref) = refs[-11:]
    i = pl.program_id(0)
    x = _tok_select(i, xp_ref, xs_ref)
    h = (_rms(x, g_ref[...]) * (1.0 + mod_ref[1:2, :]) + mod_ref[0:1, :]).astype(BF16)
    cos = cos_ref[...]
    sin = sin_ref[...]
    lane = lax.broadcasted_iota(jnp.int32, (TM_IN, DA_W), 1)
    first = (lane % 32) < 16

    def proj(lo, hi):
        return _dot(h, w_ref[:, lo:hi])

    def rope(t):
        swapped = jnp.where(first, pltpu.roll(t, DA_W - 16, 1), pltpu.roll(t, 16, 1))
        return t * cos + swapped * sin

    o = 0
    qa_ref[...] = (rope(proj(o, o + DA_W)) * (DA_QK ** -0.5)).astype(BF16)
    o += DA_W
    ka = rope(proj(o, o + DA_W))
    o += DA_W
    va = proj(o, o + DA_W)
    o += DA_W
    qb_ref[...] = (proj(o, o + NA_W) * (NA_D ** -0.5)).astype(BF16)
    o += NA_W
    kb = proj(o, o + NA_W)
    o += NA_W
    vb = proj(o, o + NA_W)
    o += NA_W
    up_ref[...] = proj(o, o + POOL_W)

    @pl.when(i < NT_P)
    def _():
        for ref, val in ((ck_ref, ka), (cv_ref, va), (cnk_ref, kb), (cnv_ref, vb)):
            ref[...] = val.reshape(ref.shape)

    @pl.when(i >= NT_P)
    def _():
        for ref, val in ((ka_ref, ka), (va_ref, va), (kb_ref, kb), (vb_ref, vb)):
            ref[...] = val.astype(BF16)


def _ctx_tok(w):
    return pl.BlockSpec((TM_IN, w), lambda i: (jnp.minimum(i, NT_P - 1), 0))


def _lat_tok(w):
    return pl.BlockSpec((TM_IN, w), lambda i: (jnp.maximum(i - NT_P, 0), 0))


def _in_proj(li, xp, xs, modv, g_pre, w_in_bf, cos_t, sin_t, caches):
    n_pos_blk = N_S // TM_IN

    def pos_blk(i):
        return jnp.where(i >= NT_P, jnp.maximum(i - NT_P, 0) % n_pos_blk, n_pos_blk)

    def tok(w):
        return pl.BlockSpec((TM_IN, w), lambda i: (i, 0))

    def cache(w):
        return pl.BlockSpec((TM_IN // N_P, None, N_P, w), lambda i: (jnp.minimum(i, NT_P - 1), li, 0, 0))

    widths = (DA_W, DA_W, NA_W, NA_W)
    n_in = 7
    return pl.pallas_call(
        _in_proj_body,
        grid=(T // TM_IN,),
        in_specs=[_ctx_tok(D), _lat_tok(D),
                  pl.BlockSpec((None, None, 6, D), lambda i: (li, _mod_row(i, TM_IN), 0, 0)),
                  pl.BlockSpec((None, 1, D), lambda i: (li, 0, 0)),
                  pl.BlockSpec((None, D, IN_COLS), lambda i: (li, 0, 0)),
                  pl.BlockSpec((TM_IN, DA_W), lambda i: (pos_blk(i), 0)),
                  pl.BlockSpec((TM_IN, DA_W), lambda i: (pos_blk(i), 0))]
                 + [pl.BlockSpec(memory_space=pl.ANY)] * len(caches),
        out_specs=[tok(DA_W), tok(NA_W), tok(POOL_W)] + [_lat_tok(w) for w in widths]
                  + [cache(w) for w in widths],
        out_shape=[jax.ShapeDtypeStruct((T, DA_W), BF16),
                   jax.ShapeDtypeStruct((T, NA_W), BF16),
                   jax.ShapeDtypeStruct((T, POOL_W), F32)]
                  + [jax.ShapeDtypeStruct((T_S, w), BF16) for w in widths]
                  + [jax.ShapeDtypeStruct((NB_P, DEPTH, N_P, w), F32) for w in widths],
        input_output_aliases={n_in + k: 7 + k for k in range(len(caches))},
        compiler_params=_cparams(("arbitrary",)),
        name=f"in_proj_l{li}",
    )(xp, xs, modv, g_pre, w_in_bf, cos_t, sin_t, *caches)


def _pool_body(u_ref, w_ref, s_ref, o_ref, *, n):
    u = u_ref[...]
    row = lax.broadcasted_iota(jnp.int32, (n, POOL_W), 0)
    lane = lax.broadcasted_iota(jnp.int32, (n, POOL_W), 1)

    def shift_dn(a, k):
        return jnp.where(row >= k, pltpu.roll(a, k, 0), 0.0)

    def shift_up(a, k):
        return jnp.where(row < n - k, pltpu.roll(a, n - k, 0), 0.0)

    fwd = u
    bwd = shift_dn(u, 1)
    mean = jnp.zeros_like(u)
    k = 1
    for gi, w in enumerate(POOL_WINDOWS):
        while k < w // 2:
            fwd = fwd + shift_up(fwd, k)
            bwd = bwd + shift_dn(bwd, k)
            k *= 2
        cnt = (jnp.minimum(row + w // 2, n) - jnp.maximum(row - w // 2, 0)).astype(F32)
        in_group = (lane >= gi * POOL_CH) & (lane < (gi + 1) * POOL_CH)
        mean = jnp.where(in_group, (fwd + bwd) / cnt, mean)
    pooled = mean - u
    mixed = jnp.dot(pooled, w_ref[...], precision=lax.Precision.HIGHEST, preferred_element_type=F32)
    o_ref[...] = (mixed * s_ref[...]).astype(BF16)


def _pool_call(li, up, w_bd, scale, n, nseg, blk0, name):
    return pl.pallas_call(
        functools.partial(_pool_body, n=n),
        grid=(nseg,),
        in_specs=[pl.BlockSpec((n, POOL_W), lambda i: (blk0 + i, 0)),
                  pl.BlockSpec((None, POOL_W, POOL_W), lambda i: (li, 0, 0)),
                  pl.BlockSpec((None, 1, POOL_W), lambda i: (li, 0, 0))],
        out_specs=pl.BlockSpec((n, POOL_W), lambda i: (i, 0)),
        out_shape=jax.ShapeDtypeStruct((nseg * n, POOL_W), BF16),
        compiler_params=_cparams(("parallel",)),
        name=name,
    )(up, w_bd, scale)


def _lambda(lp, lam_init):
    a = jnp.sum(lp[0:1, :] * lp[1:2, :], axis=1, keepdims=True)
    b = jnp.sum(lp[2:3, :] * lp[3:4, :], axis=1, keepdims=True)
    return jnp.exp(a) - jnp.exp(b) + lam_init


def _softmax_parts(parts):
    m = functools.reduce(jnp.maximum, [jnp.max(s, axis=-1, keepdims=True) for s in parts])
    es = [jnp.exp(s - m) for s in parts]
    inv = 1.0 / functools.reduce(jnp.add, [jnp.sum(e, axis=-1, keepdims=True) for e in es])
    return [e * inv for e in es]


def _diff_head(q, ks, vs, lam, gd, lam_init):
    p1 = _softmax_parts([_dot_nt(q[:, :DA_QK], k[:, :DA_QK]) for k in ks])
    p2 = _softmax_parts([_dot_nt(q[:, DA_QK:], k[:, DA_QK:]) for k in ks])
    o = None
    for a, b, v in zip(p1, p2, vs):
        t = _dot((a - lam * b).astype(BF16), v)
        o = t if o is None else o + t
    return _rms(o, gd) * (1.0 - lam_init)


def _plain_head(q, ks, vs, biases):
    scores = []
    for k, bias in zip(ks, biases):
        s = _dot_nt(q, k)
        scores.append(s if bias is None else s + bias)
    ps = _softmax_parts(scores)
    o = None
    for p, v in zip(ps, vs):
        t = _dot(p.astype(BF16), v)
        o = t if o is None else o + t
    return o


def _attn_ctx_body(lam_ref, gd_ref, qa_ref, ka_ref, va_ref, qb_ref, kb_ref, vb_ref, o_ref, *, lam_init):
    lam = _lambda(lam_ref[...], lam_init)
    gd = gd_ref[...]
    for h in range(DA_H):
        sl = slice(h * DA_V, (h + 1) * DA_V)
        o = _diff_head(qa_ref[:, sl], [ka_ref[:, sl].astype(BF16)], [va_ref[:, sl].astype(BF16)],
                       lam, gd, lam_init)
        o_ref[:, sl] = o.astype(BF16)
    for h in range(NA_H):
        sl = slice(h * NA_D, (h + 1) * NA_D)
        o = _plain_head(qb_ref[:, sl], [kb_ref[:, sl].astype(BF16)], [vb_ref[:, sl].astype(BF16)], [None])
        o_ref[:, DA_W + h * NA_D:DA_W + (h + 1) * NA_D] = o.astype(BF16)


def _attn_ctx(li, lam_p, g_diff, qa, ka, va, qb, kb, vb, lam_init):
    def blk(w):
        return pl.BlockSpec((N_P, w), lambda b: (b, 0))

    def cblk(w):
        return pl.BlockSpec((None, None, N_P, w), lambda b: (b, li, 0, 0))

    return pl.pallas_call(
        functools.partial(_attn_ctx_body, lam_init=lam_init),
        grid=(NB_P,),
        in_specs=[pl.BlockSpec((None, 4, DA_QK), lambda b: (li, 0, 0)),
                  pl.BlockSpec((None, 1, DA_V), lambda b: (li, 0, 0)),
                  blk(DA_W), cblk(DA_W), cblk(DA_W), blk(NA_W), cblk(NA_W), cblk(NA_W)],
        out_specs=pl.BlockSpec((N_P, AB_W), lambda b: (b, 0)),
        out_shape=jax.ShapeDtypeStruct((T_P, AB_W), BF16),
        compiler_params=_cparams(("parallel",)),
        name=f"attn_ctx_l{li}",
    )(lam_p, g_diff, qa, ka, va, qb, kb, vb)


G_ROWS = N_S // GRID_W
QR_PER_BLK = TQ // GRID_W


def _fill_na_bias(j, strip_ref, bias_scr):
    for jj in range(N_S // TQ):
        @pl.when(j == jj)
        def _():
            for rr in range(QR_PER_BLK):
                qr = jj * QR_PER_BLK + rr
                rs = min(max(qr - NA_ROWS // 2, 0), G_ROWS - NA_ROWS)
                lo, hi = rs * GRID_W, (rs + NA_ROWS) * GRID_W
                a0 = (rs - qr + NA_ROWS - 1) * GRID_W
                q = slice(rr * GRID_W, (rr + 1) * GRID_W)
                for h in range(NA_H):
                    if lo > 0:
                        bias_scr[h, q, 0:lo] = jnp.full((GRID_W, lo), NEG, F32)
                    bias_scr[h, q, lo:hi] = strip_ref[h, :, a0:a0 + NA_ROWS * GRID_W]
                    if hi < N_S:
                        bias_scr[h, q, hi:N_S] = jnp.full((GRID_W, N_S - hi), NEG, F32)


def _attn_lat_body(lam_ref, gd_ref, qa_ref, ka_ref, va_ref, ck_ref, cv_ref,
                   qb_ref, kb_ref, vb_ref, cnk_ref, cnv_ref, strip_ref, o_ref, bias_scr, *, lam_init):
    _fill_na_bias(pl.program_id(1), strip_ref, bias_scr)
    lam = _lambda(lam_ref[...], lam_init)
    gd = gd_ref[...]
    for h in range(DA_H):
        sl = slice(h * DA_V, (h + 1) * DA_V)
        ks = [ka_ref[:, sl], ck_ref[:, h, :].astype(BF16)]
        vs = [va_ref[:, sl], cv_ref[:, h, :].astype(BF16)]
        o_ref[:, sl] = _diff_head(qa_ref[:, sl], ks, vs, lam, gd, lam_init).astype(BF16)
    for h in range(NA_H):
        sl = slice(h * NA_D, (h + 1) * NA_D)
        ks = [kb_ref[:, sl], cnk_ref[:, h, :].astype(BF16)]
        vs = [vb_ref[:, sl], cnv_ref[:, h, :].astype(BF16)]
        o = _plain_head(qb_ref[:, sl], ks, vs, [bias_scr[h], None])
        o_ref[:, DA_W + h * NA_D:DA_W + (h + 1) * NA_D] = o.astype(BF16)


def _attn_lat(li, lam_p, g_diff, qa, ka, va, ck, cv, qb, kb, vb, cnk, cnv, strip, lam_init):
    nq = N_S // TQ
    q0 = T_P // TQ

    def qblk(w):
        return pl.BlockSpec((TQ, w), lambda b, j: (q0 + b * nq + j, 0))

    def kvblk(w):
        return pl.BlockSpec((N_S, w), lambda b, j: (b, 0))

    def cblk(heads, dim):
        return pl.BlockSpec((None, None, PAST, heads, dim), lambda b, j: (b, li, 0, 0, 0))

    return pl.pallas_call(
        functools.partial(_attn_lat_body, lam_init=lam_init),
        grid=(NB_S, nq),
        in_specs=[pl.BlockSpec((None, 4, DA_QK), lambda b, j: (li, 0, 0)),
                  pl.BlockSpec((None, 1, DA_V), lambda b, j: (li, 0, 0)),
                  qblk(DA_W), kvblk(DA_W), kvblk(DA_W), cblk(DA_H, DA_V), cblk(DA_H, DA_V),
                  qblk(NA_W), kvblk(NA_W), kvblk(NA_W), cblk(NA_H, NA_D), cblk(NA_H, NA_D),
                  pl.BlockSpec((None, NA_H, GRID_W, (2 * NA_ROWS - 1) * GRID_W), lambda b, j: (li, 0, 0, 0))],
        out_specs=pl.BlockSpec((TQ, AB_W), lambda b, j: (b * nq + j, 0)),
        out_shape=jax.ShapeDtypeStruct((T_S, AB_W), BF16),
        scratch_shapes=[pltpu.VMEM((NA_H, TQ, N_S), F32)],
        compiler_params=_cparams(("parallel", "parallel")),
        name=f"attn_lat_l{li}",
    )(lam_p, g_diff, qa, ka, va, ck, cv, qb, kb, vb, cnk, cnv, strip)


def _na_bias_strips(table):
    w = np.arange(GRID_W)
    cs = np.clip(w - NA_COLS // 2, 0, GRID_W - NA_COLS)
    col_ok = (w[None, :] >= cs[:, None]) & (w[None, :] < cs[:, None] + NA_COLS)
    sel_c = (np.clip(w[None, :, None] - w[:, None, None] + NA_COLS - 1, 0, 2 * NA_COLS - 2)
             == np.arange(2 * NA_COLS - 1)).astype(np.float32)
    t1 = jnp.einsum('lhab,cdb->lhcad', table.astype(F32), jnp.asarray(sel_c),
                    precision=lax.Precision.HIGHEST)
    t1 = jnp.where(jnp.asarray(col_ok)[None, None, :, None, :], t1, NEG)
    return t1.reshape(DEPTH, NA_H, GRID_W, (2 * NA_ROWS - 1) * GRID_W)


def _route_t(logits_t, b_col, tm):
    per = N_EXP // N_GRP
    s = 1.0 / (1.0 + jnp.exp(-logits_t))
    sel3 = (s + b_col).reshape(N_GRP, per, tm)
    midx = lax.broadcasted_iota(jnp.int32, (N_GRP, per, tm), 1)
    gidx = lax.broadcasted_iota(jnp.int32, (N_GRP, per, tm), 0)
    ninf = -jnp.inf
    m1 = jnp.max(sel3, axis=1, keepdims=True)
    i1 = jnp.min(jnp.where(sel3 == m1, midx, per), axis=1, keepdims=True)
    m2 = jnp.max(jnp.where(midx == i1, ninf, sel3), axis=1, keepdims=True)
    gsc = jnp.broadcast_to(m1 + m2, (N_GRP, per, tm))
    cnt = jnp.zeros((N_GRP, per, tm), jnp.int32)
    for g in range(N_GRP):
        sg = gsc[g][None]
        cnt = cnt + ((sg > gsc) | ((sg == gsc) & (g < gidx))).astype(jnp.int32)
    x = jnp.where(cnt < TOPK_GRP, sel3, ninf).reshape(N_EXP, tm)
    eidx = lax.broadcasted_iota(jnp.int32, (N_EXP, tm), 0)
    cnt = jnp.zeros((N_EXP, tm), jnp.int32)
    for e in range(N_EXP):
        row = x[e:e + 1, :]
        cnt = cnt + ((row > x) | ((row == x) & (e < eidx))).astype(jnp.int32)
    chosen = cnt < TOP_K
    w = jnp.where(chosen, s, 0.0)
    return chosen, w / jnp.sum(w, axis=0, keepdims=True) * ROUTED_SCALE


def _merge_body(oabp_ref, oabs_ref, ocp_ref, ocs_ref, xp_ref, xs_ref, mod_ref, gpost_ref, gpre_ref,
                wout_ref, wrt_ref, br_ref, tri_ref, y_ref, h_ref, gates_ref, rank_ref, pcu_ref):
    i = pl.program_id(0)
    mix = (_dot(_tok_select(i, oabp_ref, oabs_ref), wout_ref[0:AB_W, :])
           + _dot(_tok_select(i, ocp_ref, ocs_ref), wout_ref[AB_W:D, :]))
    y = _tok_select(i, xp_ref, xs_ref) + mod_ref[2:3, :] * _rms(mix, gpost_ref[...])
    y_ref[...] = y
    h = _rms(y, gpre_ref[...]) * (1.0 + mod_ref[4:5, :]) + mod_ref[3:4, :]
    h_ref[...] = h.astype(BF16)
    logits_t = lax.dot_general(wrt_ref[...], h, (((1,), (1,)), ((), ())),
                               precision=lax.Precision.HIGHEST, preferred_element_type=F32)
    chosen, gates = _route_t(logits_t, br_ref[...], TM_IN)
    gates_ref[...] = gates
    ch = jnp.where(chosen, 1.0, 0.0)
    rank = _dot(ch.astype(BF16), tri_ref[...])
    rank_ref[...] = jnp.where(chosen, rank, -1.0)
    for j in range(TM_IN // B_SORT):
        cnt = jnp.sum(ch[:, j * B_SORT:(j + 1) * B_SORT], axis=1, keepdims=True)
        units = jnp.floor((cnt + (UNIT - 1)) * (1.0 / UNIT))
        pcu_ref[j] = jnp.broadcast_to(units, (N_EXP, 128))


def _merge(li, oabp, oabs, ocp, ocs, xp, xs, modv, g_post, g_pre, w_out_bf, w_router_t, b_router_col, tri):
    def tok(w):
        return pl.BlockSpec((TM_IN, w), lambda i: (i, 0))

    def par(*shape):
        return pl.BlockSpec((None,) + shape, lambda i: (li,) + (0,) * len(shape))

    tok_t = pl.BlockSpec((N_EXP, TM_IN), lambda i: (0, i))
    nb = TM_IN // B_SORT
    return pl.pallas_call(
        _merge_body,
        grid=(T // TM_IN,),
        in_specs=[_ctx_tok(AB_W), _lat_tok(AB_W), _ctx_tok(POOL_W), _lat_tok(POOL_W), _ctx_tok(D), _lat_tok(D),
                  pl.BlockSpec((None, None, 6, D), lambda i: (li, _mod_row(i, TM_IN), 0, 0)),
                  par(1, D), par(1, D), par(D, D), par(N_EXP, D), par(N_EXP, 1),
                  pl.BlockSpec((TM_IN, TM_IN), lambda i: (0, 0))],
        out_specs=[tok(D), tok(D), tok_t, tok_t,
                   pl.BlockSpec((nb, N_EXP, 128), lambda i: (i, 0, 0))],
        out_shape=[jax.ShapeDtypeStruct((T, D), F32),
                   jax.ShapeDtypeStruct((T, D), BF16),
                   jax.ShapeDtypeStruct((N_EXP, T), F32),
                   jax.ShapeDtypeStruct((N_EXP, T), F32),
                   jax.ShapeDtypeStruct((N_BLK, N_EXP, 128), F32)],
        compiler_params=_cparams(("parallel",)),
        name=f"merge_l{li}",
    )(oabp, oabs, ocp, ocs, xp, xs, modv, g_post, g_pre, w_out_bf, w_router_t, b_router_col, tri)


def _rank_matrix():
    t = np.arange(TM_IN)
    m = (t[:, None] < t[None, :]) & (t[:, None] // B_SORT == t[None, :] // B_SORT)
    return jnp.asarray(m, BF16)


def _moe_tables(pcu_f):
    pcu = pcu_f[:, :, 0].astype(jnp.int32)
    incl = jnp.cumsum(pcu, axis=1)
    uo = incl - pcu
    tot = jnp.sum(pcu, axis=0)
    gt = (tot + UPT - 1) // UPT
    steps = (gt + TPS - 1) // TPS
    sincl = jnp.cumsum(steps)
    sstart = sincl - steps
    gstart_u = sstart * UPS
    bstart = gstart_u[None, :] + jnp.cumsum(pcu, axis=0) - pcu
    u = jnp.arange(UPB, dtype=jnp.int32)
    eou = jnp.sum((u[None, :, None] >= incl[:, None, :]).astype(jnp.int32), axis=-1)
    valid = eou < N_EXP
    onehot = eou[:, :, None] == jnp.arange(N_EXP, dtype=jnp.int32)[None, None, :]
    run_u0 = jnp.sum(jnp.where(onehot, uo[:, None, :], 0), axis=-1)
    run_g0 = jnp.sum(jnp.where(onehot, bstart[:, None, :], 0), axis=-1)
    dst = run_g0 + u[None, :] - run_u0
    spare0 = SPARE_UNIT0 + (jnp.arange(N_BLK, dtype=jnp.int32) % 2)[:, None] * (UPB + ZPB)
    dst_sort = jnp.where(valid, dst, spare0 + u[None, :])
    dst_comb = jnp.where(valid, dst, ZERO_UNIT)
    loc = jnp.where(valid, (u[None, :] - run_u0) * UNIT, -(1 << 20))
    k = jnp.arange(UPT - 1, dtype=jnp.int32)
    ztail = gt * UPT - tot
    zdst = jnp.where(k[None, :] < ztail[:, None], (gstart_u + tot)[:, None] + k[None, :], -1).reshape(N_BLK, ZPB)
    zdst = jnp.where(zdst >= 0, zdst, spare0 + UPB + jnp.arange(ZPB, dtype=jnp.int32)[None, :])
    step = jnp.arange(NS_MAX, dtype=jnp.int32)
    te = jnp.minimum(jnp.sum((step[:, None] >= sincl[None, :]).astype(jnp.int32), axis=-1), N_EXP - 1)
    mine = te[:, None] == jnp.arange(N_EXP, dtype=jnp.int32)[None, :]
    left = jnp.sum(jnp.where(mine, (gt - TPS * (step[:, None] - sstart[None, :])), 0), axis=-1)
    ntile = jnp.clip(left, 0, TPS)
    return (jnp.minimum(eou, N_EXP - 1), loc, dst_sort, dst_comb, zdst, te, ntile, sincl[-1:])


def _unit_rows(u):
    if isinstance(u, int):
        return pl.ds(u * UNIT, UNIT)
    return pl.ds(pl.multiple_of(u * UNIT, UNIT), UNIT)


def _sort_body(eou_ref, loc_ref, dst_ref, zdst_ref, h_ref, rank_ref, gates_ref, xs_hbm, p_ref,
               xs_scr, zero_scr, sem):
    b = pl.program_id(0)
    slot = b % 2

    def copy(src, d, s):
        return pltpu.make_async_copy(src, xs_hbm.at[_unit_rows(d)], sem.at[s])

    def wait_step(s):
        lax.fori_loop(0, UPB + ZPB, lambda k, c: (copy(zero_scr, 0, s).wait(), c)[1], 0, unroll=True)

    @pl.when(b == 0)
    def _():
        zero_scr[...] = jnp.zeros((UNIT, XS_W), BF16)

    @pl.when(b >= 2)
    def _():
        wait_step(slot)

    base = lax.broadcasted_iota(jnp.int32, (UNIT, B_SORT), 0)

    def onehot_unit(u, carry):
        match = rank_ref[pl.ds(eou_ref[b, u], 1), :] == (base + loc_ref[b, u]).astype(F32)
        p_ref[_unit_rows(u), :] = jnp.where(match, 1.0, 0.0).astype(BF16)
        return carry

    lax.fori_loop(0, UPB, onehot_unit, 0, unroll=UNROLL)
    g = gates_ref[...]
    g_hi = g.astype(BF16)
    g_hl = jnp.concatenate([g_hi, (g - g_hi.astype(F32)).astype(BF16)], axis=0)
    h = h_ref[...]
    for c in range(RB // 512):
        rows = slice(c * 512, (c + 1) * 512)
        xs_scr[slot, rows, 0:D] = _dot(p_ref[rows, :], h).astype(BF16)
        xs_scr[slot, rows, D:XS_W] = _dot_nt(p_ref[rows, :], g_hl).astype(BF16)

    def start_run(u, carry):
        copy(xs_scr.at[slot, _unit_rows(u)], dst_ref[b, u], slot).start()
        return carry

    def start_zero(j, carry):
        copy(zero_scr, zdst_ref[b, j], slot).start()
        return carry

    lax.fori_loop(0, UPB, start_run, 0, unroll=UNROLL)
    lax.fori_loop(0, ZPB, start_zero, 0, unroll=True)

    @pl.when(b == N_BLK - 1)
    def _():
        wait_step(1 - slot)
        wait_step(slot)


def _moe_sort(li, eou, loc, dst, zdst, h, rank_t, gates_t):
    blk_t = pl.BlockSpec((N_EXP, B_SORT), lambda b, *_: (0, b))
    return pl.pallas_call(
        _sort_body,
        grid_spec=pltpu.PrefetchScalarGridSpec(
            num_scalar_prefetch=4,
            grid=(N_BLK,),
            in_specs=[pl.BlockSpec((B_SORT, D), lambda b, *_: (b, 0)), blk_t, blk_t],
            out_specs=[pl.BlockSpec(memory_space=pl.ANY),
                       pl.BlockSpec((RB, B_SORT), lambda b, *_: (b, 0))],
            scratch_shapes=[pltpu.VMEM((2, RB, XS_W), BF16),
                            pltpu.VMEM((UNIT, XS_W), BF16),
                            pltpu.SemaphoreType.DMA((2,))]),
        out_shape=[jax.ShapeDtypeStruct((R_SORT, XS_W), BF16),
                   jax.ShapeDtypeStruct((N_BLK * RB, B_SORT), BF16)],
        compiler_params=_cparams(("arbitrary",)),
        name=f"moe_sort_l{li}",
    )(eou, loc, dst, zdst, h, rank_t, gates_t)


def _experts_body(te_ref, ntile_ref, ns_ref, xs_hbm, wg_ref, wu_ref, wd_ref, os_hbm,
                  xs_buf, o_buf, wg_s, wu_s, wd_s, sem_in, sem_out):
    i = pl.program_id(0)
    ns = ns_ref[0]
    slot = i % 2

    def tiles_of(step):
        return jnp.where((step >= 0) & (step < ns), ntile_ref[jnp.clip(step, 0, NS_MAX - 1)], 0)

    def tile_rows(step, t):
        return pl.ds(pl.multiple_of(step * STEP_ROWS + t * TMG, TMG), TMG)

    def buf_rows(t):
        return pl.ds(pl.multiple_of(t * TMG, TMG), TMG)

    def copy_in(step, t, s):
        return pltpu.make_async_copy(xs_hbm.at[tile_rows(step, t)], xs_buf.at[s, buf_rows(t)], sem_in.at[s])

    def copy_out(step, t, s):
        return pltpu.make_async_copy(o_buf.at[s, buf_rows(t)], os_hbm.at[tile_rows(step, t)], sem_out.at[s])

    def for_tiles(step, fn):
        lax.fori_loop(0, tiles_of(step), lambda t, c: (fn(t), c)[1], 0)

    @pl.when(i == 0)
    def _():
        for_tiles(0, lambda t: copy_in(0, t, 0).start())

    for_tiles(i + 1, lambda t: copy_in(i + 1, t, 1 - slot).start())
    @pl.when(i <= ns)
    def _():
        for_tiles(i - 2, lambda t: copy_out(i - 2, t, slot).wait())

    @pl.when(i == ns)
    def _():
        for_tiles(i - 1, lambda t: copy_out(i - 1, t, 1 - slot).wait())

    ntile = tiles_of(i)
    e = te_ref[jnp.minimum(i, NS_MAX - 1)]
    new_expert = (i == 0) | (e != te_ref[jnp.maximum(i - 1, 0)])

    @pl.when((ntile > 0) & new_expert)
    def _():
        wg_s[...] = wg_ref[...].astype(BF16)
        wu_s[...] = wu_ref[...].astype(BF16)
        wd_s[...] = wd_ref[...].astype(BF16)

    for_tiles(i, lambda t: copy_in(i, t, slot).wait())
    for k in range(1, TPS + 1):
        rows = k * TMG

        @pl.when(ntile == k)
        def _():
            x = xs_buf[slot, 0:rows, 0:D]
            lane = lax.broadcasted_iota(jnp.int32, (rows, 128), 1)
            mine = (lane == e) | (lane == e + N_EXP)
            gate = jnp.sum(jnp.where(mine, xs_buf[slot, 0:rows, D:XS_W].astype(F32), 0.0),
                           axis=1, keepdims=True)
            hid = _silu(_dot(x, wg_s[...])) * _dot(x, wu_s[...]) * gate
            o_buf[slot, 0:rows, :] = _dot(hid.astype(BF16), wd_s[...]).astype(BF16)

    for_tiles(i, lambda t: copy_out(i, t, slot).start())

    @pl.when(i == NS_MAX)
    def _():
        o_buf[slot, 0:UNIT, :] = jnp.zeros((UNIT, D), BF16)
        zero = pltpu.make_async_copy(o_buf.at[slot, 0:UNIT], os_hbm.at[_unit_rows(ZERO_UNIT)], sem_out.at[slot])
        zero.start()
        zero.wait()


def _moe_experts(li, te, ntile, ns, xs, w_gate, w_up, w_down):
    def wspec(*shape):
        return pl.BlockSpec((None, None) + shape,
                            lambda i, te, ntile, ns: (li, te[jnp.minimum(i, ns[0] - 1)]) + (0,) * len(shape))

    return pl.pallas_call(
        _experts_body,
        grid_spec=pltpu.PrefetchScalarGridSpec(
            num_scalar_prefetch=3,
            grid=(NS_MAX + 1,),
            in_specs=[pl.BlockSpec(memory_space=pl.ANY),
                      wspec(D, EXP_DIM), wspec(D, EXP_DIM), wspec(EXP_DIM, D)],
            out_specs=pl.BlockSpec(memory_space=pl.ANY),
            scratch_shapes=[pltpu.VMEM((2, STEP_ROWS, XS_W), BF16),
                            pltpu.VMEM((2, STEP_ROWS, D), BF16),
                            pltpu.VMEM((D, EXP_DIM), BF16),
                            pltpu.VMEM((D, EXP_DIM), BF16),
                            pltpu.VMEM((EXP_DIM, D), BF16),
                            pltpu.SemaphoreType.DMA((2,)),
                            pltpu.SemaphoreType.DMA((2,))]),
        out_shape=jax.ShapeDtypeStruct((R_OUT, D), BF16),
        compiler_params=_cparams(("arbitrary",)),
        name=f"moe_experts_l{li}",
    )(te, ntile, ns, xs, w_gate, w_up, w_down)


def _combine_body(dst_ref, h_ref, p_ref, y_ref, mod_ref, gpost_ref,
                  wsg_ref, wsu_ref, wsd_ref, os_hbm, op_ref, ol_ref, os_scr, sem):
    b = pl.program_id(0)
    slot = b % 2

    def copy(d, u, s):
        return pltpu.make_async_copy(os_hbm.at[_unit_rows(d)], os_scr.at[s, _unit_rows(u)], sem.at[s])

    def fetch(blk, s):
        lax.fori_loop(0, UPB, lambda u, c: (copy(dst_ref[blk, u], u, s).start(), c)[1], 0, unroll=UNROLL)

    @pl.when(b == 0)
    def _():
        fetch(0, 0)

    @pl.when(b + 1 < N_BLK)
    def _():
        fetch(b + 1, 1 - slot)

    h = h_ref[...]
    shared = _dot((_silu(_dot(h, wsg_ref[...])) * _dot(h, wsu_ref[...])).astype(BF16), wsd_ref[...])
    lax.fori_loop(0, UPB, lambda k, c: (copy(0, 0, slot).wait(), c)[1], 0, unroll=True)
    routed = lax.dot_general(p_ref[...], os_scr[slot], (((0,), (0,)), ((), ())),
                             preferred_element_type=F32)
    out = y_ref[...] + mod_ref[5:6, :] * _rms(routed + shared, gpost_ref[...])

    @pl.when(b < T_P // B_SORT)
    def _():
        op_ref[...] = out

    @pl.when(b >= T_P // B_SORT)
    def _():
        ol_ref[...] = out


def _moe_combine(li, dst, h, p, y, modv, g_post, wsg_bf, wsu_bf, wsd_bf, os):
    nbp = T_P // B_SORT
    def tok(w):
        return pl.BlockSpec((B_SORT, w), lambda b, *_: (b, 0))

    def par(*shape):
        return pl.BlockSpec((None,) + shape, lambda b, *_: (li,) + (0,) * len(shape))

    return pl.pallas_call(
        _combine_body,
        grid_spec=pltpu.PrefetchScalarGridSpec(
            num_scalar_prefetch=1,
            grid=(N_BLK,),
            in_specs=[tok(D), pl.BlockSpec((RB, B_SORT), lambda b, *_: (b, 0)), tok(D),
                      pl.BlockSpec((None, None, 6, D), lambda b, *_: (li, _mod_row(b, B_SORT), 0, 0)),
                      par(1, D), par(D, EXP_DIM), par(D, EXP_DIM), par(EXP_DIM, D),
                      pl.BlockSpec(memory_space=pl.ANY)],
            out_specs=[pl.BlockSpec((B_SORT, D), lambda b, *_: (jnp.minimum(b, nbp - 1), 0)),
                       pl.BlockSpec((B_SORT, D), lambda b, *_: (jnp.maximum(b - nbp, 0), 0))],
            scratch_shapes=[pltpu.VMEM((2, RB, D), BF16),
                            pltpu.SemaphoreType.DMA((2,))]),
        out_shape=[jax.ShapeDtypeStruct((T_P, D), F32), jax.ShapeDtypeStruct((T_S, D), F32)],
        compiler_params=_cparams(("arbitrary",)),
        name=f"moe_combine_l{li}",
    )(dst, h, p, y, modv, g_post, wsg_bf, wsu_bf, wsd_bf, os)


def _block_diag(w):
    out = jnp.zeros((DEPTH, POOL_W, POOL_W), F32)
    for g in range(len(POOL_WINDOWS)):
        out = out.at[:, g * POOL_CH:(g + 1) * POOL_CH, g * POOL_CH:(g + 1) * POOL_CH].set(w[:, g])
    return out


def kernel(x_prompt, x_sample, cache_diff_k, cache_diff_v, cache_na_k, cache_na_v, c, c_ctx,
           w_ada, b_ada, g_pre_mix, g_post_mix, g_pre_ffn, g_post_ffn, w_in, w_out,
           diff_lambda, g_diff, na_bias, pool_w, pool_scale, w_router, b_router,
           w_gate, w_up, w_down, ws_gate, ws_up, ws_down):
    xp, xs = x_prompt.reshape(T_P, D), x_sample.reshape(T_S, D)
    cvec = jnp.concatenate([c_ctx[None, :], c, jnp.zeros((3, D), F32)], axis=0)
    modv = _modulation(cvec, w_ada, b_ada)[:, :1 + NB_S].reshape(DEPTH, 1 + NB_S, 6, D)

    cos_t, sin_t = _rope_tables()
    w_in_bf = w_in.astype(BF16)
    w_out_bf = w_out.astype(BF16)
    wsg_bf, wsu_bf, wsd_bf = ws_gate.astype(BF16), ws_up.astype(BF16), ws_down.astype(BF16)
    pool_bd = _block_diag(pool_w)
    row = lambda a: a.reshape(DEPTH, 1, a.shape[-1])
    g_pre_mix, g_post_mix, g_pre_ffn, g_post_ffn = map(row, (g_pre_mix, g_post_mix, g_pre_ffn, g_post_ffn))
    g_diff, pool_scale = row(g_diff), row(pool_scale)
    w_router_t = jnp.swapaxes(w_router, 1, 2)
    b_router_col = b_router.reshape(DEPTH, N_EXP, 1)
    tri = _rank_matrix()
    ck, cv, cnk, cnv = cache_diff_k, cache_diff_v, cache_na_k, cache_na_v

    strips = _na_bias_strips(na_bias)

    caches = ()
    for li in range(DEPTH):
        lam_init = 0.8 - 0.6 * math.exp(-0.3 * li)
        qa, qb, up, ka, va, kb, vb, *caches = _in_proj(li, xp, xs, modv, g_pre_mix, w_in_bf, cos_t, sin_t,
                                                      caches)
        ocp = _pool_call(li, up, pool_bd, pool_scale, N_P, NB_P, 0, f"pool_ctx_l{li}")
        ocs = _pool_call(li, up, pool_bd, pool_scale, N_S, NB_S, T_P // N_S, f"pool_lat_l{li}")
        oabp = _attn_ctx(li, diff_lambda, g_diff, qa, caches[0], caches[1], qb, caches[2], caches[3], lam_init)
        oabs = _attn_lat(li, diff_lambda, g_diff, qa, ka, va, ck, cv, qb, kb, vb, cnk, cnv, strips, lam_init)
        y, h, gates_t, rank_t, pcu = _merge(li, oabp, oabs, ocp, ocs, xp, xs, modv, g_post_mix, g_pre_ffn,
                                            w_out_bf, w_router_t, b_router_col, tri)
        eou, loc, dst_sort, dst_comb, zdst, te, ntile, ns = _moe_tables(pcu)
        srt, p = _moe_sort(li, eou, loc, dst_sort, zdst, h, rank_t, gates_t)
        os = _moe_experts(li, te, ntile, ns, srt, w_gate, w_up, w_down)
        xp, xs = _moe_combine(li, dst_comb, h, p, y, modv, g_post_ffn, wsg_bf, wsu_bf, wsd_bf, os)

    new_dk, new_dv, new_nk, new_nv = caches
    return (xp.reshape(NB_P, N_P, D), xs.reshape(NB_S, N_S, D),
            new_dk.reshape(NB_P, DEPTH, N_P, DA_H, 2 * DA_QK), new_dv.reshape(NB_P, DEPTH, N_P, DA_H, DA_V),
            new_nk.reshape(NB_P, DEPTH, N_P, NA_H, NA_D), new_nv.reshape(NB_P, DEPTH, N_P, NA_H, NA_D))
```

```python
import functools
import math

import numpy as np
import jax
import jax.numpy as jnp
from jax import lax
from jax.experimental import pallas as pl
from jax.experimental.pallas import tpu as pltpu

F32 = jnp.float32
BF16 = jnp.bfloat16

D = 1024
DEPTH = 2
NB_P, N_P = 16, 256
NB_S, N_S = 4, 1024
T_P = NB_P * N_P
T_S = NB_S * N_S
T = T_P + T_S
PAST = 512
GRID_W = 64
DA_H, DA_QK, DA_V = 4, 64, 128
DA_W = DA_H * DA_V
NA_H, NA_D = 4, 64
NA_W = NA_H * NA_D
NA_ROWS, NA_COLS = 8, 16
POOL_WINDOWS = (2, 4, 8, 16)
POOL_CH = 64
POOL_W = 256
IN_COLS = 3 * DA_W + 3 * NA_W + POOL_W
AB_W = DA_W + NA_W
N_EXP, TOP_K, N_GRP, TOPK_GRP = 64, 8, 8, 4
EXP_DIM = 256
ROUTED_SCALE = 2.5
EPS = 1e-6
NEG = -1e30
ROPE_THETA = 10000.0

VMEM_LIMIT = 50 * 1024 * 1024

TM_IN = 512
NT_P = T_P // TM_IN
TQ = 256

B_SORT = 256
UNIT = 16
N_BLK = T // B_SORT
RUN_PAD_ROWS = TOP_K * B_SORT + N_EXP * (UNIT - 1)
RB = -(-RUN_PAD_ROWS // 256) * 256
UPB = RB // UNIT
TMG = 256
UPT = TMG // UNIT
NT_MAX = (N_BLK * RUN_PAD_ROWS) // TMG + N_EXP
XS_W = D + 128
EXP_PER_BLK = N_EXP // N_BLK
ZPB = EXP_PER_BLK * (UPT - 1)
TPS = 6
STEP_ROWS = TPS * TMG
NS_MAX = NT_MAX // TPS + (N_EXP * (TPS - 1)) // TPS
ZERO_UNIT = NT_MAX * UPT
SPARE_UNIT0 = ZERO_UNIT + UPT
R_OUT = (NT_MAX + 1) * TMG
R_SORT = R_OUT + -(-2 * (UPB + ZPB) // UPT) * TMG
UNROLL = 8


def _cparams(sem):
    return pltpu.CompilerParams(dimension_semantics=sem, vmem_limit_bytes=VMEM_LIMIT)


def _rms(x, g):
    return x * lax.rsqrt(jnp.mean(x * x, axis=-1, keepdims=True) + EPS) * g


def _dot(a, b):
    return jnp.dot(a, b, preferred_element_type=F32)


def _dot_nt(a, b):
    return lax.dot_general(a, b, (((1,), (1,)), ((), ())), preferred_element_type=F32)


def _silu(x):
    return x / (1.0 + jnp.exp(-x))


def _mod_row(i, tm):
    off = i * tm - T_P
    return jnp.where(off >= 0, 1 + jnp.maximum(off, 0) // N_S, 0)


def _mod_body(c_ref, w_ref, b_ref, o_ref):
    c = c_ref[...]
    o_ref[...] = jnp.dot(_silu(c), w_ref[...], precision=lax.Precision.HIGHEST,
                         preferred_element_type=F32) + b_ref[...]


def _modulation(cvec, w_ada, b_ada):
    tn = 1536
    return pl.pallas_call(
        _mod_body,
        grid=(DEPTH, 6 * D // tn),
        in_specs=[pl.BlockSpec((8, D), lambda l, j: (0, 0)),
                  pl.BlockSpec((None, D, tn), lambda l, j: (l, 0, j)),
                  pl.BlockSpec((None, 1, tn), lambda l, j: (l, 0, j))],
        out_specs=pl.BlockSpec((None, 8, tn), lambda l, j: (l, 0, j)),
        out_shape=jax.ShapeDtypeStruct((DEPTH, 8, 6 * D), F32),
        compiler_params=_cparams(("parallel", "parallel")),
        name="modulation",
    )(cvec, w_ada, b_ada.reshape(DEPTH, 1, 6 * D))


def _rope_tables():
    nf = DA_QK // 4
    t = np.arange(N_S)
    pos = np.stack([t // GRID_W, t % GRID_W], axis=-1).astype(np.float32)
    inv = np.power(np.float32(ROPE_THETA), -np.arange(nf, dtype=np.float32) / nf)
    ang = pos[:, :, None] * inv
    cos = np.cos(ang)
    sin = np.sin(ang)
    cos64 = np.concatenate([cos[:, 0], cos[:, 0], cos[:, 1], cos[:, 1]], axis=-1)
    sin64 = np.concatenate([-sin[:, 0], sin[:, 0], -sin[:, 1], sin[:, 1]], axis=-1)
    reps = DA_W // DA_QK
    cos_t = np.concatenate([np.tile(cos64, (1, reps)), np.ones((TM_IN, DA_W), np.float32)], axis=0)
    sin_t = np.concatenate([np.tile(sin64, (1, reps)), np.zeros((TM_IN, DA_W), np.float32)], axis=0)
    return jnp.asarray(cos_t, F32), jnp.asarray(sin_t, F32)


def _tok_select(i, ctx_ref, lat_ref):
    return jnp.where(i < NT_P, ctx_ref[...], lat_ref[...])


def _in_proj_body(*refs):
    (xp_ref, xs_ref, mod_ref, g_ref, w_ref, cos_ref, sin_ref) = refs[:7]
    (qa_ref, qb_ref, up_ref, ka_ref, va_ref, kb_ref, vb_ref, ck_ref, cv_ref, cnk_ref, cnv_ref) = refs[-11:]
    i = pl.program_id(0)
    x = _tok_select(i, xp_ref, xs_ref)
    h = (_rms(x, g_ref[...]) * (1.0 + mod_ref[1:2, :]) + mod_ref[0:1, :]).astype(BF16)
    cos = cos_ref[...]
    sin = sin_ref[...]
    lane = lax.broadcasted_iota(jnp.int32, (TM_IN, DA_W), 1)
    first = (lane % 32) < 16

    def proj(lo, hi):
        return _dot(h, w_ref[:, lo:hi])

    def rope(t):
        swapped = jnp.where(first, pltpu.roll(t, DA_W - 16, 1), pltpu.roll(t, 16, 1))
        return t * cos + swapped * sin

    o = 0
    qa_ref[...] = (rope(proj(o, o + DA_W)) * (DA_QK ** -0.5)).astype(BF16)
    o += DA_W
    ka = rope(proj(o, o + DA_W))
    o += DA_W
    va = proj(o, o + DA_W)
    o += DA_W
    qb_ref[...] = (proj(o, o + NA_W) * (NA_D ** -0.5)).astype(BF16)
    o += NA_W
    kb = proj(o, o + NA_W)
    o += NA_W
    vb = proj(o, o + NA_W)
    o += NA_W
    up_ref[...] = proj(o, o + POOL_W)

    @pl.when(i < NT_P)
    def _():
        for ref, val in ((ck_ref, ka), (cv_ref, va), (cnk_ref, kb), (cnv_ref, vb)):
            ref[...] = val.reshape(ref.shape)

    @pl.when(i >= NT_P)
    def _():
        for ref, val in ((ka_ref, ka), (va_ref, va), (kb_ref, kb), (vb_ref, vb)):
            ref[...] = val.astype(BF16)


def _ctx_tok(w):
    return pl.BlockSpec((TM_IN, w), lambda i: (jnp.minimum(i, NT_P - 1), 0))


def _lat_tok(w):
    return pl.BlockSpec((TM_IN, w), lambda i: (jnp.maximum(i - NT_P, 0), 0))


def _in_proj(li, xp, xs, modv, g_pre, w_in_bf, cos_t, sin_t, caches):
    n_pos_blk = N_S // TM_IN

    def pos_blk(i):
        return jnp.where(i >= NT_P, jnp.maximum(i - NT_P, 0) % n_pos_blk, n_pos_blk)

    def tok(w):
        return pl.BlockSpec((TM_IN, w), lambda i: (i, 0))

    def cache(w):
        return pl.BlockSpec((TM_IN // N_P, None, N_P, w), lambda i: (jnp.minimum(i, NT_P - 1), li, 0, 0))

    widths = (DA_W, DA_W, NA_W, NA_W)
    n_in = 7
    return pl.pallas_call(
        _in_proj_body,
        grid=(T // TM_IN,),
        in_specs=[_ctx_tok(D), _lat_tok(D),
                  pl.BlockSpec((None, None, 6, D), lambda i: (li, _mod_row(i, TM_IN), 0, 0)),
                  pl.BlockSpec((None, 1, D), lambda i: (li, 0, 0)),
                  pl.BlockSpec((None, D, IN_COLS), lambda i: (li, 0, 0)),
                  pl.BlockSpec((TM_IN, DA_W), lambda i: (pos_blk(i), 0)),
                  pl.BlockSpec((TM_IN, DA_W), lambda i: (pos_blk(i), 0))]
                 + [pl.BlockSpec(memory_space=pl.ANY)] * len(caches),
        out_specs=[tok(DA_W), tok(NA_W), tok(POOL_W)] + [_lat_tok(w) for w in widths]
                  + [cache(w) for w in widths],
        out_shape=[jax.ShapeDtypeStruct((T, DA_W), BF16),
                   jax.ShapeDtypeStruct((T, NA_W), BF16),
                   jax.ShapeDtypeStruct((T, POOL_W), F32)]
                  + [jax.ShapeDtypeStruct((T_S, w), BF16) for w in widths]
                  + [jax.ShapeDtypeStruct((NB_P, DEPTH, N_P, w), F32) for w in widths],
        input_output_aliases={n_in + k: 7 + k for k in range(len(caches))},
        compiler_params=_cparams(("arbitrary",)),
        name=f"in_proj_l{li}",
    )(xp, xs, modv, g_pre, w_in_bf, cos_t, sin_t, *caches)


def _pool_body(u_ref, w_ref, s_ref, o_ref, *, n):
    u = u_ref[...]
    row = lax.broadcasted_iota(jnp.int32, (n, POOL_W), 0)
    lane = lax.broadcasted_iota(jnp.int32, (n, POOL_W), 1)

    def shift_dn(a, k):
        return jnp.where(row >= k, pltpu.roll(a, k, 0), 0.0)

    def shift_up(a, k):
        return jnp.where(row < n - k, pltpu.roll(a, n - k, 0), 0.0)

    fwd = u
    bwd = shift_dn(u, 1)
    mean = jnp.zeros_like(u)
    k = 1
    for gi, w in enumerate(POOL_WINDOWS):
        while k < w // 2:
            fwd = fwd + shift_up(fwd, k)
            bwd = bwd + shift_dn(bwd, k)
            k *= 2
        cnt = (jnp.minimum(row + w // 2, n) - jnp.maximum(row - w // 2, 0)).astype(F32)
        in_group = (lane >= gi * POOL_CH) & (lane < (gi + 1) * POOL_CH)
        mean = jnp.where(in_group, (fwd + bwd) / cnt, mean)
    pooled = mean - u
    mixed = jnp.dot(pooled, w_ref[...], precision=lax.Precision.HIGHEST, preferred_element_type=F32)
    o_ref[...] = (mixed * s_ref[...]).astype(BF16)


def _pool_call(li, up, w_bd, scale, n, nseg, blk0, name):
    return pl.pallas_call(
        functools.partial(_pool_body, n=n),
        grid=(nseg,),
        in_specs=[pl.BlockSpec((n, POOL_W), lambda i: (blk0 + i, 0)),
                  pl.BlockSpec((None, POOL_W, POOL_W), lambda i: (li, 0, 0)),
                  pl.BlockSpec((None, 1, POOL_W), lambda i: (li, 0, 0))],
        out_specs=pl.BlockSpec((n, POOL_W), lambda i: (i, 0)),
        out_shape=jax.ShapeDtypeStruct((nseg * n, POOL_W), BF16),
        compiler_params=_cparams(("parallel",)),
        name=name,
    )(up, w_bd, scale)


def _lambda(lp, lam_init):
    a = jnp.sum(lp[0:1, :] * lp[1:2, :], axis=1, keepdims=True)
    b = jnp.sum(lp[2:3, :] * lp[3:4, :], axis=1, keepdims=True)
    return jnp.exp(a) - jnp.exp(b) + lam_init


def _softmax_av(scores, vs):
    m = functools.reduce(jnp.maximum, [jnp.max(s, axis=-1, keepdims=True) for s in scores])
    es = [jnp.exp(s - m) for s in scores]
    den = functools.reduce(jnp.add, [jnp.sum(e, axis=-1, keepdims=True) for e in es])
    o = functools.reduce(jnp.add, [_dot(e.astype(BF16), v) for e, v in zip(es, vs)])
    return o * (1.0 / den)


def _diff_head(q, ks, vs, lam, gd, lam_init):
    o1 = _softmax_av([_dot_nt(q[:, :DA_QK], k[:, :DA_QK]) for k in ks], vs)
    o2 = _softmax_av([_dot_nt(q[:, DA_QK:], k[:, DA_QK:]) for k in ks], vs)
    return _rms(o1 - lam * o2, gd) * (1.0 - lam_init)


def _plain_head(q, ks, vs, biases):
    scores = []
    for k, bias in zip(ks, biases):
        s = _dot_nt(q, k)
        scores.append(s if bias is None else s + bias)
    return _softmax_av(scores, vs)


def _attn_ctx_body(lam_ref, gd_ref, qa_ref, ka_ref, va_ref, qb_ref, kb_ref, vb_ref, o_ref, *, lam_init):
    lam = _lambda(lam_ref[...], lam_init)
    gd = gd_ref[...]
    for h in range(DA_H):
        sl = slice(h * DA_V, (h + 1) * DA_V)
        o = _diff_head(qa_ref[:, sl], [ka_ref[:, sl].astype(BF16)], [va_ref[:, sl].astype(BF16)],
                       lam, gd, lam_init)
        o_ref[:, sl] = o.astype(BF16)
    for h in range(NA_H):
        sl = slice(h * NA_D, (h + 1) * NA_D)
        o = _plain_head(qb_ref[:, sl], [kb_ref[:, sl].astype(BF16)], [vb_ref[:, sl].astype(BF16)], [None])
        o_ref[:, DA_W + h * NA_D:DA_W + (h + 1) * NA_D] = o.astype(BF16)


def _attn_ctx(li, lam_p, g_diff, qa, ka, va, qb, kb, vb, lam_init):
    def blk(w):
        return pl.BlockSpec((N_P, w), lambda b: (b, 0))

    def cblk(w):
        return pl.BlockSpec((None, None, N_P, w), lambda b: (b, li, 0, 0))

    return pl.pallas_call(
        functools.partial(_attn_ctx_body, lam_init=lam_init),
        grid=(NB_P,),
        in_specs=[pl.BlockSpec((None, 4, DA_QK), lambda b: (li, 0, 0)),
                  pl.BlockSpec((None, 1, DA_V), lambda b: (li, 0, 0)),
                  blk(DA_W), cblk(DA_W), cblk(DA_W), blk(NA_W), cblk(NA_W), cblk(NA_W)],
        out_specs=pl.BlockSpec((N_P, AB_W), lambda b: (b, 0)),
        out_shape=jax.ShapeDtypeStruct((T_P, AB_W), BF16),
        compiler_params=_cparams(("parallel",)),
        name=f"attn_ctx_l{li}",
    )(lam_p, g_diff, qa, ka, va, qb, kb, vb)


G_ROWS = N_S // GRID_W
QR_PER_BLK = TQ // GRID_W


def _fill_na_bias(j, strip_ref, bias_scr):
    for jj in range(N_S // TQ):
        @pl.when(j == jj)
        def _():
            for rr in range(QR_PER_BLK):
                qr = jj * QR_PER_BLK + rr
                rs = min(max(qr - NA_ROWS // 2, 0), G_ROWS - NA_ROWS)
                lo, hi = rs * GRID_W, (rs + NA_ROWS) * GRID_W
                a0 = (rs - qr + NA_ROWS - 1) * GRID_W
                q = slice(rr * GRID_W, (rr + 1) * GRID_W)
                for h in range(NA_H):
                    if lo > 0:
                        bias_scr[h, q, 0:lo] = jnp.full((GRID_W, lo), NEG, F32)
                    bias_scr[h, q, lo:hi] = strip_ref[h, :, a0:a0 + NA_ROWS * GRID_W]
                    if hi < N_S:
                        bias_scr[h, q, hi:N_S] = jnp.full((GRID_W, N_S - hi), NEG, F32)


def _attn_lat_body(lam_ref, gd_ref, qa_ref, ka_ref, va_ref, ck_ref, cv_ref,
                   qb_ref, kb_ref, vb_ref, cnk_ref, cnv_ref, strip_ref, o_ref, bias_scr, *, lam_init):
    _fill_na_bias(pl.program_id(1), strip_ref, bias_scr)
    lam = _lambda(lam_ref[...], lam_init)
    gd = gd_ref[...]
    for h in range(DA_H):
        sl = slice(h * DA_V, (h + 1) * DA_V)
        ks = [ka_ref[:, sl], ck_ref[:, sl].astype(BF16)]
        vs = [va_ref[:, sl], cv_ref[:, sl].astype(BF16)]
        o_ref[:, sl] = _diff_head(qa_ref[:, sl], ks, vs, lam, gd, lam_init).astype(BF16)
    for h in range(NA_H):
        sl = slice(h * NA_D, (h + 1) * NA_D)
        ks = [kb_ref[:, sl], cnk_ref[:, sl].astype(BF16)]
        vs = [vb_ref[:, sl], cnv_ref[:, sl].astype(BF16)]
        o = _plain_head(qb_ref[:, sl], ks, vs, [bias_scr[h], None])
        o_ref[:, DA_W + h * NA_D:DA_W + (h + 1) * NA_D] = o.astype(BF16)


def _attn_lat(li, lam_p, g_diff, qa, ka, va, ck, cv, qb, kb, vb, cnk, cnv, strip, lam_init):
    nq = N_S // TQ
    q0 = T_P // TQ

    def qblk(w):
        return pl.BlockSpec((TQ, w), lambda b, j: (q0 + b * nq + j, 0))

    def kvblk(w):
        return pl.BlockSpec((N_S, w), lambda b, j: (b, 0))

    def cblk(w):
        return pl.BlockSpec((None, None, PAST, w), lambda b, j: (b, li, 0, 0))

    return pl.pallas_call(
        functools.partial(_attn_lat_body, lam_init=lam_init),
        grid=(NB_S, nq),
        in_specs=[pl.BlockSpec((None, 4, DA_QK), lambda b, j: (li, 0, 0)),
                  pl.BlockSpec((None, 1, DA_V), lambda b, j: (li, 0, 0)),
                  qblk(DA_W), kvblk(DA_W), kvblk(DA_W), cblk(DA_W), cblk(DA_W),
                  qblk(NA_W), kvblk(NA_W), kvblk(NA_W), cblk(NA_W), cblk(NA_W),
                  pl.BlockSpec((None, NA_H, GRID_W, (2 * NA_ROWS - 1) * GRID_W), lambda b, j: (li, 0, 0, 0))],
        out_specs=pl.BlockSpec((TQ, AB_W), lambda b, j: (b * nq + j, 0)),
        out_shape=jax.ShapeDtypeStruct((T_S, AB_W), BF16),
        scratch_shapes=[pltpu.VMEM((NA_H, TQ, N_S), F32)],
        compiler_params=_cparams(("parallel", "parallel")),
        name=f"attn_lat_l{li}",
    )(lam_p, g_diff, qa, ka, va, ck, cv, qb, kb, vb, cnk, cnv, strip)


def _na_bias_strips(table):
    w = np.arange(GRID_W)
    cs = np.clip(w - NA_COLS // 2, 0, GRID_W - NA_COLS)
    col_ok = (w[None, :] >= cs[:, None]) & (w[None, :] < cs[:, None] + NA_COLS)
    sel_c = (np.clip(w[None, :, None] - w[:, None, None] + NA_COLS - 1, 0, 2 * NA_COLS - 2)
             == np.arange(2 * NA_COLS - 1)).astype(np.float32)
    t1 = jnp.einsum('lhab,cdb->lhcad', table.astype(F32), jnp.asarray(sel_c),
                    precision=lax.Precision.HIGHEST)
    t1 = jnp.where(jnp.asarray(col_ok)[None, None, :, None, :], t1, NEG)
    return t1.reshape(DEPTH, NA_H, GRID_W, (2 * NA_ROWS - 1) * GRID_W)


def _route_t(logits_t, b_col, tm):
    per = N_EXP // N_GRP
    s = 1.0 / (1.0 + jnp.exp(-logits_t))
    sel3 = (s + b_col).reshape(N_GRP, per, tm)
    midx = lax.broadcasted_iota(jnp.int32, (N_GRP, per, tm), 1)
    gidx = lax.broadcasted_iota(jnp.int32, (N_GRP, per, tm), 0)
    ninf = -jnp.inf
    m1 = jnp.max(sel3, axis=1, keepdims=True)
    i1 = jnp.min(jnp.where(sel3 == m1, midx, per), axis=1, keepdims=True)
    m2 = jnp.max(jnp.where(midx == i1, ninf, sel3), axis=1, keepdims=True)
    gsc = jnp.broadcast_to(m1 + m2, (N_GRP, per, tm))
    cnt = jnp.zeros((N_GRP, per, tm), jnp.int32)
    for g in range(N_GRP):
        sg = gsc[g][None]
        cnt = cnt + ((sg > gsc) | ((sg == gsc) & (g < gidx))).astype(jnp.int32)
    x = jnp.where(cnt < TOPK_GRP, sel3, ninf).reshape(N_EXP, tm)
    eidx = lax.broadcasted_iota(jnp.int32, (N_EXP, tm), 0)
    cnt = jnp.zeros((N_EXP, tm), jnp.int32)
    for e in range(N_EXP):
        row = x[e:e + 1, :]
        cnt = cnt + ((row > x) | ((row == x) & (e < eidx))).astype(jnp.int32)
    chosen = cnt < TOP_K
    w = jnp.where(chosen, s, 0.0)
    return chosen, w / jnp.sum(w, axis=0, keepdims=True) * ROUTED_SCALE


def _merge_body(oabp_ref, oabs_ref, ocp_ref, ocs_ref, xp_ref, xs_ref, mod_ref, gpost_ref, gpre_ref,
                wout_ref, wrt_ref, br_ref, tri_ref, y_ref, h_ref, gates_ref, rank_ref, pcu_ref):
    i = pl.program_id(0)
    mix = (_dot(_tok_select(i, oabp_ref, oabs_ref), wout_ref[0:AB_W, :])
           + _dot(_tok_select(i, ocp_ref, ocs_ref), wout_ref[AB_W:D, :]))
    y = _tok_select(i, xp_ref, xs_ref) + mod_ref[2:3, :] * _rms(mix, gpost_ref[...])
    y_ref[...] = y
    h = _rms(y, gpre_ref[...]) * (1.0 + mod_ref[4:5, :]) + mod_ref[3:4, :]
    h_ref[...] = h.astype(BF16)
    logits_t = lax.dot_general(wrt_ref[...], h, (((1,), (1,)), ((), ())),
                               precision=lax.Precision.HIGHEST, preferred_element_type=F32)
    chosen, gates = _route_t(logits_t, br_ref[...], TM_IN)
    gates_ref[...] = gates
    ch = jnp.where(chosen, 1.0, 0.0)
    rank = _dot(ch.astype(BF16), tri_ref[...])
    rank_ref[...] = jnp.where(chosen, rank, -1.0)
    for j in range(TM_IN // B_SORT):
        cnt = jnp.sum(ch[:, j * B_SORT:(j + 1) * B_SORT], axis=1, keepdims=True)
        units = jnp.floor((cnt + (UNIT - 1)) * (1.0 / UNIT))
        pcu_ref[j] = jnp.broadcast_to(units, (N_EXP, 128))


def _merge(li, oabp, oabs, ocp, ocs, xp, xs, modv, g_post, g_pre, w_out_bf, w_router_t, b_router_col, tri):
    def tok(w):
        return pl.BlockSpec((TM_IN, w), lambda i: (i, 0))

    def par(*shape):
        return pl.BlockSpec((None,) + shape, lambda i: (li,) + (0,) * len(shape))

    tok_t = pl.BlockSpec((N_EXP, TM_IN), lambda i: (0, i))
    nb = TM_IN // B_SORT
    return pl.pallas_call(
        _merge_body,
        grid=(T // TM_IN,),
        in_specs=[_ctx_tok(AB_W), _lat_tok(AB_W), _ctx_tok(POOL_W), _lat_tok(POOL_W), _ctx_tok(D), _lat_tok(D),
                  pl.BlockSpec((None, None, 6, D), lambda i: (li, _mod_row(i, TM_IN), 0, 0)),
                  par(1, D), par(1, D), par(D, D), par(N_EXP, D), par(N_EXP, 1),
                  pl.BlockSpec((TM_IN, TM_IN), lambda i: (0, 0))],
        out_specs=[tok(D), tok(D), tok_t, tok_t,
                   pl.BlockSpec((nb, N_EXP, 128), lambda i: (i, 0, 0))],
        out_shape=[jax.ShapeDtypeStruct((T, D), F32),
                   jax.ShapeDtypeStruct((T, D), BF16),
                   jax.ShapeDtypeStruct((N_EXP, T), F32),
                   jax.ShapeDtypeStruct((N_EXP, T), F32),
                   jax.ShapeDtypeStruct((N_BLK, N_EXP, 128), F32)],
        compiler_params=_cparams(("parallel",)),
        name=f"merge_l{li}",
    )(oabp, oabs, ocp, ocs, xp, xs, modv, g_post, g_pre, w_out_bf, w_router_t, b_router_col, tri)


def _rank_matrix():
    t = np.arange(TM_IN)
    m = (t[:, None] < t[None, :]) & (t[:, None] // B_SORT == t[None, :] // B_SORT)
    return jnp.asarray(m, BF16)


def _moe_tables(pcu_f):
    pcu = pcu_f[:, :, 0].astype(jnp.int32)
    incl = jnp.cumsum(pcu, axis=1)
    uo = incl - pcu
    tot = jnp.sum(pcu, axis=0)
    gt = (tot + UPT - 1) // UPT
    gstart_t = jnp.cumsum(gt) - gt
    gstart_u = gstart_t * UPT
    steps = (gt + TPS - 1) // TPS
    sincl = jnp.cumsum(steps)
    sstart = sincl - steps
    bstart = gstart_u[None, :] + jnp.cumsum(pcu, axis=0) - pcu
    u = jnp.arange(UPB, dtype=jnp.int32)
    eou = jnp.sum((u[None, :, None] >= incl[:, None, :]).astype(jnp.int32), axis=-1)
    valid = eou < N_EXP
    onehot = eou[:, :, None] == jnp.arange(N_EXP, dtype=jnp.int32)[None, None, :]
    run_u0 = jnp.sum(jnp.where(onehot, uo[:, None, :], 0), axis=-1)
    run_g0 = jnp.sum(jnp.where(onehot, bstart[:, None, :], 0), axis=-1)
    dst = run_g0 + u[None, :] - run_u0
    spare0 = SPARE_UNIT0 + (jnp.arange(N_BLK, dtype=jnp.int32) % 2)[:, None] * (UPB + ZPB)
    dst_sort = jnp.where(valid, dst, spare0 + u[None, :])
    dst_comb = jnp.where(valid, dst, ZERO_UNIT)
    loc = jnp.where(valid, (u[None, :] - run_u0) * UNIT, -(1 << 20))
    k = jnp.arange(UPT - 1, dtype=jnp.int32)
    ztail = gt * UPT - tot
    zdst = jnp.where(k[None, :] < ztail[:, None], (gstart_u + tot)[:, None] + k[None, :], -1).reshape(N_BLK, ZPB)
    zdst = jnp.where(zdst >= 0, zdst, spare0 + UPB + jnp.arange(ZPB, dtype=jnp.int32)[None, :])
    step = jnp.arange(NS_MAX, dtype=jnp.int32)
    te = jnp.minimum(jnp.sum((step[:, None] >= sincl[None, :]).astype(jnp.int32), axis=-1), N_EXP - 1)
    mine = te[:, None] == jnp.arange(N_EXP, dtype=jnp.int32)[None, :]
    pick = lambda v: jnp.sum(jnp.where(mine, v[None, :], 0), axis=-1)
    first = gt - TPS * (steps - 1)
    j = step - pick(sstart)
    ntile = jnp.where(j == 0, pick(first), TPS)
    tile0 = pick(gstart_t) + jnp.where(j == 0, 0, pick(first) + TPS * (j - 1))
    return (jnp.minimum(eou, N_EXP - 1), loc, dst_sort, dst_comb, zdst, te, ntile, tile0, sincl[-1:])


def _unit_rows(u):
    if isinstance(u, int):
        return pl.ds(u * UNIT, UNIT)
    return pl.ds(pl.multiple_of(u * UNIT, UNIT), UNIT)


def _sort_body(eou_ref, loc_ref, dst_ref, zdst_ref, h_ref, rank_ref, gates_ref, xs_hbm, p_ref,
               xs_scr, zero_scr, sem):
    b = pl.program_id(0)
    slot = b % 2

    def copy(src, d, s):
        return pltpu.make_async_copy(src, xs_hbm.at[_unit_rows(d)], sem.at[s])

    def wait_step(s):
        lax.fori_loop(0, UPB + ZPB, lambda k, c: (copy(zero_scr, 0, s).wait(), c)[1], 0, unroll=True)

    @pl.when(b == 0)
    def _():
        zero_scr[...] = jnp.zeros((UNIT, XS_W), BF16)

    @pl.when(b >= 2)
    def _():
        wait_step(slot)

    base = lax.broadcasted_iota(jnp.int32, (UNIT, B_SORT), 0)

    def onehot_unit(u, carry):
        match = rank_ref[pl.ds(eou_ref[b, u], 1), :] == (base + loc_ref[b, u]).astype(F32)
        p_ref[_unit_rows(u), :] = jnp.where(match, 1.0, 0.0).astype(BF16)
        return carry

    lax.fori_loop(0, UPB, onehot_unit, 0, unroll=UNROLL)
    g = gates_ref[...]
    g_hi = g.astype(BF16)
    g_hl = jnp.concatenate([g_hi, (g - g_hi.astype(F32)).astype(BF16)], axis=0)
    h = h_ref[...]
    for c in range(RB // 512):
        rows = slice(c * 512, (c + 1) * 512)
        xs_scr[slot, rows, 0:D] = _dot(p_ref[rows, :], h).astype(BF16)
        xs_scr[slot, rows, D:XS_W] = _dot_nt(p_ref[rows, :], g_hl).astype(BF16)

    def start_run(u, carry):
        copy(xs_scr.at[slot, _unit_rows(u)], dst_ref[b, u], slot).start()
        return carry

    def start_zero(j, carry):
        copy(zero_scr, zdst_ref[b, j], slot).start()
        return carry

    lax.fori_loop(0, UPB, start_run, 0, unroll=UNROLL)
    lax.fori_loop(0, ZPB, start_zero, 0, unroll=True)

    @pl.when(b == N_BLK - 1)
    def _():
        wait_step(1 - slot)
        wait_step(slot)


def _moe_sort(li, eou, loc, dst, zdst, h, rank_t, gates_t):
    blk_t = pl.BlockSpec((N_EXP, B_SORT), lambda b, *_: (0, b))
    return pl.pallas_call(
        _sort_body,
        grid_spec=pltpu.PrefetchScalarGridSpec(
            num_scalar_prefetch=4,
            grid=(N_BLK,),
            in_specs=[pl.BlockSpec((B_SORT, D), lambda b, *_: (b, 0)), blk_t, blk_t],
            out_specs=[pl.BlockSpec(memory_space=pl.ANY),
                       pl.BlockSpec((RB, B_SORT), lambda b, *_: (b, 0))],
            scratch_shapes=[pltpu.VMEM((2, RB, XS_W), BF16),
                            pltpu.VMEM((UNIT, XS_W), BF16),
                            pltpu.SemaphoreType.DMA((2,))]),
        out_shape=[jax.ShapeDtypeStruct((R_SORT, XS_W), BF16),
                   jax.ShapeDtypeStruct((N_BLK * RB, B_SORT), BF16)],
        compiler_params=_cparams(("arbitrary",)),
        name=f"moe_sort_l{li}",
    )(eou, loc, dst, zdst, h, rank_t, gates_t)


def _experts_body(te_ref, ntile_ref, tile0_ref, ns_ref, xs_hbm, wg_ref, wu_ref, wd_ref, os_hbm,
                  xs_buf, o_buf, wg_s, wu_s, wd_s, sem_in, sem_out):
    i = pl.program_id(0)
    ns = ns_ref[0]
    slot = i % 2

    def tiles_of(step):
        return jnp.where((step >= 0) & (step < ns), ntile_ref[jnp.clip(step, 0, NS_MAX - 1)], 0)

    def tile_rows(step, t):
        return pl.ds(pl.multiple_of((tile0_ref[jnp.clip(step, 0, NS_MAX - 1)] + t) * TMG, TMG), TMG)

    def buf_rows(t):
        return pl.ds(pl.multiple_of(t * TMG, TMG), TMG)

    def copy_in(step, t, s):
        return pltpu.make_async_copy(xs_hbm.at[tile_rows(step, t)], xs_buf.at[s, buf_rows(t)], sem_in.at[s])

    def copy_out(step, t, s):
        return pltpu.make_async_copy(o_buf.at[s, buf_rows(t)], os_hbm.at[tile_rows(step, t)], sem_out.at[s])

    def for_tiles(step, fn):
        lax.fori_loop(0, tiles_of(step), lambda t, c: (fn(t), c)[1], 0)

    @pl.when(i == 0)
    def _():
        for_tiles(0, lambda t: copy_in(0, t, 0).start())

    for_tiles(i + 1, lambda t: copy_in(i + 1, t, 1 - slot).start())
    @pl.when(i <= ns)
    def _():
        for_tiles(i - 2, lambda t: copy_out(i - 2, t, slot).wait())

    @pl.when(i == ns)
    def _():
        for_tiles(i - 1, lambda t: copy_out(i - 1, t, 1 - slot).wait())

    ntile = tiles_of(i)
    e = te_ref[jnp.minimum(i, NS_MAX - 1)]
    new_expert = (i == 0) | (e != te_ref[jnp.maximum(i - 1, 0)])

    @pl.when((ntile > 0) & new_expert)
    def _():
        wg_s[...] = wg_ref[...].astype(BF16)
        wu_s[...] = wu_ref[...].astype(BF16)
        wd_s[...] = wd_ref[...].astype(BF16)

    for_tiles(i, lambda t: copy_in(i, t, slot).wait())
    for k in range(1, TPS + 1):
        rows = k * TMG

        @pl.when(ntile == k)
        def _():
            x = xs_buf[slot, 0:rows, 0:D]
            lane = lax.broadcasted_iota(jnp.int32, (rows, 128), 1)
            mine = (lane == e) | (lane == e + N_EXP)
            gate = jnp.sum(jnp.where(mine, xs_buf[slot, 0:rows, D:XS_W].astype(F32), 0.0),
                           axis=1, keepdims=True)
            hid = _silu(_dot(x, wg_s[...])) * _dot(x, wu_s[...]) * gate
            o_buf[slot, 0:rows, :] = _dot(hid.astype(BF16), wd_s[...]).astype(BF16)

    for_tiles(i, lambda t: copy_out(i, t, slot).start())

    @pl.when(i == NS_MAX)
    def _():
        o_buf[slot, 0:UNIT, :] = jnp.zeros((UNIT, D), BF16)
        zero = pltpu.make_async_copy(o_buf.at[slot, 0:UNIT], os_hbm.at[_unit_rows(ZERO_UNIT)], sem_out.at[slot])
        zero.start()
        zero.wait()


def _moe_experts(li, te, ntile, tile0, ns, xs, w_gate, w_up, w_down):
    def wspec(*shape):
        return pl.BlockSpec((None, None) + shape,
                            lambda i, te, ntile, tile0, ns: (li, te[jnp.minimum(i, ns[0] - 1)]) + (0,) * len(shape))

    return pl.pallas_call(
        _experts_body,
        grid_spec=pltpu.PrefetchScalarGridSpec(
            num_scalar_prefetch=4,
            grid=(NS_MAX + 1,),
            in_specs=[pl.BlockSpec(memory_space=pl.ANY),
                      wspec(D, EXP_DIM), wspec(D, EXP_DIM), wspec(EXP_DIM, D)],
            out_specs=pl.BlockSpec(memory_space=pl.ANY),
            scratch_shapes=[pltpu.VMEM((2, STEP_ROWS, XS_W), BF16),
                            pltpu.VMEM((2, STEP_ROWS, D), BF16),
                            pltpu.VMEM((D, EXP_DIM), BF16),
                            pltpu.VMEM((D, EXP_DIM), BF16),
                            pltpu.VMEM((EXP_DIM, D), BF16),
                            pltpu.SemaphoreType.DMA((2,)),
                            pltpu.SemaphoreType.DMA((2,))]),
        out_shape=jax.ShapeDtypeStruct((R_OUT, D), BF16),
        compiler_params=_cparams(("arbitrary",)),
        name=f"moe_experts_l{li}",
    )(te, ntile, tile0, ns, xs, w_gate, w_up, w_down)


def _combine_body(dst_ref, h_ref, p_ref, y_ref, mod_ref, gpost_ref,
                  wsg_ref, wsu_ref, wsd_ref, os_hbm, op_ref, ol_ref, os_scr, sem):
    b = pl.program_id(0)
    slot = b % 2

    def copy(d, u, s):
        return pltpu.make_async_copy(os_hbm.at[_unit_rows(d)], os_scr.at[s, _unit_rows(u)], sem.at[s])

    def fetch(blk, s):
        lax.fori_loop(0, UPB, lambda u, c: (copy(dst_ref[blk, u], u, s).start(), c)[1], 0, unroll=UNROLL)

    @pl.when(b == 0)
    def _():
        fetch(0, 0)

    @pl.when(b + 1 < N_BLK)
    def _():
        fetch(b + 1, 1 - slot)

    h = h_ref[...]
    shared = _dot((_silu(_dot(h, wsg_ref[...])) * _dot(h, wsu_ref[...])).astype(BF16), wsd_ref[...])
    lax.fori_loop(0, UPB, lambda k, c: (copy(0, 0, slot).wait(), c)[1], 0, unroll=True)
    routed = lax.dot_general(p_ref[...], os_scr[slot], (((0,), (0,)), ((), ())),
                             preferred_element_type=F32)
    out = y_ref[...] + mod_ref[5:6, :] * _rms(routed + shared, gpost_ref[...])

    @pl.when(b < T_P // B_SORT)
    def _():
        op_ref[...] = out

    @pl.when(b >= T_P // B_SORT)
    def _():
        ol_ref[...] = out


def _moe_combine(li, dst, h, p, y, modv, g_post, wsg_bf, wsu_bf, wsd_bf, os):
    nbp = T_P // B_SORT
    def tok(w):
        return pl.BlockSpec((B_SORT, w), lambda b, *_: (b, 0))

    def par(*shape):
        return pl.BlockSpec((None,) + shape, lambda b, *_: (li,) + (0,) * len(shape))

    return pl.pallas_call(
        _combine_body,
        grid_spec=pltpu.PrefetchScalarGridSpec(
            num_scalar_prefetch=1,
            grid=(N_BLK,),
            in_specs=[tok(D), pl.BlockSpec((RB, B_SORT), lambda b, *_: (b, 0)), tok(D),
                      pl.BlockSpec((None, None, 6, D), lambda b, *_: (li, _mod_row(b, B_SORT), 0, 0)),
                      par(1, D), par(D, EXP_DIM), par(D, EXP_DIM), par(EXP_DIM, D),
                      pl.BlockSpec(memory_space=pl.ANY)],
            out_specs=[pl.BlockSpec((B_SORT, D), lambda b, *_: (jnp.minimum(b, nbp - 1), 0)),
                       pl.BlockSpec((B_SORT, D), lambda b, *_: (jnp.maximum(b - nbp, 0), 0))],
            scratch_shapes=[pltpu.VMEM((2, RB, D), BF16),
                            pltpu.SemaphoreType.DMA((2,))]),
        out_shape=[jax.ShapeDtypeStruct((T_P, D), F32), jax.ShapeDtypeStruct((T_S, D), F32)],
        compiler_params=_cparams(("arbitrary",)),
        name=f"moe_combine_l{li}",
    )(dst, h, p, y, modv, g_post, wsg_bf, wsu_bf, wsd_bf, os)


def _block_diag(w):
    out = jnp.zeros((DEPTH, POOL_W, POOL_W), F32)
    for g in range(len(POOL_WINDOWS)):
        out = out.at[:, g * POOL_CH:(g + 1) * POOL_CH, g * POOL_CH:(g + 1) * POOL_CH].set(w[:, g])
    return out


def kernel(x_prompt, x_sample, cache_diff_k, cache_diff_v, cache_na_k, cache_na_v, c, c_ctx,
           w_ada, b_ada, g_pre_mix, g_post_mix, g_pre_ffn, g_post_ffn, w_in, w_out,
           diff_lambda, g_diff, na_bias, pool_w, pool_scale, w_router, b_router,
           w_gate, w_up, w_down, ws_gate, ws_up, ws_down):
    xp, xs = x_prompt.reshape(T_P, D), x_sample.reshape(T_S, D)
    cvec = jnp.concatenate([c_ctx[None, :], c, jnp.zeros((3, D), F32)], axis=0)
    modv = _modulation(cvec, w_ada, b_ada)[:, :1 + NB_S].reshape(DEPTH, 1 + NB_S, 6, D)

    cos_t, sin_t = _rope_tables()
    w_in_bf = w_in.astype(BF16)
    w_out_bf = w_out.astype(BF16)
    wsg_bf, wsu_bf, wsd_bf = ws_gate.astype(BF16), ws_up.astype(BF16), ws_down.astype(BF16)
    pool_bd = _block_diag(pool_w)
    row = lambda a: a.reshape(DEPTH, 1, a.shape[-1])
    g_pre_mix, g_post_mix, g_pre_ffn, g_post_ffn = map(row, (g_pre_mix, g_post_mix, g_pre_ffn, g_post_ffn))
    g_diff, pool_scale = row(g_diff), row(pool_scale)
    w_router_t = jnp.swapaxes(w_router, 1, 2)
    b_router_col = b_router.reshape(DEPTH, N_EXP, 1)
    tri = _rank_matrix()
    ck = cache_diff_k.reshape(NB_S, DEPTH, PAST, DA_W)
    cv = cache_diff_v.reshape(NB_S, DEPTH, PAST, DA_W)
    cnk = cache_na_k.reshape(NB_S, DEPTH, PAST, NA_W)
    cnv = cache_na_v.reshape(NB_S, DEPTH, PAST, NA_W)

    strips = _na_bias_strips(na_bias)

    caches = ()
    for li in range(DEPTH):
        lam_init = 0.8 - 0.6 * math.exp(-0.3 * li)
        qa, qb, up, ka, va, kb, vb, *caches = _in_proj(li, xp, xs, modv, g_pre_mix, w_in_bf, cos_t, sin_t,
                                                      caches)
        ocp = _pool_call(li, up, pool_bd, pool_scale, N_P, NB_P, 0, f"pool_ctx_l{li}")
        ocs = _pool_call(li, up, pool_bd, pool_scale, N_S, NB_S, T_P // N_S, f"pool_lat_l{li}")
        oabp = _attn_ctx(li, diff_lambda, g_diff, qa, caches[0], caches[1], qb, caches[2], caches[3], lam_init)
        oabs = _attn_lat(li, diff_lambda, g_diff, qa, ka, va, ck, cv, qb, kb, vb, cnk, cnv, strips, lam_init)
        y, h, gates_t, rank_t, pcu = _merge(li, oabp, oabs, ocp, ocs, xp, xs, modv, g_post_mix, g_pre_ffn,
                                            w_out_bf, w_router_t, b_router_col, tri)
        eou, loc, dst_sort, dst_comb, zdst, te, ntile, tile0, ns = _moe_tables(pcu)
        srt, p = _moe_sort(li, eou, loc, dst_sort, zdst, h, rank_t, gates_t)
        os = _moe_experts(li, te, ntile, tile0, ns, srt, w_gate, w_up, w_down)
        xp, xs = _moe_combine(li, dst_comb, h, p, y, modv, g_post_ffn, wsg_bf, wsu_bf, wsd_bf, os)

    new_dk, new_dv, new_nk, new_nv = caches
    return (xp.reshape(NB_P, N_P, D), xs.reshape(NB_S, N_S, D),
            new_dk.reshape(NB_P, DEPTH, N_P, DA_H, 2 * DA_QK), new_dv.reshape(NB_P, DEPTH, N_P, DA_H, DA_V),
            new_nk.reshape(NB_P, DEPTH, N_P, NA_H, NA_D), new_nv.reshape(NB_P, DEPTH, N_P, NA_H, NA_D))
```

```python
import functools
import math

import numpy as np
import jax
import jax.numpy as jnp
from jax import lax
from jax.experimental import pallas as pl
from jax.experimental.pallas import tpu as pltpu

F32 = jnp.float32
BF16 = jnp.bfloat16

D = 1024
DEPTH = 2
NB_P, N_P = 16, 256
NB_S, N_S = 4, 1024
T_P = NB_P * N_P
T_S = NB_S * N_S
T = T_P + T_S
PAST = 512
GRID_W = 64
DA_H, DA_QK, DA_V = 4, 64, 128
DA_W = DA_H * DA_V
NA_H, NA_D = 4, 64
NA_W = NA_H * NA_D
NA_ROWS, NA_COLS = 8, 16
POOL_WINDOWS = (2, 4, 8, 16)
POOL_CH = 64
POOL_W = 256
IN_COLS = 3 * DA_W + 3 * NA_W + POOL_W
AB_W = DA_W + NA_W
N_EXP, TOP_K, N_GRP, TOPK_GRP = 64, 8, 8, 4
EXP_DIM = 256
ROUTED_SCALE = 2.5
EPS = 1e-6
NEG = -1e30
ROPE_THETA = 10000.0

VMEM_LIMIT = 50 * 1024 * 1024

TM_IN = 512
NT_P = T_P // TM_IN
TQ = 256

B_SORT = 256
UNIT = 16
N_BLK = T // B_SORT
RUN_PAD_ROWS = TOP_K * B_SORT + N_EXP * (UNIT - 1)
RB = -(-RUN_PAD_ROWS // 256) * 256
UPB = RB // UNIT
TMG = 256
UPT = TMG // UNIT
NT_MAX = (N_BLK * RUN_PAD_ROWS) // TMG + N_EXP
XS_W = D + 128
EXP_PER_BLK = N_EXP // N_BLK
ZPB = EXP_PER_BLK * (UPT - 1)
TPS = 6
STEP_ROWS = TPS * TMG
NS_MAX = NT_MAX // TPS + (N_EXP * (TPS - 1)) // TPS
ZERO_UNIT = NT_MAX * UPT
SPARE_UNIT0 = ZERO_UNIT + UPT
R_OUT = (NT_MAX + 1) * TMG
R_SORT = R_OUT + -(-2 * (UPB + ZPB) // UPT) * TMG
UNROLL = 8


def _cparams(sem):
    return pltpu.CompilerParams(dimension_semantics=sem, vmem_limit_bytes=VMEM_LIMIT)


def _rms(x, g):
    return x * lax.rsqrt(jnp.mean(x * x, axis=-1, keepdims=True) + EPS) * g


def _dot(a, b):
    return jnp.dot(a, b, preferred_element_type=F32)


def _dot_nt(a, b):
    return lax.dot_general(a, b, (((1,), (1,)), ((), ())), preferred_element_type=F32)


def _silu(x):
    return x / (1.0 + jnp.exp(-x))


def _mod_row(i, tm):
    off = i * tm - T_P
    return jnp.where(off >= 0, 1 + jnp.maximum(off, 0) // N_S, 0)


def _mod_body(c_ref, w_ref, b_ref, o_ref):
    c = c_ref[...]
    o_ref[...] = jnp.dot(_silu(c), w_ref[...], precision=lax.Precision.HIGHEST,
                         preferred_element_type=F32) + b_ref[...]


def _modulation(cvec, w_ada, b_ada):
    tn = 1536
    return pl.pallas_call(
        _mod_body,
        grid=(DEPTH, 6 * D // tn),
        in_specs=[pl.BlockSpec((8, D), lambda l, j: (0, 0)),
                  pl.BlockSpec((None, D, tn), lambda l, j: (l, 0, j)),
                  pl.BlockSpec((None, 1, tn), lambda l, j: (l, 0, j))],
        out_specs=pl.BlockSpec((None, 8, tn), lambda l, j: (l, 0, j)),
        out_shape=jax.ShapeDtypeStruct((DEPTH, 8, 6 * D), F32),
        compiler_params=_cparams(("parallel", "parallel")),
        name="modulation",
    )(cvec, w_ada, b_ada.reshape(DEPTH, 1, 6 * D))


def _rope_tables():
    nf = DA_QK // 4
    t = np.arange(N_S)
    pos = np.stack([t // GRID_W, t % GRID_W], axis=-1).astype(np.float32)
    inv = np.power(np.float32(ROPE_THETA), -np.arange(nf, dtype=np.float32) / nf)
    ang = pos[:, :, None] * inv
    cos = np.cos(ang)
    sin = np.sin(ang)
    cos64 = np.concatenate([cos[:, 0], cos[:, 0], cos[:, 1], cos[:, 1]], axis=-1)
    sin64 = np.concatenate([-sin[:, 0], sin[:, 0], -sin[:, 1], sin[:, 1]], axis=-1)
    reps = DA_W // DA_QK
    cos_t = np.concatenate([np.tile(cos64, (1, reps)), np.ones((TM_IN, DA_W), np.float32)], axis=0)
    sin_t = np.concatenate([np.tile(sin64, (1, reps)), np.zeros((TM_IN, DA_W), np.float32)], axis=0)
    return jnp.asarray(cos_t, F32), jnp.asarray(sin_t, F32)


def _tok_select(i, ctx_ref, lat_ref):
    return jnp.where(i < NT_P, ctx_ref[...], lat_ref[...])


def _in_proj_body(*refs):
    (xp_ref, xs_ref, mod_ref, g_ref, w_ref, cos_ref, sin_ref) = refs[:7]
    (qa_ref, qb_ref, up_ref, ka_ref, va_ref, kb_ref, vb_ref, ck_ref, cv_ref, cnk_ref, cnv_ref) = refs[-11:]
    i = pl.program_id(0)
    x = _tok_select(i, xp_ref, xs_ref)
    h = (_rms(x, g_ref[...]) * (1.0 + mod_ref[1:2, :]) + mod_ref[0:1, :]).astype(BF16)
    cos = cos_ref[...]
    sin = sin_ref[...]
    lane = lax.broadcasted_iota(jnp.int32, (TM_IN, DA_W), 1)
    first = (lane % 32) < 16

    def proj(lo, hi):
        return _dot(h, w_ref[:, lo:hi])

    def rope(t):
        swapped = jnp.where(first, pltpu.roll(t, DA_W - 16, 1), pltpu.roll(t, 16, 1))
        return t * cos + swapped * sin

    o = 0
    qa_ref[...] = (rope(proj(o, o + DA_W)) * (DA_QK ** -0.5)).astype(BF16)
    o += DA_W
    ka = rope(proj(o, o + DA_W))
    o += DA_W
    va = proj(o, o + DA_W)
    o += DA_W
    qb_ref[...] = (proj(o, o + NA_W) * (NA_D ** -0.5)).astype(BF16)
    o += NA_W
    kb = proj(o, o + NA_W)
    o += NA_W
    vb = proj(o, o + NA_W)
    o += NA_W
    up_ref[...] = proj(o, o + POOL_W)

    @pl.when(i < NT_P)
    def _():
        for ref, val in ((ck_ref, ka), (cv_ref, va), (cnk_ref, kb), (cnv_ref, vb)):
            ref[...] = val.reshape(ref.shape)

    @pl.when(i >= NT_P)
    def _():
        for ref, val in ((ka_ref, ka), (va_ref, va), (kb_ref, kb), (vb_ref, vb)):
            ref[...] = val.astype(BF16)


def _ctx_tok(w):
    return pl.BlockSpec((TM_IN, w), lambda i: (jnp.minimum(i, NT_P - 1), 0))


def _lat_tok(w):
    return pl.BlockSpec((TM_IN, w), lambda i: (jnp.maximum(i - NT_P, 0), 0))


def _in_proj(li, xp, xs, modv, g_pre, w_in_bf, cos_t, sin_t, caches):
    n_pos_blk = N_S // TM_IN

    def pos_blk(i):
        return jnp.where(i >= NT_P, jnp.maximum(i - NT_P, 0) % n_pos_blk, n_pos_blk)

    def tok(w):
        return pl.BlockSpec((TM_IN, w), lambda i: (i, 0))

    def cache(w):
        return pl.BlockSpec((TM_IN // N_P, None, N_P, w), lambda i: (jnp.minimum(i, NT_P - 1), li, 0, 0))

    widths = (DA_W, DA_W, NA_W, NA_W)
    n_in = 7
    return pl.pallas_call(
        _in_proj_body,
        grid=(T // TM_IN,),
        in_specs=[_ctx_tok(D), _lat_tok(D),
                  pl.BlockSpec((None, None, 6, D), lambda i: (li, _mod_row(i, TM_IN), 0, 0)),
                  pl.BlockSpec((None, 1, D), lambda i: (li, 0, 0)),
                  pl.BlockSpec((None, D, IN_COLS), lambda i: (li, 0, 0)),
                  pl.BlockSpec((TM_IN, DA_W), lambda i: (pos_blk(i), 0)),
                  pl.BlockSpec((TM_IN, DA_W), lambda i: (pos_blk(i), 0))]
                 + [pl.BlockSpec(memory_space=pl.ANY)] * len(caches),
        out_specs=[tok(DA_W), tok(NA_W), tok(POOL_W)] + [_lat_tok(w) for w in widths]
                  + [cache(w) for w in widths],
        out_shape=[jax.ShapeDtypeStruct((T, DA_W), BF16),
                   jax.ShapeDtypeStruct((T, NA_W), BF16),
                   jax.ShapeDtypeStruct((T, POOL_W), F32)]
                  + [jax.ShapeDtypeStruct((T_S, w), BF16) for w in widths]
                  + [jax.ShapeDtypeStruct((NB_P, DEPTH, N_P, w), F32) for w in widths],
        input_output_aliases={n_in + k: 7 + k for k in range(len(caches))},
        compiler_params=_cparams(("arbitrary",)),
        name=f"in_proj_l{li}",
    )(xp, xs, modv, g_pre, w_in_bf, cos_t, sin_t, *caches)


def _pool_body(u_ref, w_ref, s_ref, o_ref, *, n):
    u = u_ref[...]
    row = lax.broadcasted_iota(jnp.int32, (n, POOL_W), 0)
    lane = lax.broadcasted_iota(jnp.int32, (n, POOL_W), 1)

    def shift_dn(a, k):
        return jnp.where(row >= k, pltpu.roll(a, k, 0), 0.0)

    def shift_up(a, k):
        return jnp.where(row < n - k, pltpu.roll(a, n - k, 0), 0.0)

    fwd = u
    bwd = shift_dn(u, 1)
    mean = jnp.zeros_like(u)
    k = 1
    for gi, w in enumerate(POOL_WINDOWS):
        while k < w // 2:
            fwd = fwd + shift_up(fwd, k)
            bwd = bwd + shift_dn(bwd, k)
            k *= 2
        cnt = (jnp.minimum(row + w // 2, n) - jnp.maximum(row - w // 2, 0)).astype(F32)
        in_group = (lane >= gi * POOL_CH) & (lane < (gi + 1) * POOL_CH)
        mean = jnp.where(in_group, (fwd + bwd) / cnt, mean)
    pooled = mean - u
    mixed = jnp.dot(pooled, w_ref[...], precision=lax.Precision.HIGHEST, preferred_element_type=F32)
    o_ref[...] = (mixed * s_ref[...]).astype(BF16)


def _pool_call(li, up, w_bd, scale, n, nseg, blk0, name):
    return pl.pallas_call(
        functools.partial(_pool_body, n=n),
        grid=(nseg,),
        in_specs=[pl.BlockSpec((n, POOL_W), lambda i: (blk0 + i, 0)),
                  pl.BlockSpec((None, POOL_W, POOL_W), lambda i: (li, 0, 0)),
                  pl.BlockSpec((None, 1, POOL_W), lambda i: (li, 0, 0))],
        out_specs=pl.BlockSpec((n, POOL_W), lambda i: (i, 0)),
        out_shape=jax.ShapeDtypeStruct((nseg * n, POOL_W), BF16),
        compiler_params=_cparams(("parallel",)),
        name=name,
    )(up, w_bd, scale)


def _lambda(lp, lam_init):
    a = jnp.sum(lp[0:1, :] * lp[1:2, :], axis=1, keepdims=True)
    b = jnp.sum(lp[2:3, :] * lp[3:4, :], axis=1, keepdims=True)
    return jnp.exp(a) - jnp.exp(b) + lam_init


def _softmax_av(scores, vs):
    m = functools.reduce(jnp.maximum, [jnp.max(s, axis=-1, keepdims=True) for s in scores])
    es = [jnp.exp(s - m) for s in scores]
    den = functools.reduce(jnp.add, [jnp.sum(e, axis=-1, keepdims=True) for e in es])
    o = functools.reduce(jnp.add, [_dot(e.astype(BF16), v) for e, v in zip(es, vs)])
    return o * (1.0 / den)


def _diff_head(q, ks, vs, lam, gd, lam_init):
    o1 = _softmax_av([_dot_nt(q[:, :DA_QK], k[:, :DA_QK]) for k in ks], vs)
    o2 = _softmax_av([_dot_nt(q[:, DA_QK:], k[:, DA_QK:]) for k in ks], vs)
    return _rms(o1 - lam * o2, gd) * (1.0 - lam_init)


def _plain_head(q, ks, vs, biases):
    scores = []
    for k, bias in zip(ks, biases):
        s = _dot_nt(q, k)
        scores.append(s if bias is None else s + bias)
    return _softmax_av(scores, vs)


def _attn_ctx_body(lam_ref, gd_ref, qa_ref, ka_ref, va_ref, qb_ref, kb_ref, vb_ref, o_ref, *, lam_init):
    lam = _lambda(lam_ref[...], lam_init)
    gd = gd_ref[...]
    for h in range(DA_H):
        sl = slice(h * DA_V, (h + 1) * DA_V)
        o = _diff_head(qa_ref[:, sl], [ka_ref[:, sl].astype(BF16)], [va_ref[:, sl].astype(BF16)],
                       lam, gd, lam_init)
        o_ref[:, sl] = o.astype(BF16)
    for h in range(NA_H):
        sl = slice(h * NA_D, (h + 1) * NA_D)
        o = _plain_head(qb_ref[:, sl], [kb_ref[:, sl].astype(BF16)], [vb_ref[:, sl].astype(BF16)], [None])
        o_ref[:, DA_W + h * NA_D:DA_W + (h + 1) * NA_D] = o.astype(BF16)


def _attn_ctx(li, lam_p, g_diff, qa, ka, va, qb, kb, vb, lam_init):
    def blk(w):
        return pl.BlockSpec((N_P, w), lambda b: (b, 0))

    def cblk(w):
        return pl.BlockSpec((None, None, N_P, w), lambda b: (b, li, 0, 0))

    return pl.pallas_call(
        functools.partial(_attn_ctx_body, lam_init=lam_init),
        grid=(NB_P,),
        in_specs=[pl.BlockSpec((None, 4, DA_QK), lambda b: (li, 0, 0)),
                  pl.BlockSpec((None, 1, DA_V), lambda b: (li, 0, 0)),
                  blk(DA_W), cblk(DA_W), cblk(DA_W), blk(NA_W), cblk(NA_W), cblk(NA_W)],
        out_specs=pl.BlockSpec((N_P, AB_W), lambda b: (b, 0)),
        out_shape=jax.ShapeDtypeStruct((T_P, AB_W), BF16),
        compiler_params=_cparams(("parallel",)),
        name=f"attn_ctx_l{li}",
    )(lam_p, g_diff, qa, ka, va, qb, kb, vb)


G_ROWS = N_S // GRID_W
QR_PER_BLK = TQ // GRID_W


def _fill_na_bias(j, strip_ref, bias_scr):
    for jj in range(N_S // TQ):
        @pl.when(j == jj)
        def _():
            for rr in range(QR_PER_BLK):
                qr = jj * QR_PER_BLK + rr
                rs = min(max(qr - NA_ROWS // 2, 0), G_ROWS - NA_ROWS)
                lo, hi = rs * GRID_W, (rs + NA_ROWS) * GRID_W
                a0 = (rs - qr + NA_ROWS - 1) * GRID_W
                q = slice(rr * GRID_W, (rr + 1) * GRID_W)
                for h in range(NA_H):
                    if lo > 0:
                        bias_scr[h, q, 0:lo] = jnp.full((GRID_W, lo), NEG, F32)
                    bias_scr[h, q, lo:hi] = strip_ref[h, :, a0:a0 + NA_ROWS * GRID_W]
                    if hi < N_S:
                        bias_scr[h, q, hi:N_S] = jnp.full((GRID_W, N_S - hi), NEG, F32)


def _attn_lat_body(lam_ref, gd_ref, qa_ref, ka_ref, va_ref, ck_ref, cv_ref,
                   qb_ref, kb_ref, vb_ref, cnk_ref, cnv_ref, strip_ref, o_ref, bias_scr, *, lam_init):
    _fill_na_bias(pl.program_id(1), strip_ref, bias_scr)
    lam = _lambda(lam_ref[...], lam_init)
    gd = gd_ref[...]
    for h in range(DA_H):
        sl = slice(h * DA_V, (h + 1) * DA_V)
        ks = [ka_ref[:, sl], ck_ref[:, sl].astype(BF16)]
        vs = [va_ref[:, sl], cv_ref[:, sl].astype(BF16)]
        o_ref[:, sl] = _diff_head(qa_ref[:, sl], ks, vs, lam, gd, lam_init).astype(BF16)
    for h in range(NA_H):
        sl = slice(h * NA_D, (h + 1) * NA_D)
        ks = [kb_ref[:, sl], cnk_ref[:, sl].astype(BF16)]
        vs = [vb_ref[:, sl], cnv_ref[:, sl].astype(BF16)]
        o = _plain_head(qb_ref[:, sl], ks, vs, [bias_scr[h], None])
        o_ref[:, DA_W + h * NA_D:DA_W + (h + 1) * NA_D] = o.astype(BF16)


def _attn_lat(li, lam_p, g_diff, qa, ka, va, ck, cv, qb, kb, vb, cnk, cnv, strip, lam_init):
    nq = N_S // TQ
    q0 = T_P // TQ

    def qblk(w):
        return pl.BlockSpec((TQ, w), lambda b, j: (q0 + b * nq + j, 0))

    def kvblk(w):
        return pl.BlockSpec((N_S, w), lambda b, j: (b, 0))

    def cblk(w):
        return pl.BlockSpec((None, None, PAST, w), lambda b, j: (b, li, 0, 0))

    return pl.pallas_call(
        functools.partial(_attn_lat_body, lam_init=lam_init),
        grid=(NB_S, nq),
        in_specs=[pl.BlockSpec((None, 4, DA_QK), lambda b, j: (li, 0, 0)),
                  pl.BlockSpec((None, 1, DA_V), lambda b, j: (li, 0, 0)),
                  qblk(DA_W), kvblk(DA_W), kvblk(DA_W), cblk(DA_W), cblk(DA_W),
                  qblk(NA_W), kvblk(NA_W), kvblk(NA_W), cblk(NA_W), cblk(NA_W),
                  pl.BlockSpec((None, NA_H, GRID_W, (2 * NA_ROWS - 1) * GRID_W), lambda b, j: (li, 0, 0, 0))],
        out_specs=pl.BlockSpec((TQ, AB_W), lambda b, j: (b * nq + j, 0)),
        out_shape=jax.ShapeDtypeStruct((T_S, AB_W), BF16),
        scratch_shapes=[pltpu.VMEM((NA_H, TQ, N_S), F32)],
        compiler_params=_cparams(("parallel", "parallel")),
        name=f"attn_lat_l{li}",
    )(lam_p, g_diff, qa, ka, va, ck, cv, qb, kb, vb, cnk, cnv, strip)


def _na_bias_strips(table):
    w = np.arange(GRID_W)
    cs = np.clip(w - NA_COLS // 2, 0, GRID_W - NA_COLS)
    col_ok = (w[None, :] >= cs[:, None]) & (w[None, :] < cs[:, None] + NA_COLS)
    sel_c = (np.clip(w[None, :, None] - w[:, None, None] + NA_COLS - 1, 0, 2 * NA_COLS - 2)
             == np.arange(2 * NA_COLS - 1)).astype(np.float32)
    t1 = jnp.einsum('lhab,cdb->lhcad', table.astype(F32), jnp.asarray(sel_c),
                    precision=lax.Precision.HIGHEST)
    t1 = jnp.where(jnp.asarray(col_ok)[None, None, :, None, :], t1, NEG)
    return t1.reshape(DEPTH, NA_H, GRID_W, (2 * NA_ROWS - 1) * GRID_W)


def _route_t(logits_t, b_col, tm):
    per = N_EXP // N_GRP
    s = 1.0 / (1.0 + jnp.exp(-logits_t))
    sel3 = (s + b_col).reshape(N_GRP, per, tm)
    midx = lax.broadcasted_iota(jnp.int32, (N_GRP, per, tm), 1)
    gidx = lax.broadcasted_iota(jnp.int32, (N_GRP, per, tm), 0)
    ninf = -jnp.inf
    m1 = jnp.max(sel3, axis=1, keepdims=True)
    i1 = jnp.min(jnp.where(sel3 == m1, midx, per), axis=1, keepdims=True)
    m2 = jnp.max(jnp.where(midx == i1, ninf, sel3), axis=1, keepdims=True)
    gsc = jnp.broadcast_to(m1 + m2, (N_GRP, per, tm))
    cnt = jnp.zeros((N_GRP, per, tm), jnp.int32)
    for g in range(N_GRP):
        sg = gsc[g][None]
        cnt = cnt + ((sg > gsc) | ((sg == gsc) & (g < gidx))).astype(jnp.int32)
    x = jnp.where(cnt < TOPK_GRP, sel3, ninf).reshape(N_EXP, tm)
    eidx = lax.broadcasted_iota(jnp.int32, (N_EXP, tm), 0)
    cnt = jnp.zeros((N_EXP, tm), jnp.int32)
    for e in range(N_EXP):
        row = x[e:e + 1, :]
        cnt = cnt + ((row > x) | ((row == x) & (e < eidx))).astype(jnp.int32)
    chosen = cnt < TOP_K
    w = jnp.where(chosen, s, 0.0)
    return chosen, w / jnp.sum(w, axis=0, keepdims=True) * ROUTED_SCALE


def _merge_body(oabp_ref, oabs_ref, ocp_ref, ocs_ref, xp_ref, xs_ref, mod_ref, gpost_ref, gpre_ref,
                wout_ref, wrt_ref, br_ref, tri_ref, y_ref, h_ref, gates_ref, rank_ref, pcu_ref):
    i = pl.program_id(0)
    mix = (_dot(_tok_select(i, oabp_ref, oabs_ref), wout_ref[0:AB_W, :])
           + _dot(_tok_select(i, ocp_ref, ocs_ref), wout_ref[AB_W:D, :]))
    y = _tok_select(i, xp_ref, xs_ref) + mod_ref[2:3, :] * _rms(mix, gpost_ref[...])
    y_ref[...] = y
    h = _rms(y, gpre_ref[...]) * (1.0 + mod_ref[4:5, :]) + mod_ref[3:4, :]
    h_ref[...] = h.astype(BF16)
    logits_t = lax.dot_general(wrt_ref[...], h, (((1,), (1,)), ((), ())),
                               precision=lax.Precision.HIGHEST, preferred_element_type=F32)
    chosen, gates = _route_t(logits_t, br_ref[...], TM_IN)
    gates_ref[...] = gates
    ch = jnp.where(chosen, 1.0, 0.0)
    rank = _dot(ch.astype(BF16), tri_ref[...])
    rank_ref[...] = jnp.where(chosen, rank, -1.0)
    for j in range(TM_IN // B_SORT):
        cnt = jnp.sum(ch[:, j * B_SORT:(j + 1) * B_SORT], axis=1, keepdims=True)
        units = jnp.floor((cnt + (UNIT - 1)) * (1.0 / UNIT))
        pcu_ref[j] = jnp.broadcast_to(units, (N_EXP, 128))


def _merge(li, oabp, oabs, ocp, ocs, xp, xs, modv, g_post, g_pre, w_out_bf, w_router_t, b_router_col, tri):
    def tok(w):
        return pl.BlockSpec((TM_IN, w), lambda i: (i, 0))

    def par(*shape):
        return pl.BlockSpec((None,) + shape, lambda i: (li,) + (0,) * len(shape))

    tok_t = pl.BlockSpec((N_EXP, TM_IN), lambda i: (0, i))
    nb = TM_IN // B_SORT
    return pl.pallas_call(
        _merge_body,
        grid=(T // TM_IN,),
        in_specs=[_ctx_tok(AB_W), _lat_tok(AB_W), _ctx_tok(POOL_W), _lat_tok(POOL_W), _ctx_tok(D), _lat_tok(D),
                  pl.BlockSpec((None, None, 6, D), lambda i: (li, _mod_row(i, TM_IN), 0, 0)),
                  par(1, D), par(1, D), par(D, D), par(N_EXP, D), par(N_EXP, 1),
                  pl.BlockSpec((TM_IN, TM_IN), lambda i: (0, 0))],
        out_specs=[tok(D), tok(D), tok_t, tok_t,
                   pl.BlockSpec((nb, N_EXP, 128), lambda i: (i, 0, 0))],
        out_shape=[jax.ShapeDtypeStruct((T, D), F32),
                   jax.ShapeDtypeStruct((T, D), BF16),
                   jax.ShapeDtypeStruct((N_EXP, T), F32),
                   jax.ShapeDtypeStruct((N_EXP, T), F32),
                   jax.ShapeDtypeStruct((N_BLK, N_EXP, 128), F32)],
        compiler_params=_cparams(("parallel",)),
        name=f"merge_l{li}",
    )(oabp, oabs, ocp, ocs, xp, xs, modv, g_post, g_pre, w_out_bf, w_router_t, b_router_col, tri)


def _rank_matrix():
    t = np.arange(TM_IN)
    m = (t[:, None] < t[None, :]) & (t[:, None] // B_SORT == t[None, :] // B_SORT)
    return jnp.asarray(m, BF16)


def _moe_tables(pcu_f):
    pcu = pcu_f[:, :, 0].astype(jnp.int32)
    incl = jnp.cumsum(pcu, axis=1)
    uo = incl - pcu
    tot = jnp.sum(pcu, axis=0)
    gt = (tot + UPT - 1) // UPT
    gstart_t = jnp.cumsum(gt) - gt
    gstart_u = gstart_t * UPT
    steps = (gt + TPS - 1) // TPS
    sincl = jnp.cumsum(steps)
    sstart = sincl - steps
    bstart = gstart_u[None, :] + jnp.cumsum(pcu, axis=0) - pcu
    u = jnp.arange(UPB, dtype=jnp.int32)
    eou = jnp.sum((u[None, :, None] >= incl[:, None, :]).astype(jnp.int32), axis=-1)
    valid = eou < N_EXP
    onehot = eou[:, :, None] == jnp.arange(N_EXP, dtype=jnp.int32)[None, None, :]
    run_u0 = jnp.sum(jnp.where(onehot, uo[:, None, :], 0), axis=-1)
    run_g0 = jnp.sum(jnp.where(onehot, bstart[:, None, :], 0), axis=-1)
    dst = run_g0 + u[None, :] - run_u0
    spare0 = SPARE_UNIT0 + (jnp.arange(N_BLK, dtype=jnp.int32) % 2)[:, None] * (UPB + ZPB)
    dst_sort = jnp.where(valid, dst, spare0 + u[None, :])
    dst_comb = jnp.where(valid, dst, ZERO_UNIT)
    loc = jnp.where(valid, (u[None, :] - run_u0) * UNIT, -(1 << 20))
    k = jnp.arange(UPT - 1, dtype=jnp.int32)
    ztail = gt * UPT - tot
    zdst = jnp.where(k[None, :] < ztail[:, None], (gstart_u + tot)[:, None] + k[None, :], -1).reshape(N_BLK, ZPB)
    zdst = jnp.where(zdst >= 0, zdst, spare0 + UPB + jnp.arange(ZPB, dtype=jnp.int32)[None, :])
    step = jnp.arange(NS_MAX, dtype=jnp.int32)
    te = jnp.minimum(jnp.sum((step[:, None] >= sincl[None, :]).astype(jnp.int32), axis=-1), N_EXP - 1)
    mine = te[:, None] == jnp.arange(N_EXP, dtype=jnp.int32)[None, :]
    pick = lambda v: jnp.sum(jnp.where(mine, v[None, :], 0), axis=-1)
    first = gt - TPS * (steps - 1)
    j = step - pick(sstart)
    ntile = jnp.where(j == 0, pick(first), TPS)
    tile0 = pick(gstart_t) + jnp.where(j == 0, 0, pick(first) + TPS * (j - 1))
    ns = sincl[-1]
    eord = jnp.cumsum((j == 0).astype(jnp.int32)) - 1
    wslot = jnp.where(j == 0, eord % 2, -1)
    nxt = pick(sincl)
    nexte = jnp.sum(jnp.where(step[None, :] == nxt[:, None], te[None, :], 0), axis=-1)
    nexte = jnp.where(nxt < ns, nexte, -1)
    return (jnp.minimum(eou, N_EXP - 1), loc, dst_sort, dst_comb, zdst, te, ntile, tile0, wslot, nexte,
            sincl[-1:])


def _unit_rows(u):
    if isinstance(u, int):
        return pl.ds(u * UNIT, UNIT)
    return pl.ds(pl.multiple_of(u * UNIT, UNIT), UNIT)


def _sort_body(eou_ref, loc_ref, dst_ref, zdst_ref, h_ref, rank_ref, gates_ref, xs_hbm, p_ref,
               xs_scr, zero_scr, sem):
    b = pl.program_id(0)
    slot = b % 2

    def copy(src, d, s):
        return pltpu.make_async_copy(src, xs_hbm.at[_unit_rows(d)], sem.at[s])

    def wait_step(s):
        lax.fori_loop(0, UPB + ZPB, lambda k, c: (copy(zero_scr, 0, s).wait(), c)[1], 0, unroll=True)

    @pl.when(b == 0)
    def _():
        zero_scr[...] = jnp.zeros((UNIT, XS_W), BF16)

    @pl.when(b >= 2)
    def _():
        wait_step(slot)

    base = lax.broadcasted_iota(jnp.int32, (UNIT, B_SORT), 0)

    def onehot_unit(u, carry):
        match = rank_ref[pl.ds(eou_ref[b, u], 1), :] == (base + loc_ref[b, u]).astype(F32)
        p_ref[_unit_rows(u), :] = jnp.where(match, 1.0, 0.0).astype(BF16)
        return carry

    lax.fori_loop(0, UPB, onehot_unit, 0, unroll=UNROLL)
    g = gates_ref[...]
    g_hi = g.astype(BF16)
    g_hl = jnp.concatenate([g_hi, (g - g_hi.astype(F32)).astype(BF16)], axis=0)
    h = h_ref[...]
    for c in range(RB // 512):
        rows = slice(c * 512, (c + 1) * 512)
        xs_scr[slot, rows, 0:D] = _dot(p_ref[rows, :], h).astype(BF16)
        xs_scr[slot, rows, D:XS_W] = _dot_nt(p_ref[rows, :], g_hl).astype(BF16)

    def start_run(u, carry):
        copy(xs_scr.at[slot, _unit_rows(u)], dst_ref[b, u], slot).start()
        return carry

    def start_zero(j, carry):
        copy(zero_scr, zdst_ref[b, j], slot).start()
        return carry

    lax.fori_loop(0, UPB, start_run, 0, unroll=UNROLL)
    lax.fori_loop(0, ZPB, start_zero, 0, unroll=True)

    @pl.when(b == N_BLK - 1)
    def _():
        wait_step(1 - slot)
        wait_step(slot)


def _moe_sort(li, eou, loc, dst, zdst, h, rank_t, gates_t):
    blk_t = pl.BlockSpec((N_EXP, B_SORT), lambda b, *_: (0, b))
    return pl.pallas_call(
        _sort_body,
        grid_spec=pltpu.PrefetchScalarGridSpec(
            num_scalar_prefetch=4,
            grid=(N_BLK,),
            in_specs=[pl.BlockSpec((B_SORT, D), lambda b, *_: (b, 0)), blk_t, blk_t],
            out_specs=[pl.BlockSpec(memory_space=pl.ANY),
                       pl.BlockSpec((RB, B_SORT), lambda b, *_: (b, 0))],
            scratch_shapes=[pltpu.VMEM((2, RB, XS_W), BF16),
                            pltpu.VMEM((UNIT, XS_W), BF16),
                            pltpu.SemaphoreType.DMA((2,))]),
        out_shape=[jax.ShapeDtypeStruct((R_SORT, XS_W), BF16),
                   jax.ShapeDtypeStruct((N_BLK * RB, B_SORT), BF16)],
        compiler_params=_cparams(("arbitrary",)),
        name=f"moe_sort_l{li}",
    )(eou, loc, dst, zdst, h, rank_t, gates_t)


def _experts_body(te_ref, ntile_ref, tile0_ref, wslot_ref, nexte_ref, ns_ref,
                  xs_hbm, wg_hbm, wu_hbm, wd_hbm, os_hbm,
                  xs_buf, o_buf, wg_f, wu_f, wd_f, wg_s, wu_s, wd_s, sem_in, sem_out, sem_w, *, li):
    ns = ns_ref[0]

    def tiles_of(step):
        return jnp.where((step >= 0) & (step < ns), ntile_ref[jnp.clip(step, 0, NS_MAX - 1)], 0)

    def tile_rows(step, t):
        return pl.ds(pl.multiple_of((tile0_ref[jnp.clip(step, 0, NS_MAX - 1)] + t) * TMG, TMG), TMG)

    def buf_rows(t):
        return pl.ds(pl.multiple_of(t * TMG, TMG), TMG)

    def copy_in(step, t, s):
        return pltpu.make_async_copy(xs_hbm.at[tile_rows(step, t)], xs_buf.at[s, buf_rows(t)], sem_in.at[s])

    def copy_out(step, t, s):
        return pltpu.make_async_copy(o_buf.at[s, buf_rows(t)], os_hbm.at[tile_rows(step, t)], sem_out.at[s])

    def for_tiles(step, fn):
        lax.fori_loop(0, tiles_of(step), lambda t, c: (fn(t), c)[1], 0)

    def weight_copies(e, s):
        return [pltpu.make_async_copy(hbm.at[li, e], buf.at[s], sem_w.at[s])
                for hbm, buf in ((wg_hbm, wg_f), (wu_hbm, wu_f), (wd_hbm, wd_f))]

    for c in weight_copies(te_ref[0], 0):
        c.start()
    for_tiles(0, lambda t: copy_in(0, t, 0).start())

    def step(i, carry):
        slot = i % 2
        for_tiles(i + 1, lambda t: copy_in(i + 1, t, 1 - slot).start())
        for_tiles(i - 2, lambda t: copy_out(i - 2, t, slot).wait())
        e = te_ref[i]
        ws = wslot_ref[i]

        @pl.when(ws >= 0)
        def _():
            for c in weight_copies(e, ws):
                c.wait()
            wg_s[...] = wg_f[ws].astype(BF16)
            wu_s[...] = wu_f[ws].astype(BF16)
            wd_s[...] = wd_f[ws].astype(BF16)

            @pl.when(nexte_ref[i] >= 0)
            def _():
                for c in weight_copies(nexte_ref[i], 1 - ws):
                    c.start()

        ntile = ntile_ref[i]
        for_tiles(i, lambda t: copy_in(i, t, slot).wait())
        for k in range(1, TPS + 1):
            rows = k * TMG

            @pl.when(ntile == k)
            def _():
                x = xs_buf[slot, 0:rows, 0:D]
                lane = lax.broadcasted_iota(jnp.int32, (rows, 128), 1)
                mine = (lane == e) | (lane == e + N_EXP)
                gate = jnp.sum(jnp.where(mine, xs_buf[slot, 0:rows, D:XS_W].astype(F32), 0.0),
                               axis=1, keepdims=True)
                hid = _silu(_dot(x, wg_s[...])) * _dot(x, wu_s[...]) * gate
                o_buf[slot, 0:rows, :] = _dot(hid.astype(BF16), wd_s[...]).astype(BF16)

        for_tiles(i, lambda t: copy_out(i, t, slot).start())
        return carry

    lax.fori_loop(0, ns, step, 0)
    for_tiles(ns - 2, lambda t: copy_out(ns - 2, t, ns % 2).wait())
    for_tiles(ns - 1, lambda t: copy_out(ns - 1, t, (ns - 1) % 2).wait())
    o_buf[0, 0:UNIT, :] = jnp.zeros((UNIT, D), BF16)
    zero = pltpu.make_async_copy(o_buf.at[0, 0:UNIT], os_hbm.at[_unit_rows(ZERO_UNIT)], sem_out.at[0])
    zero.start()
    zero.wait()


def _moe_experts(li, te, ntile, tile0, wslot, nexte, ns, xs, w_gate, w_up, w_down):
    return pl.pallas_call(
        functools.partial(_experts_body, li=li),
        grid_spec=pltpu.PrefetchScalarGridSpec(
            num_scalar_prefetch=6,
            grid=(1,),
            in_specs=[pl.BlockSpec(memory_space=pl.ANY)] * 4,
            out_specs=pl.BlockSpec(memory_space=pl.ANY),
            scratch_shapes=[pltpu.VMEM((2, STEP_ROWS, XS_W), BF16),
                            pltpu.VMEM((2, STEP_ROWS, D), BF16),
                            pltpu.VMEM((2, D, EXP_DIM), F32),
                            pltpu.VMEM((2, D, EXP_DIM), F32),
                            pltpu.VMEM((2, EXP_DIM, D), F32),
                            pltpu.VMEM((D, EXP_DIM), BF16),
                            pltpu.VMEM((D, EXP_DIM), BF16),
                            pltpu.VMEM((EXP_DIM, D), BF16),
                            pltpu.SemaphoreType.DMA((2,)),
                            pltpu.SemaphoreType.DMA((2,)),
                            pltpu.SemaphoreType.DMA((2,))]),
        out_shape=jax.ShapeDtypeStruct((R_OUT, D), BF16),
        compiler_params=_cparams(("arbitrary",)),
        name=f"moe_experts_l{li}",
    )(te, ntile, tile0, wslot, nexte, ns, xs, w_gate, w_up, w_down)


def _combine_body(dst_ref, h_ref, p_ref, y_ref, mod_ref, gpost_ref,
                  wsg_ref, wsu_ref, wsd_ref, os_hbm, op_ref, ol_ref, os_scr, sem):
    b = pl.program_id(0)
    slot = b % 2

    def copy(d, u, s):
        return pltpu.make_async_copy(os_hbm.at[_unit_rows(d)], os_scr.at[s, _unit_rows(u)], sem.at[s])

    def fetch(blk, s):
        lax.fori_loop(0, UPB, lambda u, c: (copy(dst_ref[blk, u], u, s).start(), c)[1], 0, unroll=UNROLL)

    @pl.when(b == 0)
    def _():
        fetch(0, 0)

    @pl.when(b + 1 < N_BLK)
    def _():
        fetch(b + 1, 1 - slot)

    h = h_ref[...]
    shared = _dot((_silu(_dot(h, wsg_ref[...])) * _dot(h, wsu_ref[...])).astype(BF16), wsd_ref[...])
    lax.fori_loop(0, UPB, lambda k, c: (copy(0, 0, slot).wait(), c)[1], 0, unroll=True)
    routed = lax.dot_general(os_scr[slot], p_ref[...], (((0,), (0,)), ((), ())),
                             preferred_element_type=F32).T
    out = y_ref[...] + mod_ref[5:6, :] * _rms(routed + shared, gpost_ref[...])

    @pl.when(b < T_P // B_SORT)
    def _():
        op_ref[...] = out

    @pl.when(b >= T_P // B_SORT)
    def _():
        ol_ref[...] = out


def _moe_combine(li, dst, h, p, y, modv, g_post, wsg_bf, wsu_bf, wsd_bf, os):
    nbp = T_P // B_SORT
    def tok(w):
        return pl.BlockSpec((B_SORT, w), lambda b, *_: (b, 0))

    def par(*shape):
        return pl.BlockSpec((None,) + shape, lambda b, *_: (li,) + (0,) * len(shape))

    return pl.pallas_call(
        _combine_body,
        grid_spec=pltpu.PrefetchScalarGridSpec(
            num_scalar_prefetch=1,
            grid=(N_BLK,),
            in_specs=[tok(D), pl.BlockSpec((RB, B_SORT), lambda b, *_: (b, 0)), tok(D),
                      pl.BlockSpec((None, None, 6, D), lambda b, *_: (li, _mod_row(b, B_SORT), 0, 0)),
                      par(1, D), par(D, EXP_DIM), par(D, EXP_DIM), par(EXP_DIM, D),
                      pl.BlockSpec(memory_space=pl.ANY)],
            out_specs=[pl.BlockSpec((B_SORT, D), lambda b, *_: (jnp.minimum(b, nbp - 1), 0)),
                       pl.BlockSpec((B_SORT, D), lambda b, *_: (jnp.maximum(b - nbp, 0), 0))],
            scratch_shapes=[pltpu.VMEM((2, RB, D), BF16),
                            pltpu.SemaphoreType.DMA((2,))]),
        out_shape=[jax.ShapeDtypeStruct((T_P, D), F32), jax.ShapeDtypeStruct((T_S, D), F32)],
        compiler_params=_cparams(("arbitrary",)),
        name=f"moe_combine_l{li}",
    )(dst, h, p, y, modv, g_post, wsg_bf, wsu_bf, wsd_bf, os)


def _block_diag(w):
    out = jnp.zeros((DEPTH, POOL_W, POOL_W), F32)
    for g in range(len(POOL_WINDOWS)):
        out = out.at[:, g * POOL_CH:(g + 1) * POOL_CH, g * POOL_CH:(g + 1) * POOL_CH].set(w[:, g])
    return out


def kernel(x_prompt, x_sample, cache_diff_k, cache_diff_v, cache_na_k, cache_na_v, c, c_ctx,
           w_ada, b_ada, g_pre_mix, g_post_mix, g_pre_ffn, g_post_ffn, w_in, w_out,
           diff_lambda, g_diff, na_bias, pool_w, pool_scale, w_router, b_router,
           w_gate, w_up, w_down, ws_gate, ws_up, ws_down):
    xp, xs = x_prompt.reshape(T_P, D), x_sample.reshape(T_S, D)
    cvec = jnp.concatenate([c_ctx[None, :], c, jnp.zeros((3, D), F32)], axis=0)
    modv = _modulation(cvec, w_ada, b_ada)[:, :1 + NB_S].reshape(DEPTH, 1 + NB_S, 6, D)

    cos_t, sin_t = _rope_tables()
    w_in_bf = w_in.astype(BF16)
    w_out_bf = w_out.astype(BF16)
    wsg_bf, wsu_bf, wsd_bf = ws_gate.astype(BF16), ws_up.astype(BF16), ws_down.astype(BF16)
    pool_bd = _block_diag(pool_w)
    row = lambda a: a.reshape(DEPTH, 1, a.shape[-1])
    g_pre_mix, g_post_mix, g_pre_ffn, g_post_ffn = map(row, (g_pre_mix, g_post_mix, g_pre_ffn, g_post_ffn))
    g_diff, pool_scale = row(g_diff), row(pool_scale)
    w_router_t = jnp.swapaxes(w_router, 1, 2)
    b_router_col = b_router.reshape(DEPTH, N_EXP, 1)
    tri = _rank_matrix()
    ck = cache_diff_k.reshape(NB_S, DEPTH, PAST, DA_W)
    cv = cache_diff_v.reshape(NB_S, DEPTH, PAST, DA_W)
    cnk = cache_na_k.reshape(NB_S, DEPTH, PAST, NA_W)
    cnv = cache_na_v.reshape(NB_S, DEPTH, PAST, NA_W)

    strips = _na_bias_strips(na_bias)

    caches = ()
    for li in range(DEPTH):
        lam_init = 0.8 - 0.6 * math.exp(-0.3 * li)
        qa, qb, up, ka, va, kb, vb, *caches = _in_proj(li, xp, xs, modv, g_pre_mix, w_in_bf, cos_t, sin_t,
                                                      caches)
        ocp = _pool_call(li, up, pool_bd, pool_scale, N_P, NB_P, 0, f"pool_ctx_l{li}")
        ocs = _pool_call(li, up, pool_bd, pool_scale, N_S, NB_S, T_P // N_S, f"pool_lat_l{li}")
        oabp = _attn_ctx(li, diff_lambda, g_diff, qa, caches[0], caches[1], qb, caches[2], caches[3], lam_init)
        oabs = _attn_lat(li, diff_lambda, g_diff, qa, ka, va, ck, cv, qb, kb, vb, cnk, cnv, strips, lam_init)
        y, h, gates_t, rank_t, pcu = _merge(li, oabp, oabs, ocp, ocs, xp, xs, modv, g_post_mix, g_pre_ffn,
                                            w_out_bf, w_router_t, b_router_col, tri)
        eou, loc, dst_sort, dst_comb, zdst, te, ntile, tile0, wslot, nexte, ns = _moe_tables(pcu)
        srt, p = _moe_sort(li, eou, loc, dst_sort, zdst, h, rank_t, gates_t)
        os = _moe_experts(li, te, ntile, tile0, wslot, nexte, ns, srt, w_gate, w_up, w_down)
        xp, xs = _moe_combine(li, dst_comb, h, p, y, modv, g_post_ffn, wsg_bf, wsu_bf, wsd_bf, os)

    new_dk, new_dv, new_nk, new_nv = caches
    return (xp.reshape(NB_P, N_P, D), xs.reshape(NB_S, N_S, D),
            new_dk.reshape(NB_P, DEPTH, N_P, DA_H, 2 * DA_QK), new_dv.reshape(NB_P, DEPTH, N_P, DA_H, DA_V),
            new_nk.reshape(NB_P, DEPTH, N_P, NA_H, NA_D), new_nv.reshape(NB_P, DEPTH, N_P, NA_H, NA_D))
```

```python
import functools
import math

import numpy as np
import jax
import jax.numpy as jnp
from jax import lax
from jax.experimental import pallas as pl
from jax.experimental.pallas import tpu as pltpu

F32 = jnp.float32
BF16 = jnp.bfloat16

D = 1024
DEPTH = 2
NB_P, N_P = 16, 256
NB_S, N_S = 4, 1024
T_P = NB_P * N_P
T_S = NB_S * N_S
T = T_P + T_S
PAST = 512
GRID_W = 64
DA_H, DA_QK, DA_V = 4, 64, 128
DA_W = DA_H * DA_V
NA_H, NA_D = 4, 64
NA_W = NA_H * NA_D
NA_ROWS, NA_COLS = 8, 16
POOL_WINDOWS = (2, 4, 8, 16)
POOL_CH = 64
POOL_W = 256
IN_COLS = 3 * DA_W + 3 * NA_W + POOL_W
AB_W = DA_W + NA_W
N_EXP, TOP_K, N_GRP, TOPK_GRP = 64, 8, 8, 4
EXP_DIM = 256
ROUTED_SCALE = 2.5
EPS = 1e-6
NEG = -1e30
ROPE_THETA = 10000.0

VMEM_LIMIT = 56 * 1024 * 1024

TM_IN = 512
NT_P = T_P // TM_IN
TQ = 512

B_SORT = 256
UNIT = 16
N_BLK = T // B_SORT
RUN_PAD_ROWS = TOP_K * B_SORT + N_EXP * (UNIT - 1)
RB = -(-RUN_PAD_ROWS // 256) * 256
UPB = RB // UNIT
TMG = 256
UPT = TMG // UNIT
NT_MAX = (N_BLK * RUN_PAD_ROWS) // TMG + N_EXP
XS_W = D + 128
EXP_PER_BLK = N_EXP // N_BLK
ZPB = EXP_PER_BLK * (UPT - 1)
TPS = 6
STEP_ROWS = TPS * TMG
NS_MAX = NT_MAX // TPS + (N_EXP * (TPS - 1)) // TPS
ZERO_UNIT = NT_MAX * UPT
SPARE_UNIT0 = ZERO_UNIT + UPT
R_OUT = (NT_MAX + 1) * TMG
R_SORT = R_OUT + -(-2 * (UPB + ZPB) // UPT) * TMG
SORT_CHUNK = 512
UNROLL = 8


def _cparams(sem):
    return pltpu.CompilerParams(dimension_semantics=sem, vmem_limit_bytes=VMEM_LIMIT)


def _rms(x, g):
    return x * lax.rsqrt(jnp.mean(x * x, axis=-1, keepdims=True) + EPS) * g


def _dot(a, b):
    return jnp.dot(a, b, preferred_element_type=F32)


def _dot_nt(a, b):
    return lax.dot_general(a, b, (((1,), (1,)), ((), ())), preferred_element_type=F32)


def _silu(x):
    return x / (1.0 + jnp.exp(-x))


def _mod_row(i, tm):
    off = i * tm - T_P
    return jnp.where(off >= 0, 1 + jnp.maximum(off, 0) // N_S, 0)


def _mod_body(c_ref, w_ref, b_ref, o_ref):
    c = c_ref[...]
    o_ref[...] = jnp.dot(_silu(c), w_ref[...], precision=lax.Precision.HIGHEST,
                         preferred_element_type=F32) + b_ref[...]


def _modulation(cvec, w_ada, b_ada):
    tn = 1536
    return pl.pallas_call(
        _mod_body,
        grid=(DEPTH, 6 * D // tn),
        in_specs=[pl.BlockSpec((8, D), lambda l, j: (0, 0)),
                  pl.BlockSpec((None, D, tn), lambda l, j: (l, 0, j)),
                  pl.BlockSpec((None, 1, tn), lambda l, j: (l, 0, j))],
        out_specs=pl.BlockSpec((None, 8, tn), lambda l, j: (l, 0, j)),
        out_shape=jax.ShapeDtypeStruct((DEPTH, 8, 6 * D), F32),
        compiler_params=_cparams(("parallel", "parallel")),
        name="modulation",
    )(cvec, w_ada, b_ada.reshape(DEPTH, 1, 6 * D))


def _rope_tables():
    nf = DA_QK // 4
    t = np.arange(N_S)
    pos = np.stack([t // GRID_W, t % GRID_W], axis=-1).astype(np.float32)
    inv = np.power(np.float32(ROPE_THETA), -np.arange(nf, dtype=np.float32) / nf)
    ang = pos[:, :, None] * inv
    cos = np.cos(ang)
    sin = np.sin(ang)
    cos64 = np.concatenate([cos[:, 0], cos[:, 0], cos[:, 1], cos[:, 1]], axis=-1)
    sin64 = np.concatenate([-sin[:, 0], sin[:, 0], -sin[:, 1], sin[:, 1]], axis=-1)
    reps = DA_W // DA_QK
    cos_t = np.concatenate([np.tile(cos64, (1, reps)), np.ones((TM_IN, DA_W), np.float32)], axis=0)
    sin_t = np.concatenate([np.tile(sin64, (1, reps)), np.zeros((TM_IN, DA_W), np.float32)], axis=0)
    return jnp.asarray(cos_t, F32), jnp.asarray(sin_t, F32)


def _tok_select(i, ctx_ref, lat_ref):
    return jnp.where(i < NT_P, ctx_ref[...], lat_ref[...])


def _in_proj_body(*refs):
    (xp_ref, xs_ref, mod_ref, g_ref, w_ref, cos_ref, sin_ref, pw_ref, ps_ref) = refs[:9]
    (qa_ref, qb_ref, ocp_ref, up_ref, ka_ref, va_ref, kb_ref, vb_ref,
     ck_ref, cv_ref, cnk_ref, cnv_ref) = refs[-12:]
    i = pl.program_id(0)
    x = _tok_select(i, xp_ref, xs_ref)
    h = (_rms(x, g_ref[...]) * (1.0 + mod_ref[1:2, :]) + mod_ref[0:1, :]).astype(BF16)
    cos = cos_ref[...]
    sin = sin_ref[...]
    lane = lax.broadcasted_iota(jnp.int32, (TM_IN, DA_W), 1)
    first = (lane % 32) < 16

    def proj(lo, hi):
        return _dot(h, w_ref[:, lo:hi])

    def rope(t):
        swapped = jnp.where(first, pltpu.roll(t, DA_W - 16, 1), pltpu.roll(t, 16, 1))
        return t * cos + swapped * sin

    o = 0
    qa_ref[...] = (rope(proj(o, o + DA_W)) * (DA_QK ** -0.5)).astype(BF16)
    o += DA_W
    ka = rope(proj(o, o + DA_W))
    o += DA_W
    va = proj(o, o + DA_W)
    o += DA_W
    qb_ref[...] = (proj(o, o + NA_W) * (NA_D ** -0.5)).astype(BF16)
    o += NA_W
    kb = proj(o, o + NA_W)
    o += NA_W
    vb = proj(o, o + NA_W)
    o += NA_W
    up = proj(o, o + POOL_W)

    @pl.when(i < NT_P)
    def _():
        for ref, val in ((ck_ref, ka), (cv_ref, va), (cnk_ref, kb), (cnv_ref, vb)):
            ref[...] = val.reshape(ref.shape)
        ocp_ref[...] = _pool_rows(up, N_P, pw_ref[...], ps_ref[...])

    @pl.when(i >= NT_P)
    def _():
        for ref, val in ((ka_ref, ka), (va_ref, va), (kb_ref, kb), (vb_ref, vb)):
            ref[...] = val.astype(BF16)
        up_ref[...] = up


def _ctx_tok(w):
    return pl.BlockSpec((TM_IN, w), lambda i: (jnp.minimum(i, NT_P - 1), 0))


def _lat_tok(w):
    return pl.BlockSpec((TM_IN, w), lambda i: (jnp.maximum(i - NT_P, 0), 0))


def _in_proj(li, xp, xs, modv, g_pre, w_in_bf, cos_t, sin_t, pool_bd, pool_scale, caches):
    n_pos_blk = N_S // TM_IN

    def pos_blk(i):
        return jnp.where(i >= NT_P, jnp.maximum(i - NT_P, 0) % n_pos_blk, n_pos_blk)

    def tok(w):
        return pl.BlockSpec((TM_IN, w), lambda i: (i, 0))

    def cache(w):
        return pl.BlockSpec((TM_IN // N_P, None, N_P, w), lambda i: (jnp.minimum(i, NT_P - 1), li, 0, 0))

    widths = (DA_W, DA_W, NA_W, NA_W)
    n_in, n_plain_out = 9, 8
    return pl.pallas_call(
        _in_proj_body,
        grid=(T // TM_IN,),
        in_specs=[_ctx_tok(D), _lat_tok(D),
                  pl.BlockSpec((None, None, 6, D), lambda i: (li, _mod_row(i, TM_IN), 0, 0)),
                  pl.BlockSpec((None, 1, D), lambda i: (li, 0, 0)),
                  pl.BlockSpec((None, D, IN_COLS), lambda i: (li, 0, 0)),
                  pl.BlockSpec((TM_IN, DA_W), lambda i: (pos_blk(i), 0)),
                  pl.BlockSpec((TM_IN, DA_W), lambda i: (pos_blk(i), 0)),
                  pl.BlockSpec((None, POOL_W, POOL_W), lambda i: (li, 0, 0)),
                  pl.BlockSpec((None, 1, POOL_W), lambda i: (li, 0, 0))]
                 + [pl.BlockSpec(memory_space=pl.ANY)] * len(caches),
        out_specs=[tok(DA_W), tok(NA_W), _ctx_tok(POOL_W), _lat_tok(POOL_W)] + [_lat_tok(w) for w in widths]
                  + [cache(w) for w in widths],
        out_shape=[jax.ShapeDtypeStruct((T, DA_W), BF16),
                   jax.ShapeDtypeStruct((T, NA_W), BF16),
                   jax.ShapeDtypeStruct((T_P, POOL_W), BF16),
                   jax.ShapeDtypeStruct((T_S, POOL_W), F32)]
                  + [jax.ShapeDtypeStruct((T_S, w), BF16) for w in widths]
                  + [jax.ShapeDtypeStruct((NB_P, DEPTH, N_P, w), F32) for w in widths],
        input_output_aliases={n_in + k: n_plain_out + k for k in range(len(caches))},
        compiler_params=_cparams(("arbitrary",)),
        name=f"in_proj_l{li}",
    )(xp, xs, modv, g_pre, w_in_bf, cos_t, sin_t, pool_bd, pool_scale, *caches)


def _pool_rows(u, n, w_mix, scale):
    rows = u.shape[0]
    row = lax.broadcasted_iota(jnp.int32, (rows, POOL_W), 0) % n
    lane = lax.broadcasted_iota(jnp.int32, (rows, POOL_W), 1)

    def shift_dn(a, k):
        return jnp.where(row >= k, pltpu.roll(a, k, 0), 0.0)

    def shift_up(a, k):
        return jnp.where(row < n - k, pltpu.roll(a, rows - k, 0), 0.0)

    fwd = u
    bwd = shift_dn(u, 1)
    mean = jnp.zeros_like(u)
    k = 1
    for gi, w in enumerate(POOL_WINDOWS):
        while k < w // 2:
            fwd = fwd + shift_up(fwd, k)
            bwd = bwd + shift_dn(bwd, k)
            k *= 2
        cnt = (jnp.minimum(row + w // 2, n) - jnp.maximum(row - w // 2, 0)).astype(F32)
        in_group = (lane >= gi * POOL_CH) & (lane < (gi + 1) * POOL_CH)
        mean = jnp.where(in_group, (fwd + bwd) / cnt, mean)
    pooled = mean - u
    mixed = jnp.dot(pooled, w_mix, precision=lax.Precision.HIGHEST, preferred_element_type=F32)
    return (mixed * scale).astype(BF16)


def _pool_body(u_ref, w_ref, s_ref, o_ref):
    o_ref[...] = _pool_rows(u_ref[...], N_S, w_ref[...], s_ref[...])


def _pool_lat(li, up, w_bd, scale):
    return pl.pallas_call(
        _pool_body,
        grid=(NB_S,),
        in_specs=[pl.BlockSpec((N_S, POOL_W), lambda i: (i, 0)),
                  pl.BlockSpec((None, POOL_W, POOL_W), lambda i: (li, 0, 0)),
                  pl.BlockSpec((None, 1, POOL_W), lambda i: (li, 0, 0))],
        out_specs=pl.BlockSpec((N_S, POOL_W), lambda i: (i, 0)),
        out_shape=jax.ShapeDtypeStruct((T_S, POOL_W), BF16),
        compiler_params=_cparams(("parallel",)),
        name=f"pool_lat_l{li}",
    )(up, w_bd, scale)


def _lambda(lp, lam_init):
    a = jnp.sum(lp[0:1, :] * lp[1:2, :], axis=1, keepdims=True)
    b = jnp.sum(lp[2:3, :] * lp[3:4, :], axis=1, keepdims=True)
    return jnp.exp(a) - jnp.exp(b) + lam_init


def _softmax_av(scores, vs):
    m = functools.reduce(jnp.maximum, [jnp.max(s, axis=-1, keepdims=True) for s in scores])
    es = [jnp.exp(s - m) for s in scores]
    den = functools.reduce(jnp.add, [jnp.sum(e, axis=-1, keepdims=True) for e in es])
    o = functools.reduce(jnp.add, [_dot(e.astype(BF16), v) for e, v in zip(es, vs)])
    return o * (1.0 / den)


def _diff_head(q, ks, vs, lam, gd, lam_init):
    o1 = _softmax_av([_dot_nt(q[:, :DA_QK], k[:, :DA_QK]) for k in ks], vs)
    o2 = _softmax_av([_dot_nt(q[:, DA_QK:], k[:, DA_QK:]) for k in ks], vs)
    return _rms(o1 - lam * o2, gd) * (1.0 - lam_init)


def _plain_head(q, ks, vs, biases):
    scores = []
    for k, bias in zip(ks, biases):
        s = _dot_nt(q, k)
        scores.append(s if bias is None else s + bias)
    return _softmax_av(scores, vs)


def _attn_ctx_body(lam_ref, gd_ref, qa_ref, ka_ref, va_ref, qb_ref, kb_ref, vb_ref, o_ref, *, lam_init):
    lam = _lambda(lam_ref[...], lam_init)
    gd = gd_ref[...]
    for h in range(DA_H):
        sl = slice(h * DA_V, (h + 1) * DA_V)
        o = _diff_head(qa_ref[:, sl], [ka_ref[:, sl].astype(BF16)], [va_ref[:, sl].astype(BF16)],
                       lam, gd, lam_init)
        o_ref[:, sl] = o.astype(BF16)
    for h in range(NA_H):
        sl = slice(h * NA_D, (h + 1) * NA_D)
        o = _plain_head(qb_ref[:, sl], [kb_ref[:, sl].astype(BF16)], [vb_ref[:, sl].astype(BF16)], [None])
        o_ref[:, DA_W + h * NA_D:DA_W + (h + 1) * NA_D] = o.astype(BF16)


def _attn_ctx(li, lam_p, g_diff, qa, ka, va, qb, kb, vb, lam_init):
    def blk(w):
        return pl.BlockSpec((N_P, w), lambda b: (b, 0))

    def cblk(w):
        return pl.BlockSpec((None, None, N_P, w), lambda b: (b, li, 0, 0))

    return pl.pallas_call(
        functools.partial(_attn_ctx_body, lam_init=lam_init),
        grid=(NB_P,),
        in_specs=[pl.BlockSpec((None, 4, DA_QK), lambda b: (li, 0, 0)),
                  pl.BlockSpec((None, 1, DA_V), lambda b: (li, 0, 0)),
                  blk(DA_W), cblk(DA_W), cblk(DA_W), blk(NA_W), cblk(NA_W), cblk(NA_W)],
        out_specs=pl.BlockSpec((N_P, AB_W), lambda b: (b, 0)),
        out_shape=jax.ShapeDtypeStruct((T_P, AB_W), BF16),
        compiler_params=_cparams(("parallel",)),
        name=f"attn_ctx_l{li}",
    )(lam_p, g_diff, qa, ka, va, qb, kb, vb)


G_ROWS = N_S // GRID_W
QR_PER_BLK = TQ // GRID_W


def _fill_na_bias(j, strip_ref, bias_scr):
    for jj in range(N_S // TQ):
        @pl.when(j == jj)
        def _():
            for rr in range(QR_PER_BLK):
                qr = jj * QR_PER_BLK + rr
                rs = min(max(qr - NA_ROWS // 2, 0), G_ROWS - NA_ROWS)
                lo, hi = rs * GRID_W, (rs + NA_ROWS) * GRID_W
                a0 = (rs - qr + NA_ROWS - 1) * GRID_W
                q = slice(rr * GRID_W, (rr + 1) * GRID_W)
                for h in range(NA_H):
                    if lo > 0:
                        bias_scr[h, q, 0:lo] = jnp.full((GRID_W, lo), NEG, F32)
                    bias_scr[h, q, lo:hi] = strip_ref[h, :, a0:a0 + NA_ROWS * GRID_W]
                    if hi < N_S:
                        bias_scr[h, q, hi:N_S] = jnp.full((GRID_W, N_S - hi), NEG, F32)


def _attn_lat_body(lam_ref, gd_ref, qa_ref, ka_ref, va_ref, ck_ref, cv_ref,
                   qb_ref, kb_ref, vb_ref, cnk_ref, cnv_ref, strip_ref, o_ref, bias_scr, *, lam_init):
    _fill_na_bias(pl.program_id(1), strip_ref, bias_scr)
    lam = _lambda(lam_ref[...], lam_init)
    gd = gd_ref[...]
    for h in range(DA_H):
        sl = slice(h * DA_V, (h + 1) * DA_V)
        ks = [ka_ref[:, sl], ck_ref[:, sl].astype(BF16)]
        vs = [va_ref[:, sl], cv_ref[:, sl].astype(BF16)]
        o_ref[:, sl] = _diff_head(qa_ref[:, sl], ks, vs, lam, gd, lam_init).astype(BF16)
    for h in range(NA_H):
        sl = slice(h * NA_D, (h + 1) * NA_D)
        ks = [kb_ref[:, sl], cnk_ref[:, sl].astype(BF16)]
        vs = [vb_ref[:, sl], cnv_ref[:, sl].astype(BF16)]
        o = _plain_head(qb_ref[:, sl], ks, vs, [bias_scr[h], None])
        o_ref[:, DA_W + h * NA_D:DA_W + (h + 1) * NA_D] = o.astype(BF16)


def _attn_lat(li, lam_p, g_diff, qa, ka, va, ck, cv, qb, kb, vb, cnk, cnv, strip, lam_init):
    nq = N_S // TQ
    q0 = T_P // TQ

    def qblk(w):
        return pl.BlockSpec((TQ, w), lambda b, j: (q0 + b * nq + j, 0))

    def kvblk(w):
        return pl.BlockSpec((N_S, w), lambda b, j: (b, 0))

    def cblk(w):
        return pl.BlockSpec((None, None, PAST, w), lambda b, j: (b, li, 0, 0))

    return pl.pallas_call(
        functools.partial(_attn_lat_body, lam_init=lam_init),
        grid=(NB_S, nq),
        in_specs=[pl.BlockSpec((None, 4, DA_QK), lambda b, j: (li, 0, 0)),
                  pl.BlockSpec((None, 1, DA_V), lambda b, j: (li, 0, 0)),
                  qblk(DA_W), kvblk(DA_W), kvblk(DA_W), cblk(DA_W), cblk(DA_W),
                  qblk(NA_W), kvblk(NA_W), kvblk(NA_W), cblk(NA_W), cblk(NA_W),
                  pl.BlockSpec((None, NA_H, GRID_W, (2 * NA_ROWS - 1) * GRID_W), lambda b, j: (li, 0, 0, 0))],
        out_specs=pl.BlockSpec((TQ, AB_W), lambda b, j: (b * nq + j, 0)),
        out_shape=jax.ShapeDtypeStruct((T_S, AB_W), BF16),
        scratch_shapes=[pltpu.VMEM((NA_H, TQ, N_S), F32)],
        compiler_params=_cparams(("parallel", "parallel")),
        name=f"attn_lat_l{li}",
    )(lam_p, g_diff, qa, ka, va, ck, cv, qb, kb, vb, cnk, cnv, strip)


def _na_bias_strips(table):
    w = np.arange(GRID_W)
    cs = np.clip(w - NA_COLS // 2, 0, GRID_W - NA_COLS)
    col_ok = (w[None, :] >= cs[:, None]) & (w[None, :] < cs[:, None] + NA_COLS)
    sel_c = (np.clip(w[None, :, None] - w[:, None, None] + NA_COLS - 1, 0, 2 * NA_COLS - 2)
             == np.arange(2 * NA_COLS - 1)).astype(np.float32)
    t1 = jnp.einsum('lhab,cdb->lhcad', table.astype(F32), jnp.asarray(sel_c),
                    precision=lax.Precision.HIGHEST)
    t1 = jnp.where(jnp.asarray(col_ok)[None, None, :, None, :], t1, NEG)
    return t1.reshape(DEPTH, NA_H, GRID_W, (2 * NA_ROWS - 1) * GRID_W)


def _route_t(logits_t, b_col, tm):
    per = N_EXP // N_GRP
    s = 1.0 / (1.0 + jnp.exp(-logits_t))
    sel3 = (s + b_col).reshape(N_GRP, per, tm)
    midx = lax.broadcasted_iota(jnp.int32, (N_GRP, per, tm), 1)
    gidx = lax.broadcasted_iota(jnp.int32, (N_GRP, per, tm), 0)
    ninf = -jnp.inf
    m1 = jnp.max(sel3, axis=1, keepdims=True)
    i1 = jnp.min(jnp.where(sel3 == m1, midx, per), axis=1, keepdims=True)
    m2 = jnp.max(jnp.where(midx == i1, ninf, sel3), axis=1, keepdims=True)
    gsc = jnp.broadcast_to(m1 + m2, (N_GRP, per, tm))
    cnt = jnp.zeros((N_GRP, per, tm), jnp.int32)
    for g in range(N_GRP):
        sg = gsc[g][None]
        cnt = cnt + ((sg > gsc) | ((sg == gsc) & (g < gidx))).astype(jnp.int32)
    x = jnp.where(cnt < TOPK_GRP, sel3, ninf).reshape(N_EXP, tm)
    eidx = lax.broadcasted_iota(jnp.int32, (N_EXP, tm), 0)
    cnt = jnp.zeros((N_EXP, tm), jnp.int32)
    for e in range(N_EXP):
        row = x[e:e + 1, :]
        cnt = cnt + ((row > x) | ((row == x) & (e < eidx))).astype(jnp.int32)
    chosen = cnt < TOP_K
    w = jnp.where(chosen, s, 0.0)
    return chosen, w / jnp.sum(w, axis=0, keepdims=True) * ROUTED_SCALE


def _merge_body(oabp_ref, oabs_ref, ocp_ref, ocs_ref, xp_ref, xs_ref, mod_ref, gpost_ref, gpre_ref,
                wout_ref, wrt_ref, br_ref, tri_ref, y_ref, h_ref, gates_ref, rank_ref, pcu_ref):
    i = pl.program_id(0)
    mix = (_dot(_tok_select(i, oabp_ref, oabs_ref), wout_ref[0:AB_W, :])
           + _dot(_tok_select(i, ocp_ref, ocs_ref), wout_ref[AB_W:D, :]))
    y = _tok_select(i, xp_ref, xs_ref) + mod_ref[2:3, :] * _rms(mix, gpost_ref[...])
    y_ref[...] = y
    h = _rms(y, gpre_ref[...]) * (1.0 + mod_ref[4:5, :]) + mod_ref[3:4, :]
    h_ref[...] = h.astype(BF16)
    logits_t = lax.dot_general(wrt_ref[...], h, (((1,), (1,)), ((), ())),
                               precision=lax.Precision.HIGHEST, preferred_element_type=F32)
    chosen, gates = _route_t(logits_t, br_ref[...], TM_IN)
    gates_ref[...] = gates
    ch = jnp.where(chosen, 1.0, 0.0)
    rank = _dot(ch.astype(BF16), tri_ref[...])
    rank_ref[...] = jnp.where(chosen, rank, -1.0)
    for j in range(TM_IN // B_SORT):
        cnt = jnp.sum(ch[:, j * B_SORT:(j + 1) * B_SORT], axis=1, keepdims=True)
        units = jnp.floor((cnt + (UNIT - 1)) * (1.0 / UNIT))
        pcu_ref[j] = jnp.broadcast_to(units, (N_EXP, 128))


def _merge(li, oabp, oabs, ocp, ocs, xp, xs, modv, g_post, g_pre, w_out_bf, w_router_t, b_router_col, tri):
    def tok(w):
        return pl.BlockSpec((TM_IN, w), lambda i: (i, 0))

    def par(*shape):
        return pl.BlockSpec((None,) + shape, lambda i: (li,) + (0,) * len(shape))

    tok_t = pl.BlockSpec((N_EXP, TM_IN), lambda i: (0, i))
    nb = TM_IN // B_SORT
    return pl.pallas_call(
        _merge_body,
        grid=(T // TM_IN,),
        in_specs=[_ctx_tok(AB_W), _lat_tok(AB_W), _ctx_tok(POOL_W), _lat_tok(POOL_W), _ctx_tok(D), _lat_tok(D),
                  pl.BlockSpec((None, None, 6, D), lambda i: (li, _mod_row(i, TM_IN), 0, 0)),
                  par(1, D), par(1, D), par(D, D), par(N_EXP, D), par(N_EXP, 1),
                  pl.BlockSpec((TM_IN, TM_IN), lambda i: (0, 0))],
        out_specs=[tok(D), tok(D), tok_t, tok_t,
                   pl.BlockSpec((nb, N_EXP, 128), lambda i: (i, 0, 0))],
        out_shape=[jax.ShapeDtypeStruct((T, D), F32),
                   jax.ShapeDtypeStruct((T, D), BF16),
                   jax.ShapeDtypeStruct((N_EXP, T), F32),
                   jax.ShapeDtypeStruct((N_EXP, T), F32),
                   jax.ShapeDtypeStruct((N_BLK, N_EXP, 128), F32)],
        compiler_params=_cparams(("parallel",)),
        name=f"merge_l{li}",
    )(oabp, oabs, ocp, ocs, xp, xs, modv, g_post, g_pre, w_out_bf, w_router_t, b_router_col, tri)


def _rank_matrix():
    t = np.arange(TM_IN)
    m = (t[:, None] < t[None, :]) & (t[:, None] // B_SORT == t[None, :] // B_SORT)
    return jnp.asarray(m, BF16)


def _moe_tables(pcu_f):
    pcu = pcu_f[:, :, 0].astype(jnp.int32)
    incl = jnp.cumsum(pcu, axis=1)
    uo = incl - pcu
    tot = jnp.sum(pcu, axis=0)
    gt = (tot + UPT - 1) // UPT
    gstart_t = jnp.cumsum(gt) - gt
    gstart_u = gstart_t * UPT
    steps = (gt + TPS - 1) // TPS
    sincl = jnp.cumsum(steps)
    sstart = sincl - steps
    bstart = gstart_u[None, :] + jnp.cumsum(pcu, axis=0) - pcu
    u = jnp.arange(UPB, dtype=jnp.int32)
    eou = jnp.sum((u[None, :, None] >= incl[:, None, :]).astype(jnp.int32), axis=-1)
    valid = eou < N_EXP
    onehot = eou[:, :, None] == jnp.arange(N_EXP, dtype=jnp.int32)[None, None, :]
    run_u0 = jnp.sum(jnp.where(onehot, uo[:, None, :], 0), axis=-1)
    run_g0 = jnp.sum(jnp.where(onehot, bstart[:, None, :], 0), axis=-1)
    dst = run_g0 + u[None, :] - run_u0
    spare0 = SPARE_UNIT0 + (jnp.arange(N_BLK, dtype=jnp.int32) % 2)[:, None] * (UPB + ZPB)
    dst_sort = jnp.where(valid, dst, spare0 + u[None, :])
    dst_comb = jnp.where(valid, dst, ZERO_UNIT)
    loc = jnp.where(valid, (u[None, :] - run_u0) * UNIT, -(1 << 20))
    k = jnp.arange(UPT - 1, dtype=jnp.int32)
    ztail = gt * UPT - tot
    zdst = jnp.where(k[None, :] < ztail[:, None], (gstart_u + tot)[:, None] + k[None, :], -1).reshape(N_BLK, ZPB)
    zdst = jnp.where(zdst >= 0, zdst, spare0 + UPB + jnp.arange(ZPB, dtype=jnp.int32)[None, :])
    step = jnp.arange(NS_MAX, dtype=jnp.int32)
    te = jnp.minimum(jnp.sum((step[:, None] >= sincl[None, :]).astype(jnp.int32), axis=-1), N_EXP - 1)
    mine = te[:, None] == jnp.arange(N_EXP, dtype=jnp.int32)[None, :]
    pick = lambda v: jnp.sum(jnp.where(mine, v[None, :], 0), axis=-1)
    first = gt - TPS * (steps - 1)
    j = step - pick(sstart)
    ntile = jnp.where(j == 0, pick(first), TPS)
    tile0 = pick(gstart_t) + jnp.where(j == 0, 0, pick(first) + TPS * (j - 1))
    ns = sincl[-1]
    eord = jnp.cumsum((j == 0).astype(jnp.int32)) - 1
    wslot = jnp.where(j == 0, eord % 2, -1)
    nxt = pick(sincl)
    nexte = jnp.sum(jnp.where(step[None, :] == nxt[:, None], te[None, :], 0), axis=-1)
    nexte = jnp.where(nxt < ns, nexte, -1)
    flat = lambda a: a.reshape(-1)
    return (flat(jnp.minimum(eou, N_EXP - 1)), flat(loc), flat(dst_sort), flat(dst_comb), flat(zdst),
            te, ntile, tile0, wslot, nexte, sincl[-1:])


def _unit_rows(u):
    if isinstance(u, int):
        return pl.ds(u * UNIT, UNIT)
    return pl.ds(pl.multiple_of(u * UNIT, UNIT), UNIT)


def _sort_body(eou_ref, loc_ref, dst_ref, zdst_ref, h_ref, rank_ref, gates_ref, xs_hbm, p_ref,
               xs_scr, zero_scr, sem):
    b = pl.program_id(0)
    slot = b % 2

    def copy(src, d, s):
        return pltpu.make_async_copy(src, xs_hbm.at[_unit_rows(d)], sem.at[s])

    def wait_step(s):
        lax.fori_loop(0, UPB + ZPB, lambda k, c: (copy(zero_scr, 0, s).wait(), c)[1], 0, unroll=True)

    @pl.when(b == 0)
    def _():
        zero_scr[...] = jnp.zeros((UNIT, XS_W), BF16)

    @pl.when(b >= 2)
    def _():
        wait_step(slot)

    base = lax.broadcasted_iota(jnp.int32, (UNIT, B_SORT), 0)
    g = gates_ref[...]
    g_hi = g.astype(BF16)
    g_hl = jnp.concatenate([g_hi, (g - g_hi.astype(F32)).astype(BF16)], axis=0)
    h = h_ref[...]
    upc = SORT_CHUNK // UNIT
    for c in range(RB // SORT_CHUNK):
        for u in range(c * upc, (c + 1) * upc):
            match = rank_ref[pl.ds(eou_ref[b * UPB + u], 1), :] == (base + loc_ref[b * UPB + u]).astype(F32)
            p_ref[_unit_rows(u), :] = jnp.where(match, 1.0, 0.0).astype(BF16)
        rows = slice(c * SORT_CHUNK, (c + 1) * SORT_CHUNK)
        xs_scr[slot, rows, 0:D] = _dot(p_ref[rows, :], h).astype(BF16)
        xs_scr[slot, rows, D:XS_W] = _dot_nt(p_ref[rows, :], g_hl).astype(BF16)
        for u in range(c * upc, (c + 1) * upc):
            copy(xs_scr.at[slot, _unit_rows(u)], dst_ref[b * UPB + u], slot).start()
    for j in range(ZPB):
        copy(zero_scr, zdst_ref[b * ZPB + j], slot).start()

    @pl.when(b == N_BLK - 1)
    def _():
        wait_step(1 - slot)
        wait_step(slot)


def _moe_sort(li, eou, loc, dst, zdst, h, rank_t, gates_t):
    blk_t = pl.BlockSpec((N_EXP, B_SORT), lambda b, *_: (0, b))
    return pl.pallas_call(
        _sort_body,
        grid_spec=pltpu.PrefetchScalarGridSpec(
            num_scalar_prefetch=4,
            grid=(N_BLK,),
            in_specs=[pl.BlockSpec((B_SORT, D), lambda b, *_: (b, 0)), blk_t, blk_t],
            out_specs=[pl.BlockSpec(memory_space=pl.ANY),
                       pl.BlockSpec((RB, B_SORT), lambda b, *_: (b, 0))],
            scratch_shapes=[pltpu.VMEM((2, RB, XS_W), BF16),
                            pltpu.VMEM((UNIT, XS_W), BF16),
                            pltpu.SemaphoreType.DMA((2,))]),
        out_shape=[jax.ShapeDtypeStruct((R_SORT, XS_W), BF16),
                   jax.ShapeDtypeStruct((N_BLK * RB, B_SORT), BF16)],
        compiler_params=_cparams(("arbitrary",)),
        name=f"moe_sort_l{li}",
    )(eou, loc, dst, zdst, h, rank_t, gates_t)


def _experts_body(te_ref, ntile_ref, tile0_ref, wslot_ref, nexte_ref, ns_ref,
                  xs_hbm, wg_hbm, wu_hbm, wd_hbm, os_hbm,
                  xs_buf, o_buf, wg_f, wu_f, wd_f, wg_s, wu_s, wd_s, sem_in, sem_out, sem_w, *, li):
    ns = ns_ref[0]

    def tiles_of(step):
        return jnp.where((step >= 0) & (step < ns), ntile_ref[jnp.clip(step, 0, NS_MAX - 1)], 0)

    def tile_rows(step, t):
        return pl.ds(pl.multiple_of((tile0_ref[jnp.clip(step, 0, NS_MAX - 1)] + t) * TMG, TMG), TMG)

    def buf_rows(t):
        return pl.ds(pl.multiple_of(t * TMG, TMG), TMG)

    def copy_in(step, t, s):
        return pltpu.make_async_copy(xs_hbm.at[tile_rows(step, t)], xs_buf.at[s, buf_rows(t)], sem_in.at[s])

    def copy_out(step, t, s):
        return pltpu.make_async_copy(o_buf.at[s, buf_rows(t)], os_hbm.at[tile_rows(step, t)], sem_out.at[s])

    def for_tiles(step, fn):
        lax.fori_loop(0, tiles_of(step), lambda t, c: (fn(t), c)[1], 0)

    def weight_copies(e, s):
        return [pltpu.make_async_copy(hbm.at[li, e], buf.at[s], sem_w.at[s])
                for hbm, buf in ((wg_hbm, wg_f), (wu_hbm, wu_f), (wd_hbm, wd_f))]

    for c in weight_copies(te_ref[0], 0):
        c.start()
    for_tiles(0, lambda t: copy_in(0, t, 0).start())

    def step(i, carry):
        slot = i % 2
        for_tiles(i + 1, lambda t: copy_in(i + 1, t, 1 - slot).start())
        for_tiles(i - 2, lambda t: copy_out(i - 2, t, slot).wait())
        e = te_ref[i]
        ws = wslot_ref[i]

        @pl.when(ws >= 0)
        def _():
            for c in weight_copies(e, ws):
                c.wait()
            wg_s[...] = wg_f[ws].astype(BF16)
            wu_s[...] = wu_f[ws].astype(BF16)
            wd_s[...] = wd_f[ws].astype(BF16)

            @pl.when(nexte_ref[i] >= 0)
            def _():
                for c in weight_copies(nexte_ref[i], 1 - ws):
                    c.start()

        ntile = ntile_ref[i]
        for_tiles(i, lambda t: copy_in(i, t, slot).wait())
        for k in range(1, TPS + 1):
            rows = k * TMG

            @pl.when(ntile == k)
            def _():
                x = xs_buf[slot, 0:rows, 0:D]
                lane = lax.broadcasted_iota(jnp.int32, (rows, 128), 1)
                mine = (lane == e) | (lane == e + N_EXP)
                gate = jnp.sum(jnp.where(mine, xs_buf[slot, 0:rows, D:XS_W].astype(F32), 0.0),
                               axis=1, keepdims=True)
                hid = _silu(_dot(x, wg_s[...])) * _dot(x, wu_s[...]) * gate
                o_buf[slot, 0:rows, :] = _dot(hid.astype(BF16), wd_s[...]).astype(BF16)

        for_tiles(i, lambda t: copy_out(i, t, slot).start())
        return carry

    lax.fori_loop(0, ns, step, 0)
    for_tiles(ns - 2, lambda t: copy_out(ns - 2, t, ns % 2).wait())
    for_tiles(ns - 1, lambda t: copy_out(ns - 1, t, (ns - 1) % 2).wait())
    o_buf[0, 0:UNIT, :] = jnp.zeros((UNIT, D), BF16)
    zero = pltpu.make_async_copy(o_buf.at[0, 0:UNIT], os_hbm.at[_unit_rows(ZERO_UNIT)], sem_out.at[0])
    zero.start()
    zero.wait()


def _moe_experts(li, te, ntile, tile0, wslot, nexte, ns, xs, w_gate, w_up, w_down):
    return pl.pallas_call(
        functools.partial(_experts_body, li=li),
        grid_spec=pltpu.PrefetchScalarGridSpec(
            num_scalar_prefetch=6,
            grid=(1,),
            in_specs=[pl.BlockSpec(memory_space=pl.ANY)] * 4,
            out_specs=pl.BlockSpec(memory_space=pl.ANY),
            scratch_shapes=[pltpu.VMEM((2, STEP_ROWS, XS_W), BF16),
                            pltpu.VMEM((2, STEP_ROWS, D), BF16),
                            pltpu.VMEM((2, D, EXP_DIM), F32),
                            pltpu.VMEM((2, D, EXP_DIM), F32),
                            pltpu.VMEM((2, EXP_DIM, D), F32),
                            pltpu.VMEM((D, EXP_DIM), BF16),
                            pltpu.VMEM((D, EXP_DIM), BF16),
                            pltpu.VMEM((EXP_DIM, D), BF16),
                            pltpu.SemaphoreType.DMA((2,)),
                            pltpu.SemaphoreType.DMA((2,)),
                            pltpu.SemaphoreType.DMA((2,))]),
        out_shape=jax.ShapeDtypeStruct((R_OUT, D), BF16),
        compiler_params=_cparams(("arbitrary",)),
        name=f"moe_experts_l{li}",
    )(te, ntile, tile0, wslot, nexte, ns, xs, w_gate, w_up, w_down)


def _combine_body(dst_ref, h_ref, p_ref, y_ref, mod_ref, gpost_ref,
                  wsg_ref, wsu_ref, wsd_ref, os_hbm, op_ref, ol_ref, os_scr, sem):
    b = pl.program_id(0)
    slot = b % 2

    def copy(d, u, s):
        return pltpu.make_async_copy(os_hbm.at[_unit_rows(d)], os_scr.at[s, _unit_rows(u)], sem.at[s])

    def fetch(blk, s):
        for u in range(UPB):
            copy(dst_ref[blk * UPB + u], u, s).start()

    def wait_block(s):
        lax.fori_loop(0, UPB, lambda k, c: (copy(0, 0, s).wait(), c)[1], 0, unroll=True)

    @pl.when(b == 0)
    def _():
        fetch(0, 0)

    wait_block(slot)
    fetch(jnp.minimum(b + 1, N_BLK - 1), 1 - slot)
    h = h_ref[...]
    shared = _dot((_silu(_dot(h, wsg_ref[...])) * _dot(h, wsu_ref[...])).astype(BF16), wsd_ref[...])
    routed = lax.dot_general(os_scr[slot], p_ref[...], (((0,), (0,)), ((), ())),
                             preferred_element_type=F32).T
    out = y_ref[...] + mod_ref[5:6, :] * _rms(routed + shared, gpost_ref[...])

    @pl.when(b < T_P // B_SORT)
    def _():
        op_ref[...] = out

    @pl.when(b >= T_P // B_SORT)
    def _():
        ol_ref[...] = out

    @pl.when(b == N_BLK - 1)
    def _():
        wait_block(1 - slot)


def _moe_combine(li, dst, h, p, y, modv, g_post, wsg_bf, wsu_bf, wsd_bf, os):
    nbp = T_P // B_SORT
    def tok(w):
        return pl.BlockSpec((B_SORT, w), lambda b, *_: (b, 0))

    def par(*shape):
        return pl.BlockSpec((None,) + shape, lambda b, *_: (li,) + (0,) * len(shape))

    return pl.pallas_call(
        _combine_body,
        grid_spec=pltpu.PrefetchScalarGridSpec(
            num_scalar_prefetch=1,
            grid=(N_BLK,),
            in_specs=[tok(D), pl.BlockSpec((RB, B_SORT), lambda b, *_: (b, 0)), tok(D),
                      pl.BlockSpec((None, None, 6, D), lambda b, *_: (li, _mod_row(b, B_SORT), 0, 0)),
                      par(1, D), par(D, EXP_DIM), par(D, EXP_DIM), par(EXP_DIM, D),
                      pl.BlockSpec(memory_space=pl.ANY)],
            out_specs=[pl.BlockSpec((B_SORT, D), lambda b, *_: (jnp.minimum(b, nbp - 1), 0)),
                       pl.BlockSpec((B_SORT, D), lambda b, *_: (jnp.maximum(b - nbp, 0), 0))],
            scratch_shapes=[pltpu.VMEM((2, RB, D), BF16),
                            pltpu.SemaphoreType.DMA((2,))]),
        out_shape=[jax.ShapeDtypeStruct((T_P, D), F32), jax.ShapeDtypeStruct((T_S, D), F32)],
        compiler_params=_cparams(("arbitrary",)),
        name=f"moe_combine_l{li}",
    )(dst, h, p, y, modv, g_post, wsg_bf, wsu_bf, wsd_bf, os)


def _block_diag(w):
    out = jnp.zeros((DEPTH, POOL_W, POOL_W), F32)
    for g in range(len(POOL_WINDOWS)):
        out = out.at[:, g * POOL_CH:(g + 1) * POOL_CH, g * POOL_CH:(g + 1) * POOL_CH].set(w[:, g])
    return out


def kernel(x_prompt, x_sample, cache_diff_k, cache_diff_v, cache_na_k, cache_na_v, c, c_ctx,
           w_ada, b_ada, g_pre_mix, g_post_mix, g_pre_ffn, g_post_ffn, w_in, w_out,
           diff_lambda, g_diff, na_bias, pool_w, pool_scale, w_router, b_router,
           w_gate, w_up, w_down, ws_gate, ws_up, ws_down):
    xp, xs = x_prompt.reshape(T_P, D), x_sample.reshape(T_S, D)
    cvec = jnp.concatenate([c_ctx[None, :], c, jnp.zeros((3, D), F32)], axis=0)
    modv = _modulation(cvec, w_ada, b_ada)[:, :1 + NB_S].reshape(DEPTH, 1 + NB_S, 6, D)

    cos_t, sin_t = _rope_tables()
    w_in_bf = w_in.astype(BF16)
    w_out_bf = w_out.astype(BF16)
    wsg_bf, wsu_bf, wsd_bf = ws_gate.astype(BF16), ws_up.astype(BF16), ws_down.astype(BF16)
    pool_bd = _block_diag(pool_w)
    row = lambda a: a.reshape(DEPTH, 1, a.shape[-1])
    g_pre_mix, g_post_mix, g_pre_ffn, g_post_ffn = map(row, (g_pre_mix, g_post_mix, g_pre_ffn, g_post_ffn))
    g_diff, pool_scale = row(g_diff), row(pool_scale)
    w_router_t = jnp.swapaxes(w_router, 1, 2)
    b_router_col = b_router.reshape(DEPTH, N_EXP, 1)
    tri = _rank_matrix()
    ck = cache_diff_k.reshape(NB_S, DEPTH, PAST, DA_W)
    cv = cache_diff_v.reshape(NB_S, DEPTH, PAST, DA_W)
    cnk = cache_na_k.reshape(NB_S, DEPTH, PAST, NA_W)
    cnv = cache_na_v.reshape(NB_S, DEPTH, PAST, NA_W)

    strips = _na_bias_strips(na_bias)

    caches = ()
    for li in range(DEPTH):
        lam_init = 0.8 - 0.6 * math.exp(-0.3 * li)
        qa, qb, ocp, up, ka, va, kb, vb, *caches = _in_proj(li, xp, xs, modv, g_pre_mix, w_in_bf, cos_t, sin_t,
                                                           pool_bd, pool_scale, caches)
        ocs = _pool_lat(li, up, pool_bd, pool_scale)
        oabp = _attn_ctx(li, diff_lambda, g_diff, qa, caches[0], caches[1], qb, caches[2], caches[3], lam_init)
        oabs = _attn_lat(li, diff_lambda, g_diff, qa, ka, va, ck, cv, qb, kb, vb, cnk, cnv, strips, lam_init)
        y, h, gates_t, rank_t, pcu = _merge(li, oabp, oabs, ocp, ocs, xp, xs, modv, g_post_mix, g_pre_ffn,
                                            w_out_bf, w_router_t, b_router_col, tri)
        eou, loc, dst_sort, dst_comb, zdst, te, ntile, tile0, wslot, nexte, ns = _moe_tables(pcu)
        srt, p = _moe_sort(li, eou, loc, dst_sort, zdst, h, rank_t, gates_t)
        os = _moe_experts(li, te, ntile, tile0, wslot, nexte, ns, srt, w_gate, w_up, w_down)
        xp, xs = _moe_combine(li, dst_comb, h, p, y, modv, g_post_ffn, wsg_bf, wsu_bf, wsd_bf, os)

    new_dk, new_dv, new_nk, new_nv = caches
    return (xp.reshape(NB_P, N_P, D), xs.reshape(NB_S, N_S, D),
            new_dk.reshape(NB_P, DEPTH, N_P, DA_H, 2 * DA_QK), new_dv.reshape(NB_P, DEPTH, N_P, DA_H, DA_V),
            new_nk.reshape(NB_P, DEPTH, N_P, NA_H, NA_D), new_nv.reshape(NB_P, DEPTH, N_P, NA_H, NA_D))
```

```python
import functools
import math

import numpy as np
import jax
import jax.numpy as jnp
from jax import lax
from jax.experimental import pallas as pl
from jax.experimental.pallas import tpu as pltpu

F32 = jnp.float32
BF16 = jnp.bfloat16

D = 1024
DEPTH = 2
NB_P, N_P = 16, 256
NB_S, N_S = 4, 1024
T_P = NB_P * N_P
T_S = NB_S * N_S
T = T_P + T_S
PAST = 512
GRID_W = 64
DA_H, DA_QK, DA_V = 4, 64, 128
DA_W = DA_H * DA_V
NA_H, NA_D = 4, 64
NA_W = NA_H * NA_D
NA_ROWS, NA_COLS = 8, 16
POOL_WINDOWS = (2, 4, 8, 16)
POOL_CH = 64
POOL_W = 256
IN_COLS = 3 * DA_W + 3 * NA_W + POOL_W
AB_W = DA_W + NA_W
N_EXP, TOP_K, N_GRP, TOPK_GRP = 64, 8, 8, 4
EXP_DIM = 256
ROUTED_SCALE = 2.5
EPS = 1e-6
NEG = -1e30
ROPE_THETA = 10000.0

VMEM_LIMIT = 56 * 1024 * 1024

TM_IN = 512
NT_P = T_P // TM_IN
TQ = 512

B_SORT = 256
UNIT = 16
N_BLK = T // B_SORT
RUN_PAD_ROWS = TOP_K * B_SORT + N_EXP * (UNIT - 1)
RB = -(-RUN_PAD_ROWS // 256) * 256
UPB = RB // UNIT
TMG = 256
UPT = TMG // UNIT
NT_MAX = (N_BLK * RUN_PAD_ROWS) // TMG + N_EXP
XS_W = D + 128
EXP_PER_BLK = N_EXP // N_BLK
ZPB = EXP_PER_BLK * (UPT - 1)
TPS = 6
STEP_ROWS = TPS * TMG
NS_MAX = NT_MAX // TPS + (N_EXP * (TPS - 1)) // TPS
ZERO_UNIT = NT_MAX * UPT
SPARE_UNIT0 = ZERO_UNIT + UPT
R_OUT = (NT_MAX + 1) * TMG
R_SORT = R_OUT + -(-2 * (UPB + ZPB) // UPT) * TMG
SORT_CHUNK = 512
UNROLL = 8


def _cparams(sem):
    return pltpu.CompilerParams(dimension_semantics=sem, vmem_limit_bytes=VMEM_LIMIT)


def _rms(x, g):
    return x * lax.rsqrt(jnp.mean(x * x, axis=-1, keepdims=True) + EPS) * g


def _dot(a, b):
    return jnp.dot(a, b, preferred_element_type=F32)


def _dot_nt(a, b):
    return lax.dot_general(a, b, (((1,), (1,)), ((), ())), preferred_element_type=F32)


def _silu(x):
    return x / (1.0 + jnp.exp(-x))


def _mod_row(i, tm):
    off = i * tm - T_P
    return jnp.where(off >= 0, 1 + jnp.maximum(off, 0) // N_S, 0)


def _mod_body(c_ref, w_ref, b_ref, o_ref):
    c = c_ref[...]
    o_ref[...] = jnp.dot(_silu(c), w_ref[...], precision=lax.Precision.HIGHEST,
                         preferred_element_type=F32) + b_ref[...]


def _modulation(cvec, w_ada, b_ada):
    tn = 1536
    return pl.pallas_call(
        _mod_body,
        grid=(DEPTH, 6 * D // tn),
        in_specs=[pl.BlockSpec((8, D), lambda l, j: (0, 0)),
                  pl.BlockSpec((None, D, tn), lambda l, j: (l, 0, j)),
                  pl.BlockSpec((None, 1, tn), lambda l, j: (l, 0, j))],
        out_specs=pl.BlockSpec((None, 8, tn), lambda l, j: (l, 0, j)),
        out_shape=jax.ShapeDtypeStruct((DEPTH, 8, 6 * D), F32),
        compiler_params=_cparams(("parallel", "parallel")),
        name="modulation",
    )(cvec, w_ada, b_ada.reshape(DEPTH, 1, 6 * D))


def _rope_tables():
    nf = DA_QK // 4
    t = np.arange(N_S)
    pos = np.stack([t // GRID_W, t % GRID_W], axis=-1).astype(np.float32)
    inv = np.power(np.float32(ROPE_THETA), -np.arange(nf, dtype=np.float32) / nf)
    ang = pos[:, :, None] * inv
    cos = np.cos(ang)
    sin = np.sin(ang)
    cos64 = np.concatenate([cos[:, 0], cos[:, 0], cos[:, 1], cos[:, 1]], axis=-1)
    sin64 = np.concatenate([-sin[:, 0], sin[:, 0], -sin[:, 1], sin[:, 1]], axis=-1)
    reps = DA_W // DA_QK
    cos_t = np.concatenate([np.tile(cos64, (1, reps)), np.ones((TM_IN, DA_W), np.float32)], axis=0)
    sin_t = np.concatenate([np.tile(sin64, (1, reps)), np.zeros((TM_IN, DA_W), np.float32)], axis=0)
    return jnp.asarray(cos_t, F32), jnp.asarray(sin_t, F32)


def _tok_select(i, ctx_ref, lat_ref):
    return jnp.where(i < NT_P, ctx_ref[...], lat_ref[...])


def _in_proj_body(*refs):
    (xp_ref, xs_ref, mod_ref, g_ref, w_ref, cos_ref, sin_ref, pw_ref, ps_ref) = refs[:9]
    (qa_ref, qb_ref, ocp_ref, up_ref, ka_ref, va_ref, kb_ref, vb_ref,
     ck_ref, cv_ref, cnk_ref, cnv_ref) = refs[-12:]
    i = pl.program_id(0)
    x = _tok_select(i, xp_ref, xs_ref)
    h = (_rms(x, g_ref[...]) * (1.0 + mod_ref[1:2, :]) + mod_ref[0:1, :]).astype(BF16)
    cos = cos_ref[...]
    sin = sin_ref[...]
    lane = lax.broadcasted_iota(jnp.int32, (TM_IN, DA_W), 1)
    first = (lane % 32) < 16

    def proj(lo, hi):
        return _dot(h, w_ref[:, lo:hi])

    def rope(t):
        swapped = jnp.where(first, pltpu.roll(t, DA_W - 16, 1), pltpu.roll(t, 16, 1))
        return t * cos + swapped * sin

    o = 0
    qa_ref[...] = (rope(proj(o, o + DA_W)) * (DA_QK ** -0.5)).astype(BF16)
    o += DA_W
    ka = rope(proj(o, o + DA_W))
    o += DA_W
    va = proj(o, o + DA_W)
    o += DA_W
    qb_ref[...] = (proj(o, o + NA_W) * (NA_D ** -0.5)).astype(BF16)
    o += NA_W
    kb = proj(o, o + NA_W)
    o += NA_W
    vb = proj(o, o + NA_W)
    o += NA_W
    up = proj(o, o + POOL_W)

    @pl.when(i < NT_P)
    def _():
        for ref, val in ((ck_ref, ka), (cv_ref, va), (cnk_ref, kb), (cnv_ref, vb)):
            ref[...] = val.reshape(ref.shape)
        ocp_ref[...] = _pool_rows(up, N_P, pw_ref[...], ps_ref[...])

    @pl.when(i >= NT_P)
    def _():
        for ref, val in ((ka_ref, ka), (va_ref, va), (kb_ref, kb), (vb_ref, vb)):
            ref[...] = val.astype(BF16)
        up_ref[...] = up


def _ctx_tok(w):
    return pl.BlockSpec((TM_IN, w), lambda i: (jnp.minimum(i, NT_P - 1), 0))


def _lat_tok(w):
    return pl.BlockSpec((TM_IN, w), lambda i: (jnp.maximum(i - NT_P, 0), 0))


def _in_proj(li, xp, xs, modv, g_pre, w_in_bf, cos_t, sin_t, pool_bd, pool_scale, caches):
    n_pos_blk = N_S // TM_IN

    def pos_blk(i):
        return jnp.where(i >= NT_P, jnp.maximum(i - NT_P, 0) % n_pos_blk, n_pos_blk)

    def tok(w):
        return pl.BlockSpec((TM_IN, w), lambda i: (i, 0))

    def cache(w):
        return pl.BlockSpec((TM_IN // N_P, None, N_P, w), lambda i: (jnp.minimum(i, NT_P - 1), li, 0, 0))

    widths = (DA_W, DA_W, NA_W, NA_W)
    n_in, n_plain_out = 9, 8
    return pl.pallas_call(
        _in_proj_body,
        grid=(T // TM_IN,),
        in_specs=[_ctx_tok(D), _lat_tok(D),
                  pl.BlockSpec((None, None, 6, D), lambda i: (li, _mod_row(i, TM_IN), 0, 0)),
                  pl.BlockSpec((None, 1, D), lambda i: (li, 0, 0)),
                  pl.BlockSpec((None, D, IN_COLS), lambda i: (li, 0, 0)),
                  pl.BlockSpec((TM_IN, DA_W), lambda i: (pos_blk(i), 0)),
                  pl.BlockSpec((TM_IN, DA_W), lambda i: (pos_blk(i), 0)),
                  pl.BlockSpec((None, POOL_W, POOL_W), lambda i: (li, 0, 0)),
                  pl.BlockSpec((None, 1, POOL_W), lambda i: (li, 0, 0))]
                 + [pl.BlockSpec(memory_space=pl.ANY)] * len(caches),
        out_specs=[tok(DA_W), tok(NA_W), _ctx_tok(POOL_W), _lat_tok(POOL_W)] + [_lat_tok(w) for w in widths]
                  + [cache(w) for w in widths],
        out_shape=[jax.ShapeDtypeStruct((T, DA_W), BF16),
                   jax.ShapeDtypeStruct((T, NA_W), BF16),
                   jax.ShapeDtypeStruct((T_P, POOL_W), BF16),
                   jax.ShapeDtypeStruct((T_S, POOL_W), F32)]
                  + [jax.ShapeDtypeStruct((T_S, w), BF16) for w in widths]
                  + [jax.ShapeDtypeStruct((NB_P, DEPTH, N_P, w), F32) for w in widths],
        input_output_aliases={n_in + k: n_plain_out + k for k in range(len(caches))},
        compiler_params=_cparams(("arbitrary",)),
        name=f"in_proj_l{li}",
    )(xp, xs, modv, g_pre, w_in_bf, cos_t, sin_t, pool_bd, pool_scale, *caches)


def _pool_rows(u, n, w_mix, scale):
    rows = u.shape[0]
    row = lax.broadcasted_iota(jnp.int32, (rows, POOL_W), 0) % n
    lane = lax.broadcasted_iota(jnp.int32, (rows, POOL_W), 1)

    def shift_dn(a, k):
        return jnp.where(row >= k, pltpu.roll(a, k, 0), 0.0)

    def shift_up(a, k):
        return jnp.where(row < n - k, pltpu.roll(a, rows - k, 0), 0.0)

    fwd = u
    bwd = shift_dn(u, 1)
    mean = jnp.zeros_like(u)
    k = 1
    for gi, w in enumerate(POOL_WINDOWS):
        while k < w // 2:
            fwd = fwd + shift_up(fwd, k)
            bwd = bwd + shift_dn(bwd, k)
            k *= 2
        cnt = (jnp.minimum(row + w // 2, n) - jnp.maximum(row - w // 2, 0)).astype(F32)
        in_group = (lane >= gi * POOL_CH) & (lane < (gi + 1) * POOL_CH)
        mean = jnp.where(in_group, (fwd + bwd) / cnt, mean)
    pooled = mean - u
    mixed = jnp.dot(pooled, w_mix, precision=lax.Precision.HIGHEST, preferred_element_type=F32)
    return (mixed * scale).astype(BF16)


def _pool_body(u_ref, w_ref, s_ref, o_ref):
    o_ref[...] = _pool_rows(u_ref[...], N_S, w_ref[...], s_ref[...])


def _pool_lat(li, up, w_bd, scale):
    return pl.pallas_call(
        _pool_body,
        grid=(NB_S,),
        in_specs=[pl.BlockSpec((N_S, POOL_W), lambda i: (i, 0)),
                  pl.BlockSpec((None, POOL_W, POOL_W), lambda i: (li, 0, 0)),
                  pl.BlockSpec((None, 1, POOL_W), lambda i: (li, 0, 0))],
        out_specs=pl.BlockSpec((N_S, POOL_W), lambda i: (i, 0)),
        out_shape=jax.ShapeDtypeStruct((T_S, POOL_W), BF16),
        compiler_params=_cparams(("parallel",)),
        name=f"pool_lat_l{li}",
    )(up, w_bd, scale)


def _lambda(lp, lam_init):
    a = jnp.sum(lp[0:1, :] * lp[1:2, :], axis=1, keepdims=True)
    b = jnp.sum(lp[2:3, :] * lp[3:4, :], axis=1, keepdims=True)
    return jnp.exp(a) - jnp.exp(b) + lam_init


def _softmax_av(scores, vs):
    m = functools.reduce(jnp.maximum, [jnp.max(s, axis=-1, keepdims=True) for s in scores])
    es = [jnp.exp(s - m) for s in scores]
    den = functools.reduce(jnp.add, [jnp.sum(e, axis=-1, keepdims=True) for e in es])
    o = functools.reduce(jnp.add, [_dot(e.astype(BF16), v) for e, v in zip(es, vs)])
    return o * (1.0 / den)


def _diff_head(q, ks, vs, lam, gd, lam_init):
    o1 = _softmax_av([_dot_nt(q[:, :DA_QK], k[:, :DA_QK]) for k in ks], vs)
    o2 = _softmax_av([_dot_nt(q[:, DA_QK:], k[:, DA_QK:]) for k in ks], vs)
    return _rms(o1 - lam * o2, gd) * (1.0 - lam_init)


def _plain_head(q, ks, vs, biases):
    scores = []
    for k, bias in zip(ks, biases):
        s = _dot_nt(q, k)
        scores.append(s if bias is None else s + bias)
    return _softmax_av(scores, vs)


def _attn_ctx_body(lam_ref, gd_ref, qa_ref, ka_ref, va_ref, qb_ref, kb_ref, vb_ref, o_ref, *, lam_init):
    lam = _lambda(lam_ref[...], lam_init)
    gd = gd_ref[...]
    for h in range(DA_H):
        sl = slice(h * DA_V, (h + 1) * DA_V)
        o = _diff_head(qa_ref[:, sl], [ka_ref[:, sl].astype(BF16)], [va_ref[:, sl].astype(BF16)],
                       lam, gd, lam_init)
        o_ref[:, sl] = o.astype(BF16)
    for h in range(NA_H):
        sl = slice(h * NA_D, (h + 1) * NA_D)
        o = _plain_head(qb_ref[:, sl], [kb_ref[:, sl].astype(BF16)], [vb_ref[:, sl].astype(BF16)], [None])
        o_ref[:, DA_W + h * NA_D:DA_W + (h + 1) * NA_D] = o.astype(BF16)


def _attn_ctx(li, lam_p, g_diff, qa, ka, va, qb, kb, vb, lam_init):
    def blk(w):
        return pl.BlockSpec((N_P, w), lambda b: (b, 0))

    def cblk(w):
        return pl.BlockSpec((None, None, N_P, w), lambda b: (b, li, 0, 0))

    return pl.pallas_call(
        functools.partial(_attn_ctx_body, lam_init=lam_init),
        grid=(NB_P,),
        in_specs=[pl.BlockSpec((None, 4, DA_QK), lambda b: (li, 0, 0)),
                  pl.BlockSpec((None, 1, DA_V), lambda b: (li, 0, 0)),
                  blk(DA_W), cblk(DA_W), cblk(DA_W), blk(NA_W), cblk(NA_W), cblk(NA_W)],
        out_specs=pl.BlockSpec((N_P, AB_W), lambda b: (b, 0)),
        out_shape=jax.ShapeDtypeStruct((T_P, AB_W), BF16),
        compiler_params=_cparams(("parallel",)),
        name=f"attn_ctx_l{li}",
    )(lam_p, g_diff, qa, ka, va, qb, kb, vb)


G_ROWS = N_S // GRID_W
QR_PER_BLK = TQ // GRID_W


def _fill_na_bias(j, strip_ref, bias_scr):
    for jj in range(N_S // TQ):
        @pl.when(j == jj)
        def _():
            for rr in range(QR_PER_BLK):
                qr = jj * QR_PER_BLK + rr
                rs = min(max(qr - NA_ROWS // 2, 0), G_ROWS - NA_ROWS)
                lo, hi = rs * GRID_W, (rs + NA_ROWS) * GRID_W
                a0 = (rs - qr + NA_ROWS - 1) * GRID_W
                q = slice(rr * GRID_W, (rr + 1) * GRID_W)
                for h in range(NA_H):
                    if lo > 0:
                        bias_scr[h, q, 0:lo] = jnp.full((GRID_W, lo), NEG, F32)
                    bias_scr[h, q, lo:hi] = strip_ref[h, :, a0:a0 + NA_ROWS * GRID_W]
                    if hi < N_S:
                        bias_scr[h, q, hi:N_S] = jnp.full((GRID_W, N_S - hi), NEG, F32)


def _attn_lat_body(lam_ref, gd_ref, qa_ref, ka_ref, va_ref, ck_ref, cv_ref,
                   qb_ref, kb_ref, vb_ref, cnk_ref, cnv_ref, strip_ref, o_ref, bias_scr, *, lam_init):
    _fill_na_bias(pl.program_id(1), strip_ref, bias_scr)
    lam = _lambda(lam_ref[...], lam_init)
    gd = gd_ref[...]
    for h in range(DA_H):
        sl = slice(h * DA_V, (h + 1) * DA_V)
        ks = [ka_ref[:, sl], ck_ref[:, sl].astype(BF16)]
        vs = [va_ref[:, sl], cv_ref[:, sl].astype(BF16)]
        o_ref[:, sl] = _diff_head(qa_ref[:, sl], ks, vs, lam, gd, lam_init).astype(BF16)
    for h in range(NA_H):
        sl = slice(h * NA_D, (h + 1) * NA_D)
        ks = [kb_ref[:, sl], cnk_ref[:, sl].astype(BF16)]
        vs = [vb_ref[:, sl], cnv_ref[:, sl].astype(BF16)]
        o = _plain_head(qb_ref[:, sl], ks, vs, [bias_scr[h], None])
        o_ref[:, DA_W + h * NA_D:DA_W + (h + 1) * NA_D] = o.astype(BF16)


def _attn_lat(li, lam_p, g_diff, qa, ka, va, ck, cv, qb, kb, vb, cnk, cnv, strip, lam_init):
    nq = N_S // TQ
    q0 = T_P // TQ

    def qblk(w):
        return pl.BlockSpec((TQ, w), lambda b, j: (q0 + b * nq + j, 0))

    def kvblk(w):
        return pl.BlockSpec((N_S, w), lambda b, j: (b, 0))

    def cblk(w):
        return pl.BlockSpec((None, None, PAST, w), lambda b, j: (b, li, 0, 0))

    return pl.pallas_call(
        functools.partial(_attn_lat_body, lam_init=lam_init),
        grid=(NB_S, nq),
        in_specs=[pl.BlockSpec((None, 4, DA_QK), lambda b, j: (li, 0, 0)),
                  pl.BlockSpec((None, 1, DA_V), lambda b, j: (li, 0, 0)),
                  qblk(DA_W), kvblk(DA_W), kvblk(DA_W), cblk(DA_W), cblk(DA_W),
                  qblk(NA_W), kvblk(NA_W), kvblk(NA_W), cblk(NA_W), cblk(NA_W),
                  pl.BlockSpec((None, NA_H, GRID_W, (2 * NA_ROWS - 1) * GRID_W), lambda b, j: (li, 0, 0, 0))],
        out_specs=pl.BlockSpec((TQ, AB_W), lambda b, j: (b * nq + j, 0)),
        out_shape=jax.ShapeDtypeStruct((T_S, AB_W), BF16),
        scratch_shapes=[pltpu.VMEM((NA_H, TQ, N_S), F32)],
        compiler_params=_cparams(("parallel", "parallel")),
        name=f"attn_lat_l{li}",
    )(lam_p, g_diff, qa, ka, va, ck, cv, qb, kb, vb, cnk, cnv, strip)


def _na_bias_strips(table):
    w = np.arange(GRID_W)
    cs = np.clip(w - NA_COLS // 2, 0, GRID_W - NA_COLS)
    col_ok = (w[None, :] >= cs[:, None]) & (w[None, :] < cs[:, None] + NA_COLS)
    sel_c = (np.clip(w[None, :, None] - w[:, None, None] + NA_COLS - 1, 0, 2 * NA_COLS - 2)
             == np.arange(2 * NA_COLS - 1)).astype(np.float32)
    t1 = jnp.einsum('lhab,cdb->lhcad', table.astype(F32), jnp.asarray(sel_c),
                    precision=lax.Precision.HIGHEST)
    t1 = jnp.where(jnp.asarray(col_ok)[None, None, :, None, :], t1, NEG)
    return t1.reshape(DEPTH, NA_H, GRID_W, (2 * NA_ROWS - 1) * GRID_W)


def _route_t(logits_t, b_col, tm):
    per = N_EXP // N_GRP
    s = 1.0 / (1.0 + jnp.exp(-logits_t))
    sel3 = (s + b_col).reshape(N_GRP, per, tm)
    midx = lax.broadcasted_iota(jnp.int32, (N_GRP, per, tm), 1)
    gidx = lax.broadcasted_iota(jnp.int32, (N_GRP, per, tm), 0)
    ninf = -jnp.inf
    m1 = jnp.max(sel3, axis=1, keepdims=True)
    i1 = jnp.min(jnp.where(sel3 == m1, midx, per), axis=1, keepdims=True)
    m2 = jnp.max(jnp.where(midx == i1, ninf, sel3), axis=1, keepdims=True)
    gsc = jnp.broadcast_to(m1 + m2, (N_GRP, per, tm))
    cnt = jnp.zeros((N_GRP, per, tm), jnp.int32)
    for g in range(N_GRP):
        sg = gsc[g][None]
        cnt = cnt + ((sg > gsc) | ((sg == gsc) & (g < gidx))).astype(jnp.int32)
    x = jnp.where(cnt < TOPK_GRP, sel3, ninf).reshape(N_EXP, tm)
    eidx = lax.broadcasted_iota(jnp.int32, (N_EXP, tm), 0)
    cnt = jnp.zeros((N_EXP, tm), jnp.int32)
    for e in range(N_EXP):
        row = x[e:e + 1, :]
        cnt = cnt + ((row > x) | ((row == x) & (e < eidx))).astype(jnp.int32)
    chosen = cnt < TOP_K
    w = jnp.where(chosen, s, 0.0)
    return chosen, w / jnp.sum(w, axis=0, keepdims=True) * ROUTED_SCALE


def _merge_body(oabp_ref, oabs_ref, ocp_ref, ocs_ref, xp_ref, xs_ref, mod_ref, gpost_ref, gpre_ref,
                wout_ref, wrt_ref, br_ref, tri_ref, y_ref, h_ref, gates_ref, rank_ref, pcu_ref):
    i = pl.program_id(0)
    mix = (_dot(_tok_select(i, oabp_ref, oabs_ref), wout_ref[0:AB_W, :])
           + _dot(_tok_select(i, ocp_ref, ocs_ref), wout_ref[AB_W:D, :]))
    y = _tok_select(i, xp_ref, xs_ref) + mod_ref[2:3, :] * _rms(mix, gpost_ref[...])
    y_ref[...] = y
    h = _rms(y, gpre_ref[...]) * (1.0 + mod_ref[4:5, :]) + mod_ref[3:4, :]
    h_ref[...] = h.astype(BF16)
    logits_t = lax.dot_general(wrt_ref[...], h, (((1,), (1,)), ((), ())),
                               precision=lax.Precision.HIGHEST, preferred_element_type=F32)
    chosen, gates = _route_t(logits_t, br_ref[...], TM_IN)
    gates_ref[...] = gates
    ch = jnp.where(chosen, 1.0, 0.0)
    rank = _dot(ch.astype(BF16), tri_ref[...])
    rank_ref[...] = jnp.where(chosen, rank, -1.0)
    for j in range(TM_IN // B_SORT):
        cnt = jnp.sum(ch[:, j * B_SORT:(j + 1) * B_SORT], axis=1, keepdims=True)
        units = jnp.floor((cnt + (UNIT - 1)) * (1.0 / UNIT))
        pcu_ref[j] = jnp.broadcast_to(units, (N_EXP, 128))


def _merge(li, oabp, oabs, ocp, ocs, xp, xs, modv, g_post, g_pre, w_out_bf, w_router_t, b_router_col, tri):
    def tok(w):
        return pl.BlockSpec((TM_IN, w), lambda i: (i, 0))

    def par(*shape):
        return pl.BlockSpec((None,) + shape, lambda i: (li,) + (0,) * len(shape))

    tok_t = pl.BlockSpec((N_EXP, TM_IN), lambda i: (0, i))
    nb = TM_IN // B_SORT
    return pl.pallas_call(
        _merge_body,
        grid=(T // TM_IN,),
        in_specs=[_ctx_tok(AB_W), _lat_tok(AB_W), _ctx_tok(POOL_W), _lat_tok(POOL_W), _ctx_tok(D), _lat_tok(D),
                  pl.BlockSpec((None, None, 6, D), lambda i: (li, _mod_row(i, TM_IN), 0, 0)),
                  par(1, D), par(1, D), par(D, D), par(N_EXP, D), par(N_EXP, 1),
                  pl.BlockSpec((TM_IN, TM_IN), lambda i: (0, 0))],
        out_specs=[tok(D), tok(D), tok_t, tok_t,
                   pl.BlockSpec((nb, N_EXP, 128), lambda i: (i, 0, 0))],
        out_shape=[jax.ShapeDtypeStruct((T, D), F32),
                   jax.ShapeDtypeStruct((T, D), BF16),
                   jax.ShapeDtypeStruct((N_EXP, T), F32),
                   jax.ShapeDtypeStruct((N_EXP, T), F32),
                   jax.ShapeDtypeStruct((N_BLK, N_EXP, 128), F32)],
        compiler_params=_cparams(("parallel",)),
        name=f"merge_l{li}",
    )(oabp, oabs, ocp, ocs, xp, xs, modv, g_post, g_pre, w_out_bf, w_router_t, b_router_col, tri)


def _rank_matrix():
    t = np.arange(TM_IN)
    m = (t[:, None] < t[None, :]) & (t[:, None] // B_SORT == t[None, :] // B_SORT)
    return jnp.asarray(m, BF16)


def _moe_tables(pcu_f):
    pcu = pcu_f[:, :, 0].astype(jnp.int32)
    incl = jnp.cumsum(pcu, axis=1)
    uo = incl - pcu
    tot = jnp.sum(pcu, axis=0)
    gt = (tot + UPT - 1) // UPT
    gstart_t = jnp.cumsum(gt) - gt
    gstart_u = gstart_t * UPT
    steps = (gt + TPS - 1) // TPS
    sincl = jnp.cumsum(steps)
    sstart = sincl - steps
    bstart = gstart_u[None, :] + jnp.cumsum(pcu, axis=0) - pcu
    u = jnp.arange(UPB, dtype=jnp.int32)
    eou = jnp.sum((u[None, :, None] >= incl[:, None, :]).astype(jnp.int32), axis=-1)
    valid = eou < N_EXP
    onehot = eou[:, :, None] == jnp.arange(N_EXP, dtype=jnp.int32)[None, None, :]
    run_u0 = jnp.sum(jnp.where(onehot, uo[:, None, :], 0), axis=-1)
    run_g0 = jnp.sum(jnp.where(onehot, bstart[:, None, :], 0), axis=-1)
    dst = run_g0 + u[None, :] - run_u0
    spare0 = SPARE_UNIT0 + (jnp.arange(N_BLK, dtype=jnp.int32) % 2)[:, None] * (UPB + ZPB)
    dst_sort = jnp.where(valid, dst, spare0 + u[None, :])
    dst_comb = jnp.where(valid, dst, ZERO_UNIT)
    loc = jnp.where(valid, (u[None, :] - run_u0) * UNIT, -(1 << 20))
    k = jnp.arange(UPT - 1, dtype=jnp.int32)
    ztail = gt * UPT - tot
    zdst = jnp.where(k[None, :] < ztail[:, None], (gstart_u + tot)[:, None] + k[None, :], -1).reshape(N_BLK, ZPB)
    zdst = jnp.where(zdst >= 0, zdst, spare0 + UPB + jnp.arange(ZPB, dtype=jnp.int32)[None, :])
    step = jnp.arange(NS_MAX, dtype=jnp.int32)
    te = jnp.minimum(jnp.sum((step[:, None] >= sincl[None, :]).astype(jnp.int32), axis=-1), N_EXP - 1)
    mine = te[:, None] == jnp.arange(N_EXP, dtype=jnp.int32)[None, :]
    pick = lambda v: jnp.sum(jnp.where(mine, v[None, :], 0), axis=-1)
    first = gt - TPS * (steps - 1)
    j = step - pick(sstart)
    ntile = jnp.where(j == 0, pick(first), TPS)
    tile0 = pick(gstart_t) + jnp.where(j == 0, 0, pick(first) + TPS * (j - 1))
    ns = sincl[-1]
    wslot = (jnp.cumsum((j == 0).astype(jnp.int32)) - 1) % 2
    nxt = pick(sincl)
    nexte = jnp.sum(jnp.where(step[None, :] == nxt[:, None], te[None, :], 0), axis=-1)
    nexte = jnp.where(j == 0, jnp.where(nxt < ns, nexte, -1), -2)
    flat = lambda a: a.reshape(-1)
    return (flat(jnp.minimum(eou, N_EXP - 1)), flat(loc), flat(dst_sort), flat(dst_comb), flat(zdst),
            te, ntile, tile0, wslot, nexte, sincl[-1:])


def _unit_rows(u):
    if isinstance(u, int):
        return pl.ds(u * UNIT, UNIT)
    return pl.ds(pl.multiple_of(u * UNIT, UNIT), UNIT)


def _sort_body(eou_ref, loc_ref, dst_ref, zdst_ref, h_ref, rank_ref, gates_ref, xs_hbm, p_ref,
               xs_scr, zero_scr, sem):
    b = pl.program_id(0)
    slot = b % 2

    def copy(src, d, s):
        return pltpu.make_async_copy(src, xs_hbm.at[_unit_rows(d)], sem.at[s])

    def wait_step(s):
        lax.fori_loop(0, UPB + ZPB, lambda k, c: (copy(zero_scr, 0, s).wait(), c)[1], 0, unroll=True)

    @pl.when(b == 0)
    def _():
        zero_scr[...] = jnp.zeros((UNIT, XS_W), BF16)

    @pl.when(b >= 2)
    def _():
        wait_step(slot)

    base = lax.broadcasted_iota(jnp.int32, (UNIT, B_SORT), 0)
    g = gates_ref[...]
    g_hi = g.astype(BF16)
    g_hl = jnp.concatenate([g_hi, (g - g_hi.astype(F32)).astype(BF16)], axis=0)
    h = h_ref[...]
    upc = SORT_CHUNK // UNIT
    for c in range(RB // SORT_CHUNK):
        for u in range(c * upc, (c + 1) * upc):
            match = rank_ref[pl.ds(eou_ref[b * UPB + u], 1), :] == (base + loc_ref[b * UPB + u]).astype(F32)
            p_ref[_unit_rows(u), :] = jnp.where(match, 1.0, 0.0).astype(BF16)
        rows = slice(c * SORT_CHUNK, (c + 1) * SORT_CHUNK)
        xs_scr[slot, rows, 0:D] = _dot(p_ref[rows, :], h).astype(BF16)
        xs_scr[slot, rows, D:XS_W] = _dot_nt(p_ref[rows, :], g_hl).astype(BF16)
        for u in range(c * upc, (c + 1) * upc):
            copy(xs_scr.at[slot, _unit_rows(u)], dst_ref[b * UPB + u], slot).start()
    for j in range(ZPB):
        copy(zero_scr, zdst_ref[b * ZPB + j], slot).start()

    @pl.when(b == N_BLK - 1)
    def _():
        wait_step(1 - slot)
        wait_step(slot)


def _moe_sort(li, eou, loc, dst, zdst, h, rank_t, gates_t):
    blk_t = pl.BlockSpec((N_EXP, B_SORT), lambda b, *_: (0, b))
    return pl.pallas_call(
        _sort_body,
        grid_spec=pltpu.PrefetchScalarGridSpec(
            num_scalar_prefetch=4,
            grid=(N_BLK,),
            in_specs=[pl.BlockSpec((B_SORT, D), lambda b, *_: (b, 0)), blk_t, blk_t],
            out_specs=[pl.BlockSpec(memory_space=pl.ANY),
                       pl.BlockSpec((RB, B_SORT), lambda b, *_: (b, 0))],
            scratch_shapes=[pltpu.VMEM((2, RB, XS_W), BF16),
                            pltpu.VMEM((UNIT, XS_W), BF16),
                            pltpu.SemaphoreType.DMA((2,))]),
        out_shape=[jax.ShapeDtypeStruct((R_SORT, XS_W), BF16),
                   jax.ShapeDtypeStruct((N_BLK * RB, B_SORT), BF16)],
        compiler_params=_cparams(("arbitrary",)),
        name=f"moe_sort_l{li}",
    )(eou, loc, dst, zdst, h, rank_t, gates_t)


def _experts_body(te_ref, ntile_ref, tile0_ref, wslot_ref, nexte_ref, ns_ref,
                  xs_hbm, wg_hbm, wu_hbm, wd_hbm, os_hbm,
                  xs_buf, o_buf, wg_f, wu_f, wd_f, sem_in, sem_out, sem_w, *, li):
    ns = ns_ref[0]

    def tiles_of(step):
        return jnp.where((step >= 0) & (step < ns), ntile_ref[jnp.clip(step, 0, NS_MAX - 1)], 0)

    def tile_rows(step, t):
        return pl.ds(pl.multiple_of((tile0_ref[jnp.clip(step, 0, NS_MAX - 1)] + t) * TMG, TMG), TMG)

    def buf_rows(t):
        return pl.ds(pl.multiple_of(t * TMG, TMG), TMG)

    def copy_in(step, t, s):
        return pltpu.make_async_copy(xs_hbm.at[tile_rows(step, t)], xs_buf.at[s, buf_rows(t)], sem_in.at[s])

    def copy_out(step, t, s):
        return pltpu.make_async_copy(o_buf.at[s, buf_rows(t)], os_hbm.at[tile_rows(step, t)], sem_out.at[s])

    def for_tiles(step, fn):
        lax.fori_loop(0, tiles_of(step), lambda t, c: (fn(t), c)[1], 0)

    def weight_copies(e, s):
        return [pltpu.make_async_copy(hbm.at[li, e], buf.at[s], sem_w.at[s])
                for hbm, buf in ((wg_hbm, wg_f), (wu_hbm, wu_f), (wd_hbm, wd_f))]

    for c in weight_copies(te_ref[0], 0):
        c.start()
    for_tiles(0, lambda t: copy_in(0, t, 0).start())

    def step(i, carry):
        slot = i % 2
        for_tiles(i + 1, lambda t: copy_in(i + 1, t, 1 - slot).start())
        for_tiles(i - 2, lambda t: copy_out(i - 2, t, slot).wait())
        e = te_ref[i]
        ws = wslot_ref[i]

        @pl.when(nexte_ref[i] >= -1)
        def _():
            for c in weight_copies(e, ws):
                c.wait()

            @pl.when(nexte_ref[i] >= 0)
            def _():
                for c in weight_copies(nexte_ref[i], 1 - ws):
                    c.start()

        ntile = ntile_ref[i]
        for_tiles(i, lambda t: copy_in(i, t, slot).wait())
        for k in range(1, TPS + 1):
            rows = k * TMG

            @pl.when(ntile == k)
            def _():
                x = xs_buf[slot, 0:rows, 0:D]
                lane = lax.broadcasted_iota(jnp.int32, (rows, 128), 1)
                mine = (lane == e) | (lane == e + N_EXP)
                gate = jnp.sum(jnp.where(mine, xs_buf[slot, 0:rows, D:XS_W].astype(F32), 0.0),
                               axis=1, keepdims=True)
                hid = (_silu(_dot(x, wg_f[ws].astype(BF16))) * _dot(x, wu_f[ws].astype(BF16))) * gate
                o_buf[slot, 0:rows, :] = _dot(hid.astype(BF16), wd_f[ws].astype(BF16)).astype(BF16)

        for_tiles(i, lambda t: copy_out(i, t, slot).start())
        return carry

    lax.fori_loop(0, ns, step, 0)
    for_tiles(ns - 2, lambda t: copy_out(ns - 2, t, ns % 2).wait())
    for_tiles(ns - 1, lambda t: copy_out(ns - 1, t, (ns - 1) % 2).wait())
    o_buf[0, 0:UNIT, :] = jnp.zeros((UNIT, D), BF16)
    zero = pltpu.make_async_copy(o_buf.at[0, 0:UNIT], os_hbm.at[_unit_rows(ZERO_UNIT)], sem_out.at[0])
    zero.start()
    zero.wait()


def _moe_experts(li, te, ntile, tile0, wslot, nexte, ns, xs, w_gate, w_up, w_down):
    return pl.pallas_call(
        functools.partial(_experts_body, li=li),
        grid_spec=pltpu.PrefetchScalarGridSpec(
            num_scalar_prefetch=6,
            grid=(1,),
            in_specs=[pl.BlockSpec(memory_space=pl.ANY)] * 4,
            out_specs=pl.BlockSpec(memory_space=pl.ANY),
            scratch_shapes=[pltpu.VMEM((2, STEP_ROWS, XS_W), BF16),
                            pltpu.VMEM((2, STEP_ROWS, D), BF16),
                            pltpu.VMEM((2, D, EXP_DIM), F32),
                            pltpu.VMEM((2, D, EXP_DIM), F32),
                            pltpu.VMEM((2, EXP_DIM, D), F32),
                            pltpu.SemaphoreType.DMA((2,)),
                            pltpu.SemaphoreType.DMA((2,)),
                            pltpu.SemaphoreType.DMA((2,))]),
        out_shape=jax.ShapeDtypeStruct((R_OUT, D), BF16),
        compiler_params=_cparams(("arbitrary",)),
        name=f"moe_experts_l{li}",
    )(te, ntile, tile0, wslot, nexte, ns, xs, w_gate, w_up, w_down)


def _combine_body(dst_ref, h_ref, p_ref, y_ref, mod_ref, gpost_ref,
                  wsg_ref, wsu_ref, wsd_ref, os_hbm, op_ref, ol_ref, os_scr, sem):
    b = pl.program_id(0)
    slot = b % 2

    def copy(d, u, s):
        return pltpu.make_async_copy(os_hbm.at[_unit_rows(d)], os_scr.at[s, _unit_rows(u)], sem.at[s])

    def fetch(blk, s):
        lax.fori_loop(0, UPB, lambda u, c: (copy(dst_ref[blk * UPB + u], u, s).start(), c)[1], 0,
                      unroll=UNROLL)

    @pl.when(b == 0)
    def _():
        fetch(0, 0)

    @pl.when(b + 1 < N_BLK)
    def _():
        fetch(b + 1, 1 - slot)

    h = h_ref[...]
    shared = _dot((_silu(_dot(h, wsg_ref[...])) * _dot(h, wsu_ref[...])).astype(BF16), wsd_ref[...])
    lax.fori_loop(0, UPB, lambda k, c: (copy(0, 0, slot).wait(), c)[1], 0, unroll=True)
    routed = lax.dot_general(os_scr[slot], p_ref[...], (((0,), (0,)), ((), ())),
                             preferred_element_type=F32).T
    out = y_ref[...] + mod_ref[5:6, :] * _rms(routed + shared, gpost_ref[...])

    @pl.when(b < T_P // B_SORT)
    def _():
        op_ref[...] = out

    @pl.when(b >= T_P // B_SORT)
    def _():
        ol_ref[...] = out


def _moe_combine(li, dst, h, p, y, modv, g_post, wsg_bf, wsu_bf, wsd_bf, os):
    nbp = T_P // B_SORT
    def tok(w):
        return pl.BlockSpec((B_SORT, w), lambda b, *_: (b, 0))

    def par(*shape):
        return pl.BlockSpec((None,) + shape, lambda b, *_: (li,) + (0,) * len(shape))

    return pl.pallas_call(
        _combine_body,
        grid_spec=pltpu.PrefetchScalarGridSpec(
            num_scalar_prefetch=1,
            grid=(N_BLK,),
            in_specs=[tok(D), pl.BlockSpec((RB, B_SORT), lambda b, *_: (b, 0)), tok(D),
                      pl.BlockSpec((None, None, 6, D), lambda b, *_: (li, _mod_row(b, B_SORT), 0, 0)),
                      par(1, D), par(D, EXP_DIM), par(D, EXP_DIM), par(EXP_DIM, D),
                      pl.BlockSpec(memory_space=pl.ANY)],
            out_specs=[pl.BlockSpec((B_SORT, D), lambda b, *_: (jnp.minimum(b, nbp - 1), 0)),
                       pl.BlockSpec((B_SORT, D), lambda b, *_: (jnp.maximum(b - nbp, 0), 0))],
            scratch_shapes=[pltpu.VMEM((2, RB, D), BF16),
                            pltpu.SemaphoreType.DMA((2,))]),
        out_shape=[jax.ShapeDtypeStruct((T_P, D), F32), jax.ShapeDtypeStruct((T_S, D), F32)],
        compiler_params=_cparams(("arbitrary",)),
        name=f"moe_combine_l{li}",
    )(dst, h, p, y, modv, g_post, wsg_bf, wsu_bf, wsd_bf, os)


def _block_diag(w):
    out = jnp.zeros((DEPTH, POOL_W, POOL_W), F32)
    for g in range(len(POOL_WINDOWS)):
        out = out.at[:, g * POOL_CH:(g + 1) * POOL_CH, g * POOL_CH:(g + 1) * POOL_CH].set(w[:, g])
    return out


def kernel(x_prompt, x_sample, cache_diff_k, cache_diff_v, cache_na_k, cache_na_v, c, c_ctx,
           w_ada, b_ada, g_pre_mix, g_post_mix, g_pre_ffn, g_post_ffn, w_in, w_out,
           diff_lambda, g_diff, na_bias, pool_w, pool_scale, w_router, b_router,
           w_gate, w_up, w_down, ws_gate, ws_up, ws_down):
    xp, xs = x_prompt.reshape(T_P, D), x_sample.reshape(T_S, D)
    cvec = jnp.concatenate([c_ctx[None, :], c, jnp.zeros((3, D), F32)], axis=0)
    modv = _modulation(cvec, w_ada, b_ada)[:, :1 + NB_S].reshape(DEPTH, 1 + NB_S, 6, D)

    cos_t, sin_t = _rope_tables()
    w_in_bf = w_in.astype(BF16)
    w_out_bf = w_out.astype(BF16)
    wsg_bf, wsu_bf, wsd_bf = ws_gate.astype(BF16), ws_up.astype(BF16), ws_down.astype(BF16)
    pool_bd = _block_diag(pool_w)
    row = lambda a: a.reshape(DEPTH, 1, a.shape[-1])
    g_pre_mix, g_post_mix, g_pre_ffn, g_post_ffn = map(row, (g_pre_mix, g_post_mix, g_pre_ffn, g_post_ffn))
    g_diff, pool_scale = row(g_diff), row(pool_scale)
    w_router_t = jnp.swapaxes(w_router, 1, 2)
    b_router_col = b_router.reshape(DEPTH, N_EXP, 1)
    tri = _rank_matrix()
    ck = cache_diff_k.reshape(NB_S, DEPTH, PAST, DA_W)
    cv = cache_diff_v.reshape(NB_S, DEPTH, PAST, DA_W)
    cnk = cache_na_k.reshape(NB_S, DEPTH, PAST, NA_W)
    cnv = cache_na_v.reshape(NB_S, DEPTH, PAST, NA_W)

    strips = _na_bias_strips(na_bias)

    caches = ()
    for li in range(DEPTH):
        lam_init = 0.8 - 0.6 * math.exp(-0.3 * li)
        qa, qb, ocp, up, ka, va, kb, vb, *caches = _in_proj(li, xp, xs, modv, g_pre_mix, w_in_bf, cos_t, sin_t,
                                                           pool_bd, pool_scale, caches)
        ocs = _pool_lat(li, up, pool_bd, pool_scale)
        oabp = _attn_ctx(li, diff_lambda, g_diff, qa, caches[0], caches[1], qb, caches[2], caches[3], lam_init)
        oabs = _attn_lat(li, diff_lambda, g_diff, qa, ka, va, ck, cv, qb, kb, vb, cnk, cnv, strips, lam_init)
        y, h, gates_t, rank_t, pcu = _merge(li, oabp, oabs, ocp, ocs, xp, xs, modv, g_post_mix, g_pre_ffn,
                                            w_out_bf, w_router_t, b_router_col, tri)
        eou, loc, dst_sort, dst_comb, zdst, te, ntile, tile0, wslot, nexte, ns = _moe_tables(pcu)
        srt, p = _moe_sort(li, eou, loc, dst_sort, zdst, h, rank_t, gates_t)
        os = _moe_experts(li, te, ntile, tile0, wslot, nexte, ns, srt, w_gate, w_up, w_down)
        xp, xs = _moe_combine(li, dst_comb, h, p, y, modv, g_post_ffn, wsg_bf, wsu_bf, wsd_bf, os)

    new_dk, new_dv, new_nk, new_nv = caches
    return (xp.reshape(NB_P, N_P, D), xs.reshape(NB_S, N_S, D),
            new_dk.reshape(NB_P, DEPTH, N_P, DA_H, 2 * DA_QK), new_dv.reshape(NB_P, DEPTH, N_P, DA_H, DA_V),
            new_nk.reshape(NB_P, DEPTH, N_P, NA_H, NA_D), new_nv.reshape(NB_P, DEPTH, N_P, NA_H, NA_D))
```

```python
import functools
import math

import numpy as np
import jax
import jax.numpy as jnp
from jax import lax
from jax.experimental import pallas as pl
from jax.experimental.pallas import tpu as pltpu

F32 = jnp.float32
BF16 = jnp.bfloat16

D = 1024
DEPTH = 2
NB_P, N_P = 16, 256
NB_S, N_S = 4, 1024
T_P = NB_P * N_P
T_S = NB_S * N_S
T = T_P + T_S
PAST = 512
GRID_W = 64
DA_H, DA_QK, DA_V = 4, 64, 128
DA_W = DA_H * DA_V
NA_H, NA_D = 4, 64
NA_W = NA_H * NA_D
NA_ROWS, NA_COLS = 8, 16
POOL_WINDOWS = (2, 4, 8, 16)
POOL_CH = 64
POOL_W = 256
IN_COLS = 3 * DA_W + 3 * NA_W + POOL_W
AB_W = DA_W + NA_W
N_EXP, TOP_K, N_GRP, TOPK_GRP = 64, 8, 8, 4
EXP_DIM = 256
ROUTED_SCALE = 2.5
EPS = 1e-6
NEG = -1e30
ROPE_THETA = 10000.0

VMEM_LIMIT = 56 * 1024 * 1024

TM_IN = 512
NT_P = T_P // TM_IN
TQ = 512

B_SORT = 256
UNIT = 16
N_BLK = T // B_SORT
RUN_PAD_ROWS = TOP_K * B_SORT + N_EXP * (UNIT - 1)
RB = -(-RUN_PAD_ROWS // 256) * 256
UPB = RB // UNIT
TMG = 256
UPT = TMG // UNIT
NT_MAX = (N_BLK * RUN_PAD_ROWS) // TMG + N_EXP
XS_W = D + 128
EXP_PER_BLK = N_EXP // N_BLK
ZPB = EXP_PER_BLK * (UPT - 1)
TPS = 6
STEP_ROWS = TPS * TMG
NS_MAX = NT_MAX // TPS + (N_EXP * (TPS - 1)) // TPS
ZERO_UNIT = NT_MAX * UPT
SPARE_UNIT0 = ZERO_UNIT + UPT
R_OUT = (NT_MAX + 1) * TMG
R_SORT = R_OUT + -(-2 * (UPB + ZPB) // UPT) * TMG
SORT_CHUNK = 512
UNROLL = 8


def _cparams(sem):
    return pltpu.CompilerParams(dimension_semantics=sem, vmem_limit_bytes=VMEM_LIMIT)


def _rms(x, g):
    return x * lax.rsqrt(jnp.mean(x * x, axis=-1, keepdims=True) + EPS) * g


def _dot(a, b):
    return jnp.dot(a, b, preferred_element_type=F32)


def _dot_nt(a, b):
    return lax.dot_general(a, b, (((1,), (1,)), ((), ())), preferred_element_type=F32)


def _silu(x):
    return x / (1.0 + jnp.exp(-x))


def _mod_row(i, tm):
    off = i * tm - T_P
    return jnp.where(off >= 0, 1 + jnp.maximum(off, 0) // N_S, 0)


def _mod_body(c_ref, w_ref, b_ref, o_ref):
    c = c_ref[...]
    o_ref[...] = jnp.dot(_silu(c), w_ref[...], precision=lax.Precision.HIGHEST,
                         preferred_element_type=F32) + b_ref[...]


def _modulation(cvec, w_ada, b_ada):
    tn = 1536
    return pl.pallas_call(
        _mod_body,
        grid=(DEPTH, 6 * D // tn),
        in_specs=[pl.BlockSpec((8, D), lambda l, j: (0, 0)),
                  pl.BlockSpec((None, D, tn), lambda l, j: (l, 0, j)),
                  pl.BlockSpec((None, 1, tn), lambda l, j: (l, 0, j))],
        out_specs=pl.BlockSpec((None, 8, tn), lambda l, j: (l, 0, j)),
        out_shape=jax.ShapeDtypeStruct((DEPTH, 8, 6 * D), F32),
        compiler_params=_cparams(("parallel", "parallel")),
        name="modulation",
    )(cvec, w_ada, b_ada.reshape(DEPTH, 1, 6 * D))


def _rope_tables():
    nf = DA_QK // 4
    t = np.arange(N_S)
    pos = np.stack([t // GRID_W, t % GRID_W], axis=-1).astype(np.float32)
    inv = np.power(np.float32(ROPE_THETA), -np.arange(nf, dtype=np.float32) / nf)
    ang = pos[:, :, None] * inv
    cos = np.cos(ang)
    sin = np.sin(ang)
    cos64 = np.concatenate([cos[:, 0], cos[:, 0], cos[:, 1], cos[:, 1]], axis=-1)
    sin64 = np.concatenate([-sin[:, 0], sin[:, 0], -sin[:, 1], sin[:, 1]], axis=-1)
    reps = DA_W // DA_QK
    cos_t = np.concatenate([np.tile(cos64, (1, reps)), np.ones((TM_IN, DA_W), np.float32)], axis=0)
    sin_t = np.concatenate([np.tile(sin64, (1, reps)), np.zeros((TM_IN, DA_W), np.float32)], axis=0)
    return jnp.asarray(cos_t, F32), jnp.asarray(sin_t, F32)


def _tok_select(i, ctx_ref, lat_ref):
    return jnp.where(i < NT_P, ctx_ref[...], lat_ref[...])


def _in_proj_body(*refs):
    (xp_ref, xs_ref, mod_ref, g_ref, w_ref, cos_ref, sin_ref, pw_ref, ps_ref) = refs[:9]
    (qa_ref, qb_ref, ocp_ref, up_ref, ka_ref, va_ref, kb_ref, vb_ref,
     ck_ref, cv_ref, cnk_ref, cnv_ref) = refs[-12:]
    i = pl.program_id(0)
    x = _tok_select(i, xp_ref, xs_ref)
    h = (_rms(x, g_ref[...]) * (1.0 + mod_ref[1:2, :]) + mod_ref[0:1, :]).astype(BF16)
    cos = cos_ref[...]
    sin = sin_ref[...]
    lane = lax.broadcasted_iota(jnp.int32, (TM_IN, DA_W), 1)
    first = (lane % 32) < 16

    def proj(lo, hi):
        return _dot(h, w_ref[:, lo:hi])

    def rope(t):
        swapped = jnp.where(first, pltpu.roll(t, DA_W - 16, 1), pltpu.roll(t, 16, 1))
        return t * cos + swapped * sin

    o = 0
    qa_ref[...] = (rope(proj(o, o + DA_W)) * (DA_QK ** -0.5)).astype(BF16)
    o += DA_W
    ka = rope(proj(o, o + DA_W))
    o += DA_W
    va = proj(o, o + DA_W)
    o += DA_W
    qb_ref[...] = (proj(o, o + NA_W) * (NA_D ** -0.5)).astype(BF16)
    o += NA_W
    kb = proj(o, o + NA_W)
    o += NA_W
    vb = proj(o, o + NA_W)
    o += NA_W
    up = proj(o, o + POOL_W)

    @pl.when(i < NT_P)
    def _():
        for ref, val in ((ck_ref, ka), (cv_ref, va), (cnk_ref, kb), (cnv_ref, vb)):
            ref[...] = val.reshape(ref.shape)
        ocp_ref[...] = _pool_rows(up, N_P, pw_ref[...], ps_ref[...])

    @pl.when(i >= NT_P)
    def _():
        for ref, val in ((ka_ref, ka), (va_ref, va), (kb_ref, kb), (vb_ref, vb)):
            ref[...] = val.astype(BF16)
        up_ref[...] = up


def _ctx_tok(w):
    return pl.BlockSpec((TM_IN, w), lambda i: (jnp.minimum(i, NT_P - 1), 0))


def _lat_tok(w):
    return pl.BlockSpec((TM_IN, w), lambda i: (jnp.maximum(i - NT_P, 0), 0))


def _in_proj(li, xp, xs, modv, g_pre, w_in_bf, cos_t, sin_t, pool_bd, pool_scale, caches):
    n_pos_blk = N_S // TM_IN

    def pos_blk(i):
        return jnp.where(i >= NT_P, jnp.maximum(i - NT_P, 0) % n_pos_blk, n_pos_blk)

    def tok(w):
        return pl.BlockSpec((TM_IN, w), lambda i: (i, 0))

    def cache(w):
        return pl.BlockSpec((TM_IN // N_P, None, N_P, w), lambda i: (jnp.minimum(i, NT_P - 1), li, 0, 0))

    widths = (DA_W, DA_W, NA_W, NA_W)
    n_in, n_plain_out = 9, 8
    return pl.pallas_call(
        _in_proj_body,
        grid=(T // TM_IN,),
        in_specs=[_ctx_tok(D), _lat_tok(D),
                  pl.BlockSpec((None, None, 6, D), lambda i: (li, _mod_row(i, TM_IN), 0, 0)),
                  pl.BlockSpec((None, 1, D), lambda i: (li, 0, 0)),
                  pl.BlockSpec((None, D, IN_COLS), lambda i: (li, 0, 0)),
                  pl.BlockSpec((TM_IN, DA_W), lambda i: (pos_blk(i), 0)),
                  pl.BlockSpec((TM_IN, DA_W), lambda i: (pos_blk(i), 0)),
                  pl.BlockSpec((None, POOL_W, POOL_W), lambda i: (li, 0, 0)),
                  pl.BlockSpec((None, 1, POOL_W), lambda i: (li, 0, 0))]
                 + [pl.BlockSpec(memory_space=pl.ANY)] * len(caches),
        out_specs=[tok(DA_W), tok(NA_W), _ctx_tok(POOL_W), _lat_tok(POOL_W)] + [_lat_tok(w) for w in widths]
                  + [cache(w) for w in widths],
        out_shape=[jax.ShapeDtypeStruct((T, DA_W), BF16),
                   jax.ShapeDtypeStruct((T, NA_W), BF16),
                   jax.ShapeDtypeStruct((T_P, POOL_W), BF16),
                   jax.ShapeDtypeStruct((T_S, POOL_W), F32)]
                  + [jax.ShapeDtypeStruct((T_S, w), BF16) for w in widths]
                  + [jax.ShapeDtypeStruct((NB_P, DEPTH, N_P, w), F32) for w in widths],
        input_output_aliases={n_in + k: n_plain_out + k for k in range(len(caches))},
        compiler_params=_cparams(("arbitrary",)),
        name=f"in_proj_l{li}",
    )(xp, xs, modv, g_pre, w_in_bf, cos_t, sin_t, pool_bd, pool_scale, *caches)


def _pool_rows(u, n, w_mix, scale):
    rows = u.shape[0]
    row = lax.broadcasted_iota(jnp.int32, (rows, POOL_W), 0) % n
    lane = lax.broadcasted_iota(jnp.int32, (rows, POOL_W), 1)

    def shift_dn(a, k):
        return jnp.where(row >= k, pltpu.roll(a, k, 0), 0.0)

    def shift_up(a, k):
        return jnp.where(row < n - k, pltpu.roll(a, rows - k, 0), 0.0)

    fwd = u
    bwd = shift_dn(u, 1)
    mean = jnp.zeros_like(u)
    k = 1
    for gi, w in enumerate(POOL_WINDOWS):
        while k < w // 2:
            fwd = fwd + shift_up(fwd, k)
            bwd = bwd + shift_dn(bwd, k)
            k *= 2
        cnt = (jnp.minimum(row + w // 2, n) - jnp.maximum(row - w // 2, 0)).astype(F32)
        in_group = (lane >= gi * POOL_CH) & (lane < (gi + 1) * POOL_CH)
        mean = jnp.where(in_group, (fwd + bwd) / cnt, mean)
    pooled = mean - u
    mixed = jnp.dot(pooled, w_mix, precision=lax.Precision.HIGHEST, preferred_element_type=F32)
    return (mixed * scale).astype(BF16)


def _pool_body(u_ref, w_ref, s_ref, o_ref):
    o_ref[...] = _pool_rows(u_ref[...], N_S, w_ref[...], s_ref[...])


def _pool_lat(li, up, w_bd, scale):
    return pl.pallas_call(
        _pool_body,
        grid=(NB_S,),
        in_specs=[pl.BlockSpec((N_S, POOL_W), lambda i: (i, 0)),
                  pl.BlockSpec((None, POOL_W, POOL_W), lambda i: (li, 0, 0)),
                  pl.BlockSpec((None, 1, POOL_W), lambda i: (li, 0, 0))],
        out_specs=pl.BlockSpec((N_S, POOL_W), lambda i: (i, 0)),
        out_shape=jax.ShapeDtypeStruct((T_S, POOL_W), BF16),
        compiler_params=_cparams(("parallel",)),
        name=f"pool_lat_l{li}",
    )(up, w_bd, scale)


def _lambda(lp, lam_init):
    a = jnp.sum(lp[0:1, :] * lp[1:2, :], axis=1, keepdims=True)
    b = jnp.sum(lp[2:3, :] * lp[3:4, :], axis=1, keepdims=True)
    return jnp.exp(a) - jnp.exp(b) + lam_init


def _softmax_av(scores, vs):
    m = functools.reduce(jnp.maximum, [jnp.max(s, axis=-1, keepdims=True) for s in scores])
    es = [jnp.exp(s - m) for s in scores]
    den = functools.reduce(jnp.add, [jnp.sum(e, axis=-1, keepdims=True) for e in es])
    o = functools.reduce(jnp.add, [_dot(e.astype(BF16), v) for e, v in zip(es, vs)])
    return o * (1.0 / den)


def _diff_head(q, ks, vs, lam, gd, lam_init):
    o1 = _softmax_av([_dot_nt(q[:, :DA_QK], k[:, :DA_QK]) for k in ks], vs)
    o2 = _softmax_av([_dot_nt(q[:, DA_QK:], k[:, DA_QK:]) for k in ks], vs)
    return _rms(o1 - lam * o2, gd) * (1.0 - lam_init)


def _plain_head(q, ks, vs, biases):
    scores = []
    for k, bias in zip(ks, biases):
        s = _dot_nt(q, k)
        scores.append(s if bias is None else s + bias)
    return _softmax_av(scores, vs)


def _attn_ctx_body(lam_ref, gd_ref, qa_ref, ka_ref, va_ref, qb_ref, kb_ref, vb_ref, o_ref, *, lam_init):
    lam = _lambda(lam_ref[...], lam_init)
    gd = gd_ref[...]
    for h in range(DA_H):
        sl = slice(h * DA_V, (h + 1) * DA_V)
        o = _diff_head(qa_ref[:, sl], [ka_ref[:, sl].astype(BF16)], [va_ref[:, sl].astype(BF16)],
                       lam, gd, lam_init)
        o_ref[:, sl] = o.astype(BF16)
    for h in range(NA_H):
        sl = slice(h * NA_D, (h + 1) * NA_D)
        o = _plain_head(qb_ref[:, sl], [kb_ref[:, sl].astype(BF16)], [vb_ref[:, sl].astype(BF16)], [None])
        o_ref[:, DA_W + h * NA_D:DA_W + (h + 1) * NA_D] = o.astype(BF16)


def _attn_ctx(li, lam_p, g_diff, qa, ka, va, qb, kb, vb, lam_init):
    def blk(w):
        return pl.BlockSpec((N_P, w), lambda b: (b, 0))

    def cblk(w):
        return pl.BlockSpec((None, None, N_P, w), lambda b: (b, li, 0, 0))

    return pl.pallas_call(
        functools.partial(_attn_ctx_body, lam_init=lam_init),
        grid=(NB_P,),
        in_specs=[pl.BlockSpec((None, 4, DA_QK), lambda b: (li, 0, 0)),
                  pl.BlockSpec((None, 1, DA_V), lambda b: (li, 0, 0)),
                  blk(DA_W), cblk(DA_W), cblk(DA_W), blk(NA_W), cblk(NA_W), cblk(NA_W)],
        out_specs=pl.BlockSpec((N_P, AB_W), lambda b: (b, 0)),
        out_shape=jax.ShapeDtypeStruct((T_P, AB_W), BF16),
        compiler_params=_cparams(("parallel",)),
        name=f"attn_ctx_l{li}",
    )(lam_p, g_diff, qa, ka, va, qb, kb, vb)


G_ROWS = N_S // GRID_W
QR_PER_BLK = TQ // GRID_W


def _fill_na_bias(j, strip_ref, bias_scr):
    for jj in range(N_S // TQ):
        @pl.when(j == jj)
        def _():
            for rr in range(QR_PER_BLK):
                qr = jj * QR_PER_BLK + rr
                rs = min(max(qr - NA_ROWS // 2, 0), G_ROWS - NA_ROWS)
                lo, hi = rs * GRID_W, (rs + NA_ROWS) * GRID_W
                a0 = (rs - qr + NA_ROWS - 1) * GRID_W
                q = slice(rr * GRID_W, (rr + 1) * GRID_W)
                for h in range(NA_H):
                    if lo > 0:
                        bias_scr[h, q, 0:lo] = jnp.full((GRID_W, lo), NEG, F32)
                    bias_scr[h, q, lo:hi] = strip_ref[h, :, a0:a0 + NA_ROWS * GRID_W]
                    if hi < N_S:
                        bias_scr[h, q, hi:N_S] = jnp.full((GRID_W, N_S - hi), NEG, F32)


def _attn_lat_body(lam_ref, gd_ref, qa_ref, ka_ref, va_ref, ck_ref, cv_ref,
                   qb_ref, kb_ref, vb_ref, cnk_ref, cnv_ref, strip_ref, o_ref, bias_scr, *, lam_init):
    _fill_na_bias(pl.program_id(1), strip_ref, bias_scr)
    lam = _lambda(lam_ref[...], lam_init)
    gd = gd_ref[...]
    for h in range(DA_H):
        sl = slice(h * DA_V, (h + 1) * DA_V)
        ks = [ka_ref[:, sl], ck_ref[:, sl].astype(BF16)]
        vs = [va_ref[:, sl], cv_ref[:, sl].astype(BF16)]
        o_ref[:, sl] = _diff_head(qa_ref[:, sl], ks, vs, lam, gd, lam_init).astype(BF16)
    for h in range(NA_H):
        sl = slice(h * NA_D, (h + 1) * NA_D)
        ks = [kb_ref[:, sl], cnk_ref[:, sl].astype(BF16)]
        vs = [vb_ref[:, sl], cnv_ref[:, sl].astype(BF16)]
        o = _plain_head(qb_ref[:, sl], ks, vs, [bias_scr[h], None])
        o_ref[:, DA_W + h * NA_D:DA_W + (h + 1) * NA_D] = o.astype(BF16)


def _attn_lat(li, lam_p, g_diff, qa, ka, va, ck, cv, qb, kb, vb, cnk, cnv, strip, lam_init):
    nq = N_S // TQ
    q0 = T_P // TQ

    def qblk(w):
        return pl.BlockSpec((TQ, w), lambda b, j: (q0 + b * nq + j, 0))

    def kvblk(w):
        return pl.BlockSpec((N_S, w), lambda b, j: (b, 0))

    def cblk(w):
        return pl.BlockSpec((None, None, PAST, w), lambda b, j: (b, li, 0, 0))

    return pl.pallas_call(
        functools.partial(_attn_lat_body, lam_init=lam_init),
        grid=(NB_S, nq),
        in_specs=[pl.BlockSpec((None, 4, DA_QK), lambda b, j: (li, 0, 0)),
                  pl.BlockSpec((None, 1, DA_V), lambda b, j: (li, 0, 0)),
                  qblk(DA_W), kvblk(DA_W), kvblk(DA_W), cblk(DA_W), cblk(DA_W),
                  qblk(NA_W), kvblk(NA_W), kvblk(NA_W), cblk(NA_W), cblk(NA_W),
                  pl.BlockSpec((None, NA_H, GRID_W, (2 * NA_ROWS - 1) * GRID_W), lambda b, j: (li, 0, 0, 0))],
        out_specs=pl.BlockSpec((TQ, AB_W), lambda b, j: (b * nq + j, 0)),
        out_shape=jax.ShapeDtypeStruct((T_S, AB_W), BF16),
        scratch_shapes=[pltpu.VMEM((NA_H, TQ, N_S), F32)],
        compiler_params=_cparams(("parallel", "parallel")),
        name=f"attn_lat_l{li}",
    )(lam_p, g_diff, qa, ka, va, ck, cv, qb, kb, vb, cnk, cnv, strip)


def _na_bias_strips(table):
    w = np.arange(GRID_W)
    cs = np.clip(w - NA_COLS // 2, 0, GRID_W - NA_COLS)
    col_ok = (w[None, :] >= cs[:, None]) & (w[None, :] < cs[:, None] + NA_COLS)
    sel_c = (np.clip(w[None, :, None] - w[:, None, None] + NA_COLS - 1, 0, 2 * NA_COLS - 2)
             == np.arange(2 * NA_COLS - 1)).astype(np.float32)
    t1 = jnp.einsum('lhab,cdb->lhcad', table.astype(F32), jnp.asarray(sel_c),
                    precision=lax.Precision.HIGHEST)
    t1 = jnp.where(jnp.asarray(col_ok)[None, None, :, None, :], t1, NEG)
    return t1.reshape(DEPTH, NA_H, GRID_W, (2 * NA_ROWS - 1) * GRID_W)


def _route_t(logits_t, b_col, tm):
    per = N_EXP // N_GRP
    s = 1.0 / (1.0 + jnp.exp(-logits_t))
    sel3 = (s + b_col).reshape(N_GRP, per, tm)
    midx = lax.broadcasted_iota(jnp.int32, (N_GRP, per, tm), 1)
    gidx = lax.broadcasted_iota(jnp.int32, (N_GRP, per, tm), 0)
    ninf = -jnp.inf
    m1 = jnp.max(sel3, axis=1, keepdims=True)
    i1 = jnp.min(jnp.where(sel3 == m1, midx, per), axis=1, keepdims=True)
    m2 = jnp.max(jnp.where(midx == i1, ninf, sel3), axis=1, keepdims=True)
    gsc = jnp.broadcast_to(m1 + m2, (N_GRP, per, tm))
    cnt = jnp.zeros((N_GRP, per, tm), jnp.int32)
    for g in range(N_GRP):
        sg = gsc[g][None]
        cnt = cnt + ((sg > gsc) | ((sg == gsc) & (g < gidx))).astype(jnp.int32)
    x = jnp.where(cnt < TOPK_GRP, sel3, ninf).reshape(N_EXP, tm)
    eidx = lax.broadcasted_iota(jnp.int32, (N_EXP, tm), 0)
    chosen = jnp.zeros((N_EXP, tm), jnp.bool_)
    for _ in range(TOP_K):
        m = jnp.max(x, axis=0, keepdims=True)
        first = jnp.min(jnp.where(x == m, eidx, N_EXP), axis=0, keepdims=True)
        hit = eidx == first
        chosen = chosen | hit
        x = jnp.where(hit, ninf, x)
    w = jnp.where(chosen, s, 0.0)
    return chosen, w / jnp.sum(w, axis=0, keepdims=True) * ROUTED_SCALE


def _merge_body(oabp_ref, oabs_ref, ocp_ref, ocs_ref, xp_ref, xs_ref, mod_ref, gpost_ref, gpre_ref,
                wout_ref, wrt_ref, br_ref, tri_ref, y_ref, h_ref, gates_ref, rank_ref, pcu_ref):
    i = pl.program_id(0)
    mix = (_dot(_tok_select(i, oabp_ref, oabs_ref), wout_ref[0:AB_W, :])
           + _dot(_tok_select(i, ocp_ref, ocs_ref), wout_ref[AB_W:D, :]))
    y = _tok_select(i, xp_ref, xs_ref) + mod_ref[2:3, :] * _rms(mix, gpost_ref[...])
    y_ref[...] = y
    h = _rms(y, gpre_ref[...]) * (1.0 + mod_ref[4:5, :]) + mod_ref[3:4, :]
    h_ref[...] = h.astype(BF16)
    logits_t = lax.dot_general(wrt_ref[...], h, (((1,), (1,)), ((), ())),
                               precision=lax.Precision.HIGHEST, preferred_element_type=F32)
    chosen, gates = _route_t(logits_t, br_ref[...], TM_IN)
    gates_ref[...] = gates
    ch = jnp.where(chosen, 1.0, 0.0)
    rank = _dot(ch.astype(BF16), tri_ref[...])
    rank_ref[...] = jnp.where(chosen, rank, -1.0)
    for j in range(TM_IN // B_SORT):
        cnt = jnp.sum(ch[:, j * B_SORT:(j + 1) * B_SORT], axis=1, keepdims=True)
        units = jnp.floor((cnt + (UNIT - 1)) * (1.0 / UNIT))
        pcu_ref[j] = jnp.broadcast_to(units, (N_EXP, 128))


def _merge(li, oabp, oabs, ocp, ocs, xp, xs, modv, g_post, g_pre, w_out_bf, w_router_t, b_router_col, tri):
    def tok(w):
        return pl.BlockSpec((TM_IN, w), lambda i: (i, 0))

    def par(*shape):
        return pl.BlockSpec((None,) + shape, lambda i: (li,) + (0,) * len(shape))

    tok_t = pl.BlockSpec((N_EXP, TM_IN), lambda i: (0, i))
    nb = TM_IN // B_SORT
    return pl.pallas_call(
        _merge_body,
        grid=(T // TM_IN,),
        in_specs=[_ctx_tok(AB_W), _lat_tok(AB_W), _ctx_tok(POOL_W), _lat_tok(POOL_W), _ctx_tok(D), _lat_tok(D),
                  pl.BlockSpec((None, None, 6, D), lambda i: (li, _mod_row(i, TM_IN), 0, 0)),
                  par(1, D), par(1, D), par(D, D), par(N_EXP, D), par(N_EXP, 1),
                  pl.BlockSpec((TM_IN, TM_IN), lambda i: (0, 0))],
        out_specs=[tok(D), tok(D), tok_t, tok_t,
                   pl.BlockSpec((nb, N_EXP, 128), lambda i: (i, 0, 0))],
        out_shape=[jax.ShapeDtypeStruct((T, D), F32),
                   jax.ShapeDtypeStruct((T, D), BF16),
                   jax.ShapeDtypeStruct((N_EXP, T), F32),
                   jax.ShapeDtypeStruct((N_EXP, T), F32),
                   jax.ShapeDtypeStruct((N_BLK, N_EXP, 128), F32)],
        compiler_params=_cparams(("parallel",)),
        name=f"merge_l{li}",
    )(oabp, oabs, ocp, ocs, xp, xs, modv, g_post, g_pre, w_out_bf, w_router_t, b_router_col, tri)


def _rank_matrix():
    t = np.arange(TM_IN)
    m = (t[:, None] < t[None, :]) & (t[:, None] // B_SORT == t[None, :] // B_SORT)
    return jnp.asarray(m, BF16)


def _moe_tables(pcu_f):
    pcu = pcu_f[:, :, 0].astype(jnp.int32)
    incl = jnp.cumsum(pcu, axis=1)
    uo = incl - pcu
    tot = jnp.sum(pcu, axis=0)
    gt = (tot + UPT - 1) // UPT
    gstart_t = jnp.cumsum(gt) - gt
    gstart_u = gstart_t * UPT
    steps = (gt + TPS - 1) // TPS
    sincl = jnp.cumsum(steps)
    sstart = sincl - steps
    bstart = gstart_u[None, :] + jnp.cumsum(pcu, axis=0) - pcu
    u = jnp.arange(UPB, dtype=jnp.int32)
    eou = jnp.sum((u[None, :, None] >= incl[:, None, :]).astype(jnp.int32), axis=-1)
    valid = eou < N_EXP
    onehot = eou[:, :, None] == jnp.arange(N_EXP, dtype=jnp.int32)[None, None, :]
    run_u0 = jnp.sum(jnp.where(onehot, uo[:, None, :], 0), axis=-1)
    run_g0 = jnp.sum(jnp.where(onehot, bstart[:, None, :], 0), axis=-1)
    dst = run_g0 + u[None, :] - run_u0
    spare0 = SPARE_UNIT0 + (jnp.arange(N_BLK, dtype=jnp.int32) % 2)[:, None] * (UPB + ZPB)
    dst_sort = jnp.where(valid, dst, spare0 + u[None, :])
    dst_comb = jnp.where(valid, dst, ZERO_UNIT)
    loc = jnp.where(valid, (u[None, :] - run_u0) * UNIT, -(1 << 20))
    k = jnp.arange(UPT - 1, dtype=jnp.int32)
    ztail = gt * UPT - tot
    zdst = jnp.where(k[None, :] < ztail[:, None], (gstart_u + tot)[:, None] + k[None, :], -1).reshape(N_BLK, ZPB)
    zdst = jnp.where(zdst >= 0, zdst, spare0 + UPB + jnp.arange(ZPB, dtype=jnp.int32)[None, :])
    step = jnp.arange(NS_MAX, dtype=jnp.int32)
    te = jnp.minimum(jnp.sum((step[:, None] >= sincl[None, :]).astype(jnp.int32), axis=-1), N_EXP - 1)
    mine = te[:, None] == jnp.arange(N_EXP, dtype=jnp.int32)[None, :]
    pick = lambda v: jnp.sum(jnp.where(mine, v[None, :], 0), axis=-1)
    first = gt - TPS * (steps - 1)
    j = step - pick(sstart)
    ntile = jnp.where(j == 0, pick(first), TPS)
    tile0 = pick(gstart_t) + jnp.where(j == 0, 0, pick(first) + TPS * (j - 1))
    ns = sincl[-1]
    wslot = (jnp.cumsum((j == 0).astype(jnp.int32)) - 1) % 2
    nxt = pick(sincl)
    nexte = jnp.sum(jnp.where(step[None, :] == nxt[:, None], te[None, :], 0), axis=-1)
    nexte = jnp.where(j == 0, jnp.where(nxt < ns, nexte, -1), -2)
    flat = lambda a: a.reshape(-1)
    return (flat(jnp.minimum(eou, N_EXP - 1)), flat(loc), flat(dst_sort), flat(dst_comb), flat(zdst),
            te, ntile, tile0, wslot, nexte, sincl[-1:])


def _unit_rows(u):
    if isinstance(u, int):
        return pl.ds(u * UNIT, UNIT)
    return pl.ds(pl.multiple_of(u * UNIT, UNIT), UNIT)


def _sort_body(eou_ref, loc_ref, dst_ref, zdst_ref, h_ref, rank_ref, gates_ref, xs_hbm, p_ref,
               xs_scr, zero_scr, sem):
    b = pl.program_id(0)
    slot = b % 2

    def copy(src, d, s):
        return pltpu.make_async_copy(src, xs_hbm.at[_unit_rows(d)], sem.at[s])

    def wait_step(s):
        lax.fori_loop(0, UPB + ZPB, lambda k, c: (copy(zero_scr, 0, s).wait(), c)[1], 0, unroll=True)

    @pl.when(b == 0)
    def _():
        zero_scr[...] = jnp.zeros((UNIT, XS_W), BF16)

    @pl.when(b >= 2)
    def _():
        wait_step(slot)

    base = lax.broadcasted_iota(jnp.int32, (UNIT, B_SORT), 0)
    g = gates_ref[...]
    g_hi = g.astype(BF16)
    g_hl = jnp.concatenate([g_hi, (g - g_hi.astype(F32)).astype(BF16)], axis=0)
    h = h_ref[...]
    upc = SORT_CHUNK // UNIT
    for c in range(RB // SORT_CHUNK):
        for u in range(c * upc, (c + 1) * upc):
            match = rank_ref[pl.ds(eou_ref[b * UPB + u], 1), :] == (base + loc_ref[b * UPB + u]).astype(F32)
            p_ref[_unit_rows(u), :] = jnp.where(match, 1.0, 0.0).astype(BF16)
        rows = slice(c * SORT_CHUNK, (c + 1) * SORT_CHUNK)
        xs_scr[slot, rows, 0:D] = _dot(p_ref[rows, :], h).astype(BF16)
        xs_scr[slot, rows, D:XS_W] = _dot_nt(p_ref[rows, :], g_hl).astype(BF16)
        for u in range(c * upc, (c + 1) * upc):
            copy(xs_scr.at[slot, _unit_rows(u)], dst_ref[b * UPB + u], slot).start(priority=u % 2)
    for j in range(ZPB):
        copy(zero_scr, zdst_ref[b * ZPB + j], slot).start(priority=j % 2)

    @pl.when(b == N_BLK - 1)
    def _():
        wait_step(1 - slot)
        wait_step(slot)


def _moe_sort(li, eou, loc, dst, zdst, h, rank_t, gates_t):
    blk_t = pl.BlockSpec((N_EXP, B_SORT), lambda b, *_: (0, b))
    return pl.pallas_call(
        _sort_body,
        grid_spec=pltpu.PrefetchScalarGridSpec(
            num_scalar_prefetch=4,
            grid=(N_BLK,),
            in_specs=[pl.BlockSpec((B_SORT, D), lambda b, *_: (b, 0)), blk_t, blk_t],
            out_specs=[pl.BlockSpec(memory_space=pl.ANY),
                       pl.BlockSpec((RB, B_SORT), lambda b, *_: (b, 0))],
            scratch_shapes=[pltpu.VMEM((2, RB, XS_W), BF16),
                            pltpu.VMEM((UNIT, XS_W), BF16),
                            pltpu.SemaphoreType.DMA((2,))]),
        out_shape=[jax.ShapeDtypeStruct((R_SORT, XS_W), BF16),
                   jax.ShapeDtypeStruct((N_BLK * RB, B_SORT), BF16)],
        compiler_params=_cparams(("arbitrary",)),
        name=f"moe_sort_l{li}",
    )(eou, loc, dst, zdst, h, rank_t, gates_t)


def _experts_body(te_ref, ntile_ref, tile0_ref, wslot_ref, nexte_ref, ns_ref,
                  xs_hbm, wg_hbm, wu_hbm, wd_hbm, os_hbm,
                  xs_buf, o_buf, wg_f, wu_f, wd_f, sem_in, sem_out, sem_w, *, li):
    ns = ns_ref[0]

    def tiles_of(step):
        return jnp.where((step >= 0) & (step < ns), ntile_ref[jnp.clip(step, 0, NS_MAX - 1)], 0)

    def tile_rows(step, t):
        return pl.ds(pl.multiple_of((tile0_ref[jnp.clip(step, 0, NS_MAX - 1)] + t) * TMG, TMG), TMG)

    def buf_rows(t):
        return pl.ds(pl.multiple_of(t * TMG, TMG), TMG)

    def copy_in(step, t, s):
        return pltpu.make_async_copy(xs_hbm.at[tile_rows(step, t)], xs_buf.at[s, buf_rows(t)], sem_in.at[s])

    def copy_out(step, t, s):
        return pltpu.make_async_copy(o_buf.at[s, buf_rows(t)], os_hbm.at[tile_rows(step, t)], sem_out.at[s])

    def for_tiles(step, fn):
        lax.fori_loop(0, tiles_of(step), lambda t, c: (fn(t), c)[1], 0)

    def weight_copies(e, s):
        return [pltpu.make_async_copy(hbm.at[li, e], buf.at[s], sem_w.at[s])
                for hbm, buf in ((wg_hbm, wg_f), (wu_hbm, wu_f), (wd_hbm, wd_f))]

    for c in weight_copies(te_ref[0], 0):
        c.start()
    for_tiles(0, lambda t: copy_in(0, t, 0).start())

    def step(i, carry):
        slot = i % 2
        for_tiles(i + 1, lambda t: copy_in(i + 1, t, 1 - slot).start())
        for_tiles(i - 2, lambda t: copy_out(i - 2, t, slot).wait())
        e = te_ref[i]
        ws = wslot_ref[i]

        @pl.when(nexte_ref[i] >= -1)
        def _():
            for c in weight_copies(e, ws):
                c.wait()

            @pl.when(nexte_ref[i] >= 0)
            def _():
                for c in weight_copies(nexte_ref[i], 1 - ws):
                    c.start(priority=1)

        ntile = ntile_ref[i]
        for_tiles(i, lambda t: copy_in(i, t, slot).wait())
        for k in range(1, TPS + 1):
            rows = k * TMG

            @pl.when(ntile == k)
            def _():
                x = xs_buf[slot, 0:rows, 0:D]
                lane = lax.broadcasted_iota(jnp.int32, (rows, 128), 1)
                mine = (lane == e) | (lane == e + N_EXP)
                gate = jnp.sum(jnp.where(mine, xs_buf[slot, 0:rows, D:XS_W].astype(F32), 0.0),
                               axis=1, keepdims=True)
                hid = (_silu(_dot(x, wg_f[ws].astype(BF16))) * _dot(x, wu_f[ws].astype(BF16))) * gate
                o_buf[slot, 0:rows, :] = _dot(hid.astype(BF16), wd_f[ws].astype(BF16)).astype(BF16)

        for_tiles(i, lambda t: copy_out(i, t, slot).start(priority=1))
        return carry

    lax.fori_loop(0, ns, step, 0)
    for_tiles(ns - 2, lambda t: copy_out(ns - 2, t, ns % 2).wait())
    for_tiles(ns - 1, lambda t: copy_out(ns - 1, t, (ns - 1) % 2).wait())
    o_buf[0, 0:UNIT, :] = jnp.zeros((UNIT, D), BF16)
    zero = pltpu.make_async_copy(o_buf.at[0, 0:UNIT], os_hbm.at[_unit_rows(ZERO_UNIT)], sem_out.at[0])
    zero.start()
    zero.wait()


def _moe_experts(li, te, ntile, tile0, wslot, nexte, ns, xs, w_gate, w_up, w_down):
    return pl.pallas_call(
        functools.partial(_experts_body, li=li),
        grid_spec=pltpu.PrefetchScalarGridSpec(
            num_scalar_prefetch=6,
            grid=(1,),
            in_specs=[pl.BlockSpec(memory_space=pl.ANY)] * 4,
            out_specs=pl.BlockSpec(memory_space=pl.ANY),
            scratch_shapes=[pltpu.VMEM((2, STEP_ROWS, XS_W), BF16),
                            pltpu.VMEM((2, STEP_ROWS, D), BF16),
                            pltpu.VMEM((2, D, EXP_DIM), F32),
                            pltpu.VMEM((2, D, EXP_DIM), F32),
                            pltpu.VMEM((2, EXP_DIM, D), F32),
                            pltpu.SemaphoreType.DMA((2,)),
                            pltpu.SemaphoreType.DMA((2,)),
                            pltpu.SemaphoreType.DMA((2,))]),
        out_shape=jax.ShapeDtypeStruct((R_OUT, D), BF16),
        compiler_params=_cparams(("arbitrary",)),
        name=f"moe_experts_l{li}",
    )(te, ntile, tile0, wslot, nexte, ns, xs, w_gate, w_up, w_down)


def _combine_body(dst_ref, h_ref, p_ref, y_ref, mod_ref, gpost_ref,
                  wsg_ref, wsu_ref, wsd_ref, os_hbm, op_ref, ol_ref, os_scr, sem):
    b = pl.program_id(0)
    slot = b % 2

    def copy(d, u, s):
        return pltpu.make_async_copy(os_hbm.at[_unit_rows(d)], os_scr.at[s, _unit_rows(u)], sem.at[s])

    def fetch(blk, s):
        def pair(v, c):
            for q in range(2):
                copy(dst_ref[blk * UPB + 2 * v + q], 2 * v + q, s).start(priority=q)
            return c

        lax.fori_loop(0, UPB // 2, pair, 0, unroll=UNROLL // 2)

    @pl.when(b == 0)
    def _():
        fetch(0, 0)

    @pl.when(b + 1 < N_BLK)
    def _():
        fetch(b + 1, 1 - slot)

    h = h_ref[...]
    shared = _dot((_silu(_dot(h, wsg_ref[...])) * _dot(h, wsu_ref[...])).astype(BF16), wsd_ref[...])
    lax.fori_loop(0, UPB, lambda k, c: (copy(0, 0, slot).wait(), c)[1], 0, unroll=True)
    routed = lax.dot_general(os_scr[slot], p_ref[...], (((0,), (0,)), ((), ())),
                             preferred_element_type=F32).T
    out = y_ref[...] + mod_ref[5:6, :] * _rms(routed + shared, gpost_ref[...])

    @pl.when(b < T_P // B_SORT)
    def _():
        op_ref[...] = out

    @pl.when(b >= T_P // B_SORT)
    def _():
        ol_ref[...] = out


def _moe_combine(li, dst, h, p, y, modv, g_post, wsg_bf, wsu_bf, wsd_bf, os):
    nbp = T_P // B_SORT
    def tok(w):
        return pl.BlockSpec((B_SORT, w), lambda b, *_: (b, 0))

    def par(*shape):
        return pl.BlockSpec((None,) + shape, lambda b, *_: (li,) + (0,) * len(shape))

    return pl.pallas_call(
        _combine_body,
        grid_spec=pltpu.PrefetchScalarGridSpec(
            num_scalar_prefetch=1,
            grid=(N_BLK,),
            in_specs=[tok(D), pl.BlockSpec((RB, B_SORT), lambda b, *_: (b, 0)), tok(D),
                      pl.BlockSpec((None, None, 6, D), lambda b, *_: (li, _mod_row(b, B_SORT), 0, 0)),
                      par(1, D), par(D, EXP_DIM), par(D, EXP_DIM), par(EXP_DIM, D),
                      pl.BlockSpec(memory_space=pl.ANY)],
            out_specs=[pl.BlockSpec((B_SORT, D), lambda b, *_: (jnp.minimum(b, nbp - 1), 0)),
                       pl.BlockSpec((B_SORT, D), lambda b, *_: (jnp.maximum(b - nbp, 0), 0))],
            scratch_shapes=[pltpu.VMEM((2, RB, D), BF16),
                            pltpu.SemaphoreType.DMA((2,))]),
        out_shape=[jax.ShapeDtypeStruct((T_P, D), F32), jax.ShapeDtypeStruct((T_S, D), F32)],
        compiler_params=_cparams(("arbitrary",)),
        name=f"moe_combine_l{li}",
    )(dst, h, p, y, modv, g_post, wsg_bf, wsu_bf, wsd_bf, os)


def _block_diag(w):
    out = jnp.zeros((DEPTH, POOL_W, POOL_W), F32)
    for g in range(len(POOL_WINDOWS)):
        out = out.at[:, g * POOL_CH:(g + 1) * POOL_CH, g * POOL_CH:(g + 1) * POOL_CH].set(w[:, g])
    return out


def kernel(x_prompt, x_sample, cache_diff_k, cache_diff_v, cache_na_k, cache_na_v, c, c_ctx,
           w_ada, b_ada, g_pre_mix, g_post_mix, g_pre_ffn, g_post_ffn, w_in, w_out,
           diff_lambda, g_diff, na_bias, pool_w, pool_scale, w_router, b_router,
           w_gate, w_up, w_down, ws_gate, ws_up, ws_down):
    xp, xs = x_prompt.reshape(T_P, D), x_sample.reshape(T_S, D)
    cvec = jnp.concatenate([c_ctx[None, :], c, jnp.zeros((3, D), F32)], axis=0)
    modv = _modulation(cvec, w_ada, b_ada)[:, :1 + NB_S].reshape(DEPTH, 1 + NB_S, 6, D)

    cos_t, sin_t = _rope_tables()
    w_in_bf = w_in.astype(BF16)
    w_out_bf = w_out.astype(BF16)
    wsg_bf, wsu_bf, wsd_bf = ws_gate.astype(BF16), ws_up.astype(BF16), ws_down.astype(BF16)
    pool_bd = _block_diag(pool_w)
    row = lambda a: a.reshape(DEPTH, 1, a.shape[-1])
    g_pre_mix, g_post_mix, g_pre_ffn, g_post_ffn = map(row, (g_pre_mix, g_post_mix, g_pre_ffn, g_post_ffn))
    g_diff, pool_scale = row(g_diff), row(pool_scale)
    w_router_t = jnp.swapaxes(w_router, 1, 2)
    b_router_col = b_router.reshape(DEPTH, N_EXP, 1)
    tri = _rank_matrix()
    ck = cache_diff_k.reshape(NB_S, DEPTH, PAST, DA_W)
    cv = cache_diff_v.reshape(NB_S, DEPTH, PAST, DA_W)
    cnk = cache_na_k.reshape(NB_S, DEPTH, PAST, NA_W)
    cnv = cache_na_v.reshape(NB_S, DEPTH, PAST, NA_W)

    strips = _na_bias_strips(na_bias)

    caches = ()
    for li in range(DEPTH):
        lam_init = 0.8 - 0.6 * math.exp(-0.3 * li)
        qa, qb, ocp, up, ka, va, kb, vb, *caches = _in_proj(li, xp, xs, modv, g_pre_mix, w_in_bf, cos_t, sin_t,
                                                           pool_bd, pool_scale, caches)
        ocs = _pool_lat(li, up, pool_bd, pool_scale)
        oabp = _attn_ctx(li, diff_lambda, g_diff, qa, caches[0], caches[1], qb, caches[2], caches[3], lam_init)
        oabs = _attn_lat(li, diff_lambda, g_diff, qa, ka, va, ck, cv, qb, kb, vb, cnk, cnv, strips, lam_init)
        y, h, gates_t, rank_t, pcu = _merge(li, oabp, oabs, ocp, ocs, xp, xs, modv, g_post_mix, g_pre_ffn,
                                            w_out_bf, w_router_t, b_router_col, tri)
        eou, loc, dst_sort, dst_comb, zdst, te, ntile, tile0, wslot, nexte, ns = _moe_tables(pcu)
        srt, p = _moe_sort(li, eou, loc, dst_sort, zdst, h, rank_t, gates_t)
        os = _moe_experts(li, te, ntile, tile0, wslot, nexte, ns, srt, w_gate, w_up, w_down)
        xp, xs = _moe_combine(li, dst_comb, h, p, y, modv, g_post_ffn, wsg_bf, wsu_bf, wsd_bf, os)

    new_dk, new_dv, new_nk, new_nv = caches
    return (xp.reshape(NB_P, N_P, D), xs.reshape(NB_S, N_S, D),
            new_dk.reshape(NB_P, DEPTH, N_P, DA_H, 2 * DA_QK), new_dv.reshape(NB_P, DEPTH, N_P, DA_H, DA_V),
            new_nk.reshape(NB_P, DEPTH, N_P, NA_H, NA_D), new_nv.reshape(NB_P, DEPTH, N_P, NA_H, NA_D))
```

```python
import functools
import math

import numpy as np
import jax
import jax.numpy as jnp
from jax import lax
from jax.experimental import pallas as pl
from jax.experimental.pallas import tpu as pltpu

F32 = jnp.float32
BF16 = jnp.bfloat16

D = 1024
DEPTH = 2
NB_P, N_P = 16, 256
NB_S, N_S = 4, 1024
T_P = NB_P * N_P
T_S = NB_S * N_S
T = T_P + T_S
PAST = 512
GRID_W = 64
DA_H, DA_QK, DA_V = 4, 64, 128
DA_W = DA_H * DA_V
NA_H, NA_D = 4, 64
NA_W = NA_H * NA_D
NA_ROWS, NA_COLS = 8, 16
POOL_WINDOWS = (2, 4, 8, 16)
POOL_CH = 64
POOL_W = 256
IN_COLS = 3 * DA_W + 3 * NA_W + POOL_W
AB_W = DA_W + NA_W
N_EXP, TOP_K, N_GRP, TOPK_GRP = 64, 8, 8, 4
EXP_DIM = 256
ROUTED_SCALE = 2.5
EPS = 1e-6
NEG = -1e30
ROPE_THETA = 10000.0

LANES = 128
VMEM_LIMIT = 56 * 1024 * 1024

TM_IN = 512
NT_P = T_P // TM_IN
TQ = 512

B_SORT = 256
UNIT = 16
N_BLK = T // B_SORT
SORT_CHUNK = 1024
RUN_PAD_ROWS = TOP_K * B_SORT + N_EXP * (UNIT - 1)
RB = -(-RUN_PAD_ROWS // SORT_CHUNK) * SORT_CHUNK
UPB = RB // UNIT
TMG = 256
UPT = TMG // UNIT
NT_MAX = (N_BLK * RUN_PAD_ROWS) // TMG + N_EXP
XS_W = D + LANES
EXP_PER_BLK = N_EXP // N_BLK
ZPB = EXP_PER_BLK * (UPT - 1)
TPS = 6
STEP_ROWS = TPS * TMG
NS_MAX = NT_MAX // TPS + (N_EXP * (TPS - 1)) // TPS
ZERO_UNIT = NT_MAX * UPT
SPARE_UNIT0 = ZERO_UNIT + UPT
R_OUT = (NT_MAX + 1) * TMG
R_SORT = R_OUT + -(-2 * (UPB + ZPB) // UPT) * TMG
UNROLL = 8


def _cparams(sem):
    return pltpu.CompilerParams(dimension_semantics=sem, vmem_limit_bytes=VMEM_LIMIT)


def _rms(x, g):
    return x * lax.rsqrt(jnp.mean(x * x, axis=-1, keepdims=True) + EPS) * g


def _dot(a, b):
    return jnp.dot(a, b, preferred_element_type=F32)


def _dot_nt(a, b):
    return lax.dot_general(a, b, (((1,), (1,)), ((), ())), preferred_element_type=F32)


def _silu(x):
    return x / (1.0 + jnp.exp(-x))


def _mod_row(i, tm):
    off = i * tm - T_P
    return jnp.where(off >= 0, 1 + jnp.maximum(off, 0) // N_S, 0)


def _mod_body(c_ref, w_ref, b_ref, o_ref):
    c = c_ref[...]
    o_ref[...] = jnp.dot(_silu(c), w_ref[...], precision=lax.Precision.HIGHEST,
                         preferred_element_type=F32) + b_ref[...]


def _modulation(cvec, w_ada, b_ada):
    tn = 1536
    return pl.pallas_call(
        _mod_body,
        grid=(DEPTH, 6 * D // tn),
        in_specs=[pl.BlockSpec((8, D), lambda l, j: (0, 0)),
                  pl.BlockSpec((None, D, tn), lambda l, j: (l, 0, j)),
                  pl.BlockSpec((None, 1, tn), lambda l, j: (l, 0, j))],
        out_specs=pl.BlockSpec((None, 8, tn), lambda l, j: (l, 0, j)),
        out_shape=jax.ShapeDtypeStruct((DEPTH, 8, 6 * D), F32),
        compiler_params=_cparams(("parallel", "parallel")),
        name="modulation",
    )(cvec, w_ada, b_ada.reshape(DEPTH, 1, 6 * D))


def _rope_tables():
    nf = DA_QK // 4
    t = np.arange(N_S)
    pos = np.stack([t // GRID_W, t % GRID_W], axis=-1).astype(np.float32)
    inv = np.power(np.float32(ROPE_THETA), -np.arange(nf, dtype=np.float32) / nf)
    ang = pos[:, :, None] * inv
    cos = np.cos(ang)
    sin = np.sin(ang)
    cos64 = np.concatenate([cos[:, 0], cos[:, 0], cos[:, 1], cos[:, 1]], axis=-1)
    sin64 = np.concatenate([-sin[:, 0], sin[:, 0], -sin[:, 1], sin[:, 1]], axis=-1)
    reps = DA_W // DA_QK
    cos_t = np.concatenate([np.tile(cos64, (1, reps)), np.ones((TM_IN, DA_W), np.float32)], axis=0)
    sin_t = np.concatenate([np.tile(sin64, (1, reps)), np.zeros((TM_IN, DA_W), np.float32)], axis=0)
    return jnp.asarray(cos_t, F32), jnp.asarray(sin_t, F32)


def _tok_select(i, ctx_ref, lat_ref):
    return jnp.where(i < NT_P, ctx_ref[...], lat_ref[...])


def _in_proj_body(*refs):
    (xp_ref, xs_ref, mod_ref, g_ref, w_ref, cos_ref, sin_ref, pw_ref, ps_ref) = refs[:9]
    (qa_ref, qb_ref, ocp_ref, up_ref, ka_ref, va_ref, kb_ref, vb_ref,
     ck_ref, cv_ref, cnk_ref, cnv_ref) = refs[-12:]
    i = pl.program_id(0)
    x = _tok_select(i, xp_ref, xs_ref)
    h = (_rms(x, g_ref[...]) * (1.0 + mod_ref[1:2, :]) + mod_ref[0:1, :]).astype(BF16)
    cos = cos_ref[...]
    sin = sin_ref[...]
    lane = lax.broadcasted_iota(jnp.int32, (TM_IN, DA_W), 1)
    first = (lane % 32) < 16

    def proj(lo, hi):
        return _dot(h, w_ref[:, lo:hi])

    def rope(t):
        swapped = jnp.where(first, pltpu.roll(t, DA_W - 16, 1), pltpu.roll(t, 16, 1))
        return t * cos + swapped * sin

    o = 0
    qa_ref[...] = (rope(proj(o, o + DA_W)) * (DA_QK ** -0.5)).astype(BF16)
    o += DA_W
    ka = rope(proj(o, o + DA_W))
    o += DA_W
    va = proj(o, o + DA_W)
    o += DA_W
    qb_ref[...] = (proj(o, o + NA_W) * (NA_D ** -0.5)).astype(BF16)
    o += NA_W
    kb = proj(o, o + NA_W)
    o += NA_W
    vb = proj(o, o + NA_W)
    o += NA_W
    up = proj(o, o + POOL_W)

    @pl.when(i < NT_P)
    def _():
        for ref, val in ((ck_ref, ka), (cv_ref, va), (cnk_ref, kb), (cnv_ref, vb)):
            ref[...] = val.reshape(ref.shape)
        ocp_ref[...] = _pool_rows(up, N_P, pw_ref[...], ps_ref[...])

    @pl.when(i >= NT_P)
    def _():
        for ref, val in ((ka_ref, ka), (va_ref, va), (kb_ref, kb), (vb_ref, vb)):
            ref[...] = val.astype(BF16)
        up_ref[...] = up


def _ctx_tok(w):
    return pl.BlockSpec((TM_IN, w), lambda i: (jnp.minimum(i, NT_P - 1), 0))


def _lat_tok(w):
    return pl.BlockSpec((TM_IN, w), lambda i: (jnp.maximum(i - NT_P, 0), 0))


def _in_proj(li, xp, xs, modv, g_pre, w_in_bf, cos_t, sin_t, pool_bd, pool_scale, caches):
    n_pos_blk = N_S // TM_IN

    def pos_blk(i):
        return jnp.where(i >= NT_P, jnp.maximum(i - NT_P, 0) % n_pos_blk, n_pos_blk)

    def tok(w):
        return pl.BlockSpec((TM_IN, w), lambda i: (i, 0))

    def cache(w):
        return pl.BlockSpec((TM_IN // N_P, None, N_P, w), lambda i: (jnp.minimum(i, NT_P - 1), li, 0, 0))

    widths = (DA_W, DA_W, NA_W, NA_W)
    n_in, n_plain_out = 9, 8
    return pl.pallas_call(
        _in_proj_body,
        grid=(T // TM_IN,),
        in_specs=[_ctx_tok(D), _lat_tok(D),
                  pl.BlockSpec((None, None, 6, D), lambda i: (li, _mod_row(i, TM_IN), 0, 0)),
                  pl.BlockSpec((None, 1, D), lambda i: (li, 0, 0)),
                  pl.BlockSpec((None, D, IN_COLS), lambda i: (li, 0, 0)),
                  pl.BlockSpec((TM_IN, DA_W), lambda i: (pos_blk(i), 0)),
                  pl.BlockSpec((TM_IN, DA_W), lambda i: (pos_blk(i), 0)),
                  pl.BlockSpec((None, POOL_W, POOL_W), lambda i: (li, 0, 0)),
                  pl.BlockSpec((None, 1, POOL_W), lambda i: (li, 0, 0))]
                 + [pl.BlockSpec(memory_space=pl.ANY)] * len(caches),
        out_specs=[tok(DA_W), tok(NA_W), _ctx_tok(POOL_W), _lat_tok(POOL_W)] + [_lat_tok(w) for w in widths]
                  + [cache(w) for w in widths],
        out_shape=[jax.ShapeDtypeStruct((T, DA_W), BF16),
                   jax.ShapeDtypeStruct((T, NA_W), BF16),
                   jax.ShapeDtypeStruct((T_P, POOL_W), BF16),
                   jax.ShapeDtypeStruct((T_S, POOL_W), F32)]
                  + [jax.ShapeDtypeStruct((T_S, w), BF16) for w in widths]
                  + [jax.ShapeDtypeStruct((NB_P, DEPTH, N_P, w), F32) for w in widths],
        input_output_aliases={n_in + k: n_plain_out + k for k in range(len(caches))},
        compiler_params=_cparams(("arbitrary",)),
        name=f"in_proj_l{li}",
    )(xp, xs, modv, g_pre, w_in_bf, cos_t, sin_t, pool_bd, pool_scale, *caches)


def _pool_rows(u, n, w_mix, scale):
    rows = u.shape[0]
    row = lax.broadcasted_iota(jnp.int32, (rows, POOL_W), 0) % n
    lane = lax.broadcasted_iota(jnp.int32, (rows, POOL_W), 1)

    def shift_dn(a, k):
        return jnp.where(row >= k, pltpu.roll(a, k, 0), 0.0)

    def shift_up(a, k):
        return jnp.where(row < n - k, pltpu.roll(a, rows - k, 0), 0.0)

    fwd = u
    bwd = shift_dn(u, 1)
    mean = jnp.zeros_like(u)
    k = 1
    for gi, w in enumerate(POOL_WINDOWS):
        while k < w // 2:
            fwd = fwd + shift_up(fwd, k)
            bwd = bwd + shift_dn(bwd, k)
            k *= 2
        cnt = (jnp.minimum(row + w // 2, n) - jnp.maximum(row - w // 2, 0)).astype(F32)
        in_group = (lane >= gi * POOL_CH) & (lane < (gi + 1) * POOL_CH)
        mean = jnp.where(in_group, (fwd + bwd) / cnt, mean)
    pooled = mean - u
    mixed = jnp.dot(pooled, w_mix, precision=lax.Precision.HIGHEST, preferred_element_type=F32)
    return (mixed * scale).astype(BF16)


def _pool_body(u_ref, w_ref, s_ref, o_ref):
    o_ref[...] = _pool_rows(u_ref[...], N_S, w_ref[...], s_ref[...])


def _pool_lat(li, up, w_bd, scale):
    return pl.pallas_call(
        _pool_body,
        grid=(NB_S,),
        in_specs=[pl.BlockSpec((N_S, POOL_W), lambda i: (i, 0)),
                  pl.BlockSpec((None, POOL_W, POOL_W), lambda i: (li, 0, 0)),
                  pl.BlockSpec((None, 1, POOL_W), lambda i: (li, 0, 0))],
        out_specs=pl.BlockSpec((N_S, POOL_W), lambda i: (i, 0)),
        out_shape=jax.ShapeDtypeStruct((T_S, POOL_W), BF16),
        compiler_params=_cparams(("parallel",)),
        name=f"pool_lat_l{li}",
    )(up, w_bd, scale)


def _lambda(lp, lam_init):
    a = jnp.sum(lp[0:1, :] * lp[1:2, :], axis=1, keepdims=True)
    b = jnp.sum(lp[2:3, :] * lp[3:4, :], axis=1, keepdims=True)
    return jnp.exp(a) - jnp.exp(b) + lam_init


def _softmax_av(scores, vs):
    m = functools.reduce(jnp.maximum, [jnp.max(s, axis=-1, keepdims=True) for s in scores])
    es = [jnp.exp(s - m) for s in scores]
    den = functools.reduce(jnp.add, [jnp.sum(e, axis=-1, keepdims=True) for e in es])
    o = functools.reduce(jnp.add, [_dot(e.astype(BF16), v) for e, v in zip(es, vs)])
    return o * (1.0 / den)


def _diff_head(q, ks, vs, lam, gd, lam_init):
    o1 = _softmax_av([_dot_nt(q[:, :DA_QK], k[:, :DA_QK]) for k in ks], vs)
    o2 = _softmax_av([_dot_nt(q[:, DA_QK:], k[:, DA_QK:]) for k in ks], vs)
    return _rms(o1 - lam * o2, gd) * (1.0 - lam_init)


def _plain_head(q, ks, vs, biases):
    scores = []
    for k, bias in zip(ks, biases):
        s = _dot_nt(q, k)
        scores.append(s if bias is None else s + bias)
    return _softmax_av(scores, vs)


def _attn_ctx_body(lam_ref, gd_ref, qa_ref, ka_ref, va_ref, qb_ref, kb_ref, vb_ref, o_ref, *, lam_init):
    lam = _lambda(lam_ref[...], lam_init)
    gd = gd_ref[...]
    for h in range(DA_H):
        sl = slice(h * DA_V, (h + 1) * DA_V)
        o = _diff_head(qa_ref[:, sl], [ka_ref[:, sl].astype(BF16)], [va_ref[:, sl].astype(BF16)],
                       lam, gd, lam_init)
        o_ref[:, sl] = o.astype(BF16)
    for h in range(NA_H):
        sl = slice(h * NA_D, (h + 1) * NA_D)
        o = _plain_head(qb_ref[:, sl], [kb_ref[:, sl].astype(BF16)], [vb_ref[:, sl].astype(BF16)], [None])
        o_ref[:, DA_W + h * NA_D:DA_W + (h + 1) * NA_D] = o.astype(BF16)


def _attn_ctx(li, lam_p, g_diff, qa, ka, va, qb, kb, vb, lam_init):
    def blk(w):
        return pl.BlockSpec((N_P, w), lambda b: (b, 0))

    def cblk(w):
        return pl.BlockSpec((None, None, N_P, w), lambda b: (b, li, 0, 0))

    return pl.pallas_call(
        functools.partial(_attn_ctx_body, lam_init=lam_init),
        grid=(NB_P,),
        in_specs=[pl.BlockSpec((None, 4, DA_QK), lambda b: (li, 0, 0)),
                  pl.BlockSpec((None, 1, DA_V), lambda b: (li, 0, 0)),
                  blk(DA_W), cblk(DA_W), cblk(DA_W), blk(NA_W), cblk(NA_W), cblk(NA_W)],
        out_specs=pl.BlockSpec((N_P, AB_W), lambda b: (b, 0)),
        out_shape=jax.ShapeDtypeStruct((T_P, AB_W), BF16),
        compiler_params=_cparams(("parallel",)),
        name=f"attn_ctx_l{li}",
    )(lam_p, g_diff, qa, ka, va, qb, kb, vb)


G_ROWS = N_S // GRID_W
QR_PER_BLK = TQ // GRID_W


def _fill_na_bias(j, strip_ref, bias_scr):
    for jj in range(N_S // TQ):
        @pl.when(j == jj)
        def _():
            for rr in range(QR_PER_BLK):
                qr = jj * QR_PER_BLK + rr
                rs = min(max(qr - NA_ROWS // 2, 0), G_ROWS - NA_ROWS)
                lo, hi = rs * GRID_W, (rs + NA_ROWS) * GRID_W
                a0 = (rs - qr + NA_ROWS - 1) * GRID_W
                q = slice(rr * GRID_W, (rr + 1) * GRID_W)
                for h in range(NA_H):
                    if lo > 0:
                        bias_scr[h, q, 0:lo] = jnp.full((GRID_W, lo), NEG, F32)
                    bias_scr[h, q, lo:hi] = strip_ref[h, :, a0:a0 + NA_ROWS * GRID_W]
                    if hi < N_S:
                        bias_scr[h, q, hi:N_S] = jnp.full((GRID_W, N_S - hi), NEG, F32)


def _attn_lat_body(lam_ref, gd_ref, qa_ref, ka_ref, va_ref, ck_ref, cv_ref,
                   qb_ref, kb_ref, vb_ref, cnk_ref, cnv_ref, strip_ref, o_ref, bias_scr, *, lam_init):
    _fill_na_bias(pl.program_id(1), strip_ref, bias_scr)
    lam = _lambda(lam_ref[...], lam_init)
    gd = gd_ref[...]
    for h in range(DA_H):
        sl = slice(h * DA_V, (h + 1) * DA_V)
        ks = [ka_ref[:, sl], ck_ref[:, sl].astype(BF16)]
        vs = [va_ref[:, sl], cv_ref[:, sl].astype(BF16)]
        o_ref[:, sl] = _diff_head(qa_ref[:, sl], ks, vs, lam, gd, lam_init).astype(BF16)
    for h in range(NA_H):
        sl = slice(h * NA_D, (h + 1) * NA_D)
        ks = [kb_ref[:, sl], cnk_ref[:, sl].astype(BF16)]
        vs = [vb_ref[:, sl], cnv_ref[:, sl].astype(BF16)]
        o = _plain_head(qb_ref[:, sl], ks, vs, [bias_scr[h], None])
        o_ref[:, DA_W + h * NA_D:DA_W + (h + 1) * NA_D] = o.astype(BF16)


def _attn_lat(li, lam_p, g_diff, qa, ka, va, ck, cv, qb, kb, vb, cnk, cnv, strip, lam_init):
    nq = N_S // TQ
    q0 = T_P // TQ

    def qblk(w):
        return pl.BlockSpec((TQ, w), lambda b, j: (q0 + b * nq + j, 0))

    def kvblk(w):
        return pl.BlockSpec((N_S, w), lambda b, j: (b, 0))

    def cblk(w):
        return pl.BlockSpec((None, None, PAST, w), lambda b, j: (b, li, 0, 0))

    return pl.pallas_call(
        functools.partial(_attn_lat_body, lam_init=lam_init),
        grid=(NB_S, nq),
        in_specs=[pl.BlockSpec((None, 4, DA_QK), lambda b, j: (li, 0, 0)),
                  pl.BlockSpec((None, 1, DA_V), lambda b, j: (li, 0, 0)),
                  qblk(DA_W), kvblk(DA_W), kvblk(DA_W), cblk(DA_W), cblk(DA_W),
                  qblk(NA_W), kvblk(NA_W), kvblk(NA_W), cblk(NA_W), cblk(NA_W),
                  pl.BlockSpec((None, NA_H, GRID_W, (2 * NA_ROWS - 1) * GRID_W), lambda b, j: (li, 0, 0, 0))],
        out_specs=pl.BlockSpec((TQ, AB_W), lambda b, j: (b * nq + j, 0)),
        out_shape=jax.ShapeDtypeStruct((T_S, AB_W), BF16),
        scratch_shapes=[pltpu.VMEM((NA_H, TQ, N_S), F32)],
        compiler_params=_cparams(("parallel", "parallel")),
        name=f"attn_lat_l{li}",
    )(lam_p, g_diff, qa, ka, va, ck, cv, qb, kb, vb, cnk, cnv, strip)


def _na_bias_strips(table):
    w = np.arange(GRID_W)
    cs = np.clip(w - NA_COLS // 2, 0, GRID_W - NA_COLS)
    col_ok = (w[None, :] >= cs[:, None]) & (w[None, :] < cs[:, None] + NA_COLS)
    sel_c = (np.clip(w[None, :, None] - w[:, None, None] + NA_COLS - 1, 0, 2 * NA_COLS - 2)
             == np.arange(2 * NA_COLS - 1)).astype(np.float32)
    t1 = jnp.einsum('lhab,cdb->lhcad', table.astype(F32), jnp.asarray(sel_c),
                    precision=lax.Precision.HIGHEST)
    t1 = jnp.where(jnp.asarray(col_ok)[None, None, :, None, :], t1, NEG)
    return t1.reshape(DEPTH, NA_H, GRID_W, (2 * NA_ROWS - 1) * GRID_W)


def _route_t(logits_t, b_col, tm):
    per = N_EXP // N_GRP
    s = 1.0 / (1.0 + jnp.exp(-logits_t))
    sel3 = (s + b_col).reshape(N_GRP, per, tm)
    midx = lax.broadcasted_iota(jnp.int32, (N_GRP, per, tm), 1)
    gidx = lax.broadcasted_iota(jnp.int32, (N_GRP, per, tm), 0)
    ninf = -jnp.inf
    m1 = jnp.max(sel3, axis=1, keepdims=True)
    i1 = jnp.min(jnp.where(sel3 == m1, midx, per), axis=1, keepdims=True)
    m2 = jnp.max(jnp.where(midx == i1, ninf, sel3), axis=1, keepdims=True)
    gsc = jnp.broadcast_to(m1 + m2, (N_GRP, per, tm))
    cnt = jnp.zeros((N_GRP, per, tm), jnp.int32)
    for g in range(N_GRP):
        sg = gsc[g][None]
        cnt = cnt + ((sg > gsc) | ((sg == gsc) & (g < gidx))).astype(jnp.int32)
    x = jnp.where(cnt < TOPK_GRP, sel3, ninf).reshape(N_EXP, tm)
    eidx = lax.broadcasted_iota(jnp.int32, (N_EXP, tm), 0)
    chosen = jnp.zeros((N_EXP, tm), jnp.bool_)
    for _ in range(TOP_K):
        m = jnp.max(x, axis=0, keepdims=True)
        first = jnp.min(jnp.where(x == m, eidx, N_EXP), axis=0, keepdims=True)
        hit = eidx == first
        chosen = chosen | hit
        x = jnp.where(hit, ninf, x)
    w = jnp.where(chosen, s, 0.0)
    return chosen, w / jnp.sum(w, axis=0, keepdims=True) * ROUTED_SCALE


def _merge_body(oabp_ref, oabs_ref, ocp_ref, ocs_ref, xp_ref, xs_ref, mod_ref, gpost_ref, gpre_ref,
                wout_ref, wrt_ref, br_ref, tri_ref, y_ref, h_ref, gates_ref, rank_ref, pcu_ref):
    i = pl.program_id(0)
    mix = (_dot(_tok_select(i, oabp_ref, oabs_ref), wout_ref[0:AB_W, :])
           + _dot(_tok_select(i, ocp_ref, ocs_ref), wout_ref[AB_W:D, :]))
    y = _tok_select(i, xp_ref, xs_ref) + mod_ref[2:3, :] * _rms(mix, gpost_ref[...])
    y_ref[...] = y
    h = _rms(y, gpre_ref[...]) * (1.0 + mod_ref[4:5, :]) + mod_ref[3:4, :]
    h_ref[...] = h.astype(BF16)
    logits_t = lax.dot_general(wrt_ref[...], h, (((1,), (1,)), ((), ())),
                               precision=lax.Precision.HIGHEST, preferred_element_type=F32)
    chosen, gates = _route_t(logits_t, br_ref[...], TM_IN)
    gates_ref[...] = gates
    ch = jnp.where(chosen, 1.0, 0.0)
    rank = _dot(ch.astype(BF16), tri_ref[...])
    rank_ref[...] = jnp.where(chosen, rank, -1.0)
    for j in range(TM_IN // B_SORT):
        cnt = jnp.sum(ch[:, j * B_SORT:(j + 1) * B_SORT], axis=1, keepdims=True)
        units = jnp.floor((cnt + (UNIT - 1)) * (1.0 / UNIT))
        pcu_ref[j] = jnp.broadcast_to(units, (N_EXP, LANES))


def _merge(li, oabp, oabs, ocp, ocs, xp, xs, modv, g_post, g_pre, w_out_bf, w_router_t, b_router_col, tri):
    def tok(w):
        return pl.BlockSpec((TM_IN, w), lambda i: (i, 0))

    def par(*shape):
        return pl.BlockSpec((None,) + shape, lambda i: (li,) + (0,) * len(shape))

    tok_t = pl.BlockSpec((N_EXP, TM_IN), lambda i: (0, i))
    nb = TM_IN // B_SORT
    return pl.pallas_call(
        _merge_body,
        grid=(T // TM_IN,),
        in_specs=[_ctx_tok(AB_W), _lat_tok(AB_W), _ctx_tok(POOL_W), _lat_tok(POOL_W), _ctx_tok(D), _lat_tok(D),
                  pl.BlockSpec((None, None, 6, D), lambda i: (li, _mod_row(i, TM_IN), 0, 0)),
                  par(1, D), par(1, D), par(D, D), par(N_EXP, D), par(N_EXP, 1),
                  pl.BlockSpec((TM_IN, TM_IN), lambda i: (0, 0))],
        out_specs=[tok(D), tok(D), tok_t, tok_t,
                   pl.BlockSpec((nb, N_EXP, LANES), lambda i: (i, 0, 0))],
        out_shape=[jax.ShapeDtypeStruct((T, D), F32),
                   jax.ShapeDtypeStruct((T, D), BF16),
                   jax.ShapeDtypeStruct((N_EXP, T), F32),
                   jax.ShapeDtypeStruct((N_EXP, T), F32),
                   jax.ShapeDtypeStruct((N_BLK, N_EXP, LANES), F32)],
        compiler_params=_cparams(("parallel",)),
        name=f"merge_l{li}",
    )(oabp, oabs, ocp, ocs, xp, xs, modv, g_post, g_pre, w_out_bf, w_router_t, b_router_col, tri)


def _rank_matrix():
    t = np.arange(TM_IN)
    m = (t[:, None] < t[None, :]) & (t[:, None] // B_SORT == t[None, :] // B_SORT)
    return jnp.asarray(m, BF16)


def _moe_tables(pcu_f):
    pcu = pcu_f[:, :, 0].astype(jnp.int32)
    incl = jnp.cumsum(pcu, axis=1)
    uo = incl - pcu
    tot = jnp.sum(pcu, axis=0)
    gt = (tot + UPT - 1) // UPT
    gstart_t = jnp.cumsum(gt) - gt
    gstart_u = gstart_t * UPT
    steps = (gt + TPS - 1) // TPS
    sincl = jnp.cumsum(steps)
    sstart = sincl - steps
    bstart = gstart_u[None, :] + jnp.cumsum(pcu, axis=0) - pcu
    u = jnp.arange(UPB, dtype=jnp.int32)
    eou = jnp.sum((u[None, :, None] >= incl[:, None, :]).astype(jnp.int32), axis=-1)
    valid = eou < N_EXP
    onehot = eou[:, :, None] == jnp.arange(N_EXP, dtype=jnp.int32)[None, None, :]
    run_u0 = jnp.sum(jnp.where(onehot, uo[:, None, :], 0), axis=-1)
    run_g0 = jnp.sum(jnp.where(onehot, bstart[:, None, :], 0), axis=-1)
    dst = run_g0 + u[None, :] - run_u0
    spare0 = SPARE_UNIT0 + (jnp.arange(N_BLK, dtype=jnp.int32) % 2)[:, None] * (UPB + ZPB)
    dst_sort = jnp.where(valid, dst, spare0 + u[None, :])
    dst_comb = jnp.where(valid, dst, ZERO_UNIT)
    loc = jnp.where(valid, (u[None, :] - run_u0) * UNIT, -(1 << 20))
    k = jnp.arange(UPT - 1, dtype=jnp.int32)
    ztail = gt * UPT - tot
    zdst = jnp.where(k[None, :] < ztail[:, None], (gstart_u + tot)[:, None] + k[None, :], -1).reshape(N_BLK, ZPB)
    zdst = jnp.where(zdst >= 0, zdst, spare0 + UPB + jnp.arange(ZPB, dtype=jnp.int32)[None, :])
    step = jnp.arange(NS_MAX, dtype=jnp.int32)
    te = jnp.minimum(jnp.sum((step[:, None] >= sincl[None, :]).astype(jnp.int32), axis=-1), N_EXP - 1)
    mine = te[:, None] == jnp.arange(N_EXP, dtype=jnp.int32)[None, :]
    pick = lambda v: jnp.sum(jnp.where(mine, v[None, :], 0), axis=-1)
    first = gt - TPS * (steps - 1)
    j = step - pick(sstart)
    ntile = jnp.where(j == 0, pick(first), TPS)
    tile0 = pick(gstart_t) + jnp.where(j == 0, 0, pick(first) + TPS * (j - 1))
    ns = sincl[-1]
    wslot = (jnp.cumsum((j == 0).astype(jnp.int32)) - 1) % 2
    nxt = pick(sincl)
    nexte = jnp.sum(jnp.where(step[None, :] == nxt[:, None], te[None, :], 0), axis=-1)
    nexte = jnp.where(j == 0, jnp.where(nxt < ns, nexte, -1), -2)
    flat = lambda a: a.reshape(-1)
    return (flat(jnp.minimum(eou, N_EXP - 1)), flat(loc), flat(dst_sort), flat(dst_comb), flat(zdst),
            te, ntile, tile0, wslot, nexte, sincl[-1:])


def _unit_rows(u):
    if isinstance(u, int):
        return pl.ds(u * UNIT, UNIT)
    return pl.ds(pl.multiple_of(u * UNIT, UNIT), UNIT)


def _sort_body(eou_ref, loc_ref, dst_ref, zdst_ref, h_ref, rank_ref, gates_ref, xs_hbm, p_ref,
               xs_scr, zero_scr, sem):
    b = pl.program_id(0)
    slot = b % 2

    def copy(src, d, s):
        return pltpu.make_async_copy(src, xs_hbm.at[_unit_rows(d)], sem.at[s])

    def wait_step(s):
        lax.fori_loop(0, UPB + ZPB, lambda k, c: (copy(zero_scr, 0, s).wait(), c)[1], 0, unroll=True)

    @pl.when(b == 0)
    def _():
        zero_scr[...] = jnp.zeros((UNIT, XS_W), BF16)

    @pl.when(b >= 2)
    def _():
        wait_step(slot)

    base = lax.broadcasted_iota(jnp.int32, (UNIT, B_SORT), 0)
    g = gates_ref[...]
    g_hi = g.astype(BF16)
    g_hl = jnp.concatenate([g_hi, (g - g_hi.astype(F32)).astype(BF16)], axis=0)
    h = h_ref[...]
    upc = SORT_CHUNK // UNIT
    for c in range(RB // SORT_CHUNK):
        for u in range(c * upc, (c + 1) * upc):
            match = rank_ref[pl.ds(eou_ref[b * UPB + u], 1), :] == (base + loc_ref[b * UPB + u]).astype(F32)
            p_ref[_unit_rows(u), :] = jnp.where(match, 1.0, 0.0).astype(BF16)
        rows = slice(c * SORT_CHUNK, (c + 1) * SORT_CHUNK)
        xs_scr[slot, rows, 0:D] = _dot(p_ref[rows, :], h).astype(BF16)
        xs_scr[slot, rows, D:XS_W] = _dot_nt(p_ref[rows, :], g_hl).astype(BF16)
        for u in range(c * upc, (c + 1) * upc):
            copy(xs_scr.at[slot, _unit_rows(u)], dst_ref[b * UPB + u], slot).start()
    for j in range(ZPB):
        copy(zero_scr, zdst_ref[b * ZPB + j], slot).start()

    @pl.when(b == N_BLK - 1)
    def _():
        wait_step(1 - slot)
        wait_step(slot)


def _moe_sort(li, eou, loc, dst, zdst, h, rank_t, gates_t):
    blk_t = pl.BlockSpec((N_EXP, B_SORT), lambda b, *_: (0, b))
    return pl.pallas_call(
        _sort_body,
        grid_spec=pltpu.PrefetchScalarGridSpec(
            num_scalar_prefetch=4,
            grid=(N_BLK,),
            in_specs=[pl.BlockSpec((B_SORT, D), lambda b, *_: (b, 0)), blk_t, blk_t],
            out_specs=[pl.BlockSpec(memory_space=pl.ANY),
                       pl.BlockSpec((RB, B_SORT), lambda b, *_: (b, 0))],
            scratch_shapes=[pltpu.VMEM((2, RB, XS_W), BF16),
                            pltpu.VMEM((UNIT, XS_W), BF16),
                            pltpu.SemaphoreType.DMA((2,))]),
        out_shape=[jax.ShapeDtypeStruct((R_SORT, XS_W), BF16),
                   jax.ShapeDtypeStruct((N_BLK * RB, B_SORT), BF16)],
        compiler_params=_cparams(("arbitrary",)),
        name=f"moe_sort_l{li}",
    )(eou, loc, dst, zdst, h, rank_t, gates_t)


def _experts_body(te_ref, ntile_ref, tile0_ref, wslot_ref, nexte_ref, ns_ref,
                  xs_hbm, wg_hbm, wu_hbm, wd_hbm, os_hbm,
                  xs_buf, o_buf, wg_f, wu_f, wd_f, sem_in, sem_out, sem_w, *, li):
    ns = ns_ref[0]

    def tiles_of(step):
        return jnp.where((step >= 0) & (step < ns), ntile_ref[jnp.clip(step, 0, NS_MAX - 1)], 0)

    def tile_rows(step, t):
        return pl.ds(pl.multiple_of((tile0_ref[jnp.clip(step, 0, NS_MAX - 1)] + t) * TMG, TMG), TMG)

    def buf_rows(t):
        return pl.ds(pl.multiple_of(t * TMG, TMG), TMG)

    def copy_in(step, t, s):
        return pltpu.make_async_copy(xs_hbm.at[tile_rows(step, t)], xs_buf.at[s, buf_rows(t)], sem_in.at[s])

    def copy_out(step, t, s):
        return pltpu.make_async_copy(o_buf.at[s, buf_rows(t)], os_hbm.at[tile_rows(step, t)], sem_out.at[s])

    def for_tiles(step, fn):
        lax.fori_loop(0, tiles_of(step), lambda t, c: (fn(t), c)[1], 0)

    def weight_copies(e, s):
        return [pltpu.make_async_copy(hbm.at[li, e], buf.at[s], sem_w.at[s])
                for hbm, buf in ((wg_hbm, wg_f), (wu_hbm, wu_f), (wd_hbm, wd_f))]

    for c in weight_copies(te_ref[0], 0):
        c.start()
    for_tiles(0, lambda t: copy_in(0, t, 0).start())

    def step(i, carry):
        slot = i % 2
        for_tiles(i + 1, lambda t: copy_in(i + 1, t, 1 - slot).start())
        for_tiles(i - 2, lambda t: copy_out(i - 2, t, slot).wait())
        e = te_ref[i]
        ws = wslot_ref[i]

        @pl.when(nexte_ref[i] >= -1)
        def _():
            for c in weight_copies(e, ws):
                c.wait()

            @pl.when(nexte_ref[i] >= 0)
            def _():
                for c in weight_copies(nexte_ref[i], 1 - ws):
                    c.start()

        ntile = ntile_ref[i]
        for_tiles(i, lambda t: copy_in(i, t, slot).wait())
        for k in range(1, TPS + 1):
            rows = k * TMG

            @pl.when(ntile == k)
            def _():
                x = xs_buf[slot, 0:rows, 0:D]
                lane = lax.broadcasted_iota(jnp.int32, (rows, LANES), 1)
                mine = (lane == e) | (lane == e + N_EXP)
                gate = jnp.sum(jnp.where(mine, xs_buf[slot, 0:rows, D:XS_W].astype(F32), 0.0),
                               axis=1, keepdims=True)
                hid = (_silu(_dot(x, wg_f[ws].astype(BF16))) * _dot(x, wu_f[ws].astype(BF16))) * gate
                o_buf[slot, 0:rows, :] = _dot(hid.astype(BF16), wd_f[ws].astype(BF16)).astype(BF16)

        for_tiles(i, lambda t: copy_out(i, t, slot).start())
        return carry

    lax.fori_loop(0, ns, step, 0)
    for_tiles(ns - 2, lambda t: copy_out(ns - 2, t, ns % 2).wait())
    for_tiles(ns - 1, lambda t: copy_out(ns - 1, t, (ns - 1) % 2).wait())
    o_buf[0, 0:UNIT, :] = jnp.zeros((UNIT, D), BF16)
    zero = pltpu.make_async_copy(o_buf.at[0, 0:UNIT], os_hbm.at[_unit_rows(ZERO_UNIT)], sem_out.at[0])
    zero.start()
    zero.wait()


def _moe_experts(li, te, ntile, tile0, wslot, nexte, ns, xs, w_gate, w_up, w_down):
    return pl.pallas_call(
        functools.partial(_experts_body, li=li),
        grid_spec=pltpu.PrefetchScalarGridSpec(
            num_scalar_prefetch=6,
            grid=(1,),
            in_specs=[pl.BlockSpec(memory_space=pl.ANY)] * 4,
            out_specs=pl.BlockSpec(memory_space=pl.ANY),
            scratch_shapes=[pltpu.VMEM((2, STEP_ROWS, XS_W), BF16),
                            pltpu.VMEM((2, STEP_ROWS, D), BF16),
                            pltpu.VMEM((2, D, EXP_DIM), F32),
                            pltpu.VMEM((2, D, EXP_DIM), F32),
                            pltpu.VMEM((2, EXP_DIM, D), F32),
                            pltpu.SemaphoreType.DMA((2,)),
                            pltpu.SemaphoreType.DMA((2,)),
                            pltpu.SemaphoreType.DMA((2,))]),
        out_shape=jax.ShapeDtypeStruct((R_OUT, D), BF16),
        compiler_params=_cparams(("arbitrary",)),
        name=f"moe_experts_l{li}",
    )(te, ntile, tile0, wslot, nexte, ns, xs, w_gate, w_up, w_down)


def _combine_body(dst_ref, h_ref, p_ref, y_ref, mod_ref, gpost_ref,
                  wsg_ref, wsu_ref, wsd_ref, os_hbm, op_ref, ol_ref, os_scr, sem):
    b = pl.program_id(0)
    slot = b % 2

    def copy(d, u, s):
        return pltpu.make_async_copy(os_hbm.at[_unit_rows(d)], os_scr.at[s, _unit_rows(u)], sem.at[s])

    def fetch(blk, s):
        lax.fori_loop(0, UPB, lambda u, c: (copy(dst_ref[blk * UPB + u], u, s).start(), c)[1], 0,
                      unroll=UNROLL)

    @pl.when(b == 0)
    def _():
        fetch(0, 0)

    @pl.when(b + 1 < N_BLK)
    def _():
        fetch(b + 1, 1 - slot)

    h = h_ref[...]
    shared = _dot((_silu(_dot(h, wsg_ref[...])) * _dot(h, wsu_ref[...])).astype(BF16), wsd_ref[...])
    lax.fori_loop(0, UPB, lambda k, c: (copy(0, 0, slot).wait(), c)[1], 0, unroll=True)
    routed = lax.dot_general(os_scr[slot], p_ref[...], (((0,), (0,)), ((), ())),
                             preferred_element_type=F32).T
    out = y_ref[...] + mod_ref[5:6, :] * _rms(routed + shared, gpost_ref[...])

    @pl.when(b < T_P // B_SORT)
    def _():
        op_ref[...] = out

    @pl.when(b >= T_P // B_SORT)
    def _():
        ol_ref[...] = out


def _moe_combine(li, dst, h, p, y, modv, g_post, wsg_bf, wsu_bf, wsd_bf, os):
    nbp = T_P // B_SORT
    def tok(w):
        return pl.BlockSpec((B_SORT, w), lambda b, *_: (b, 0))

    def par(*shape):
        return pl.BlockSpec((None,) + shape, lambda b, *_: (li,) + (0,) * len(shape))

    return pl.pallas_call(
        _combine_body,
        grid_spec=pltpu.PrefetchScalarGridSpec(
            num_scalar_prefetch=1,
            grid=(N_BLK,),
            in_specs=[tok(D), pl.BlockSpec((RB, B_SORT), lambda b, *_: (b, 0)), tok(D),
                      pl.BlockSpec((None, None, 6, D), lambda b, *_: (li, _mod_row(b, B_SORT), 0, 0)),
                      par(1, D), par(D, EXP_DIM), par(D, EXP_DIM), par(EXP_DIM, D),
                      pl.BlockSpec(memory_space=pl.ANY)],
            out_specs=[pl.BlockSpec((B_SORT, D), lambda b, *_: (jnp.minimum(b, nbp - 1), 0)),
                       pl.BlockSpec((B_SORT, D), lambda b, *_: (jnp.maximum(b - nbp, 0), 0))],
            scratch_shapes=[pltpu.VMEM((2, RB, D), BF16),
                            pltpu.SemaphoreType.DMA((2,))]),
        out_shape=[jax.ShapeDtypeStruct((T_P, D), F32), jax.ShapeDtypeStruct((T_S, D), F32)],
        compiler_params=_cparams(("arbitrary",)),
        name=f"moe_combine_l{li}",
    )(dst, h, p, y, modv, g_post, wsg_bf, wsu_bf, wsd_bf, os)


def _block_diag(w):
    out = jnp.zeros((DEPTH, POOL_W, POOL_W), F32)
    for g in range(len(POOL_WINDOWS)):
        out = out.at[:, g * POOL_CH:(g + 1) * POOL_CH, g * POOL_CH:(g + 1) * POOL_CH].set(w[:, g])
    return out


def kernel(x_prompt, x_sample, cache_diff_k, cache_diff_v, cache_na_k, cache_na_v, c, c_ctx,
           w_ada, b_ada, g_pre_mix, g_post_mix, g_pre_ffn, g_post_ffn, w_in, w_out,
           diff_lambda, g_diff, na_bias, pool_w, pool_scale, w_router, b_router,
           w_gate, w_up, w_down, ws_gate, ws_up, ws_down):
    xp, xs = x_prompt.reshape(T_P, D), x_sample.reshape(T_S, D)
    cvec = jnp.concatenate([c_ctx[None, :], c, jnp.zeros((3, D), F32)], axis=0)
    modv = _modulation(cvec, w_ada, b_ada)[:, :1 + NB_S].reshape(DEPTH, 1 + NB_S, 6, D)

    cos_t, sin_t = _rope_tables()
    w_in_bf = w_in.astype(BF16)
    w_out_bf = w_out.astype(BF16)
    wsg_bf, wsu_bf, wsd_bf = ws_gate.astype(BF16), ws_up.astype(BF16), ws_down.astype(BF16)
    pool_bd = _block_diag(pool_w)
    row = lambda a: a.reshape(DEPTH, 1, a.shape[-1])
    g_pre_mix, g_post_mix, g_pre_ffn, g_post_ffn = map(row, (g_pre_mix, g_post_mix, g_pre_ffn, g_post_ffn))
    g_diff, pool_scale = row(g_diff), row(pool_scale)
    w_router_t = jnp.swapaxes(w_router, 1, 2)
    b_router_col = b_router.reshape(DEPTH, N_EXP, 1)
    tri = _rank_matrix()
    ck = cache_diff_k.reshape(NB_S, DEPTH, PAST, DA_W)
    cv = cache_diff_v.reshape(NB_S, DEPTH, PAST, DA_W)
    cnk = cache_na_k.reshape(NB_S, DEPTH, PAST, NA_W)
    cnv = cache_na_v.reshape(NB_S, DEPTH, PAST, NA_W)

    strips = _na_bias_strips(na_bias)

    caches = ()
    for li in range(DEPTH):
        lam_init = 0.8 - 0.6 * math.exp(-0.3 * li)
        qa, qb, ocp, up, ka, va, kb, vb, *caches = _in_proj(li, xp, xs, modv, g_pre_mix, w_in_bf, cos_t, sin_t,
                                                           pool_bd, pool_scale, caches)
        ocs = _pool_lat(li, up, pool_bd, pool_scale)
        oabp = _attn_ctx(li, diff_lambda, g_diff, qa, caches[0], caches[1], qb, caches[2], caches[3], lam_init)
        oabs = _attn_lat(li, diff_lambda, g_diff, qa, ka, va, ck, cv, qb, kb, vb, cnk, cnv, strips, lam_init)
        y, h, gates_t, rank_t, pcu = _merge(li, oabp, oabs, ocp, ocs, xp, xs, modv, g_post_mix, g_pre_ffn,
                                            w_out_bf, w_router_t, b_router_col, tri)
        eou, loc, dst_sort, dst_comb, zdst, te, ntile, tile0, wslot, nexte, ns = _moe_tables(pcu)
        srt, p = _moe_sort(li, eou, loc, dst_sort, zdst, h, rank_t, gates_t)
        os = _moe_experts(li, te, ntile, tile0, wslot, nexte, ns, srt, w_gate, w_up, w_down)
        xp, xs = _moe_combine(li, dst_comb, h, p, y, modv, g_post_ffn, wsg_bf, wsu_bf, wsd_bf, os)

    new_dk, new_dv, new_nk, new_nv = caches
    return (xp.reshape(NB_P, N_P, D), xs.reshape(NB_S, N_S, D),
            new_dk.reshape(NB_P, DEPTH, N_P, DA_H, 2 * DA_QK), new_dv.reshape(NB_P, DEPTH, N_P, DA_H, DA_V),
            new_nk.reshape(NB_P, DEPTH, N_P, NA_H, NA_D), new_nv.reshape(NB_P, DEPTH, N_P, NA_H, NA_D))
```

```python
import functools
import math

import numpy as np
import jax
import jax.numpy as jnp
from jax import lax
from jax.experimental import pallas as pl
from jax.experimental.pallas import tpu as pltpu

F32 = jnp.float32
BF16 = jnp.bfloat16

D = 1024
DEPTH = 2
NB_P, N_P = 16, 256
NB_S, N_S = 4, 1024
T_P = NB_P * N_P
T_S = NB_S * N_S
T = T_P + T_S
PAST = 512
GRID_W = 64
DA_H, DA_QK, DA_V = 4, 64, 128
DA_W = DA_H * DA_V
NA_H, NA_D = 4, 64
NA_W = NA_H * NA_D
NA_ROWS, NA_COLS = 8, 16
POOL_WINDOWS = (2, 4, 8, 16)
POOL_CH = 64
POOL_W = 256
IN_COLS = 3 * DA_W + 3 * NA_W + POOL_W
AB_W = DA_W + NA_W
N_EXP, TOP_K, N_GRP, TOPK_GRP = 64, 8, 8, 4
EXP_DIM = 256
ROUTED_SCALE = 2.5
EPS = 1e-6
NEG = -1e30
ROPE_THETA = 10000.0

LANES = 128
VMEM_LIMIT = 56 * 1024 * 1024

TM_IN = 512
NT_P = T_P // TM_IN
TQ = 512

B_SORT = 256
UNIT = 16
N_BLK = T // B_SORT
SORT_CHUNK = 1024
RUN_PAD_ROWS = TOP_K * B_SORT + N_EXP * (UNIT - 1)
RB = -(-RUN_PAD_ROWS // SORT_CHUNK) * SORT_CHUNK
UPB = RB // UNIT
MIN_UNITS = TOP_K * B_SORT // UNIT
TMG = 256
UPT = TMG // UNIT
NT_MAX = (N_BLK * RUN_PAD_ROWS) // TMG + N_EXP
XS_W = D + LANES
EXP_PER_BLK = N_EXP // N_BLK
ZPB = EXP_PER_BLK * (UPT - 1)
TPS = 6
STEP_ROWS = TPS * TMG
NS_MAX = NT_MAX // TPS + (N_EXP * (TPS - 1)) // TPS
ZERO_UNIT = NT_MAX * UPT
SPARE_UNIT0 = ZERO_UNIT + UPT
R_OUT = (NT_MAX + 1) * TMG
R_SORT = R_OUT + -(-2 * (UPB + ZPB) // UPT) * TMG
UNROLL = 8


def _cparams(sem):
    return pltpu.CompilerParams(dimension_semantics=sem, vmem_limit_bytes=VMEM_LIMIT)


def _rms(x, g):
    return x * lax.rsqrt(jnp.mean(x * x, axis=-1, keepdims=True) + EPS) * g


def _dot(a, b):
    return jnp.dot(a, b, preferred_element_type=F32)


def _dot_nt(a, b):
    return lax.dot_general(a, b, (((1,), (1,)), ((), ())), preferred_element_type=F32)


def _silu(x):
    return x / (1.0 + jnp.exp(-x))


def _mod_row(i, tm):
    off = i * tm - T_P
    return jnp.where(off >= 0, 1 + jnp.maximum(off, 0) // N_S, 0)


def _mod_body(c_ref, w_ref, b_ref, o_ref):
    c = c_ref[...]
    o_ref[...] = jnp.dot(_silu(c), w_ref[...], precision=lax.Precision.HIGHEST,
                         preferred_element_type=F32) + b_ref[...]


def _modulation(cvec, w_ada, b_ada):
    tn = 1536
    return pl.pallas_call(
        _mod_body,
        grid=(DEPTH, 6 * D // tn),
        in_specs=[pl.BlockSpec((8, D), lambda l, j: (0, 0)),
                  pl.BlockSpec((None, D, tn), lambda l, j: (l, 0, j)),
                  pl.BlockSpec((None, 1, tn), lambda l, j: (l, 0, j))],
        out_specs=pl.BlockSpec((None, 8, tn), lambda l, j: (l, 0, j)),
        out_shape=jax.ShapeDtypeStruct((DEPTH, 8, 6 * D), F32),
        compiler_params=_cparams(("parallel", "parallel")),
        name="modulation",
    )(cvec, w_ada, b_ada.reshape(DEPTH, 1, 6 * D))


def _rope_tables():
    nf = DA_QK // 4
    t = np.arange(N_S)
    pos = np.stack([t // GRID_W, t % GRID_W], axis=-1).astype(np.float32)
    inv = np.power(np.float32(ROPE_THETA), -np.arange(nf, dtype=np.float32) / nf)
    ang = pos[:, :, None] * inv
    cos = np.cos(ang)
    sin = np.sin(ang)
    cos64 = np.concatenate([cos[:, 0], cos[:, 0], cos[:, 1], cos[:, 1]], axis=-1)
    sin64 = np.concatenate([-sin[:, 0], sin[:, 0], -sin[:, 1], sin[:, 1]], axis=-1)
    reps = DA_W // DA_QK
    cos_t = np.concatenate([np.tile(cos64, (1, reps)), np.ones((TM_IN, DA_W), np.float32)], axis=0)
    sin_t = np.concatenate([np.tile(sin64, (1, reps)), np.zeros((TM_IN, DA_W), np.float32)], axis=0)
    return jnp.asarray(cos_t, F32), jnp.asarray(sin_t, F32)


def _tok_select(i, ctx_ref, lat_ref):
    return jnp.where(i < NT_P, ctx_ref[...], lat_ref[...])


def _in_proj_body(*refs):
    (xp_ref, xs_ref, mod_ref, g_ref, w_ref, cos_ref, sin_ref, pw_ref, ps_ref) = refs[:9]
    (qa_ref, qb_ref, ocp_ref, up_ref, ka_ref, va_ref, kb_ref, vb_ref,
     ck_ref, cv_ref, cnk_ref, cnv_ref) = refs[-12:]
    i = pl.program_id(0)
    x = _tok_select(i, xp_ref, xs_ref)
    h = (_rms(x, g_ref[...]) * (1.0 + mod_ref[1:2, :]) + mod_ref[0:1, :]).astype(BF16)
    cos = cos_ref[...]
    sin = sin_ref[...]
    lane = lax.broadcasted_iota(jnp.int32, (TM_IN, DA_W), 1)
    first = (lane % 32) < 16

    def proj(lo, hi):
        return _dot(h, w_ref[:, lo:hi])

    def rope(t):
        swapped = jnp.where(first, pltpu.roll(t, DA_W - 16, 1), pltpu.roll(t, 16, 1))
        return t * cos + swapped * sin

    o = 0
    qa_ref[...] = (rope(proj(o, o + DA_W)) * (DA_QK ** -0.5)).astype(BF16)
    o += DA_W
    ka = rope(proj(o, o + DA_W))
    o += DA_W
    va = proj(o, o + DA_W)
    o += DA_W
    qb_ref[...] = (proj(o, o + NA_W) * (NA_D ** -0.5)).astype(BF16)
    o += NA_W
    kb = proj(o, o + NA_W)
    o += NA_W
    vb = proj(o, o + NA_W)
    o += NA_W
    up = proj(o, o + POOL_W)

    @pl.when(i < NT_P)
    def _():
        for ref, val in ((ck_ref, ka), (cv_ref, va), (cnk_ref, kb), (cnv_ref, vb)):
            ref[...] = val.reshape(ref.shape)
        ocp_ref[...] = _pool_rows(up, N_P, pw_ref[...], ps_ref[...])

    @pl.when(i >= NT_P)
    def _():
        for ref, val in ((ka_ref, ka), (va_ref, va), (kb_ref, kb), (vb_ref, vb)):
            ref[...] = val.astype(BF16)
        up_ref[...] = up


def _ctx_tok(w):
    return pl.BlockSpec((TM_IN, w), lambda i: (jnp.minimum(i, NT_P - 1), 0))


def _lat_tok(w):
    return pl.BlockSpec((TM_IN, w), lambda i: (jnp.maximum(i - NT_P, 0), 0))


def _in_proj(li, xp, xs, modv, g_pre, w_in_bf, cos_t, sin_t, pool_bd, pool_scale, caches):
    n_pos_blk = N_S // TM_IN

    def pos_blk(i):
        return jnp.where(i >= NT_P, jnp.maximum(i - NT_P, 0) % n_pos_blk, n_pos_blk)

    def tok(w):
        return pl.BlockSpec((TM_IN, w), lambda i: (i, 0))

    def cache(w):
        return pl.BlockSpec((TM_IN // N_P, None, N_P, w), lambda i: (jnp.minimum(i, NT_P - 1), li, 0, 0))

    widths = (DA_W, DA_W, NA_W, NA_W)
    n_in, n_plain_out = 9, 8
    return pl.pallas_call(
        _in_proj_body,
        grid=(T // TM_IN,),
        in_specs=[_ctx_tok(D), _lat_tok(D),
                  pl.BlockSpec((None, None, 6, D), lambda i: (li, _mod_row(i, TM_IN), 0, 0)),
                  pl.BlockSpec((None, 1, D), lambda i: (li, 0, 0)),
                  pl.BlockSpec((None, D, IN_COLS), lambda i: (li, 0, 0)),
                  pl.BlockSpec((TM_IN, DA_W), lambda i: (pos_blk(i), 0)),
                  pl.BlockSpec((TM_IN, DA_W), lambda i: (pos_blk(i), 0)),
                  pl.BlockSpec((None, POOL_W, POOL_W), lambda i: (li, 0, 0)),
                  pl.BlockSpec((None, 1, POOL_W), lambda i: (li, 0, 0))]
                 + [pl.BlockSpec(memory_space=pl.ANY)] * len(caches),
        out_specs=[tok(DA_W), tok(NA_W), _ctx_tok(POOL_W), _lat_tok(POOL_W)] + [_lat_tok(w) for w in widths]
                  + [cache(w) for w in widths],
        out_shape=[jax.ShapeDtypeStruct((T, DA_W), BF16),
                   jax.ShapeDtypeStruct((T, NA_W), BF16),
                   jax.ShapeDtypeStruct((T_P, POOL_W), BF16),
                   jax.ShapeDtypeStruct((T_S, POOL_W), F32)]
                  + [jax.ShapeDtypeStruct((T_S, w), BF16) for w in widths]
                  + [jax.ShapeDtypeStruct((NB_P, DEPTH, N_P, w), F32) for w in widths],
        input_output_aliases={n_in + k: n_plain_out + k for k in range(len(caches))},
        compiler_params=_cparams(("arbitrary",)),
        name=f"in_proj_l{li}",
    )(xp, xs, modv, g_pre, w_in_bf, cos_t, sin_t, pool_bd, pool_scale, *caches)


def _pool_rows(u, n, w_mix, scale):
    rows = u.shape[0]
    row = lax.broadcasted_iota(jnp.int32, (rows, POOL_W), 0) % n
    lane = lax.broadcasted_iota(jnp.int32, (rows, POOL_W), 1)

    def shift_dn(a, k):
        return jnp.where(row >= k, pltpu.roll(a, k, 0), 0.0)

    def shift_up(a, k):
        return jnp.where(row < n - k, pltpu.roll(a, rows - k, 0), 0.0)

    fwd = u
    bwd = shift_dn(u, 1)
    mean = jnp.zeros_like(u)
    k = 1
    for gi, w in enumerate(POOL_WINDOWS):
        while k < w // 2:
            fwd = fwd + shift_up(fwd, k)
            bwd = bwd + shift_dn(bwd, k)
            k *= 2
        cnt = (jnp.minimum(row + w // 2, n) - jnp.maximum(row - w // 2, 0)).astype(F32)
        in_group = (lane >= gi * POOL_CH) & (lane < (gi + 1) * POOL_CH)
        mean = jnp.where(in_group, (fwd + bwd) / cnt, mean)
    pooled = mean - u
    mixed = jnp.dot(pooled, w_mix, precision=lax.Precision.HIGHEST, preferred_element_type=F32)
    return (mixed * scale).astype(BF16)


def _pool_body(u_ref, w_ref, s_ref, o_ref):
    o_ref[...] = _pool_rows(u_ref[...], N_S, w_ref[...], s_ref[...])


def _pool_lat(li, up, w_bd, scale):
    return pl.pallas_call(
        _pool_body,
        grid=(NB_S,),
        in_specs=[pl.BlockSpec((N_S, POOL_W), lambda i: (i, 0)),
                  pl.BlockSpec((None, POOL_W, POOL_W), lambda i: (li, 0, 0)),
                  pl.BlockSpec((None, 1, POOL_W), lambda i: (li, 0, 0))],
        out_specs=pl.BlockSpec((N_S, POOL_W), lambda i: (i, 0)),
        out_shape=jax.ShapeDtypeStruct((T_S, POOL_W), BF16),
        compiler_params=_cparams(("parallel",)),
        name=f"pool_lat_l{li}",
    )(up, w_bd, scale)


def _lambda(lp, lam_init):
    a = jnp.sum(lp[0:1, :] * lp[1:2, :], axis=1, keepdims=True)
    b = jnp.sum(lp[2:3, :] * lp[3:4, :], axis=1, keepdims=True)
    return jnp.exp(a) - jnp.exp(b) + lam_init


def _softmax_av(scores, vs):
    m = functools.reduce(jnp.maximum, [jnp.max(s, axis=-1, keepdims=True) for s in scores])
    es = [jnp.exp(s - m) for s in scores]
    den = functools.reduce(jnp.add, [jnp.sum(e, axis=-1, keepdims=True) for e in es])
    o = functools.reduce(jnp.add, [_dot(e.astype(BF16), v) for e, v in zip(es, vs)])
    return o * (1.0 / den)


def _diff_head(q, ks, vs, lam, gd, lam_init):
    o1 = _softmax_av([_dot_nt(q[:, :DA_QK], k[:, :DA_QK]) for k in ks], vs)
    o2 = _softmax_av([_dot_nt(q[:, DA_QK:], k[:, DA_QK:]) for k in ks], vs)
    return _rms(o1 - lam * o2, gd) * (1.0 - lam_init)


def _plain_head(q, ks, vs, biases):
    scores = []
    for k, bias in zip(ks, biases):
        s = _dot_nt(q, k)
        scores.append(s if bias is None else s + bias)
    return _softmax_av(scores, vs)


def _attn_ctx_body(lam_ref, gd_ref, qa_ref, ka_ref, va_ref, qb_ref, kb_ref, vb_ref, o_ref, *, lam_init):
    lam = _lambda(lam_ref[...], lam_init)
    gd = gd_ref[...]
    for h in range(DA_H):
        sl = slice(h * DA_V, (h + 1) * DA_V)
        o = _diff_head(qa_ref[:, sl], [ka_ref[:, sl].astype(BF16)], [va_ref[:, sl].astype(BF16)],
                       lam, gd, lam_init)
        o_ref[:, sl] = o.astype(BF16)
    for h in range(NA_H):
        sl = slice(h * NA_D, (h + 1) * NA_D)
        o = _plain_head(qb_ref[:, sl], [kb_ref[:, sl].astype(BF16)], [vb_ref[:, sl].astype(BF16)], [None])
        o_ref[:, DA_W + h * NA_D:DA_W + (h + 1) * NA_D] = o.astype(BF16)


def _attn_ctx(li, lam_p, g_diff, qa, ka, va, qb, kb, vb, lam_init):
    def blk(w):
        return pl.BlockSpec((N_P, w), lambda b: (b, 0))

    def cblk(w):
        return pl.BlockSpec((None, None, N_P, w), lambda b: (b, li, 0, 0))

    return pl.pallas_call(
        functools.partial(_attn_ctx_body, lam_init=lam_init),
        grid=(NB_P,),
        in_specs=[pl.BlockSpec((None, 4, DA_QK), lambda b: (li, 0, 0)),
                  pl.BlockSpec((None, 1, DA_V), lambda b: (li, 0, 0)),
                  blk(DA_W), cblk(DA_W), cblk(DA_W), blk(NA_W), cblk(NA_W), cblk(NA_W)],
        out_specs=pl.BlockSpec((N_P, AB_W), lambda b: (b, 0)),
        out_shape=jax.ShapeDtypeStruct((T_P, AB_W), BF16),
        compiler_params=_cparams(("parallel",)),
        name=f"attn_ctx_l{li}",
    )(lam_p, g_diff, qa, ka, va, qb, kb, vb)


G_ROWS = N_S // GRID_W
QR_PER_BLK = TQ // GRID_W


def _fill_na_bias(j, strip_ref, bias_scr):
    for jj in range(N_S // TQ):
        @pl.when(j == jj)
        def _():
            for rr in range(QR_PER_BLK):
                qr = jj * QR_PER_BLK + rr
                rs = min(max(qr - NA_ROWS // 2, 0), G_ROWS - NA_ROWS)
                lo, hi = rs * GRID_W, (rs + NA_ROWS) * GRID_W
                a0 = (rs - qr + NA_ROWS - 1) * GRID_W
                q = slice(rr * GRID_W, (rr + 1) * GRID_W)
                for h in range(NA_H):
                    if lo > 0:
                        bias_scr[h, q, 0:lo] = jnp.full((GRID_W, lo), NEG, F32)
                    bias_scr[h, q, lo:hi] = strip_ref[h, :, a0:a0 + NA_ROWS * GRID_W]
                    if hi < N_S:
                        bias_scr[h, q, hi:N_S] = jnp.full((GRID_W, N_S - hi), NEG, F32)


def _attn_lat_body(lam_ref, gd_ref, qa_ref, ka_ref, va_ref, ck_ref, cv_ref,
                   qb_ref, kb_ref, vb_ref, cnk_ref, cnv_ref, strip_ref, o_ref, bias_scr, *, lam_init):
    _fill_na_bias(pl.program_id(1), strip_ref, bias_scr)
    lam = _lambda(lam_ref[...], lam_init)
    gd = gd_ref[...]
    for h in range(DA_H):
        sl = slice(h * DA_V, (h + 1) * DA_V)
        ks = [ka_ref[:, sl], ck_ref[:, sl].astype(BF16)]
        vs = [va_ref[:, sl], cv_ref[:, sl].astype(BF16)]
        o_ref[:, sl] = _diff_head(qa_ref[:, sl], ks, vs, lam, gd, lam_init).astype(BF16)
    for h in range(NA_H):
        sl = slice(h * NA_D, (h + 1) * NA_D)
        ks = [kb_ref[:, sl], cnk_ref[:, sl].astype(BF16)]
        vs = [vb_ref[:, sl], cnv_ref[:, sl].astype(BF16)]
        o = _plain_head(qb_ref[:, sl], ks, vs, [bias_scr[h], None])
        o_ref[:, DA_W + h * NA_D:DA_W + (h + 1) * NA_D] = o.astype(BF16)


def _attn_lat(li, lam_p, g_diff, qa, ka, va, ck, cv, qb, kb, vb, cnk, cnv, strip, lam_init):
    nq = N_S // TQ
    q0 = T_P // TQ

    def qblk(w):
        return pl.BlockSpec((TQ, w), lambda b, j: (q0 + b * nq + j, 0))

    def kvblk(w):
        return pl.BlockSpec((N_S, w), lambda b, j: (b, 0))

    def cblk(w):
        return pl.BlockSpec((None, None, PAST, w), lambda b, j: (b, li, 0, 0))

    return pl.pallas_call(
        functools.partial(_attn_lat_body, lam_init=lam_init),
        grid=(NB_S, nq),
        in_specs=[pl.BlockSpec((None, 4, DA_QK), lambda b, j: (li, 0, 0)),
                  pl.BlockSpec((None, 1, DA_V), lambda b, j: (li, 0, 0)),
                  qblk(DA_W), kvblk(DA_W), kvblk(DA_W), cblk(DA_W), cblk(DA_W),
                  qblk(NA_W), kvblk(NA_W), kvblk(NA_W), cblk(NA_W), cblk(NA_W),
                  pl.BlockSpec((None, NA_H, GRID_W, (2 * NA_ROWS - 1) * GRID_W), lambda b, j: (li, 0, 0, 0))],
        out_specs=pl.BlockSpec((TQ, AB_W), lambda b, j: (b * nq + j, 0)),
        out_shape=jax.ShapeDtypeStruct((T_S, AB_W), BF16),
        scratch_shapes=[pltpu.VMEM((NA_H, TQ, N_S), F32)],
        compiler_params=_cparams(("parallel", "parallel")),
        name=f"attn_lat_l{li}",
    )(lam_p, g_diff, qa, ka, va, ck, cv, qb, kb, vb, cnk, cnv, strip)


def _na_bias_strips(table):
    w = np.arange(GRID_W)
    cs = np.clip(w - NA_COLS // 2, 0, GRID_W - NA_COLS)
    col_ok = (w[None, :] >= cs[:, None]) & (w[None, :] < cs[:, None] + NA_COLS)
    sel_c = (np.clip(w[None, :, None] - w[:, None, None] + NA_COLS - 1, 0, 2 * NA_COLS - 2)
             == np.arange(2 * NA_COLS - 1)).astype(np.float32)
    t1 = jnp.einsum('lhab,cdb->lhcad', table.astype(F32), jnp.asarray(sel_c),
                    precision=lax.Precision.HIGHEST)
    t1 = jnp.where(jnp.asarray(col_ok)[None, None, :, None, :], t1, NEG)
    return t1.reshape(DEPTH, NA_H, GRID_W, (2 * NA_ROWS - 1) * GRID_W)


def _route_t(logits_t, b_col, tm):
    per = N_EXP // N_GRP
    s = 1.0 / (1.0 + jnp.exp(-logits_t))
    sel3 = (s + b_col).reshape(N_GRP, per, tm)
    midx = lax.broadcasted_iota(jnp.int32, (N_GRP, per, tm), 1)
    gidx = lax.broadcasted_iota(jnp.int32, (N_GRP, per, tm), 0)
    ninf = -jnp.inf
    m1 = jnp.max(sel3, axis=1, keepdims=True)
    i1 = jnp.min(jnp.where(sel3 == m1, midx, per), axis=1, keepdims=True)
    m2 = jnp.max(jnp.where(midx == i1, ninf, sel3), axis=1, keepdims=True)
    gsc = jnp.broadcast_to(m1 + m2, (N_GRP, per, tm))
    cnt = jnp.zeros((N_GRP, per, tm), jnp.int32)
    for g in range(N_GRP):
        sg = gsc[g][None]
        cnt = cnt + ((sg > gsc) | ((sg == gsc) & (g < gidx))).astype(jnp.int32)
    x = jnp.where(cnt < TOPK_GRP, sel3, ninf).reshape(N_EXP, tm)
    eidx = lax.broadcasted_iota(jnp.int32, (N_EXP, tm), 0)
    chosen = jnp.zeros((N_EXP, tm), jnp.bool_)
    for _ in range(TOP_K):
        m = jnp.max(x, axis=0, keepdims=True)
        first = jnp.min(jnp.where(x == m, eidx, N_EXP), axis=0, keepdims=True)
        hit = eidx == first
        chosen = chosen | hit
        x = jnp.where(hit, ninf, x)
    w = jnp.where(chosen, s, 0.0)
    return chosen, w / jnp.sum(w, axis=0, keepdims=True) * ROUTED_SCALE


def _merge_body(oabp_ref, oabs_ref, ocp_ref, ocs_ref, xp_ref, xs_ref, mod_ref, gpost_ref, gpre_ref,
                wout_ref, wrt_ref, br_ref, tri_ref, y_ref, h_ref, gates_ref, rank_ref, pcu_ref):
    i = pl.program_id(0)
    mix = (_dot(_tok_select(i, oabp_ref, oabs_ref), wout_ref[0:AB_W, :])
           + _dot(_tok_select(i, ocp_ref, ocs_ref), wout_ref[AB_W:D, :]))
    y = _tok_select(i, xp_ref, xs_ref) + mod_ref[2:3, :] * _rms(mix, gpost_ref[...])
    y_ref[...] = y
    h = _rms(y, gpre_ref[...]) * (1.0 + mod_ref[4:5, :]) + mod_ref[3:4, :]
    h_ref[...] = h.astype(BF16)
    logits_t = lax.dot_general(wrt_ref[...], h, (((1,), (1,)), ((), ())),
                               precision=lax.Precision.HIGHEST, preferred_element_type=F32)
    chosen, gates = _route_t(logits_t, br_ref[...], TM_IN)
    gates_ref[...] = gates
    ch = jnp.where(chosen, 1.0, 0.0)
    rank = _dot(ch.astype(BF16), tri_ref[...])
    rank_ref[...] = jnp.where(chosen, rank, -1.0)
    for j in range(TM_IN // B_SORT):
        cnt = jnp.sum(ch[:, j * B_SORT:(j + 1) * B_SORT], axis=1, keepdims=True)
        units = jnp.floor((cnt + (UNIT - 1)) * (1.0 / UNIT))
        pcu_ref[j] = jnp.broadcast_to(units, (N_EXP, LANES))


def _merge(li, oabp, oabs, ocp, ocs, xp, xs, modv, g_post, g_pre, w_out_bf, w_router_t, b_router_col, tri):
    def tok(w):
        return pl.BlockSpec((TM_IN, w), lambda i: (i, 0))

    def par(*shape):
        return pl.BlockSpec((None,) + shape, lambda i: (li,) + (0,) * len(shape))

    tok_t = pl.BlockSpec((N_EXP, TM_IN), lambda i: (0, i))
    nb = TM_IN // B_SORT
    return pl.pallas_call(
        _merge_body,
        grid=(T // TM_IN,),
        in_specs=[_ctx_tok(AB_W), _lat_tok(AB_W), _ctx_tok(POOL_W), _lat_tok(POOL_W), _ctx_tok(D), _lat_tok(D),
                  pl.BlockSpec((None, None, 6, D), lambda i: (li, _mod_row(i, TM_IN), 0, 0)),
                  par(1, D), par(1, D), par(D, D), par(N_EXP, D), par(N_EXP, 1),
                  pl.BlockSpec((TM_IN, TM_IN), lambda i: (0, 0))],
        out_specs=[tok(D), tok(D), tok_t, tok_t,
                   pl.BlockSpec((nb, N_EXP, LANES), lambda i: (i, 0, 0))],
        out_shape=[jax.ShapeDtypeStruct((T, D), F32),
                   jax.ShapeDtypeStruct((T, D), BF16),
                   jax.ShapeDtypeStruct((N_EXP, T), F32),
                   jax.ShapeDtypeStruct((N_EXP, T), F32),
                   jax.ShapeDtypeStruct((N_BLK, N_EXP, LANES), F32)],
        compiler_params=_cparams(("parallel",)),
        name=f"merge_l{li}",
    )(oabp, oabs, ocp, ocs, xp, xs, modv, g_post, g_pre, w_out_bf, w_router_t, b_router_col, tri)


def _rank_matrix():
    t = np.arange(TM_IN)
    m = (t[:, None] < t[None, :]) & (t[:, None] // B_SORT == t[None, :] // B_SORT)
    return jnp.asarray(m, BF16)


def _moe_tables(pcu_f):
    pcu = pcu_f[:, :, 0].astype(jnp.int32)
    incl = jnp.cumsum(pcu, axis=1)
    uo = incl - pcu
    tot = jnp.sum(pcu, axis=0)
    gt = (tot + UPT - 1) // UPT
    gstart_t = jnp.cumsum(gt) - gt
    gstart_u = gstart_t * UPT
    steps = (gt + TPS - 1) // TPS
    sincl = jnp.cumsum(steps)
    sstart = sincl - steps
    bstart = gstart_u[None, :] + jnp.cumsum(pcu, axis=0) - pcu
    u = jnp.arange(UPB, dtype=jnp.int32)
    eou = jnp.sum((u[None, :, None] >= incl[:, None, :]).astype(jnp.int32), axis=-1)
    valid = eou < N_EXP
    onehot = eou[:, :, None] == jnp.arange(N_EXP, dtype=jnp.int32)[None, None, :]
    run_u0 = jnp.sum(jnp.where(onehot, uo[:, None, :], 0), axis=-1)
    run_g0 = jnp.sum(jnp.where(onehot, bstart[:, None, :], 0), axis=-1)
    dst = run_g0 + u[None, :] - run_u0
    spare0 = SPARE_UNIT0 + (jnp.arange(N_BLK, dtype=jnp.int32) % 2)[:, None] * (UPB + ZPB)
    dst_sort = jnp.where(valid, dst, spare0 + u[None, :])
    dst_comb = jnp.where(valid, dst, ZERO_UNIT)
    loc = jnp.where(valid, (u[None, :] - run_u0) * UNIT, -(1 << 20))
    k = jnp.arange(UPT - 1, dtype=jnp.int32)
    ztail = gt * UPT - tot
    zdst = jnp.where(k[None, :] < ztail[:, None], (gstart_u + tot)[:, None] + k[None, :], -1).reshape(N_BLK, ZPB)
    zdst = jnp.where(zdst >= 0, zdst, spare0 + UPB + jnp.arange(ZPB, dtype=jnp.int32)[None, :])
    step = jnp.arange(NS_MAX, dtype=jnp.int32)
    te = jnp.minimum(jnp.sum((step[:, None] >= sincl[None, :]).astype(jnp.int32), axis=-1), N_EXP - 1)
    mine = te[:, None] == jnp.arange(N_EXP, dtype=jnp.int32)[None, :]
    pick = lambda v: jnp.sum(jnp.where(mine, v[None, :], 0), axis=-1)
    first = gt - TPS * (steps - 1)
    j = step - pick(sstart)
    ntile = jnp.where(j == 0, pick(first), TPS)
    tile0 = pick(gstart_t) + jnp.where(j == 0, 0, pick(first) + TPS * (j - 1))
    ns = sincl[-1]
    wslot = (jnp.cumsum((j == 0).astype(jnp.int32)) - 1) % 2
    nxt = pick(sincl)
    nexte = jnp.sum(jnp.where(step[None, :] == nxt[:, None], te[None, :], 0), axis=-1)
    nexte = jnp.where(j == 0, jnp.where(nxt < ns, nexte, -1), -2)
    flat = lambda a: a.reshape(-1)
    return (flat(jnp.minimum(eou, N_EXP - 1)), flat(loc), flat(dst_sort), flat(dst_comb), flat(zdst),
            incl[:, -1], te, ntile, tile0, wslot, nexte, sincl[-1:])


def _unit_rows(u):
    if isinstance(u, int):
        return pl.ds(u * UNIT, UNIT)
    return pl.ds(pl.multiple_of(u * UNIT, UNIT), UNIT)


def _for_live_groups(n_units, fn, groups=range(UPB // UPT)):
    for g in groups:
        if (g + 1) * UPT <= MIN_UNITS:
            fn(g)
        else:
            @pl.when(g * UPT < n_units)
            def _(g=g):
                fn(g)


def _sort_body(eou_ref, loc_ref, dst_ref, zdst_ref, nun_ref, h_ref, rank_ref, gates_ref, xs_hbm, p_ref,
               xs_scr, zero_scr, sem):
    b = pl.program_id(0)
    slot = b % 2

    def copy(src, d, s):
        return pltpu.make_async_copy(src, xs_hbm.at[_unit_rows(d)], sem.at[s])

    def wait_units(n, s):
        lax.fori_loop(0, n, lambda k, c: (copy(zero_scr, 0, s).wait(), c)[1], 0, unroll=True)

    def wait_step(blk, s):
        _for_live_groups(nun_ref[blk], lambda g: wait_units(UPT, s))
        wait_units(ZPB, s)

    @pl.when(b == 0)
    def _():
        zero_scr[...] = jnp.zeros((UNIT, XS_W), BF16)

    @pl.when(b >= 2)
    def _():
        wait_step(b - 2, slot)

    base = lax.broadcasted_iota(jnp.int32, (UNIT, B_SORT), 0)
    g = gates_ref[...]
    g_hi = g.astype(BF16)
    g_hl = jnp.concatenate([g_hi, (g - g_hi.astype(F32)).astype(BF16)], axis=0)
    h = h_ref[...]
    def start_group(g):
        for u in range(g * UPT, (g + 1) * UPT):
            copy(xs_scr.at[slot, _unit_rows(u)], dst_ref[b * UPB + u], slot).start()

    upc = SORT_CHUNK // UNIT
    for c in range(RB // SORT_CHUNK):
        for u in range(c * upc, (c + 1) * upc):
            match = rank_ref[pl.ds(eou_ref[b * UPB + u], 1), :] == (base + loc_ref[b * UPB + u]).astype(F32)
            p_ref[_unit_rows(u), :] = jnp.where(match, 1.0, 0.0).astype(BF16)
        rows = slice(c * SORT_CHUNK, (c + 1) * SORT_CHUNK)
        xs_scr[slot, rows, 0:D] = _dot(p_ref[rows, :], h).astype(BF16)
        xs_scr[slot, rows, D:XS_W] = _dot_nt(p_ref[rows, :], g_hl).astype(BF16)
        _for_live_groups(nun_ref[b], start_group, range(c * upc // UPT, (c + 1) * upc // UPT))
    for j in range(ZPB):
        copy(zero_scr, zdst_ref[b * ZPB + j], slot).start()

    @pl.when(b == N_BLK - 1)
    def _():
        wait_step(b - 1, 1 - slot)
        wait_step(b, slot)


def _moe_sort(li, eou, loc, dst, zdst, nun, h, rank_t, gates_t):
    blk_t = pl.BlockSpec((N_EXP, B_SORT), lambda b, *_: (0, b))
    return pl.pallas_call(
        _sort_body,
        grid_spec=pltpu.PrefetchScalarGridSpec(
            num_scalar_prefetch=5,
            grid=(N_BLK,),
            in_specs=[pl.BlockSpec((B_SORT, D), lambda b, *_: (b, 0)), blk_t, blk_t],
            out_specs=[pl.BlockSpec(memory_space=pl.ANY),
                       pl.BlockSpec((RB, B_SORT), lambda b, *_: (b, 0))],
            scratch_shapes=[pltpu.VMEM((2, RB, XS_W), BF16),
                            pltpu.VMEM((UNIT, XS_W), BF16),
                            pltpu.SemaphoreType.DMA((2,))]),
        out_shape=[jax.ShapeDtypeStruct((R_SORT, XS_W), BF16),
                   jax.ShapeDtypeStruct((N_BLK * RB, B_SORT), BF16)],
        compiler_params=_cparams(("arbitrary",)),
        name=f"moe_sort_l{li}",
    )(eou, loc, dst, zdst, nun, h, rank_t, gates_t)


def _experts_body(te_ref, ntile_ref, tile0_ref, wslot_ref, nexte_ref, ns_ref,
                  xs_hbm, wg_hbm, wu_hbm, wd_hbm, os_hbm,
                  xs_buf, o_buf, wg_f, wu_f, wd_f, sem_in, sem_out, sem_w, *, li):
    ns = ns_ref[0]

    def tiles_of(step):
        return jnp.where((step >= 0) & (step < ns), ntile_ref[jnp.clip(step, 0, NS_MAX - 1)], 0)

    def tile_rows(step, t):
        return pl.ds(pl.multiple_of((tile0_ref[jnp.clip(step, 0, NS_MAX - 1)] + t) * TMG, TMG), TMG)

    def buf_rows(t):
        return pl.ds(pl.multiple_of(t * TMG, TMG), TMG)

    def copy_in(step, t, s):
        return pltpu.make_async_copy(xs_hbm.at[tile_rows(step, t)], xs_buf.at[s, buf_rows(t)], sem_in.at[s])

    def copy_out(step, t, s):
        return pltpu.make_async_copy(o_buf.at[s, buf_rows(t)], os_hbm.at[tile_rows(step, t)], sem_out.at[s])

    def for_tiles(step, fn):
        lax.fori_loop(0, tiles_of(step), lambda t, c: (fn(t), c)[1], 0)

    def weight_copies(e, s):
        return [pltpu.make_async_copy(hbm.at[li, e], buf.at[s], sem_w.at[s])
                for hbm, buf in ((wg_hbm, wg_f), (wu_hbm, wu_f), (wd_hbm, wd_f))]

    for c in weight_copies(te_ref[0], 0):
        c.start()
    for_tiles(0, lambda t: copy_in(0, t, 0).start())

    def step(i, carry):
        slot = i % 2
        for_tiles(i + 1, lambda t: copy_in(i + 1, t, 1 - slot).start())
        for_tiles(i - 2, lambda t: copy_out(i - 2, t, slot).wait())
        e = te_ref[i]
        ws = wslot_ref[i]

        @pl.when(nexte_ref[i] >= -1)
        def _():
            for c in weight_copies(e, ws):
                c.wait()

            @pl.when(nexte_ref[i] >= 0)
            def _():
                for c in weight_copies(nexte_ref[i], 1 - ws):
                    c.start()

        ntile = ntile_ref[i]
        for_tiles(i, lambda t: copy_in(i, t, slot).wait())
        for k in range(1, TPS + 1):
            rows = k * TMG

            @pl.when(ntile == k)
            def _():
                x = xs_buf[slot, 0:rows, 0:D]
                lane = lax.broadcasted_iota(jnp.int32, (rows, LANES), 1)
                mine = (lane == e) | (lane == e + N_EXP)
                gate = jnp.sum(jnp.where(mine, xs_buf[slot, 0:rows, D:XS_W].astype(F32), 0.0),
                               axis=1, keepdims=True)
                hid = (_silu(_dot(x, wg_f[ws].astype(BF16))) * _dot(x, wu_f[ws].astype(BF16))) * gate
                o_buf[slot, 0:rows, :] = _dot(hid.astype(BF16), wd_f[ws].astype(BF16)).astype(BF16)

        for_tiles(i, lambda t: copy_out(i, t, slot).start())
        return carry

    lax.fori_loop(0, ns, step, 0)
    for_tiles(ns - 2, lambda t: copy_out(ns - 2, t, ns % 2).wait())
    for_tiles(ns - 1, lambda t: copy_out(ns - 1, t, (ns - 1) % 2).wait())
    o_buf[0, 0:UNIT, :] = jnp.zeros((UNIT, D), BF16)
    zero = pltpu.make_async_copy(o_buf.at[0, 0:UNIT], os_hbm.at[_unit_rows(ZERO_UNIT)], sem_out.at[0])
    zero.start()
    zero.wait()


def _moe_experts(li, te, ntile, tile0, wslot, nexte, ns, xs, w_gate, w_up, w_down):
    return pl.pallas_call(
        functools.partial(_experts_body, li=li),
        grid_spec=pltpu.PrefetchScalarGridSpec(
            num_scalar_prefetch=6,
            grid=(1,),
            in_specs=[pl.BlockSpec(memory_space=pl.ANY)] * 4,
            out_specs=pl.BlockSpec(memory_space=pl.ANY),
            scratch_shapes=[pltpu.VMEM((2, STEP_ROWS, XS_W), BF16),
                            pltpu.VMEM((2, STEP_ROWS, D), BF16),
                            pltpu.VMEM((2, D, EXP_DIM), F32),
                            pltpu.VMEM((2, D, EXP_DIM), F32),
                            pltpu.VMEM((2, EXP_DIM, D), F32),
                            pltpu.SemaphoreType.DMA((2,)),
                            pltpu.SemaphoreType.DMA((2,)),
                            pltpu.SemaphoreType.DMA((2,))]),
        out_shape=jax.ShapeDtypeStruct((R_OUT, D), BF16),
        compiler_params=_cparams(("arbitrary",)),
        name=f"moe_experts_l{li}",
    )(te, ntile, tile0, wslot, nexte, ns, xs, w_gate, w_up, w_down)


def _combine_body(dst_ref, nun_ref, h_ref, p_ref, y_ref, mod_ref, gpost_ref,
                  wsg_ref, wsu_ref, wsd_ref, os_hbm, op_ref, ol_ref, os_scr, sem):
    b = pl.program_id(0)
    slot = b % 2

    def copy(d, u, s):
        return pltpu.make_async_copy(os_hbm.at[_unit_rows(d)], os_scr.at[s, _unit_rows(u)], sem.at[s])

    def fetch(blk, s):
        def group(g):
            lax.fori_loop(g * UPT, (g + 1) * UPT,
                          lambda u, c: (copy(dst_ref[blk * UPB + u], u, s).start(), c)[1], 0, unroll=UNROLL)

        _for_live_groups(nun_ref[blk], group)

    @pl.when(b == 0)
    def _():
        os_scr[:, MIN_UNITS * UNIT:RB, :] = jnp.zeros((2, RB - MIN_UNITS * UNIT, D), BF16)
        fetch(0, 0)

    @pl.when(b + 1 < N_BLK)
    def _():
        fetch(b + 1, 1 - slot)

    h = h_ref[...]
    shared = _dot((_silu(_dot(h, wsg_ref[...])) * _dot(h, wsu_ref[...])).astype(BF16), wsd_ref[...])
    _for_live_groups(nun_ref[b], lambda g: lax.fori_loop(
        0, UPT, lambda k, c: (copy(0, 0, slot).wait(), c)[1], 0, unroll=True))
    routed = lax.dot_general(os_scr[slot], p_ref[...], (((0,), (0,)), ((), ())),
                             preferred_element_type=F32).T
    out = y_ref[...] + mod_ref[5:6, :] * _rms(routed + shared, gpost_ref[...])

    @pl.when(b < T_P // B_SORT)
    def _():
        op_ref[...] = out

    @pl.when(b >= T_P // B_SORT)
    def _():
        ol_ref[...] = out


def _moe_combine(li, dst, nun, h, p, y, modv, g_post, wsg_bf, wsu_bf, wsd_bf, os):
    nbp = T_P // B_SORT
    def tok(w):
        return pl.BlockSpec((B_SORT, w), lambda b, *_: (b, 0))

    def par(*shape):
        return pl.BlockSpec((None,) + shape, lambda b, *_: (li,) + (0,) * len(shape))

    return pl.pallas_call(
        _combine_body,
        grid_spec=pltpu.PrefetchScalarGridSpec(
            num_scalar_prefetch=2,
            grid=(N_BLK,),
            in_specs=[tok(D), pl.BlockSpec((RB, B_SORT), lambda b, *_: (b, 0)), tok(D),
                      pl.BlockSpec((None, None, 6, D), lambda b, *_: (li, _mod_row(b, B_SORT), 0, 0)),
                      par(1, D), par(D, EXP_DIM), par(D, EXP_DIM), par(EXP_DIM, D),
                      pl.BlockSpec(memory_space=pl.ANY)],
            out_specs=[pl.BlockSpec((B_SORT, D), lambda b, *_: (jnp.minimum(b, nbp - 1), 0)),
                       pl.BlockSpec((B_SORT, D), lambda b, *_: (jnp.maximum(b - nbp, 0), 0))],
            scratch_shapes=[pltpu.VMEM((2, RB, D), BF16),
                            pltpu.SemaphoreType.DMA((2,))]),
        out_shape=[jax.ShapeDtypeStruct((T_P, D), F32), jax.ShapeDtypeStruct((T_S, D), F32)],
        compiler_params=_cparams(("arbitrary",)),
        name=f"moe_combine_l{li}",
    )(dst, nun, h, p, y, modv, g_post, wsg_bf, wsu_bf, wsd_bf, os)


def _block_diag(w):
    out = jnp.zeros((DEPTH, POOL_W, POOL_W), F32)
    for g in range(len(POOL_WINDOWS)):
        out = out.at[:, g * POOL_CH:(g + 1) * POOL_CH, g * POOL_CH:(g + 1) * POOL_CH].set(w[:, g])
    return out


def kernel(x_prompt, x_sample, cache_diff_k, cache_diff_v, cache_na_k, cache_na_v, c, c_ctx,
           w_ada, b_ada, g_pre_mix, g_post_mix, g_pre_ffn, g_post_ffn, w_in, w_out,
           diff_lambda, g_diff, na_bias, pool_w, pool_scale, w_router, b_router,
           w_gate, w_up, w_down, ws_gate, ws_up, ws_down):
    xp, xs = x_prompt.reshape(T_P, D), x_sample.reshape(T_S, D)
    cvec = jnp.concatenate([c_ctx[None, :], c, jnp.zeros((3, D), F32)], axis=0)
    modv = _modulation(cvec, w_ada, b_ada)[:, :1 + NB_S].reshape(DEPTH, 1 + NB_S, 6, D)

    cos_t, sin_t = _rope_tables()
    w_in_bf = w_in.astype(BF16)
    w_out_bf = w_out.astype(BF16)
    wsg_bf, wsu_bf, wsd_bf = ws_gate.astype(BF16), ws_up.astype(BF16), ws_down.astype(BF16)
    pool_bd = _block_diag(pool_w)
    row = lambda a: a.reshape(DEPTH, 1, a.shape[-1])
    g_pre_mix, g_post_mix, g_pre_ffn, g_post_ffn = map(row, (g_pre_mix, g_post_mix, g_pre_ffn, g_post_ffn))
    g_diff, pool_scale = row(g_diff), row(pool_scale)
    w_router_t = jnp.swapaxes(w_router, 1, 2)
    b_router_col = b_router.reshape(DEPTH, N_EXP, 1)
    tri = _rank_matrix()
    ck = cache_diff_k.reshape(NB_S, DEPTH, PAST, DA_W)
    cv = cache_diff_v.reshape(NB_S, DEPTH, PAST, DA_W)
    cnk = cache_na_k.reshape(NB_S, DEPTH, PAST, NA_W)
    cnv = cache_na_v.reshape(NB_S, DEPTH, PAST, NA_W)

    strips = _na_bias_strips(na_bias)

    caches = ()
    for li in range(DEPTH):
        lam_init = 0.8 - 0.6 * math.exp(-0.3 * li)
        qa, qb, ocp, up, ka, va, kb, vb, *caches = _in_proj(li, xp, xs, modv, g_pre_mix, w_in_bf, cos_t, sin_t,
                                                           pool_bd, pool_scale, caches)
        ocs = _pool_lat(li, up, pool_bd, pool_scale)
        oabp = _attn_ctx(li, diff_lambda, g_diff, qa, caches[0], caches[1], qb, caches[2], caches[3], lam_init)
        oabs = _attn_lat(li, diff_lambda, g_diff, qa, ka, va, ck, cv, qb, kb, vb, cnk, cnv, strips, lam_init)
        y, h, gates_t, rank_t, pcu = _merge(li, oabp, oabs, ocp, ocs, xp, xs, modv, g_post_mix, g_pre_ffn,
                                            w_out_bf, w_router_t, b_router_col, tri)
        eou, loc, dst_sort, dst_comb, zdst, nun, te, ntile, tile0, wslot, nexte, ns = _moe_tables(pcu)
        srt, p = _moe_sort(li, eou, loc, dst_sort, zdst, nun, h, rank_t, gates_t)
        os = _moe_experts(li, te, ntile, tile0, wslot, nexte, ns, srt, w_gate, w_up, w_down)
        xp, xs = _moe_combine(li, dst_comb, nun, h, p, y, modv, g_post_ffn, wsg_bf, wsu_bf, wsd_bf, os)

    new_dk, new_dv, new_nk, new_nv = caches
    return (xp.reshape(NB_P, N_P, D), xs.reshape(NB_S, N_S, D),
            new_dk.reshape(NB_P, DEPTH, N_P, DA_H, 2 * DA_QK), new_dv.reshape(NB_P, DEPTH, N_P, DA_H, DA_V),
            new_nk.reshape(NB_P, DEPTH, N_P, NA_H, NA_D), new_nv.reshape(NB_P, DEPTH, N_P, NA_H, NA_D))
```

```python
import functools
import math

import numpy as np
import jax
import jax.numpy as jnp
from jax import lax
from jax.experimental import pallas as pl
from jax.experimental.pallas import tpu as pltpu

F32 = jnp.float32
BF16 = jnp.bfloat16

D = 1024
DEPTH = 2
NB_P, N_P = 16, 256
NB_S, N_S = 4, 1024
T_P = NB_P * N_P
T_S = NB_S * N_S
T = T_P + T_S
PAST = 512
GRID_W = 64
DA_H, DA_QK, DA_V = 4, 64, 128
DA_W = DA_H * DA_V
NA_H, NA_D = 4, 64
NA_W = NA_H * NA_D
NA_ROWS, NA_COLS = 8, 16
POOL_WINDOWS = (2, 4, 8, 16)
POOL_CH = 64
POOL_W = 256
IN_COLS = 3 * DA_W + 3 * NA_W + POOL_W
AB_W = DA_W + NA_W
N_EXP, TOP_K, N_GRP, TOPK_GRP = 64, 8, 8, 4
EXP_DIM = 256
ROUTED_SCALE = 2.5
EPS = 1e-6
NEG = -1e30
ROPE_THETA = 10000.0

LANES = 128
VMEM_LIMIT = 56 * 1024 * 1024

TM_IN = 512
NT_P = T_P // TM_IN
TQ = 512

B_SORT = 256
UNIT = 16
N_BLK = T // B_SORT
SORT_CHUNK = 1024
RUN_PAD_ROWS = TOP_K * B_SORT + N_EXP * (UNIT - 1)
RB = -(-RUN_PAD_ROWS // SORT_CHUNK) * SORT_CHUNK
UPB = RB // UNIT
MIN_UNITS = TOP_K * B_SORT // UNIT
TMG = 256
UPT = TMG // UNIT
NT_MAX = (N_BLK * RUN_PAD_ROWS) // TMG + N_EXP
XS_W = D + LANES
EXP_PER_BLK = N_EXP // N_BLK
ZPB = EXP_PER_BLK * (UPT - 1)
TPS = 6
STEP_ROWS = TPS * TMG
NS_MAX = NT_MAX // TPS + (N_EXP * (TPS - 1)) // TPS
ZERO_UNIT = NT_MAX * UPT
SPARE_UNIT0 = ZERO_UNIT + UPT
R_OUT = (NT_MAX + 1) * TMG
R_SORT = R_OUT + -(-2 * (UPB + ZPB) // UPT) * TMG
UNROLL = 8


def _cparams(sem):
    return pltpu.CompilerParams(dimension_semantics=sem, vmem_limit_bytes=VMEM_LIMIT)


def _rms(x, g):
    return x * lax.rsqrt(jnp.mean(x * x, axis=-1, keepdims=True) + EPS) * g


def _dot(a, b):
    return jnp.dot(a, b, preferred_element_type=F32)


def _dot_nt(a, b):
    return lax.dot_general(a, b, (((1,), (1,)), ((), ())), preferred_element_type=F32)


def _silu(x):
    return x / (1.0 + jnp.exp(-x))


def _mod_row(i, tm):
    off = i * tm - T_P
    return jnp.where(off >= 0, 1 + jnp.maximum(off, 0) // N_S, 0)


def _mod_body(c_ref, w_ref, b_ref, o_ref):
    c = c_ref[...]
    o_ref[...] = jnp.dot(_silu(c), w_ref[...], precision=lax.Precision.HIGHEST,
                         preferred_element_type=F32) + b_ref[...]


def _modulation(cvec, w_ada, b_ada):
    tn = 1536
    return pl.pallas_call(
        _mod_body,
        grid=(DEPTH, 6 * D // tn),
        in_specs=[pl.BlockSpec((8, D), lambda l, j: (0, 0)),
                  pl.BlockSpec((None, D, tn), lambda l, j: (l, 0, j)),
                  pl.BlockSpec((None, 1, tn), lambda l, j: (l, 0, j))],
        out_specs=pl.BlockSpec((None, 8, tn), lambda l, j: (l, 0, j)),
        out_shape=jax.ShapeDtypeStruct((DEPTH, 8, 6 * D), F32),
        compiler_params=_cparams(("parallel", "parallel")),
        name="modulation",
    )(cvec, w_ada, b_ada.reshape(DEPTH, 1, 6 * D))


def _rope_tables():
    nf = DA_QK // 4
    t = np.arange(N_S)
    pos = np.stack([t // GRID_W, t % GRID_W], axis=-1).astype(np.float32)
    inv = np.power(np.float32(ROPE_THETA), -np.arange(nf, dtype=np.float32) / nf)
    ang = pos[:, :, None] * inv
    cos = np.cos(ang)
    sin = np.sin(ang)
    cos64 = np.concatenate([cos[:, 0], cos[:, 0], cos[:, 1], cos[:, 1]], axis=-1)
    sin64 = np.concatenate([-sin[:, 0], sin[:, 0], -sin[:, 1], sin[:, 1]], axis=-1)
    reps = DA_W // DA_QK
    cos_t = np.concatenate([np.tile(cos64, (1, reps)), np.ones((TM_IN, DA_W), np.float32)], axis=0)
    sin_t = np.concatenate([np.tile(sin64, (1, reps)), np.zeros((TM_IN, DA_W), np.float32)], axis=0)
    return jnp.asarray(cos_t, F32), jnp.asarray(sin_t, F32)


def _tok_select(i, ctx_ref, lat_ref):
    return jnp.where(i < NT_P, ctx_ref[...], lat_ref[...])


def _in_proj_body(*refs):
    (xp_ref, xs_ref, mod_ref, g_ref, w_ref, cos_ref, sin_ref, pw_ref, ps_ref) = refs[:9]
    (qa_ref, qb_ref, ocp_ref, up_ref, ka_ref, va_ref, kb_ref, vb_ref,
     ck_ref, cv_ref, cnk_ref, cnv_ref) = refs[-12:]
    i = pl.program_id(0)
    x = _tok_select(i, xp_ref, xs_ref)
    h = (_rms(x, g_ref[...]) * (1.0 + mod_ref[1:2, :]) + mod_ref[0:1, :]).astype(BF16)
    cos = cos_ref[...]
    sin = sin_ref[...]
    lane = lax.broadcasted_iota(jnp.int32, (TM_IN, DA_W), 1)
    first = (lane % 32) < 16

    def proj(lo, hi):
        return _dot(h, w_ref[:, lo:hi])

    def rope(t):
        swapped = jnp.where(first, pltpu.roll(t, DA_W - 16, 1), pltpu.roll(t, 16, 1))
        return t * cos + swapped * sin

    o = 0
    qa_ref[...] = (rope(proj(o, o + DA_W)) * (DA_QK ** -0.5)).astype(BF16)
    o += DA_W
    ka = rope(proj(o, o + DA_W))
    o += DA_W
    va = proj(o, o + DA_W)
    o += DA_W
    qb_ref[...] = (proj(o, o + NA_W) * (NA_D ** -0.5)).astype(BF16)
    o += NA_W
    kb = proj(o, o + NA_W)
    o += NA_W
    vb = proj(o, o + NA_W)
    o += NA_W
    up = proj(o, o + POOL_W)

    @pl.when(i < NT_P)
    def _():
        for ref, val in ((ck_ref, ka), (cv_ref, va), (cnk_ref, kb), (cnv_ref, vb)):
            ref[...] = val.reshape(ref.shape)
        ocp_ref[...] = _pool_rows(up, N_P, pw_ref[...], ps_ref[...])

    @pl.when(i >= NT_P)
    def _():
        for ref, val in ((ka_ref, ka), (va_ref, va), (kb_ref, kb), (vb_ref, vb)):
            ref[...] = val.astype(BF16)
        up_ref[...] = up


def _ctx_tok(w):
    return pl.BlockSpec((TM_IN, w), lambda i: (jnp.minimum(i, NT_P - 1), 0))


def _lat_tok(w):
    return pl.BlockSpec((TM_IN, w), lambda i: (jnp.maximum(i - NT_P, 0), 0))


def _in_proj(li, xp, xs, modv, g_pre, w_in_bf, cos_t, sin_t, pool_bd, pool_scale, caches):
    n_pos_blk = N_S // TM_IN

    def pos_blk(i):
        return jnp.where(i >= NT_P, jnp.maximum(i - NT_P, 0) % n_pos_blk, n_pos_blk)

    def tok(w):
        return pl.BlockSpec((TM_IN, w), lambda i: (i, 0))

    def cache(w):
        return pl.BlockSpec((TM_IN // N_P, None, N_P, w), lambda i: (jnp.minimum(i, NT_P - 1), li, 0, 0))

    widths = (DA_W, DA_W, NA_W, NA_W)
    n_in, n_plain_out = 9, 8
    return pl.pallas_call(
        _in_proj_body,
        grid=(T // TM_IN,),
        in_specs=[_ctx_tok(D), _lat_tok(D),
                  pl.BlockSpec((None, None, 6, D), lambda i: (li, _mod_row(i, TM_IN), 0, 0)),
                  pl.BlockSpec((None, 1, D), lambda i: (li, 0, 0)),
                  pl.BlockSpec((None, D, IN_COLS), lambda i: (li, 0, 0)),
                  pl.BlockSpec((TM_IN, DA_W), lambda i: (pos_blk(i), 0)),
                  pl.BlockSpec((TM_IN, DA_W), lambda i: (pos_blk(i), 0)),
                  pl.BlockSpec((None, POOL_W, POOL_W), lambda i: (li, 0, 0)),
                  pl.BlockSpec((None, 1, POOL_W), lambda i: (li, 0, 0))]
                 + [pl.BlockSpec(memory_space=pl.ANY)] * len(caches),
        out_specs=[tok(DA_W), tok(NA_W), _ctx_tok(POOL_W), _lat_tok(POOL_W)] + [_lat_tok(w) for w in widths]
                  + [cache(w) for w in widths],
        out_shape=[jax.ShapeDtypeStruct((T, DA_W), BF16),
                   jax.ShapeDtypeStruct((T, NA_W), BF16),
                   jax.ShapeDtypeStruct((T_P, POOL_W), BF16),
                   jax.ShapeDtypeStruct((T_S, POOL_W), F32)]
                  + [jax.ShapeDtypeStruct((T_S, w), BF16) for w in widths]
                  + [jax.ShapeDtypeStruct((NB_P, DEPTH, N_P, w), F32) for w in widths],
        input_output_aliases={n_in + k: n_plain_out + k for k in range(len(caches))},
        compiler_params=_cparams(("arbitrary",)),
        name=f"in_proj_l{li}",
    )(xp, xs, modv, g_pre, w_in_bf, cos_t, sin_t, pool_bd, pool_scale, *caches)


def _pool_rows(u, n, w_mix, scale):
    rows = u.shape[0]
    row = lax.broadcasted_iota(jnp.int32, (rows, POOL_W), 0) % n
    lane = lax.broadcasted_iota(jnp.int32, (rows, POOL_W), 1)

    def shift_dn(a, k):
        return jnp.where(row >= k, pltpu.roll(a, k, 0), 0.0)

    def shift_up(a, k):
        return jnp.where(row < n - k, pltpu.roll(a, rows - k, 0), 0.0)

    fwd = u
    bwd = shift_dn(u, 1)
    mean = jnp.zeros_like(u)
    k = 1
    for gi, w in enumerate(POOL_WINDOWS):
        while k < w // 2:
            fwd = fwd + shift_up(fwd, k)
            bwd = bwd + shift_dn(bwd, k)
            k *= 2
        cnt = (jnp.minimum(row + w // 2, n) - jnp.maximum(row - w // 2, 0)).astype(F32)
        in_group = (lane >= gi * POOL_CH) & (lane < (gi + 1) * POOL_CH)
        mean = jnp.where(in_group, (fwd + bwd) / cnt, mean)
    pooled = mean - u
    mixed = jnp.dot(pooled, w_mix, precision=lax.Precision.HIGHEST, preferred_element_type=F32)
    return (mixed * scale).astype(BF16)


def _pool_body(u_ref, w_ref, s_ref, o_ref):
    o_ref[...] = _pool_rows(u_ref[...], N_S, w_ref[...], s_ref[...])


def _pool_lat(li, up, w_bd, scale):
    return pl.pallas_call(
        _pool_body,
        grid=(NB_S,),
        in_specs=[pl.BlockSpec((N_S, POOL_W), lambda i: (i, 0)),
                  pl.BlockSpec((None, POOL_W, POOL_W), lambda i: (li, 0, 0)),
                  pl.BlockSpec((None, 1, POOL_W), lambda i: (li, 0, 0))],
        out_specs=pl.BlockSpec((N_S, POOL_W), lambda i: (i, 0)),
        out_shape=jax.ShapeDtypeStruct((T_S, POOL_W), BF16),
        compiler_params=_cparams(("parallel",)),
        name=f"pool_lat_l{li}",
    )(up, w_bd, scale)


def _lambda(lp, lam_init):
    a = jnp.sum(lp[0:1, :] * lp[1:2, :], axis=1, keepdims=True)
    b = jnp.sum(lp[2:3, :] * lp[3:4, :], axis=1, keepdims=True)
    return jnp.exp(a) - jnp.exp(b) + lam_init


def _softmax_av(scores, vs):
    m = functools.reduce(jnp.maximum, [jnp.max(s, axis=-1, keepdims=True) for s in scores])
    es = [jnp.exp(s - m) for s in scores]
    den = functools.reduce(jnp.add, [jnp.sum(e, axis=-1, keepdims=True) for e in es])
    o = functools.reduce(jnp.add, [_dot(e.astype(BF16), v) for e, v in zip(es, vs)])
    return o * (1.0 / den)


def _diff_head(q, ks, vs, lam, gd, lam_init):
    o1 = _softmax_av([_dot_nt(q[:, :DA_QK], k[:, :DA_QK]) for k in ks], vs)
    o2 = _softmax_av([_dot_nt(q[:, DA_QK:], k[:, DA_QK:]) for k in ks], vs)
    return _rms(o1 - lam * o2, gd) * (1.0 - lam_init)


def _plain_head(q, ks, vs, biases):
    scores = []
    for k, bias in zip(ks, biases):
        s = _dot_nt(q, k)
        scores.append(s if bias is None else s + bias)
    return _softmax_av(scores, vs)


def _attn_ctx_body(lam_ref, gd_ref, qa_ref, ka_ref, va_ref, qb_ref, kb_ref, vb_ref, o_ref, *, lam_init):
    lam = _lambda(lam_ref[...], lam_init)
    gd = gd_ref[...]
    for h in range(DA_H):
        sl = slice(h * DA_V, (h + 1) * DA_V)
        o = _diff_head(qa_ref[:, sl], [ka_ref[:, sl].astype(BF16)], [va_ref[:, sl].astype(BF16)],
                       lam, gd, lam_init)
        o_ref[:, sl] = o.astype(BF16)
    for h in range(NA_H):
        sl = slice(h * NA_D, (h + 1) * NA_D)
        o = _plain_head(qb_ref[:, sl], [kb_ref[:, sl].astype(BF16)], [vb_ref[:, sl].astype(BF16)], [None])
        o_ref[:, DA_W + h * NA_D:DA_W + (h + 1) * NA_D] = o.astype(BF16)


def _attn_ctx(li, lam_p, g_diff, qa, ka, va, qb, kb, vb, lam_init):
    def blk(w):
        return pl.BlockSpec((N_P, w), lambda b: (b, 0))

    def cblk(w):
        return pl.BlockSpec((None, None, N_P, w), lambda b: (b, li, 0, 0))

    return pl.pallas_call(
        functools.partial(_attn_ctx_body, lam_init=lam_init),
        grid=(NB_P,),
        in_specs=[pl.BlockSpec((None, 4, DA_QK), lambda b: (li, 0, 0)),
                  pl.BlockSpec((None, 1, DA_V), lambda b: (li, 0, 0)),
                  blk(DA_W), cblk(DA_W), cblk(DA_W), blk(NA_W), cblk(NA_W), cblk(NA_W)],
        out_specs=pl.BlockSpec((N_P, AB_W), lambda b: (b, 0)),
        out_shape=jax.ShapeDtypeStruct((T_P, AB_W), BF16),
        compiler_params=_cparams(("parallel",)),
        name=f"attn_ctx_l{li}",
    )(lam_p, g_diff, qa, ka, va, qb, kb, vb)


G_ROWS = N_S // GRID_W
QR_PER_BLK = TQ // GRID_W


def _fill_na_bias(j, strip_ref, bias_scr):
    for jj in range(N_S // TQ):
        @pl.when(j == jj)
        def _():
            for rr in range(QR_PER_BLK):
                qr = jj * QR_PER_BLK + rr
                rs = min(max(qr - NA_ROWS // 2, 0), G_ROWS - NA_ROWS)
                lo, hi = rs * GRID_W, (rs + NA_ROWS) * GRID_W
                a0 = (rs - qr + NA_ROWS - 1) * GRID_W
                q = slice(rr * GRID_W, (rr + 1) * GRID_W)
                for h in range(NA_H):
                    if lo > 0:
                        bias_scr[h, q, 0:lo] = jnp.full((GRID_W, lo), NEG, F32)
                    bias_scr[h, q, lo:hi] = strip_ref[h, :, a0:a0 + NA_ROWS * GRID_W]
                    if hi < N_S:
                        bias_scr[h, q, hi:N_S] = jnp.full((GRID_W, N_S - hi), NEG, F32)


def _attn_lat_body(lam_ref, gd_ref, qa_ref, ka_ref, va_ref, ck_ref, cv_ref,
                   qb_ref, kb_ref, vb_ref, cnk_ref, cnv_ref, strip_ref, o_ref, bias_scr, *, lam_init):
    _fill_na_bias(pl.program_id(1), strip_ref, bias_scr)
    lam = _lambda(lam_ref[...], lam_init)
    gd = gd_ref[...]
    for h in range(DA_H):
        sl = slice(h * DA_V, (h + 1) * DA_V)
        ks = [ka_ref[:, sl], ck_ref[:, sl].astype(BF16)]
        vs = [va_ref[:, sl], cv_ref[:, sl].astype(BF16)]
        o_ref[:, sl] = _diff_head(qa_ref[:, sl], ks, vs, lam, gd, lam_init).astype(BF16)
    for h in range(NA_H):
        sl = slice(h * NA_D, (h + 1) * NA_D)
        ks = [kb_ref[:, sl], cnk_ref[:, sl].astype(BF16)]
        vs = [vb_ref[:, sl], cnv_ref[:, sl].astype(BF16)]
        o = _plain_head(qb_ref[:, sl], ks, vs, [bias_scr[h], None])
        o_ref[:, DA_W + h * NA_D:DA_W + (h + 1) * NA_D] = o.astype(BF16)


def _attn_lat(li, lam_p, g_diff, qa, ka, va, ck, cv, qb, kb, vb, cnk, cnv, strip, lam_init):
    nq = N_S // TQ
    q0 = T_P // TQ

    def qblk(w):
        return pl.BlockSpec((TQ, w), lambda b, j: (q0 + b * nq + j, 0))

    def kvblk(w):
        return pl.BlockSpec((N_S, w), lambda b, j: (b, 0))

    def cblk(w):
        return pl.BlockSpec((None, None, PAST, w), lambda b, j: (b, li, 0, 0))

    return pl.pallas_call(
        functools.partial(_attn_lat_body, lam_init=lam_init),
        grid=(NB_S, nq),
        in_specs=[pl.BlockSpec((None, 4, DA_QK), lambda b, j: (li, 0, 0)),
                  pl.BlockSpec((None, 1, DA_V), lambda b, j: (li, 0, 0)),
                  qblk(DA_W), kvblk(DA_W), kvblk(DA_W), cblk(DA_W), cblk(DA_W),
                  qblk(NA_W), kvblk(NA_W), kvblk(NA_W), cblk(NA_W), cblk(NA_W),
                  pl.BlockSpec((None, NA_H, GRID_W, (2 * NA_ROWS - 1) * GRID_W), lambda b, j: (li, 0, 0, 0))],
        out_specs=pl.BlockSpec((TQ, AB_W), lambda b, j: (b * nq + j, 0)),
        out_shape=jax.ShapeDtypeStruct((T_S, AB_W), BF16),
        scratch_shapes=[pltpu.VMEM((NA_H, TQ, N_S), F32)],
        compiler_params=_cparams(("parallel", "parallel")),
        name=f"attn_lat_l{li}",
    )(lam_p, g_diff, qa, ka, va, ck, cv, qb, kb, vb, cnk, cnv, strip)


def _na_bias_strips(table):
    w = np.arange(GRID_W)
    cs = np.clip(w - NA_COLS // 2, 0, GRID_W - NA_COLS)
    col_ok = (w[None, :] >= cs[:, None]) & (w[None, :] < cs[:, None] + NA_COLS)
    sel_c = (np.clip(w[None, :, None] - w[:, None, None] + NA_COLS - 1, 0, 2 * NA_COLS - 2)
             == np.arange(2 * NA_COLS - 1)).astype(np.float32)
    t1 = jnp.einsum('lhab,cdb->lhcad', table.astype(F32), jnp.asarray(sel_c),
                    precision=lax.Precision.HIGHEST)
    t1 = jnp.where(jnp.asarray(col_ok)[None, None, :, None, :], t1, NEG)
    return t1.reshape(DEPTH, NA_H, GRID_W, (2 * NA_ROWS - 1) * GRID_W)


def _route_t(logits_t, b_col, tm):
    per = N_EXP // N_GRP
    s = 1.0 / (1.0 + jnp.exp(-logits_t))
    sel3 = (s + b_col).reshape(N_GRP, per, tm)
    midx = lax.broadcasted_iota(jnp.int32, (N_GRP, per, tm), 1)
    gidx = lax.broadcasted_iota(jnp.int32, (N_GRP, per, tm), 0)
    ninf = -jnp.inf
    m1 = jnp.max(sel3, axis=1, keepdims=True)
    i1 = jnp.min(jnp.where(sel3 == m1, midx, per), axis=1, keepdims=True)
    m2 = jnp.max(jnp.where(midx == i1, ninf, sel3), axis=1, keepdims=True)
    gsc = jnp.broadcast_to(m1 + m2, (N_GRP, per, tm))
    cnt = jnp.zeros((N_GRP, per, tm), jnp.int32)
    for g in range(N_GRP):
        sg = gsc[g][None]
        cnt = cnt + ((sg > gsc) | ((sg == gsc) & (g < gidx))).astype(jnp.int32)
    x = jnp.where(cnt < TOPK_GRP, sel3, ninf).reshape(N_EXP, tm)
    eidx = lax.broadcasted_iota(jnp.int32, (N_EXP, tm), 0)
    chosen = jnp.zeros((N_EXP, tm), jnp.bool_)
    for _ in range(TOP_K):
        m = jnp.max(x, axis=0, keepdims=True)
        first = jnp.min(jnp.where(x == m, eidx, N_EXP), axis=0, keepdims=True)
        hit = eidx == first
        chosen = chosen | hit
        x = jnp.where(hit, ninf, x)
    w = jnp.where(chosen, s, 0.0)
    return chosen, w / jnp.sum(w, axis=0, keepdims=True) * ROUTED_SCALE


def _merge_body(oabp_ref, oabs_ref, ocp_ref, ocs_ref, xp_ref, xs_ref, mod_ref, gpost_ref, gpre_ref,
                wout_ref, wrt_ref, br_ref, tri_ref, y_ref, h_ref, gates_ref, rank_ref, pcu_ref):
    i = pl.program_id(0)
    mix = (_dot(_tok_select(i, oabp_ref, oabs_ref), wout_ref[0:AB_W, :])
           + _dot(_tok_select(i, ocp_ref, ocs_ref), wout_ref[AB_W:D, :]))
    y = _tok_select(i, xp_ref, xs_ref) + mod_ref[2:3, :] * _rms(mix, gpost_ref[...])
    y_ref[...] = y
    h = _rms(y, gpre_ref[...]) * (1.0 + mod_ref[4:5, :]) + mod_ref[3:4, :]
    h_ref[...] = h.astype(BF16)
    logits_t = lax.dot_general(wrt_ref[...], h, (((1,), (1,)), ((), ())),
                               precision=lax.Precision.HIGHEST, preferred_element_type=F32)
    chosen, gates = _route_t(logits_t, br_ref[...], TM_IN)
    gates_ref[...] = gates
    ch = jnp.where(chosen, 1.0, 0.0)
    rank = _dot(ch.astype(BF16), tri_ref[...])
    rank_ref[...] = jnp.where(chosen, rank, -1.0)
    for j in range(TM_IN // B_SORT):
        cnt = jnp.sum(ch[:, j * B_SORT:(j + 1) * B_SORT], axis=1, keepdims=True)
        units = jnp.floor((cnt + (UNIT - 1)) * (1.0 / UNIT))
        pcu_ref[j] = jnp.broadcast_to(units, (N_EXP, LANES))


def _merge(li, oabp, oabs, ocp, ocs, xp, xs, modv, g_post, g_pre, w_out_bf, w_router_t, b_router_col, tri):
    def tok(w):
        return pl.BlockSpec((TM_IN, w), lambda i: (i, 0))

    def par(*shape):
        return pl.BlockSpec((None,) + shape, lambda i: (li,) + (0,) * len(shape))

    tok_t = pl.BlockSpec((N_EXP, TM_IN), lambda i: (0, i))
    nb = TM_IN // B_SORT
    return pl.pallas_call(
        _merge_body,
        grid=(T // TM_IN,),
        in_specs=[_ctx_tok(AB_W), _lat_tok(AB_W), _ctx_tok(POOL_W), _lat_tok(POOL_W), _ctx_tok(D), _lat_tok(D),
                  pl.BlockSpec((None, None, 6, D), lambda i: (li, _mod_row(i, TM_IN), 0, 0)),
                  par(1, D), par(1, D), par(D, D), par(N_EXP, D), par(N_EXP, 1),
                  pl.BlockSpec((TM_IN, TM_IN), lambda i: (0, 0))],
        out_specs=[tok(D), tok(D), tok_t, tok_t,
                   pl.BlockSpec((nb, N_EXP, LANES), lambda i: (i, 0, 0))],
        out_shape=[jax.ShapeDtypeStruct((T, D), F32),
                   jax.ShapeDtypeStruct((T, D), BF16),
                   jax.ShapeDtypeStruct((N_EXP, T), F32),
                   jax.ShapeDtypeStruct((N_EXP, T), F32),
                   jax.ShapeDtypeStruct((N_BLK, N_EXP, LANES), F32)],
        compiler_params=_cparams(("parallel",)),
        name=f"merge_l{li}",
    )(oabp, oabs, ocp, ocs, xp, xs, modv, g_post, g_pre, w_out_bf, w_router_t, b_router_col, tri)


def _rank_matrix():
    t = np.arange(TM_IN)
    m = (t[:, None] < t[None, :]) & (t[:, None] // B_SORT == t[None, :] // B_SORT)
    return jnp.asarray(m, BF16)


def _moe_tables(pcu_f):
    pcu = pcu_f[:, :, 0].astype(jnp.int32)
    incl = jnp.cumsum(pcu, axis=1)
    uo = incl - pcu
    tot = jnp.sum(pcu, axis=0)
    gt = (tot + UPT - 1) // UPT
    gstart_t = jnp.cumsum(gt) - gt
    gstart_u = gstart_t * UPT
    steps = (gt + TPS - 1) // TPS
    sincl = jnp.cumsum(steps)
    sstart = sincl - steps
    bstart = gstart_u[None, :] + jnp.cumsum(pcu, axis=0) - pcu
    u = jnp.arange(UPB, dtype=jnp.int32)
    eou = jnp.sum((u[None, :, None] >= incl[:, None, :]).astype(jnp.int32), axis=-1)
    valid = eou < N_EXP
    onehot = eou[:, :, None] == jnp.arange(N_EXP, dtype=jnp.int32)[None, None, :]
    run_u0 = jnp.sum(jnp.where(onehot, uo[:, None, :], 0), axis=-1)
    run_g0 = jnp.sum(jnp.where(onehot, bstart[:, None, :], 0), axis=-1)
    dst = run_g0 + u[None, :] - run_u0
    spare0 = SPARE_UNIT0 + (jnp.arange(N_BLK, dtype=jnp.int32) % 2)[:, None] * (UPB + ZPB)
    dst_sort = jnp.where(valid, dst, spare0 + u[None, :])
    dst_comb = jnp.where(valid, dst, ZERO_UNIT)
    loc = jnp.where(valid, (u[None, :] - run_u0) * UNIT, -(1 << 20))
    k = jnp.arange(UPT - 1, dtype=jnp.int32)
    ztail = gt * UPT - tot
    zdst = jnp.where(k[None, :] < ztail[:, None], (gstart_u + tot)[:, None] + k[None, :], -1).reshape(N_BLK, ZPB)
    zdst = jnp.where(zdst >= 0, zdst, spare0 + UPB + jnp.arange(ZPB, dtype=jnp.int32)[None, :])
    step = jnp.arange(NS_MAX, dtype=jnp.int32)
    te = jnp.minimum(jnp.sum((step[:, None] >= sincl[None, :]).astype(jnp.int32), axis=-1), N_EXP - 1)
    mine = te[:, None] == jnp.arange(N_EXP, dtype=jnp.int32)[None, :]
    pick = lambda v: jnp.sum(jnp.where(mine, v[None, :], 0), axis=-1)
    first = gt - TPS * (steps - 1)
    j = step - pick(sstart)
    ntile = jnp.where(j == 0, pick(first), TPS)
    tile0 = pick(gstart_t) + jnp.where(j == 0, 0, pick(first) + TPS * (j - 1))
    ns = sincl[-1]
    wslot = (jnp.cumsum((j == 0).astype(jnp.int32)) - 1) % 2
    nxt = pick(sincl)
    nexte = jnp.sum(jnp.where(step[None, :] == nxt[:, None], te[None, :], 0), axis=-1)
    nexte = jnp.where(j == 0, jnp.where(nxt < ns, nexte, -1), -2)
    flat = lambda a: a.reshape(-1)
    return (flat(jnp.minimum(eou, N_EXP - 1)), flat(loc), flat(dst_sort), flat(dst_comb), flat(zdst),
            incl[:, -1], te, ntile, tile0, wslot, nexte, sincl[-1:])


def _unit_rows(u):
    if isinstance(u, int):
        return pl.ds(u * UNIT, UNIT)
    return pl.ds(pl.multiple_of(u * UNIT, UNIT), UNIT)


def _for_live_groups(n_units, fn, groups=range(UPB // UPT)):
    for g in groups:
        if (g + 1) * UPT <= MIN_UNITS:
            fn(g)
        else:
            @pl.when(g * UPT < n_units)
            def _(g=g):
                fn(g)


def _sort_body(eou_ref, loc_ref, dst_ref, zdst_ref, nun_ref, h_ref, rank_ref, gates_ref, xs_hbm, p_ref,
               xs_scr, zero_scr, sem):
    b = pl.program_id(0)
    slot = b % 2

    def copy(src, d, s):
        return pltpu.make_async_copy(src, xs_hbm.at[_unit_rows(d)], sem.at[s])

    def wait_units(n, s):
        lax.fori_loop(0, n, lambda k, c: (copy(zero_scr, 0, s).wait(), c)[1], 0, unroll=True)

    def wait_step(blk, s):
        _for_live_groups(nun_ref[blk], lambda g: wait_units(UPT, s))
        wait_units(ZPB, s)

    @pl.when(b == 0)
    def _():
        zero_scr[...] = jnp.zeros((UNIT, XS_W), BF16)

    @pl.when(b >= 2)
    def _():
        wait_step(b - 2, slot)

    base = lax.broadcasted_iota(jnp.int32, (UNIT, B_SORT), 0)
    g = gates_ref[...]
    g_hi = g.astype(BF16)
    g_hl = jnp.concatenate([g_hi, (g - g_hi.astype(F32)).astype(BF16)], axis=0)
    h = h_ref[...]
    def start_group(g):
        for u in range(g * UPT, (g + 1) * UPT):
            copy(xs_scr.at[slot, _unit_rows(u)], dst_ref[b * UPB + u], slot).start()

    upc = SORT_CHUNK // UNIT
    for c in range(RB // SORT_CHUNK):
        for u in range(c * upc, (c + 1) * upc):
            match = rank_ref[pl.ds(eou_ref[b * UPB + u], 1), :] == (base + loc_ref[b * UPB + u]).astype(F32)
            p_ref[_unit_rows(u), :] = jnp.where(match, 1.0, 0.0).astype(BF16)
        rows = slice(c * SORT_CHUNK, (c + 1) * SORT_CHUNK)
        xs_scr[slot, rows, 0:D] = _dot(p_ref[rows, :], h).astype(BF16)
        xs_scr[slot, rows, D:XS_W] = _dot_nt(p_ref[rows, :], g_hl).astype(BF16)
        _for_live_groups(nun_ref[b], start_group, range(c * upc // UPT, (c + 1) * upc // UPT))
    for j in range(ZPB):
        copy(zero_scr, zdst_ref[b * ZPB + j], slot).start()

    @pl.when(b == N_BLK - 1)
    def _():
        wait_step(b - 1, 1 - slot)
        wait_step(b, slot)


def _moe_sort(li, eou, loc, dst, zdst, nun, h, rank_t, gates_t):
    blk_t = pl.BlockSpec((N_EXP, B_SORT), lambda b, *_: (0, b))
    return pl.pallas_call(
        _sort_body,
        grid_spec=pltpu.PrefetchScalarGridSpec(
            num_scalar_prefetch=5,
            grid=(N_BLK,),
            in_specs=[pl.BlockSpec((B_SORT, D), lambda b, *_: (b, 0)), blk_t, blk_t],
            out_specs=[pl.BlockSpec(memory_space=pl.ANY),
                       pl.BlockSpec((RB, B_SORT), lambda b, *_: (b, 0))],
            scratch_shapes=[pltpu.VMEM((2, RB, XS_W), BF16),
                            pltpu.VMEM((UNIT, XS_W), BF16),
                            pltpu.SemaphoreType.DMA((2,))]),
        out_shape=[jax.ShapeDtypeStruct((R_SORT, XS_W), BF16),
                   jax.ShapeDtypeStruct((N_BLK * RB, B_SORT), BF16)],
        compiler_params=_cparams(("arbitrary",)),
        name=f"moe_sort_l{li}",
    )(eou, loc, dst, zdst, nun, h, rank_t, gates_t)


def _experts_body(te_ref, ntile_ref, tile0_ref, wslot_ref, nexte_ref, ns_ref,
                  xs_hbm, wg_hbm, wu_hbm, wd_hbm, os_hbm,
                  xs_buf, o_buf, wg_f, wu_f, wd_f, sem_in, sem_out, sem_w, *, li):
    ns = ns_ref[0]

    def tiles_of(step):
        return jnp.where((step >= 0) & (step < ns), ntile_ref[jnp.clip(step, 0, NS_MAX - 1)], 0)

    def tile_rows(step, t):
        return pl.ds(pl.multiple_of((tile0_ref[jnp.clip(step, 0, NS_MAX - 1)] + t) * TMG, TMG), TMG)

    def buf_rows(t):
        return pl.ds(pl.multiple_of(t * TMG, TMG), TMG)

    def copy_in(step, t, s):
        return pltpu.make_async_copy(xs_hbm.at[tile_rows(step, t)], xs_buf.at[s, buf_rows(t)], sem_in.at[s])

    def copy_out(step, t, s):
        return pltpu.make_async_copy(o_buf.at[s, buf_rows(t)], os_hbm.at[tile_rows(step, t)], sem_out.at[s])

    def for_tiles(step, fn):
        lax.fori_loop(0, tiles_of(step), lambda t, c: (fn(t), c)[1], 0)

    def weight_copies(e, s):
        return [pltpu.make_async_copy(hbm.at[li, e], buf.at[s], sem_w.at[s])
                for hbm, buf in ((wg_hbm, wg_f), (wu_hbm, wu_f), (wd_hbm, wd_f))]

    for c in weight_copies(te_ref[0], 0):
        c.start()
    for_tiles(0, lambda t: copy_in(0, t, 0).start())

    def step(i, carry):
        slot = i % 2
        for_tiles(i + 1, lambda t: copy_in(i + 1, t, 1 - slot).start())
        for_tiles(i - 2, lambda t: copy_out(i - 2, t, slot).wait())
        e = te_ref[i]
        ws = wslot_ref[i]

        @pl.when(nexte_ref[i] >= -1)
        def _():
            for c in weight_copies(e, ws):
                c.wait()

            @pl.when(nexte_ref[i] >= 0)
            def _():
                for c in weight_copies(nexte_ref[i], 1 - ws):
                    c.start()

        ntile = ntile_ref[i]
        for_tiles(i, lambda t: copy_in(i, t, slot).wait())
        for k in range(1, TPS + 1):
            rows = k * TMG

            @pl.when(ntile == k)
            def _():
                x = xs_buf[slot, 0:rows, 0:D]
                lane = lax.broadcasted_iota(jnp.int32, (rows, LANES), 1)
                mine = (lane == e) | (lane == e + N_EXP)
                gate = jnp.sum(jnp.where(mine, xs_buf[slot, 0:rows, D:XS_W].astype(F32), 0.0),
                               axis=1, keepdims=True)
                hid = (_silu(_dot(x, wg_f[ws].astype(BF16))) * _dot(x, wu_f[ws].astype(BF16))) * gate
                o_buf[slot, 0:rows, :] = _dot(hid.astype(BF16), wd_f[ws].astype(BF16)).astype(BF16)

        for_tiles(i, lambda t: copy_out(i, t, slot).start())
        return carry

    lax.fori_loop(0, ns, step, 0)
    for_tiles(ns - 2, lambda t: copy_out(ns - 2, t, ns % 2).wait())
    for_tiles(ns - 1, lambda t: copy_out(ns - 1, t, (ns - 1) % 2).wait())
    o_buf[0, 0:UNIT, :] = jnp.zeros((UNIT, D), BF16)
    zero = pltpu.make_async_copy(o_buf.at[0, 0:UNIT], os_hbm.at[_unit_rows(ZERO_UNIT)], sem_out.at[0])
    zero.start()
    zero.wait()


def _moe_experts(li, te, ntile, tile0, wslot, nexte, ns, xs, w_gate, w_up, w_down):
    return pl.pallas_call(
        functools.partial(_experts_body, li=li),
        grid_spec=pltpu.PrefetchScalarGridSpec(
            num_scalar_prefetch=6,
            grid=(1,),
            in_specs=[pl.BlockSpec(memory_space=pl.ANY)] * 4,
            out_specs=pl.BlockSpec(memory_space=pl.ANY),
            scratch_shapes=[pltpu.VMEM((2, STEP_ROWS, XS_W), BF16),
                            pltpu.VMEM((2, STEP_ROWS, D), BF16),
                            pltpu.VMEM((2, D, EXP_DIM), F32),
                            pltpu.VMEM((2, D, EXP_DIM), F32),
                            pltpu.VMEM((2, EXP_DIM, D), F32),
                            pltpu.SemaphoreType.DMA((2,)),
                            pltpu.SemaphoreType.DMA((2,)),
                            pltpu.SemaphoreType.DMA((2,))]),
        out_shape=jax.ShapeDtypeStruct((R_OUT, D), BF16),
        compiler_params=_cparams(("arbitrary",)),
        name=f"moe_experts_l{li}",
    )(te, ntile, tile0, wslot, nexte, ns, xs, w_gate, w_up, w_down)


def _combine_body(dst_ref, nun_ref, h_ref, p_ref, y_ref, mod_ref, gpost_ref,
                  wsg_ref, wsu_ref, wsd_ref, os_hbm, op_ref, ol_ref, os_scr, acc_scr, sem):
    b = pl.program_id(0)
    slot = b % 2

    def copy(d, u, s):
        return pltpu.make_async_copy(os_hbm.at[_unit_rows(d)], os_scr.at[s, _unit_rows(u)], sem.at[s])

    def fetch(blk, s):
        def group(g):
            lax.fori_loop(g * UPT, (g + 1) * UPT,
                          lambda u, c: (copy(dst_ref[blk * UPB + u], u, s).start(), c)[1], 0, unroll=UNROLL)

        _for_live_groups(nun_ref[blk], group)

    @pl.when(b == 0)
    def _():
        fetch(0, 0)

    @pl.when(b + 1 < N_BLK)
    def _():
        fetch(b + 1, 1 - slot)

    h = h_ref[...]
    shared = _dot((_silu(_dot(h, wsg_ref[...])) * _dot(h, wsu_ref[...])).astype(BF16), wsd_ref[...])
    _for_live_groups(nun_ref[b], lambda g: lax.fori_loop(
        0, UPT, lambda k, c: (copy(0, 0, slot).wait(), c)[1], 0, unroll=True))
    def product_t(rows):
        return lax.dot_general(os_scr[slot, rows, :], p_ref[rows, :], (((0,), (0,)), ((), ())),
                               preferred_element_type=F32)

    def add_group(g):
        acc_scr[...] += product_t(slice(g * TMG, (g + 1) * TMG))

    acc_scr[...] = product_t(slice(0, MIN_UNITS * UNIT))
    _for_live_groups(nun_ref[b], add_group, range(MIN_UNITS // UPT, UPB // UPT))
    routed = acc_scr[...].T
    out = y_ref[...] + mod_ref[5:6, :] * _rms(routed + shared, gpost_ref[...])

    @pl.when(b < T_P // B_SORT)
    def _():
        op_ref[...] = out

    @pl.when(b >= T_P // B_SORT)
    def _():
        ol_ref[...] = out


def _moe_combine(li, dst, nun, h, p, y, modv, g_post, wsg_bf, wsu_bf, wsd_bf, os):
    nbp = T_P // B_SORT
    def tok(w):
        return pl.BlockSpec((B_SORT, w), lambda b, *_: (b, 0))

    def par(*shape):
        return pl.BlockSpec((None,) + shape, lambda b, *_: (li,) + (0,) * len(shape))

    return pl.pallas_call(
        _combine_body,
        grid_spec=pltpu.PrefetchScalarGridSpec(
            num_scalar_prefetch=2,
            grid=(N_BLK,),
            in_specs=[tok(D), pl.BlockSpec((RB, B_SORT), lambda b, *_: (b, 0)), tok(D),
                      pl.BlockSpec((None, None, 6, D), lambda b, *_: (li, _mod_row(b, B_SORT), 0, 0)),
                      par(1, D), par(D, EXP_DIM), par(D, EXP_DIM), par(EXP_DIM, D),
                      pl.BlockSpec(memory_space=pl.ANY)],
            out_specs=[pl.BlockSpec((B_SORT, D), lambda b, *_: (jnp.minimum(b, nbp - 1), 0)),
                       pl.BlockSpec((B_SORT, D), lambda b, *_: (jnp.maximum(b - nbp, 0), 0))],
            scratch_shapes=[pltpu.VMEM((2, RB, D), BF16),
                            pltpu.VMEM((D, B_SORT), F32),
                            pltpu.SemaphoreType.DMA((2,))]),
        out_shape=[jax.ShapeDtypeStruct((T_P, D), F32), jax.ShapeDtypeStruct((T_S, D), F32)],
        compiler_params=_cparams(("arbitrary",)),
        name=f"moe_combine_l{li}",
    )(dst, nun, h, p, y, modv, g_post, wsg_bf, wsu_bf, wsd_bf, os)


def _block_diag(w):
    out = jnp.zeros((DEPTH, POOL_W, POOL_W), F32)
    for g in range(len(POOL_WINDOWS)):
        out = out.at[:, g * POOL_CH:(g + 1) * POOL_CH, g * POOL_CH:(g + 1) * POOL_CH].set(w[:, g])
    return out


def kernel(x_prompt, x_sample, cache_diff_k, cache_diff_v, cache_na_k, cache_na_v, c, c_ctx,
           w_ada, b_ada, g_pre_mix, g_post_mix, g_pre_ffn, g_post_ffn, w_in, w_out,
           diff_lambda, g_diff, na_bias, pool_w, pool_scale, w_router, b_router,
           w_gate, w_up, w_down, ws_gate, ws_up, ws_down):
    xp, xs = x_prompt.reshape(T_P, D), x_sample.reshape(T_S, D)
    cvec = jnp.concatenate([c_ctx[None, :], c, jnp.zeros((3, D), F32)], axis=0)
    modv = _modulation(cvec, w_ada, b_ada)[:, :1 + NB_S].reshape(DEPTH, 1 + NB_S, 6, D)

    cos_t, sin_t = _rope_tables()
    w_in_bf = w_in.astype(BF16)
    w_out_bf = w_out.astype(BF16)
    wsg_bf, wsu_bf, wsd_bf = ws_gate.astype(BF16), ws_up.astype(BF16), ws_down.astype(BF16)
    pool_bd = _block_diag(pool_w)
    row = lambda a: a.reshape(DEPTH, 1, a.shape[-1])
    g_pre_mix, g_post_mix, g_pre_ffn, g_post_ffn = map(row, (g_pre_mix, g_post_mix, g_pre_ffn, g_post_ffn))
    g_diff, pool_scale = row(g_diff), row(pool_scale)
    w_router_t = jnp.swapaxes(w_router, 1, 2)
    b_router_col = b_router.reshape(DEPTH, N_EXP, 1)
    tri = _rank_matrix()
    ck = cache_diff_k.reshape(NB_S, DEPTH, PAST, DA_W)
    cv = cache_diff_v.reshape(NB_S, DEPTH, PAST, DA_W)
    cnk = cache_na_k.reshape(NB_S, DEPTH, PAST, NA_W)
    cnv = cache_na_v.reshape(NB_S, DEPTH, PAST, NA_W)

    strips = _na_bias_strips(na_bias)

    caches = ()
    for li in range(DEPTH):
        lam_init = 0.8 - 0.6 * math.exp(-0.3 * li)
        qa, qb, ocp, up, ka, va, kb, vb, *caches = _in_proj(li, xp, xs, modv, g_pre_mix, w_in_bf, cos_t, sin_t,
                                                           pool_bd, pool_scale, caches)
        ocs = _pool_lat(li, up, pool_bd, pool_scale)
        oabp = _attn_ctx(li, diff_lambda, g_diff, qa, caches[0], caches[1], qb, caches[2], caches[3], lam_init)
        oabs = _attn_lat(li, diff_lambda, g_diff, qa, ka, va, ck, cv, qb, kb, vb, cnk, cnv, strips, lam_init)
        y, h, gates_t, rank_t, pcu = _merge(li, oabp, oabs, ocp, ocs, xp, xs, modv, g_post_mix, g_pre_ffn,
                                            w_out_bf, w_router_t, b_router_col, tri)
        eou, loc, dst_sort, dst_comb, zdst, nun, te, ntile, tile0, wslot, nexte, ns = _moe_tables(pcu)
        srt, p = _moe_sort(li, eou, loc, dst_sort, zdst, nun, h, rank_t, gates_t)
        os = _moe_experts(li, te, ntile, tile0, wslot, nexte, ns, srt, w_gate, w_up, w_down)
        xp, xs = _moe_combine(li, dst_comb, nun, h, p, y, modv, g_post_ffn, wsg_bf, wsu_bf, wsd_bf, os)

    new_dk, new_dv, new_nk, new_nv = caches
    return (xp.reshape(NB_P, N_P, D), xs.reshape(NB_S, N_S, D),
            new_dk.reshape(NB_P, DEPTH, N_P, DA_H, 2 * DA_QK), new_dv.reshape(NB_P, DEPTH, N_P, DA_H, DA_V),
            new_nk.reshape(NB_P, DEPTH, N_P, NA_H, NA_D), new_nv.reshape(NB_P, DEPTH, N_P, NA_H, NA_D))
```

```python
import functools
import math

import numpy as np
import jax
import jax.numpy as jnp
from jax import lax
from jax.experimental import pallas as pl
from jax.experimental.pallas import tpu as pltpu

F32 = jnp.float32
BF16 = jnp.bfloat16

D = 1024
DEPTH = 2
NB_P, N_P = 16, 256
NB_S, N_S = 4, 1024
T_P = NB_P * N_P
T_S = NB_S * N_S
T = T_P + T_S
PAST = 512
GRID_W = 64
DA_H, DA_QK, DA_V = 4, 64, 128
DA_W = DA_H * DA_V
NA_H, NA_D = 4, 64
NA_W = NA_H * NA_D
NA_ROWS, NA_COLS = 8, 16
POOL_WINDOWS = (2, 4, 8, 16)
POOL_CH = 64
POOL_W = 256
IN_COLS = 3 * DA_W + 3 * NA_W + POOL_W
AB_W = DA_W + NA_W
N_EXP, TOP_K, N_GRP, TOPK_GRP = 64, 8, 8, 4
EXP_DIM = 256
ROUTED_SCALE = 2.5
EPS = 1e-6
NEG = -1e30
ROPE_THETA = 10000.0

LANES = 128
VMEM_LIMIT = 56 * 1024 * 1024

TM_IN = 512
NT_P = T_P // TM_IN
TQ = 512

B_SORT = 256
UNIT = 16
N_BLK = T // B_SORT
SORT_CHUNK = 1024
RUN_PAD_ROWS = TOP_K * B_SORT + N_EXP * (UNIT - 1)
RB = -(-RUN_PAD_ROWS // SORT_CHUNK) * SORT_CHUNK
UPB = RB // UNIT
MIN_UNITS = TOP_K * B_SORT // UNIT
TMG = 256
UPT = TMG // UNIT
NT_MAX = (N_BLK * RUN_PAD_ROWS) // TMG + N_EXP
XS_W = D + LANES
EXP_PER_BLK = N_EXP // N_BLK
ZPB = EXP_PER_BLK * (UPT - 1)
TPS = 6
STEP_ROWS = TPS * TMG
NS_MAX = NT_MAX // TPS + (N_EXP * (TPS - 1)) // TPS
ZERO_UNIT = NT_MAX * UPT
SPARE_UNIT0 = ZERO_UNIT + UPT
R_OUT = (NT_MAX + 1) * TMG
R_SORT = R_OUT + -(-2 * (UPB + ZPB) // UPT) * TMG
UNROLL = 8


def _cparams(sem):
    return pltpu.CompilerParams(dimension_semantics=sem, vmem_limit_bytes=VMEM_LIMIT)


def _rms(x, g):
    return x * lax.rsqrt(jnp.mean(x * x, axis=-1, keepdims=True) + EPS) * g


def _dot(a, b):
    return jnp.dot(a, b, preferred_element_type=F32)


def _dot_nt(a, b):
    return lax.dot_general(a, b, (((1,), (1,)), ((), ())), preferred_element_type=F32)


def _silu(x):
    return x / (1.0 + jnp.exp(-x))


def _mod_row(i, tm):
    off = i * tm - T_P
    return jnp.where(off >= 0, 1 + jnp.maximum(off, 0) // N_S, 0)


def _mod_body(c_ref, w_ref, b_ref, o_ref):
    c = c_ref[...]
    o_ref[...] = jnp.dot(_silu(c), w_ref[...], precision=lax.Precision.HIGHEST,
                         preferred_element_type=F32) + b_ref[...]


def _modulation(cvec, w_ada, b_ada):
    tn = 1536
    return pl.pallas_call(
        _mod_body,
        grid=(DEPTH, 6 * D // tn),
        in_specs=[pl.BlockSpec((8, D), lambda l, j: (0, 0)),
                  pl.BlockSpec((None, D, tn), lambda l, j: (l, 0, j)),
                  pl.BlockSpec((None, 1, tn), lambda l, j: (l, 0, j))],
        out_specs=pl.BlockSpec((None, 8, tn), lambda l, j: (l, 0, j)),
        out_shape=jax.ShapeDtypeStruct((DEPTH, 8, 6 * D), F32),
        compiler_params=_cparams(("parallel", "parallel")),
        name="modulation",
    )(cvec, w_ada, b_ada.reshape(DEPTH, 1, 6 * D))


def _rope_tables():
    nf = DA_QK // 4
    t = np.arange(N_S)
    pos = np.stack([t // GRID_W, t % GRID_W], axis=-1).astype(np.float32)
    inv = np.power(np.float32(ROPE_THETA), -np.arange(nf, dtype=np.float32) / nf)
    ang = pos[:, :, None] * inv
    cos = np.cos(ang)
    sin = np.sin(ang)
    cos64 = np.concatenate([cos[:, 0], cos[:, 0], cos[:, 1], cos[:, 1]], axis=-1)
    sin64 = np.concatenate([-sin[:, 0], sin[:, 0], -sin[:, 1], sin[:, 1]], axis=-1)
    reps = DA_W // DA_QK
    cos_t = np.concatenate([np.tile(cos64, (1, reps)), np.ones((TM_IN, DA_W), np.float32)], axis=0)
    sin_t = np.concatenate([np.tile(sin64, (1, reps)), np.zeros((TM_IN, DA_W), np.float32)], axis=0)
    return jnp.asarray(cos_t, F32), jnp.asarray(sin_t, F32)


def _tok_select(i, ctx_ref, lat_ref):
    return jnp.where(i < NT_P, ctx_ref[...], lat_ref[...])


def _in_proj_body(*refs):
    (xp_ref, xs_ref, mod_ref, g_ref, w_ref, cos_ref, sin_ref, pw_ref, ps_ref) = refs[:9]
    (qa_ref, qb_ref, ocp_ref, up_ref, ka_ref, va_ref, kb_ref, vb_ref,
     ck_ref, cv_ref, cnk_ref, cnv_ref) = refs[-12:]
    i = pl.program_id(0)
    x = _tok_select(i, xp_ref, xs_ref)
    h = (_rms(x, g_ref[...]) * (1.0 + mod_ref[1:2, :]) + mod_ref[0:1, :]).astype(BF16)
    cos = cos_ref[...]
    sin = sin_ref[...]
    lane = lax.broadcasted_iota(jnp.int32, (TM_IN, DA_W), 1)
    first = (lane % 32) < 16

    def proj(lo, hi):
        return _dot(h, w_ref[:, lo:hi])

    def rope(t):
        swapped = jnp.where(first, pltpu.roll(t, DA_W - 16, 1), pltpu.roll(t, 16, 1))
        return t * cos + swapped * sin

    o = 0
    qa_ref[...] = (rope(proj(o, o + DA_W)) * (DA_QK ** -0.5)).astype(BF16)
    o += DA_W
    ka = rope(proj(o, o + DA_W))
    o += DA_W
    va = proj(o, o + DA_W)
    o += DA_W
    qb_ref[...] = (proj(o, o + NA_W) * (NA_D ** -0.5)).astype(BF16)
    o += NA_W
    kb = proj(o, o + NA_W)
    o += NA_W
    vb = proj(o, o + NA_W)
    o += NA_W
    up = proj(o, o + POOL_W)

    @pl.when(i < NT_P)
    def _():
        for ref, val in ((ck_ref, ka), (cv_ref, va), (cnk_ref, kb), (cnv_ref, vb)):
            ref[...] = val.reshape(ref.shape)
        ocp_ref[...] = _pool_rows(up, N_P, pw_ref[...], ps_ref[...])

    @pl.when(i >= NT_P)
    def _():
        for ref, val in ((ka_ref, ka), (va_ref, va), (kb_ref, kb), (vb_ref, vb)):
            ref[...] = val.astype(BF16)
        up_ref[...] = up


def _ctx_tok(w):
    return pl.BlockSpec((TM_IN, w), lambda i: (jnp.minimum(i, NT_P - 1), 0))


def _lat_tok(w):
    return pl.BlockSpec((TM_IN, w), lambda i: (jnp.maximum(i - NT_P, 0), 0))


def _in_proj(li, xp, xs, modv, g_pre, w_in_bf, cos_t, sin_t, pool_bd, pool_scale, caches):
    n_pos_blk = N_S // TM_IN

    def pos_blk(i):
        return jnp.where(i >= NT_P, jnp.maximum(i - NT_P, 0) % n_pos_blk, n_pos_blk)

    def tok(w):
        return pl.BlockSpec((TM_IN, w), lambda i: (i, 0))

    def cache(w):
        return pl.BlockSpec((TM_IN // N_P, None, N_P, w), lambda i: (jnp.minimum(i, NT_P - 1), li, 0, 0))

    widths = (DA_W, DA_W, NA_W, NA_W)
    n_in, n_plain_out = 9, 8
    return pl.pallas_call(
        _in_proj_body,
        grid=(T // TM_IN,),
        in_specs=[_ctx_tok(D), _lat_tok(D),
                  pl.BlockSpec((None, None, 6, D), lambda i: (li, _mod_row(i, TM_IN), 0, 0)),
                  pl.BlockSpec((None, 1, D), lambda i: (li, 0, 0)),
                  pl.BlockSpec((None, D, IN_COLS), lambda i: (li, 0, 0)),
                  pl.BlockSpec((TM_IN, DA_W), lambda i: (pos_blk(i), 0)),
                  pl.BlockSpec((TM_IN, DA_W), lambda i: (pos_blk(i), 0)),
                  pl.BlockSpec((None, POOL_W, POOL_W), lambda i: (li, 0, 0)),
                  pl.BlockSpec((None, 1, POOL_W), lambda i: (li, 0, 0))]
                 + [pl.BlockSpec(memory_space=pl.ANY)] * len(caches),
        out_specs=[tok(DA_W), tok(NA_W), _ctx_tok(POOL_W), _lat_tok(POOL_W)] + [_lat_tok(w) for w in widths]
                  + [cache(w) for w in widths],
        out_shape=[jax.ShapeDtypeStruct((T, DA_W), BF16),
                   jax.ShapeDtypeStruct((T, NA_W), BF16),
                   jax.ShapeDtypeStruct((T_P, POOL_W), BF16),
                   jax.ShapeDtypeStruct((T_S, POOL_W), F32)]
                  + [jax.ShapeDtypeStruct((T_S, w), BF16) for w in widths]
                  + [jax.ShapeDtypeStruct((NB_P, DEPTH, N_P, w), F32) for w in widths],
        input_output_aliases={n_in + k: n_plain_out + k for k in range(len(caches))},
        compiler_params=_cparams(("arbitrary",)),
        name=f"in_proj_l{li}",
    )(xp, xs, modv, g_pre, w_in_bf, cos_t, sin_t, pool_bd, pool_scale, *caches)


def _pool_rows(u, n, w_mix, scale):
    rows = u.shape[0]
    row = lax.broadcasted_iota(jnp.int32, (rows, POOL_W), 0) % n
    lane = lax.broadcasted_iota(jnp.int32, (rows, POOL_W), 1)

    def shift_dn(a, k):
        return jnp.where(row >= k, pltpu.roll(a, k, 0), 0.0)

    def shift_up(a, k):
        return jnp.where(row < n - k, pltpu.roll(a, rows - k, 0), 0.0)

    fwd = u
    bwd = shift_dn(u, 1)
    mean = jnp.zeros_like(u)
    k = 1
    for gi, w in enumerate(POOL_WINDOWS):
        while k < w // 2:
            fwd = fwd + shift_up(fwd, k)
            bwd = bwd + shift_dn(bwd, k)
            k *= 2
        cnt = (jnp.minimum(row + w // 2, n) - jnp.maximum(row - w // 2, 0)).astype(F32)
        in_group = (lane >= gi * POOL_CH) & (lane < (gi + 1) * POOL_CH)
        mean = jnp.where(in_group, (fwd + bwd) / cnt, mean)
    pooled = mean - u
    mixed = jnp.dot(pooled, w_mix, precision=lax.Precision.HIGHEST, preferred_element_type=F32)
    return (mixed * scale).astype(BF16)


def _pool_body(u_ref, w_ref, s_ref, o_ref):
    o_ref[...] = _pool_rows(u_ref[...], N_S, w_ref[...], s_ref[...])


def _pool_lat(li, up, w_bd, scale):
    return pl.pallas_call(
        _pool_body,
        grid=(NB_S,),
        in_specs=[pl.BlockSpec((N_S, POOL_W), lambda i: (i, 0)),
                  pl.BlockSpec((None, POOL_W, POOL_W), lambda i: (li, 0, 0)),
                  pl.BlockSpec((None, 1, POOL_W), lambda i: (li, 0, 0))],
        out_specs=pl.BlockSpec((N_S, POOL_W), lambda i: (i, 0)),
        out_shape=jax.ShapeDtypeStruct((T_S, POOL_W), BF16),
        compiler_params=_cparams(("parallel",)),
        name=f"pool_lat_l{li}",
    )(up, w_bd, scale)


def _lambda(lp, lam_init):
    a = jnp.sum(lp[0:1, :] * lp[1:2, :], axis=1, keepdims=True)
    b = jnp.sum(lp[2:3, :] * lp[3:4, :], axis=1, keepdims=True)
    return jnp.exp(a) - jnp.exp(b) + lam_init


def _softmax_av(scores, vs):
    m = functools.reduce(jnp.maximum, [jnp.max(s, axis=-1, keepdims=True) for s in scores])
    es = [jnp.exp(s - m) for s in scores]
    den = functools.reduce(jnp.add, [jnp.sum(e, axis=-1, keepdims=True) for e in es])
    o = functools.reduce(jnp.add, [_dot(e.astype(BF16), v) for e, v in zip(es, vs)])
    return o * (1.0 / den)


def _diff_head(q, ks, vs, lam, gd, lam_init):
    o1 = _softmax_av([_dot_nt(q[:, :DA_QK], k[:, :DA_QK]) for k in ks], vs)
    o2 = _softmax_av([_dot_nt(q[:, DA_QK:], k[:, DA_QK:]) for k in ks], vs)
    return _rms(o1 - lam * o2, gd) * (1.0 - lam_init)


def _plain_head(q, ks, vs, biases):
    scores = []
    for k, bias in zip(ks, biases):
        s = _dot_nt(q, k)
        scores.append(s if bias is None else s + bias)
    return _softmax_av(scores, vs)


def _attn_ctx_body(lam_ref, gd_ref, qa_ref, ka_ref, va_ref, qb_ref, kb_ref, vb_ref, o_ref, *, lam_init):
    lam = _lambda(lam_ref[...], lam_init)
    gd = gd_ref[...]
    for h in range(DA_H):
        sl = slice(h * DA_V, (h + 1) * DA_V)
        o = _diff_head(qa_ref[:, sl], [ka_ref[:, sl].astype(BF16)], [va_ref[:, sl].astype(BF16)],
                       lam, gd, lam_init)
        o_ref[:, sl] = o.astype(BF16)
    for h in range(NA_H):
        sl = slice(h * NA_D, (h + 1) * NA_D)
        o = _plain_head(qb_ref[:, sl], [kb_ref[:, sl].astype(BF16)], [vb_ref[:, sl].astype(BF16)], [None])
        o_ref[:, DA_W + h * NA_D:DA_W + (h + 1) * NA_D] = o.astype(BF16)


def _attn_ctx(li, lam_p, g_diff, qa, ka, va, qb, kb, vb, lam_init):
    def blk(w):
        return pl.BlockSpec((N_P, w), lambda b: (b, 0))

    def cblk(w):
        return pl.BlockSpec((None, None, N_P, w), lambda b: (b, li, 0, 0))

    return pl.pallas_call(
        functools.partial(_attn_ctx_body, lam_init=lam_init),
        grid=(NB_P,),
        in_specs=[pl.BlockSpec((None, 4, DA_QK), lambda b: (li, 0, 0)),
                  pl.BlockSpec((None, 1, DA_V), lambda b: (li, 0, 0)),
                  blk(DA_W), cblk(DA_W), cblk(DA_W), blk(NA_W), cblk(NA_W), cblk(NA_W)],
        out_specs=pl.BlockSpec((N_P, AB_W), lambda b: (b, 0)),
        out_shape=jax.ShapeDtypeStruct((T_P, AB_W), BF16),
        compiler_params=_cparams(("parallel",)),
        name=f"attn_ctx_l{li}",
    )(lam_p, g_diff, qa, ka, va, qb, kb, vb)


G_ROWS = N_S // GRID_W
QR_PER_BLK = TQ // GRID_W


def _fill_na_bias(j, strip_ref, bias_scr):
    for jj in range(N_S // TQ):
        @pl.when(j == jj)
        def _():
            for rr in range(QR_PER_BLK):
                qr = jj * QR_PER_BLK + rr
                rs = min(max(qr - NA_ROWS // 2, 0), G_ROWS - NA_ROWS)
                lo, hi = rs * GRID_W, (rs + NA_ROWS) * GRID_W
                a0 = (rs - qr + NA_ROWS - 1) * GRID_W
                q = slice(rr * GRID_W, (rr + 1) * GRID_W)
                for h in range(NA_H):
                    if lo > 0:
                        bias_scr[h, q, 0:lo] = jnp.full((GRID_W, lo), NEG, F32)
                    bias_scr[h, q, lo:hi] = strip_ref[h, :, a0:a0 + NA_ROWS * GRID_W]
                    if hi < N_S:
                        bias_scr[h, q, hi:N_S] = jnp.full((GRID_W, N_S - hi), NEG, F32)


def _attn_lat_body(lam_ref, gd_ref, qa_ref, ka_ref, va_ref, ck_ref, cv_ref,
                   qb_ref, kb_ref, vb_ref, cnk_ref, cnv_ref, strip_ref, o_ref, bias_scr, *, lam_init):
    _fill_na_bias(pl.program_id(1), strip_ref, bias_scr)
    lam = _lambda(lam_ref[...], lam_init)
    gd = gd_ref[...]
    for h in range(DA_H):
        sl = slice(h * DA_V, (h + 1) * DA_V)
        ks = [ka_ref[:, sl], ck_ref[:, sl].astype(BF16)]
        vs = [va_ref[:, sl], cv_ref[:, sl].astype(BF16)]
        o_ref[:, sl] = _diff_head(qa_ref[:, sl], ks, vs, lam, gd, lam_init).astype(BF16)
    for h in range(NA_H):
        sl = slice(h * NA_D, (h + 1) * NA_D)
        ks = [kb_ref[:, sl], cnk_ref[:, sl].astype(BF16)]
        vs = [vb_ref[:, sl], cnv_ref[:, sl].astype(BF16)]
        o = _plain_head(qb_ref[:, sl], ks, vs, [bias_scr[h], None])
        o_ref[:, DA_W + h * NA_D:DA_W + (h + 1) * NA_D] = o.astype(BF16)


def _attn_lat(li, lam_p, g_diff, qa, ka, va, ck, cv, qb, kb, vb, cnk, cnv, strip, lam_init):
    nq = N_S // TQ
    q0 = T_P // TQ

    def qblk(w):
        return pl.BlockSpec((TQ, w), lambda b, j: (q0 + b * nq + j, 0))

    def kvblk(w):
        return pl.BlockSpec((N_S, w), lambda b, j: (b, 0))

    def cblk(w):
        return pl.BlockSpec((None, None, PAST, w), lambda b, j: (b, li, 0, 0))

    return pl.pallas_call(
        functools.partial(_attn_lat_body, lam_init=lam_init),
        grid=(NB_S, nq),
        in_specs=[pl.BlockSpec((None, 4, DA_QK), lambda b, j: (li, 0, 0)),
                  pl.BlockSpec((None, 1, DA_V), lambda b, j: (li, 0, 0)),
                  qblk(DA_W), kvblk(DA_W), kvblk(DA_W), cblk(DA_W), cblk(DA_W),
                  qblk(NA_W), kvblk(NA_W), kvblk(NA_W), cblk(NA_W), cblk(NA_W),
                  pl.BlockSpec((None, NA_H, GRID_W, (2 * NA_ROWS - 1) * GRID_W), lambda b, j: (li, 0, 0, 0))],
        out_specs=pl.BlockSpec((TQ, AB_W), lambda b, j: (b * nq + j, 0)),
        out_shape=jax.ShapeDtypeStruct((T_S, AB_W), BF16),
        scratch_shapes=[pltpu.VMEM((NA_H, TQ, N_S), F32)],
        compiler_params=_cparams(("parallel", "parallel")),
        name=f"attn_lat_l{li}",
    )(lam_p, g_diff, qa, ka, va, ck, cv, qb, kb, vb, cnk, cnv, strip)


def _na_bias_strips(table):
    w = np.arange(GRID_W)
    cs = np.clip(w - NA_COLS // 2, 0, GRID_W - NA_COLS)
    col_ok = (w[None, :] >= cs[:, None]) & (w[None, :] < cs[:, None] + NA_COLS)
    sel_c = (np.clip(w[None, :, None] - w[:, None, None] + NA_COLS - 1, 0, 2 * NA_COLS - 2)
             == np.arange(2 * NA_COLS - 1)).astype(np.float32)
    t1 = jnp.einsum('lhab,cdb->lhcad', table.astype(F32), jnp.asarray(sel_c),
                    precision=lax.Precision.HIGHEST)
    t1 = jnp.where(jnp.asarray(col_ok)[None, None, :, None, :], t1, NEG)
    return t1.reshape(DEPTH, NA_H, GRID_W, (2 * NA_ROWS - 1) * GRID_W)


def _route_t(logits_t, b_col, tm):
    per = N_EXP // N_GRP
    s = 1.0 / (1.0 + jnp.exp(-logits_t))
    sel3 = (s + b_col).reshape(N_GRP, per, tm)
    midx = lax.broadcasted_iota(jnp.int32, (N_GRP, per, tm), 1)
    gidx = lax.broadcasted_iota(jnp.int32, (N_GRP, per, tm), 0)
    ninf = -jnp.inf
    m1 = jnp.max(sel3, axis=1, keepdims=True)
    i1 = jnp.min(jnp.where(sel3 == m1, midx, per), axis=1, keepdims=True)
    m2 = jnp.max(jnp.where(midx == i1, ninf, sel3), axis=1, keepdims=True)
    gsc = jnp.broadcast_to(m1 + m2, (N_GRP, per, tm))
    cnt = jnp.zeros((N_GRP, per, tm), jnp.int32)
    for g in range(N_GRP):
        sg = gsc[g][None]
        cnt = cnt + ((sg > gsc) | ((sg == gsc) & (g < gidx))).astype(jnp.int32)
    x = jnp.where(cnt < TOPK_GRP, sel3, ninf).reshape(N_EXP, tm)
    eidx = lax.broadcasted_iota(jnp.int32, (N_EXP, tm), 0)
    chosen = jnp.zeros((N_EXP, tm), jnp.bool_)
    for _ in range(TOP_K):
        m = jnp.max(x, axis=0, keepdims=True)
        first = jnp.min(jnp.where(x == m, eidx, N_EXP), axis=0, keepdims=True)
        hit = eidx == first
        chosen = chosen | hit
        x = jnp.where(hit, ninf, x)
    w = jnp.where(chosen, s, 0.0)
    return chosen, w / jnp.sum(w, axis=0, keepdims=True) * ROUTED_SCALE


def _merge_body(oabp_ref, oabs_ref, ocp_ref, ocs_ref, xp_ref, xs_ref, mod_ref, gpost_ref, gpre_ref,
                wout_ref, wrt_ref, br_ref, tri_ref, y_ref, h_ref, gates_ref, rank_ref, pcu_ref):
    i = pl.program_id(0)
    mix = (_dot(_tok_select(i, oabp_ref, oabs_ref), wout_ref[0:AB_W, :])
           + _dot(_tok_select(i, ocp_ref, ocs_ref), wout_ref[AB_W:D, :]))
    y = _tok_select(i, xp_ref, xs_ref) + mod_ref[2:3, :] * _rms(mix, gpost_ref[...])
    y_ref[...] = y
    h = _rms(y, gpre_ref[...]) * (1.0 + mod_ref[4:5, :]) + mod_ref[3:4, :]
    h_ref[...] = h.astype(BF16)
    logits_t = lax.dot_general(wrt_ref[...], h, (((1,), (1,)), ((), ())),
                               precision=lax.Precision.HIGHEST, preferred_element_type=F32)
    chosen, gates = _route_t(logits_t, br_ref[...], TM_IN)
    gates_ref[...] = gates
    ch = jnp.where(chosen, 1.0, 0.0)
    rank = _dot(ch.astype(BF16), tri_ref[...])
    rank_ref[...] = jnp.where(chosen, rank, -1.0)
    for j in range(TM_IN // B_SORT):
        cnt = jnp.sum(ch[:, j * B_SORT:(j + 1) * B_SORT], axis=1, keepdims=True)
        units = jnp.floor((cnt + (UNIT - 1)) * (1.0 / UNIT))
        pcu_ref[j] = jnp.broadcast_to(units, (N_EXP, LANES))


def _merge(li, oabp, oabs, ocp, ocs, xp, xs, modv, g_post, g_pre, w_out_bf, w_router_t, b_router_col, tri):
    def tok(w):
        return pl.BlockSpec((TM_IN, w), lambda i: (i, 0))

    def par(*shape):
        return pl.BlockSpec((None,) + shape, lambda i: (li,) + (0,) * len(shape))

    tok_t = pl.BlockSpec((N_EXP, TM_IN), lambda i: (0, i))
    nb = TM_IN // B_SORT
    return pl.pallas_call(
        _merge_body,
        grid=(T // TM_IN,),
        in_specs=[_ctx_tok(AB_W), _lat_tok(AB_W), _ctx_tok(POOL_W), _lat_tok(POOL_W), _ctx_tok(D), _lat_tok(D),
                  pl.BlockSpec((None, None, 6, D), lambda i: (li, _mod_row(i, TM_IN), 0, 0)),
                  par(1, D), par(1, D), par(D, D), par(N_EXP, D), par(N_EXP, 1),
                  pl.BlockSpec((TM_IN, TM_IN), lambda i: (0, 0))],
        out_specs=[tok(D), tok(D), tok_t, tok_t,
                   pl.BlockSpec((nb, N_EXP, LANES), lambda i: (i, 0, 0))],
        out_shape=[jax.ShapeDtypeStruct((T, D), F32),
                   jax.ShapeDtypeStruct((T, D), BF16),
                   jax.ShapeDtypeStruct((N_EXP, T), F32),
                   jax.ShapeDtypeStruct((N_EXP, T), F32),
                   jax.ShapeDtypeStruct((N_BLK, N_EXP, LANES), F32)],
        compiler_params=_cparams(("parallel",)),
        name=f"merge_l{li}",
    )(oabp, oabs, ocp, ocs, xp, xs, modv, g_post, g_pre, w_out_bf, w_router_t, b_router_col, tri)


def _rank_matrix():
    t = np.arange(TM_IN)
    m = (t[:, None] < t[None, :]) & (t[:, None] // B_SORT == t[None, :] // B_SORT)
    return jnp.asarray(m, BF16)


def _moe_tables(pcu_f):
    pcu = pcu_f[:, :, 0].astype(jnp.int32)
    incl = jnp.cumsum(pcu, axis=1)
    uo = incl - pcu
    tot = jnp.sum(pcu, axis=0)
    gt = (tot + UPT - 1) // UPT
    gstart_t = jnp.cumsum(gt) - gt
    gstart_u = gstart_t * UPT
    steps = (gt + TPS - 1) // TPS
    sincl = jnp.cumsum(steps)
    sstart = sincl - steps
    bstart = gstart_u[None, :] + jnp.cumsum(pcu, axis=0) - pcu
    u = jnp.arange(UPB, dtype=jnp.int32)
    eou = jnp.sum((u[None, :, None] >= incl[:, None, :]).astype(jnp.int32), axis=-1)
    valid = eou < N_EXP
    onehot = eou[:, :, None] == jnp.arange(N_EXP, dtype=jnp.int32)[None, None, :]
    run_u0 = jnp.sum(jnp.where(onehot, uo[:, None, :], 0), axis=-1)
    run_g0 = jnp.sum(jnp.where(onehot, bstart[:, None, :], 0), axis=-1)
    dst = run_g0 + u[None, :] - run_u0
    spare0 = SPARE_UNIT0 + (jnp.arange(N_BLK, dtype=jnp.int32) % 2)[:, None] * (UPB + ZPB)
    dst_sort = jnp.where(valid, dst, spare0 + u[None, :])
    dst_comb = jnp.where(valid, dst, ZERO_UNIT)
    loc = jnp.where(valid, (u[None, :] - run_u0) * UNIT, -(1 << 20))
    k = jnp.arange(UPT - 1, dtype=jnp.int32)
    ztail = gt * UPT - tot
    zdst = jnp.where(k[None, :] < ztail[:, None], (gstart_u + tot)[:, None] + k[None, :], -1).reshape(N_BLK, ZPB)
    zdst = jnp.where(zdst >= 0, zdst, spare0 + UPB + jnp.arange(ZPB, dtype=jnp.int32)[None, :])
    step = jnp.arange(NS_MAX, dtype=jnp.int32)
    te = jnp.minimum(jnp.sum((step[:, None] >= sincl[None, :]).astype(jnp.int32), axis=-1), N_EXP - 1)
    mine = te[:, None] == jnp.arange(N_EXP, dtype=jnp.int32)[None, :]
    pick = lambda v: jnp.sum(jnp.where(mine, v[None, :], 0), axis=-1)
    first = gt - TPS * (steps - 1)
    j = step - pick(sstart)
    ntile = jnp.where(j == 0, pick(first), TPS)
    tile0 = pick(gstart_t) + jnp.where(j == 0, 0, pick(first) + TPS * (j - 1))
    ns = sincl[-1]
    wslot = (jnp.cumsum((j == 0).astype(jnp.int32)) - 1) % 2
    nxt = pick(sincl)
    nexte = jnp.sum(jnp.where(step[None, :] == nxt[:, None], te[None, :], 0), axis=-1)
    nexte = jnp.where(j == 0, jnp.where(nxt < ns, nexte, -1), -2)
    flat = lambda a: a.reshape(-1)
    return (flat(jnp.minimum(eou, N_EXP - 1)), flat(loc), flat(dst_sort), flat(dst_comb), flat(zdst),
            incl[:, -1], te, ntile, tile0, wslot, nexte, sincl[-1:])


def _unit_rows(u):
    if isinstance(u, int):
        return pl.ds(u * UNIT, UNIT)
    return pl.ds(pl.multiple_of(u * UNIT, UNIT), UNIT)


def _for_live_groups(n_units, fn, groups=range(UPB // UPT)):
    for g in groups:
        if (g + 1) * UPT <= MIN_UNITS:
            fn(g)
        else:
            @pl.when(g * UPT < n_units)
            def _(g=g):
                fn(g)


def _sort_body(eou_ref, loc_ref, dst_ref, zdst_ref, nun_ref, h_ref, rank_ref, gates_ref, xs_hbm, p_ref,
               xs_scr, zero_scr, sem):
    i = pl.program_id(0)

    def copy(src, d, s):
        return pltpu.make_async_copy(src, xs_hbm.at[_unit_rows(d)], sem.at[s])

    def wait_units(n, s):
        lax.fori_loop(0, n, lambda k, c: (copy(zero_scr, 0, s).wait(), c)[1], 0, unroll=True)

    def wait_block(blk, s):
        _for_live_groups(nun_ref[blk], lambda g: wait_units(UPT, s))
        wait_units(ZPB, s)

    @pl.when(i == 0)
    def _():
        zero_scr[...] = jnp.zeros((UNIT, XS_W), BF16)

    base = lax.broadcasted_iota(jnp.int32, (UNIT, B_SORT), 0)
    upc = SORT_CHUNK // UNIT
    for slot in range(2):
        b = 2 * i + slot
        tok = slice(slot * B_SORT, (slot + 1) * B_SORT)

        @pl.when(i >= 1)
        def _():
            wait_block(b - 2, slot)

        g = gates_ref[:, tok]
        g_hi = g.astype(BF16)
        g_hl = jnp.concatenate([g_hi, (g - g_hi.astype(F32)).astype(BF16)], axis=0)
        h = h_ref[tok, :]

        def start_group(g, b=b, slot=slot):
            for u in range(g * UPT, (g + 1) * UPT):
                copy(xs_scr.at[slot, _unit_rows(u)], dst_ref[b * UPB + u], slot).start()

        for c in range(RB // SORT_CHUNK):
            for u in range(c * upc, (c + 1) * upc):
                rank_row = rank_ref[pl.ds(eou_ref[b * UPB + u], 1), tok]
                match = rank_row == (base + loc_ref[b * UPB + u]).astype(F32)
                p_ref[pl.ds(slot * RB + u * UNIT, UNIT), :] = jnp.where(match, 1.0, 0.0).astype(BF16)
            rows = slice(c * SORT_CHUNK, (c + 1) * SORT_CHUNK)
            p_rows = p_ref[slot * RB + c * SORT_CHUNK:slot * RB + (c + 1) * SORT_CHUNK, :]
            xs_scr[slot, rows, 0:D] = _dot(p_rows, h).astype(BF16)
            xs_scr[slot, rows, D:XS_W] = _dot_nt(p_rows, g_hl).astype(BF16)
            _for_live_groups(nun_ref[b], start_group, range(c * upc // UPT, (c + 1) * upc // UPT))
        for j in range(ZPB):
            copy(zero_scr, zdst_ref[b * ZPB + j], slot).start()

    @pl.when(i == N_BLK // 2 - 1)
    def _():
        wait_block(2 * i, 0)
        wait_block(2 * i + 1, 1)


def _moe_sort(li, eou, loc, dst, zdst, nun, h, rank_t, gates_t):
    blk_t = pl.BlockSpec((N_EXP, 2 * B_SORT), lambda i, *_: (0, i))
    return pl.pallas_call(
        _sort_body,
        grid_spec=pltpu.PrefetchScalarGridSpec(
            num_scalar_prefetch=5,
            grid=(N_BLK // 2,),
            in_specs=[pl.BlockSpec((2 * B_SORT, D), lambda i, *_: (i, 0)), blk_t, blk_t],
            out_specs=[pl.BlockSpec(memory_space=pl.ANY),
                       pl.BlockSpec((2 * RB, B_SORT), lambda i, *_: (i, 0))],
            scratch_shapes=[pltpu.VMEM((2, RB, XS_W), BF16),
                            pltpu.VMEM((UNIT, XS_W), BF16),
                            pltpu.SemaphoreType.DMA((2,))]),
        out_shape=[jax.ShapeDtypeStruct((R_SORT, XS_W), BF16),
                   jax.ShapeDtypeStruct((N_BLK * RB, B_SORT), BF16)],
        compiler_params=_cparams(("arbitrary",)),
        name=f"moe_sort_l{li}",
    )(eou, loc, dst, zdst, nun, h, rank_t, gates_t)


def _experts_body(te_ref, ntile_ref, tile0_ref, wslot_ref, nexte_ref, ns_ref,
                  xs_hbm, wg_hbm, wu_hbm, wd_hbm, os_hbm,
                  xs_buf, o_buf, wg_f, wu_f, wd_f, sem_in, sem_out, sem_w, *, li):
    ns = ns_ref[0]

    def tiles_of(step):
        return jnp.where((step >= 0) & (step < ns), ntile_ref[jnp.clip(step, 0, NS_MAX - 1)], 0)

    def tile_rows(step, t):
        return pl.ds(pl.multiple_of((tile0_ref[jnp.clip(step, 0, NS_MAX - 1)] + t) * TMG, TMG), TMG)

    def buf_rows(t):
        return pl.ds(pl.multiple_of(t * TMG, TMG), TMG)

    def copy_in(step, t, s):
        return pltpu.make_async_copy(xs_hbm.at[tile_rows(step, t)], xs_buf.at[s, buf_rows(t)], sem_in.at[s])

    def copy_out(step, t, s):
        return pltpu.make_async_copy(o_buf.at[s, buf_rows(t)], os_hbm.at[tile_rows(step, t)], sem_out.at[s])

    def for_tiles(step, fn):
        lax.fori_loop(0, tiles_of(step), lambda t, c: (fn(t), c)[1], 0)

    def weight_copies(e, s):
        return [pltpu.make_async_copy(hbm.at[li, e], buf.at[s], sem_w.at[s])
                for hbm, buf in ((wg_hbm, wg_f), (wu_hbm, wu_f), (wd_hbm, wd_f))]

    for c in weight_copies(te_ref[0], 0):
        c.start()
    for_tiles(0, lambda t: copy_in(0, t, 0).start())

    def step(i, carry):
        slot = i % 2
        for_tiles(i + 1, lambda t: copy_in(i + 1, t, 1 - slot).start())
        for_tiles(i - 2, lambda t: copy_out(i - 2, t, slot).wait())
        e = te_ref[i]
        ws = wslot_ref[i]

        @pl.when(nexte_ref[i] >= -1)
        def _():
            for c in weight_copies(e, ws):
                c.wait()

            @pl.when(nexte_ref[i] >= 0)
            def _():
                for c in weight_copies(nexte_ref[i], 1 - ws):
                    c.start()

        ntile = ntile_ref[i]
        for_tiles(i, lambda t: copy_in(i, t, slot).wait())
        for k in range(1, TPS + 1):
            rows = k * TMG

            @pl.when(ntile == k)
            def _():
                x = xs_buf[slot, 0:rows, 0:D]
                lane = lax.broadcasted_iota(jnp.int32, (rows, LANES), 1)
                mine = (lane == e) | (lane == e + N_EXP)
                gate = jnp.sum(jnp.where(mine, xs_buf[slot, 0:rows, D:XS_W].astype(F32), 0.0),
                               axis=1, keepdims=True)
                hid = (_silu(_dot(x, wg_f[ws].astype(BF16))) * _dot(x, wu_f[ws].astype(BF16))) * gate
                o_buf[slot, 0:rows, :] = _dot(hid.astype(BF16), wd_f[ws].astype(BF16)).astype(BF16)

        for_tiles(i, lambda t: copy_out(i, t, slot).start())
        return carry

    lax.fori_loop(0, ns, step, 0)
    for_tiles(ns - 2, lambda t: copy_out(ns - 2, t, ns % 2).wait())
    for_tiles(ns - 1, lambda t: copy_out(ns - 1, t, (ns - 1) % 2).wait())
    o_buf[0, 0:UNIT, :] = jnp.zeros((UNIT, D), BF16)
    zero = pltpu.make_async_copy(o_buf.at[0, 0:UNIT], os_hbm.at[_unit_rows(ZERO_UNIT)], sem_out.at[0])
    zero.start()
    zero.wait()


def _moe_experts(li, te, ntile, tile0, wslot, nexte, ns, xs, w_gate, w_up, w_down):
    return pl.pallas_call(
        functools.partial(_experts_body, li=li),
        grid_spec=pltpu.PrefetchScalarGridSpec(
            num_scalar_prefetch=6,
            grid=(1,),
            in_specs=[pl.BlockSpec(memory_space=pl.ANY)] * 4,
            out_specs=pl.BlockSpec(memory_space=pl.ANY),
            scratch_shapes=[pltpu.VMEM((2, STEP_ROWS, XS_W), BF16),
                            pltpu.VMEM((2, STEP_ROWS, D), BF16),
                            pltpu.VMEM((2, D, EXP_DIM), F32),
                            pltpu.VMEM((2, D, EXP_DIM), F32),
                            pltpu.VMEM((2, EXP_DIM, D), F32),
                            pltpu.SemaphoreType.DMA((2,)),
                            pltpu.SemaphoreType.DMA((2,)),
                            pltpu.SemaphoreType.DMA((2,))]),
        out_shape=jax.ShapeDtypeStruct((R_OUT, D), BF16),
        compiler_params=_cparams(("arbitrary",)),
        name=f"moe_experts_l{li}",
    )(te, ntile, tile0, wslot, nexte, ns, xs, w_gate, w_up, w_down)


def _combine_body(dst_ref, nun_ref, h_ref, p_ref, y_ref, mod_ref, gpost_ref,
                  wsg_ref, wsu_ref, wsd_ref, os_hbm, op_ref, ol_ref, os_scr, sem):
    b = pl.program_id(0)
    slot = b % 2

    def copy(d, u, s):
        return pltpu.make_async_copy(os_hbm.at[_unit_rows(d)], os_scr.at[s, _unit_rows(u)], sem.at[s])

    def fetch(blk, s):
        def group(g):
            lax.fori_loop(g * UPT, (g + 1) * UPT,
                          lambda u, c: (copy(dst_ref[blk * UPB + u], u, s).start(), c)[1], 0, unroll=UNROLL)

        _for_live_groups(nun_ref[blk], group)

    @pl.when(b == 0)
    def _():
        os_scr[:, MIN_UNITS * UNIT:RB, :] = jnp.zeros((2, RB - MIN_UNITS * UNIT, D), BF16)
        fetch(0, 0)

    @pl.when(b + 1 < N_BLK)
    def _():
        fetch(b + 1, 1 - slot)

    h = h_ref[...]
    shared = _dot((_silu(_dot(h, wsg_ref[...])) * _dot(h, wsu_ref[...])).astype(BF16), wsd_ref[...])
    _for_live_groups(nun_ref[b], lambda g: lax.fori_loop(
        0, UPT, lambda k, c: (copy(0, 0, slot).wait(), c)[1], 0, unroll=True))
    routed = lax.dot_general(os_scr[slot], p_ref[...], (((0,), (0,)), ((), ())),
                             preferred_element_type=F32).T
    out = y_ref[...] + mod_ref[5:6, :] * _rms(routed + shared, gpost_ref[...])

    @pl.when(b < T_P // B_SORT)
    def _():
        op_ref[...] = out

    @pl.when(b >= T_P // B_SORT)
    def _():
        ol_ref[...] = out


def _moe_combine(li, dst, nun, h, p, y, modv, g_post, wsg_bf, wsu_bf, wsd_bf, os):
    nbp = T_P // B_SORT
    def tok(w):
        return pl.BlockSpec((B_SORT, w), lambda b, *_: (b, 0))

    def par(*shape):
        return pl.BlockSpec((None,) + shape, lambda b, *_: (li,) + (0,) * len(shape))

    return pl.pallas_call(
        _combine_body,
        grid_spec=pltpu.PrefetchScalarGridSpec(
            num_scalar_prefetch=2,
            grid=(N_BLK,),
            in_specs=[tok(D), pl.BlockSpec((RB, B_SORT), lambda b, *_: (b, 0)), tok(D),
                      pl.BlockSpec((None, None, 6, D), lambda b, *_: (li, _mod_row(b, B_SORT), 0, 0)),
                      par(1, D), par(D, EXP_DIM), par(D, EXP_DIM), par(EXP_DIM, D),
                      pl.BlockSpec(memory_space=pl.ANY)],
            out_specs=[pl.BlockSpec((B_SORT, D), lambda b, *_: (jnp.minimum(b, nbp - 1), 0)),
                       pl.BlockSpec((B_SORT, D), lambda b, *_: (jnp.maximum(b - nbp, 0), 0))],
            scratch_shapes=[pltpu.VMEM((2, RB, D), BF16),
                            pltpu.SemaphoreType.DMA((2,))]),
        out_shape=[jax.ShapeDtypeStruct((T_P, D), F32), jax.ShapeDtypeStruct((T_S, D), F32)],
        compiler_params=_cparams(("arbitrary",)),
        name=f"moe_combine_l{li}",
    )(dst, nun, h, p, y, modv, g_post, wsg_bf, wsu_bf, wsd_bf, os)


def _block_diag(w):
    out = jnp.zeros((DEPTH, POOL_W, POOL_W), F32)
    for g in range(len(POOL_WINDOWS)):
        out = out.at[:, g * POOL_CH:(g + 1) * POOL_CH, g * POOL_CH:(g + 1) * POOL_CH].set(w[:, g])
    return out


def kernel(x_prompt, x_sample, cache_diff_k, cache_diff_v, cache_na_k, cache_na_v, c, c_ctx,
           w_ada, b_ada, g_pre_mix, g_post_mix, g_pre_ffn, g_post_ffn, w_in, w_out,
           diff_lambda, g_diff, na_bias, pool_w, pool_scale, w_router, b_router,
           w_gate, w_up, w_down, ws_gate, ws_up, ws_down):
    xp, xs = x_prompt.reshape(T_P, D), x_sample.reshape(T_S, D)
    cvec = jnp.concatenate([c_ctx[None, :], c, jnp.zeros((3, D), F32)], axis=0)
    modv = _modulation(cvec, w_ada, b_ada)[:, :1 + NB_S].reshape(DEPTH, 1 + NB_S, 6, D)

    cos_t, sin_t = _rope_tables()
    w_in_bf = w_in.astype(BF16)
    w_out_bf = w_out.astype(BF16)
    wsg_bf, wsu_bf, wsd_bf = ws_gate.astype(BF16), ws_up.astype(BF16), ws_down.astype(BF16)
    pool_bd = _block_diag(pool_w)
    row = lambda a: a.reshape(DEPTH, 1, a.shape[-1])
    g_pre_mix, g_post_mix, g_pre_ffn, g_post_ffn = map(row, (g_pre_mix, g_post_mix, g_pre_ffn, g_post_ffn))
    g_diff, pool_scale = row(g_diff), row(pool_scale)
    w_router_t = jnp.swapaxes(w_router, 1, 2)
    b_router_col = b_router.reshape(DEPTH, N_EXP, 1)
    tri = _rank_matrix()
    ck = cache_diff_k.reshape(NB_S, DEPTH, PAST, DA_W)
    cv = cache_diff_v.reshape(NB_S, DEPTH, PAST, DA_W)
    cnk = cache_na_k.reshape(NB_S, DEPTH, PAST, NA_W)
    cnv = cache_na_v.reshape(NB_S, DEPTH, PAST, NA_W)

    strips = _na_bias_strips(na_bias)

    caches = ()
    for li in range(DEPTH):
        lam_init = 0.8 - 0.6 * math.exp(-0.3 * li)
        qa, qb, ocp, up, ka, va, kb, vb, *caches = _in_proj(li, xp, xs, modv, g_pre_mix, w_in_bf, cos_t, sin_t,
                                                           pool_bd, pool_scale, caches)
        ocs = _pool_lat(li, up, pool_bd, pool_scale)
        oabp = _attn_ctx(li, diff_lambda, g_diff, qa, caches[0], caches[1], qb, caches[2], caches[3], lam_init)
        oabs = _attn_lat(li, diff_lambda, g_diff, qa, ka, va, ck, cv, qb, kb, vb, cnk, cnv, strips, lam_init)
        y, h, gates_t, rank_t, pcu = _merge(li, oabp, oabs, ocp, ocs, xp, xs, modv, g_post_mix, g_pre_ffn,
                                            w_out_bf, w_router_t, b_router_col, tri)
        eou, loc, dst_sort, dst_comb, zdst, nun, te, ntile, tile0, wslot, nexte, ns = _moe_tables(pcu)
        srt, p = _moe_sort(li, eou, loc, dst_sort, zdst, nun, h, rank_t, gates_t)
        os = _moe_experts(li, te, ntile, tile0, wslot, nexte, ns, srt, w_gate, w_up, w_down)
        xp, xs = _moe_combine(li, dst_comb, nun, h, p, y, modv, g_post_ffn, wsg_bf, wsu_bf, wsd_bf, os)

    new_dk, new_dv, new_nk, new_nv = caches
    return (xp.reshape(NB_P, N_P, D), xs.reshape(NB_S, N_S, D),
            new_dk.reshape(NB_P, DEPTH, N_P, DA_H, 2 * DA_QK), new_dv.reshape(NB_P, DEPTH, N_P, DA_H, DA_V),
            new_nk.reshape(NB_P, DEPTH, N_P, NA_H, NA_D), new_nv.reshape(NB_P, DEPTH, N_P, NA_H, NA_D))
```

```python
import functools
import math

import numpy as np
import jax
import jax.numpy as jnp
from jax import lax
from jax.experimental import pallas as pl
from jax.experimental.pallas import tpu as pltpu

F32 = jnp.float32
BF16 = jnp.bfloat16

D = 1024
DEPTH = 2
NB_P, N_P = 16, 256
NB_S, N_S = 4, 1024
T_P = NB_P * N_P
T_S = NB_S * N_S
T = T_P + T_S
PAST = 512
GRID_W = 64
DA_H, DA_QK, DA_V = 4, 64, 128
DA_W = DA_H * DA_V
NA_H, NA_D = 4, 64
NA_W = NA_H * NA_D
NA_ROWS, NA_COLS = 8, 16
POOL_WINDOWS = (2, 4, 8, 16)
POOL_CH = 64
POOL_W = 256
IN_COLS = 3 * DA_W + 3 * NA_W + POOL_W
AB_W = DA_W + NA_W
N_EXP, TOP_K, N_GRP, TOPK_GRP = 64, 8, 8, 4
EXP_DIM = 256
ROUTED_SCALE = 2.5
EPS = 1e-6
NEG = -1e30
ROPE_THETA = 10000.0

LANES = 128
VMEM_LIMIT = 56 * 1024 * 1024

TM_IN = 512
NT_P = T_P // TM_IN
TQ = 512

B_SORT = 256
UNIT = 16
N_BLK = T // B_SORT
SORT_CHUNK = 1024
RUN_PAD_ROWS = TOP_K * B_SORT + N_EXP * (UNIT - 1)
RB = -(-RUN_PAD_ROWS // SORT_CHUNK) * SORT_CHUNK
UPB = RB // UNIT
MIN_UNITS = TOP_K * B_SORT // UNIT
TMG = 256
UPT = TMG // UNIT
NT_MAX = (N_BLK * RUN_PAD_ROWS) // TMG + N_EXP
XS_W = D + LANES
EXP_PER_BLK = N_EXP // N_BLK
ZPB = EXP_PER_BLK * (UPT - 1)
TPS = 6
STEP_ROWS = TPS * TMG
NS_MAX = NT_MAX // TPS + (N_EXP * (TPS - 1)) // TPS
ZERO_UNIT = NT_MAX * UPT
SPARE_UNIT0 = ZERO_UNIT + UPT
R_OUT = (NT_MAX + 1) * TMG
R_SORT = R_OUT + -(-2 * (UPB + ZPB) // UPT) * TMG
UNROLL = 8


def _cparams(sem):
    return pltpu.CompilerParams(dimension_semantics=sem, vmem_limit_bytes=VMEM_LIMIT)


def _rms(x, g):
    return x * lax.rsqrt(jnp.mean(x * x, axis=-1, keepdims=True) + EPS) * g


def _dot(a, b):
    return jnp.dot(a, b, preferred_element_type=F32)


def _dot_nt(a, b):
    return lax.dot_general(a, b, (((1,), (1,)), ((), ())), preferred_element_type=F32)


def _silu(x):
    return x / (1.0 + jnp.exp(-x))


def _mod_row(i, tm):
    off = i * tm - T_P
    return jnp.where(off >= 0, 1 + jnp.maximum(off, 0) // N_S, 0)


def _mod_body(c_ref, w_ref, b_ref, o_ref):
    c = c_ref[...]
    o_ref[...] = jnp.dot(_silu(c), w_ref[...], precision=lax.Precision.HIGHEST,
                         preferred_element_type=F32) + b_ref[...]


def _modulation(cvec, w_ada, b_ada):
    tn = 1536
    return pl.pallas_call(
        _mod_body,
        grid=(DEPTH, 6 * D // tn),
        in_specs=[pl.BlockSpec((8, D), lambda l, j: (0, 0)),
                  pl.BlockSpec((None, D, tn), lambda l, j: (l, 0, j)),
                  pl.BlockSpec((None, 1, tn), lambda l, j: (l, 0, j))],
        out_specs=pl.BlockSpec((None, 8, tn), lambda l, j: (l, 0, j)),
        out_shape=jax.ShapeDtypeStruct((DEPTH, 8, 6 * D), F32),
        compiler_params=_cparams(("parallel", "parallel")),
        name="modulation",
    )(cvec, w_ada, b_ada.reshape(DEPTH, 1, 6 * D))


def _rope_tables():
    nf = DA_QK // 4
    t = np.arange(N_S)
    pos = np.stack([t // GRID_W, t % GRID_W], axis=-1).astype(np.float32)
    inv = np.power(np.float32(ROPE_THETA), -np.arange(nf, dtype=np.float32) / nf)
    ang = pos[:, :, None] * inv
    cos = np.cos(ang)
    sin = np.sin(ang)
    cos64 = np.concatenate([cos[:, 0], cos[:, 0], cos[:, 1], cos[:, 1]], axis=-1)
    sin64 = np.concatenate([-sin[:, 0], sin[:, 0], -sin[:, 1], sin[:, 1]], axis=-1)
    reps = DA_W // DA_QK
    cos_t = np.concatenate([np.tile(cos64, (1, reps)), np.ones((TM_IN, DA_W), np.float32)], axis=0)
    sin_t = np.concatenate([np.tile(sin64, (1, reps)), np.zeros((TM_IN, DA_W), np.float32)], axis=0)
    return jnp.asarray(cos_t, F32), jnp.asarray(sin_t, F32)


def _tok_select(i, ctx_ref, lat_ref):
    return jnp.where(i < NT_P, ctx_ref[...], lat_ref[...])


def _in_proj_body(*refs):
    (xp_ref, xs_ref, mod_ref, g_ref, w_ref, cos_ref, sin_ref, pw_ref, ps_ref) = refs[:9]
    (qa_ref, qb_ref, ocp_ref, up_ref, ka_ref, va_ref, kb_ref, vb_ref,
     ck_ref, cv_ref, cnk_ref, cnv_ref) = refs[-12:]
    i = pl.program_id(0)
    x = _tok_select(i, xp_ref, xs_ref)
    h = (_rms(x, g_ref[...]) * (1.0 + mod_ref[1:2, :]) + mod_ref[0:1, :]).astype(BF16)
    cos = cos_ref[...]
    sin = sin_ref[...]
    lane = lax.broadcasted_iota(jnp.int32, (TM_IN, DA_W), 1)
    first = (lane % 32) < 16

    def proj(lo, hi):
        return _dot(h, w_ref[:, lo:hi])

    def rope(t):
        swapped = jnp.where(first, pltpu.roll(t, DA_W - 16, 1), pltpu.roll(t, 16, 1))
        return t * cos + swapped * sin

    o = 0
    qa_ref[...] = (rope(proj(o, o + DA_W)) * (DA_QK ** -0.5)).astype(BF16)
    o += DA_W
    ka = rope(proj(o, o + DA_W))
    o += DA_W
    va = proj(o, o + DA_W)
    o += DA_W
    qb_ref[...] = (proj(o, o + NA_W) * (NA_D ** -0.5)).astype(BF16)
    o += NA_W
    kb = proj(o, o + NA_W)
    o += NA_W
    vb = proj(o, o + NA_W)
    o += NA_W
    up = proj(o, o + POOL_W)

    @pl.when(i < NT_P)
    def _():
        for ref, val in ((ck_ref, ka), (cv_ref, va), (cnk_ref, kb), (cnv_ref, vb)):
            ref[...] = val.reshape(ref.shape)
        ocp_ref[...] = _pool_rows(up, N_P, pw_ref[...], ps_ref[...])

    @pl.when(i >= NT_P)
    def _():
        for ref, val in ((ka_ref, ka), (va_ref, va), (kb_ref, kb), (vb_ref, vb)):
            ref[...] = val.astype(BF16)
        up_ref[...] = up


def _ctx_tok(w):
    return pl.BlockSpec((TM_IN, w), lambda i: (jnp.minimum(i, NT_P - 1), 0))


def _lat_tok(w):
    return pl.BlockSpec((TM_IN, w), lambda i: (jnp.maximum(i - NT_P, 0), 0))


def _in_proj(li, xp, xs, modv, g_pre, w_in_bf, cos_t, sin_t, pool_bd, pool_scale, caches):
    n_pos_blk = N_S // TM_IN

    def pos_blk(i):
        return jnp.where(i >= NT_P, jnp.maximum(i - NT_P, 0) % n_pos_blk, n_pos_blk)

    def tok(w):
        return pl.BlockSpec((TM_IN, w), lambda i: (i, 0))

    def cache(w):
        return pl.BlockSpec((TM_IN // N_P, None, N_P, w), lambda i: (jnp.minimum(i, NT_P - 1), li, 0, 0))

    widths = (DA_W, DA_W, NA_W, NA_W)
    n_in, n_plain_out = 9, 8
    return pl.pallas_call(
        _in_proj_body,
        grid=(T // TM_IN,),
        in_specs=[_ctx_tok(D), _lat_tok(D),
                  pl.BlockSpec((None, None, 6, D), lambda i: (li, _mod_row(i, TM_IN), 0, 0)),
                  pl.BlockSpec((None, 1, D), lambda i: (li, 0, 0)),
                  pl.BlockSpec((None, D, IN_COLS), lambda i: (li, 0, 0)),
                  pl.BlockSpec((TM_IN, DA_W), lambda i: (pos_blk(i), 0)),
                  pl.BlockSpec((TM_IN, DA_W), lambda i: (pos_blk(i), 0)),
                  pl.BlockSpec((None, POOL_W, POOL_W), lambda i: (li, 0, 0)),
                  pl.BlockSpec((None, 1, POOL_W), lambda i: (li, 0, 0))]
                 + [pl.BlockSpec(memory_space=pl.ANY)] * len(caches),
        out_specs=[tok(DA_W), tok(NA_W), _ctx_tok(POOL_W), _lat_tok(POOL_W)] + [_lat_tok(w) for w in widths]
                  + [cache(w) for w in widths],
        out_shape=[jax.ShapeDtypeStruct((T, DA_W), BF16),
                   jax.ShapeDtypeStruct((T, NA_W), BF16),
                   jax.ShapeDtypeStruct((T_P, POOL_W), BF16),
                   jax.ShapeDtypeStruct((T_S, POOL_W), F32)]
                  + [jax.ShapeDtypeStruct((T_S, w), BF16) for w in widths]
                  + [jax.ShapeDtypeStruct((NB_P, DEPTH, N_P, w), F32) for w in widths],
        input_output_aliases={n_in + k: n_plain_out + k for k in range(len(caches))},
        compiler_params=_cparams(("arbitrary",)),
        name=f"in_proj_l{li}",
    )(xp, xs, modv, g_pre, w_in_bf, cos_t, sin_t, pool_bd, pool_scale, *caches)


def _pool_rows(u, n, w_mix, scale):
    rows = u.shape[0]
    row = lax.broadcasted_iota(jnp.int32, (rows, POOL_W), 0) % n
    lane = lax.broadcasted_iota(jnp.int32, (rows, POOL_W), 1)

    def shift_dn(a, k):
        return jnp.where(row >= k, pltpu.roll(a, k, 0), 0.0)

    def shift_up(a, k):
        return jnp.where(row < n - k, pltpu.roll(a, rows - k, 0), 0.0)

    fwd = u
    bwd = shift_dn(u, 1)
    mean = jnp.zeros_like(u)
    k = 1
    for gi, w in enumerate(POOL_WINDOWS):
        while k < w // 2:
            fwd = fwd + shift_up(fwd, k)
            bwd = bwd + shift_dn(bwd, k)
            k *= 2
        cnt = (jnp.minimum(row + w // 2, n) - jnp.maximum(row - w // 2, 0)).astype(F32)
        in_group = (lane >= gi * POOL_CH) & (lane < (gi + 1) * POOL_CH)
        mean = jnp.where(in_group, (fwd + bwd) / cnt, mean)
    pooled = mean - u
    mixed = jnp.dot(pooled, w_mix, precision=lax.Precision.HIGHEST, preferred_element_type=F32)
    return (mixed * scale).astype(BF16)


def _pool_body(u_ref, w_ref, s_ref, o_ref):
    o_ref[...] = _pool_rows(u_ref[...], N_S, w_ref[...], s_ref[...])


def _pool_lat(li, up, w_bd, scale):
    return pl.pallas_call(
        _pool_body,
        grid=(NB_S,),
        in_specs=[pl.BlockSpec((N_S, POOL_W), lambda i: (i, 0)),
                  pl.BlockSpec((None, POOL_W, POOL_W), lambda i: (li, 0, 0)),
                  pl.BlockSpec((None, 1, POOL_W), lambda i: (li, 0, 0))],
        out_specs=pl.BlockSpec((N_S, POOL_W), lambda i: (i, 0)),
        out_shape=jax.ShapeDtypeStruct((T_S, POOL_W), BF16),
        compiler_params=_cparams(("parallel",)),
        name=f"pool_lat_l{li}",
    )(up, w_bd, scale)


def _lambda(lp, lam_init):
    a = jnp.sum(lp[0:1, :] * lp[1:2, :], axis=1, keepdims=True)
    b = jnp.sum(lp[2:3, :] * lp[3:4, :], axis=1, keepdims=True)
    return jnp.exp(a) - jnp.exp(b) + lam_init


def _softmax_av(scores, vs):
    m = functools.reduce(jnp.maximum, [jnp.max(s, axis=-1, keepdims=True) for s in scores])
    es = [jnp.exp(s - m) for s in scores]
    den = functools.reduce(jnp.add, [jnp.sum(e, axis=-1, keepdims=True) for e in es])
    o = functools.reduce(jnp.add, [_dot(e.astype(BF16), v) for e, v in zip(es, vs)])
    return o * (1.0 / den)


def _diff_head(q, ks, vs, lam, gd, lam_init):
    o1 = _softmax_av([_dot_nt(q[:, :DA_QK], k[:, :DA_QK]) for k in ks], vs)
    o2 = _softmax_av([_dot_nt(q[:, DA_QK:], k[:, DA_QK:]) for k in ks], vs)
    return _rms(o1 - lam * o2, gd) * (1.0 - lam_init)


def _plain_head(q, ks, vs, biases):
    scores = []
    for k, bias in zip(ks, biases):
        s = _dot_nt(q, k)
        scores.append(s if bias is None else s + bias)
    return _softmax_av(scores, vs)


def _attn_ctx_body(lam_ref, gd_ref, qa_ref, ka_ref, va_ref, qb_ref, kb_ref, vb_ref, o_ref, *, lam_init):
    lam = _lambda(lam_ref[...], lam_init)
    gd = gd_ref[...]
    for h in range(DA_H):
        sl = slice(h * DA_V, (h + 1) * DA_V)
        o = _diff_head(qa_ref[:, sl], [ka_ref[:, sl].astype(BF16)], [va_ref[:, sl].astype(BF16)],
                       lam, gd, lam_init)
        o_ref[:, sl] = o.astype(BF16)
    for h in range(NA_H):
        sl = slice(h * NA_D, (h + 1) * NA_D)
        o = _plain_head(qb_ref[:, sl], [kb_ref[:, sl].astype(BF16)], [vb_ref[:, sl].astype(BF16)], [None])
        o_ref[:, DA_W + h * NA_D:DA_W + (h + 1) * NA_D] = o.astype(BF16)


def _attn_ctx(li, lam_p, g_diff, qa, ka, va, qb, kb, vb, lam_init):
    def blk(w):
        return pl.BlockSpec((N_P, w), lambda b: (b, 0))

    def cblk(w):
        return pl.BlockSpec((None, None, N_P, w), lambda b: (b, li, 0, 0))

    return pl.pallas_call(
        functools.partial(_attn_ctx_body, lam_init=lam_init),
        grid=(NB_P,),
        in_specs=[pl.BlockSpec((None, 4, DA_QK), lambda b: (li, 0, 0)),
                  pl.BlockSpec((None, 1, DA_V), lambda b: (li, 0, 0)),
                  blk(DA_W), cblk(DA_W), cblk(DA_W), blk(NA_W), cblk(NA_W), cblk(NA_W)],
        out_specs=pl.BlockSpec((N_P, AB_W), lambda b: (b, 0)),
        out_shape=jax.ShapeDtypeStruct((T_P, AB_W), BF16),
        compiler_params=_cparams(("parallel",)),
        name=f"attn_ctx_l{li}",
    )(lam_p, g_diff, qa, ka, va, qb, kb, vb)


G_ROWS = N_S // GRID_W
QR_PER_BLK = TQ // GRID_W


def _fill_na_bias(j, strip_ref, bias_scr):
    for jj in range(N_S // TQ):
        @pl.when(j == jj)
        def _():
            for rr in range(QR_PER_BLK):
                qr = jj * QR_PER_BLK + rr
                rs = min(max(qr - NA_ROWS // 2, 0), G_ROWS - NA_ROWS)
                lo, hi = rs * GRID_W, (rs + NA_ROWS) * GRID_W
                a0 = (rs - qr + NA_ROWS - 1) * GRID_W
                q = slice(rr * GRID_W, (rr + 1) * GRID_W)
                for h in range(NA_H):
                    if lo > 0:
                        bias_scr[h, q, 0:lo] = jnp.full((GRID_W, lo), NEG, F32)
                    bias_scr[h, q, lo:hi] = strip_ref[h, :, a0:a0 + NA_ROWS * GRID_W]
                    if hi < N_S:
                        bias_scr[h, q, hi:N_S] = jnp.full((GRID_W, N_S - hi), NEG, F32)


def _attn_lat_body(lam_ref, gd_ref, qa_ref, ka_ref, va_ref, ck_ref, cv_ref,
                   qb_ref, kb_ref, vb_ref, cnk_ref, cnv_ref, strip_ref, o_ref, bias_scr, *, lam_init):
    _fill_na_bias(pl.program_id(1), strip_ref, bias_scr)
    lam = _lambda(lam_ref[...], lam_init)
    gd = gd_ref[...]
    for h in range(DA_H):
        sl = slice(h * DA_V, (h + 1) * DA_V)
        ks = [ka_ref[:, sl], ck_ref[:, sl].astype(BF16)]
        vs = [va_ref[:, sl], cv_ref[:, sl].astype(BF16)]
        o_ref[:, sl] = _diff_head(qa_ref[:, sl], ks, vs, lam, gd, lam_init).astype(BF16)
    for h in range(NA_H):
        sl = slice(h * NA_D, (h + 1) * NA_D)
        ks = [kb_ref[:, sl], cnk_ref[:, sl].astype(BF16)]
        vs = [vb_ref[:, sl], cnv_ref[:, sl].astype(BF16)]
        o = _plain_head(qb_ref[:, sl], ks, vs, [bias_scr[h], None])
        o_ref[:, DA_W + h * NA_D:DA_W + (h + 1) * NA_D] = o.astype(BF16)


def _attn_lat(li, lam_p, g_diff, qa, ka, va, ck, cv, qb, kb, vb, cnk, cnv, strip, lam_init):
    nq = N_S // TQ
    q0 = T_P // TQ

    def qblk(w):
        return pl.BlockSpec((TQ, w), lambda b, j: (q0 + b * nq + j, 0))

    def kvblk(w):
        return pl.BlockSpec((N_S, w), lambda b, j: (b, 0))

    def cblk(w):
        return pl.BlockSpec((None, None, PAST, w), lambda b, j: (b, li, 0, 0))

    return pl.pallas_call(
        functools.partial(_attn_lat_body, lam_init=lam_init),
        grid=(NB_S, nq),
        in_specs=[pl.BlockSpec((None, 4, DA_QK), lambda b, j: (li, 0, 0)),
                  pl.BlockSpec((None, 1, DA_V), lambda b, j: (li, 0, 0)),
                  qblk(DA_W), kvblk(DA_W), kvblk(DA_W), cblk(DA_W), cblk(DA_W),
                  qblk(NA_W), kvblk(NA_W), kvblk(NA_W), cblk(NA_W), cblk(NA_W),
                  pl.BlockSpec((None, NA_H, GRID_W, (2 * NA_ROWS - 1) * GRID_W), lambda b, j: (li, 0, 0, 0))],
        out_specs=pl.BlockSpec((TQ, AB_W), lambda b, j: (b * nq + j, 0)),
        out_shape=jax.ShapeDtypeStruct((T_S, AB_W), BF16),
        scratch_shapes=[pltpu.VMEM((NA_H, TQ, N_S), F32)],
        compiler_params=_cparams(("parallel", "parallel")),
        name=f"attn_lat_l{li}",
    )(lam_p, g_diff, qa, ka, va, ck, cv, qb, kb, vb, cnk, cnv, strip)


def _na_bias_strips(table):
    w = np.arange(GRID_W)
    cs = np.clip(w - NA_COLS // 2, 0, GRID_W - NA_COLS)
    col_ok = (w[None, :] >= cs[:, None]) & (w[None, :] < cs[:, None] + NA_COLS)
    sel_c = (np.clip(w[None, :, None] - w[:, None, None] + NA_COLS - 1, 0, 2 * NA_COLS - 2)
             == np.arange(2 * NA_COLS - 1)).astype(np.float32)
    t1 = jnp.einsum('lhab,cdb->lhcad', table.astype(F32), jnp.asarray(sel_c),
                    precision=lax.Precision.HIGHEST)
    t1 = jnp.where(jnp.asarray(col_ok)[None, None, :, None, :], t1, NEG)
    return t1.reshape(DEPTH, NA_H, GRID_W, (2 * NA_ROWS - 1) * GRID_W)


def _route_t(logits_t, b_col, tm):
    per = N_EXP // N_GRP
    s = 1.0 / (1.0 + jnp.exp(-logits_t))
    sel3 = (s + b_col).reshape(N_GRP, per, tm)
    midx = lax.broadcasted_iota(jnp.int32, (N_GRP, per, tm), 1)
    gidx = lax.broadcasted_iota(jnp.int32, (N_GRP, per, tm), 0)
    ninf = -jnp.inf
    m1 = jnp.max(sel3, axis=1, keepdims=True)
    i1 = jnp.min(jnp.where(sel3 == m1, midx, per), axis=1, keepdims=True)
    m2 = jnp.max(jnp.where(midx == i1, ninf, sel3), axis=1, keepdims=True)
    gsc = jnp.broadcast_to(m1 + m2, (N_GRP, per, tm))
    cnt = jnp.zeros((N_GRP, per, tm), jnp.int32)
    for g in range(N_GRP):
        sg = gsc[g][None]
        cnt = cnt + ((sg > gsc) | ((sg == gsc) & (g < gidx))).astype(jnp.int32)
    x = jnp.where(cnt < TOPK_GRP, sel3, ninf).reshape(N_EXP, tm)
    eidx = lax.broadcasted_iota(jnp.int32, (N_EXP, tm), 0)
    chosen = jnp.zeros((N_EXP, tm), jnp.bool_)
    for _ in range(TOP_K):
        m = jnp.max(x, axis=0, keepdims=True)
        first = jnp.min(jnp.where(x == m, eidx, N_EXP), axis=0, keepdims=True)
        hit = eidx == first
        chosen = chosen | hit
        x = jnp.where(hit, ninf, x)
    w = jnp.where(chosen, s, 0.0)
    return chosen, w / jnp.sum(w, axis=0, keepdims=True) * ROUTED_SCALE


def _merge_body(oabp_ref, oabs_ref, ocp_ref, ocs_ref, xp_ref, xs_ref, mod_ref, gpost_ref, gpre_ref,
                wout_ref, wrt_ref, br_ref, tri_ref, y_ref, h_ref, gates_ref, rank_ref, pcu_ref):
    i = pl.program_id(0)
    mix = (_dot(_tok_select(i, oabp_ref, oabs_ref), wout_ref[0:AB_W, :])
           + _dot(_tok_select(i, ocp_ref, ocs_ref), wout_ref[AB_W:D, :]))
    y = _tok_select(i, xp_ref, xs_ref) + mod_ref[2:3, :] * _rms(mix, gpost_ref[...])
    y_ref[...] = y
    h = _rms(y, gpre_ref[...]) * (1.0 + mod_ref[4:5, :]) + mod_ref[3:4, :]
    h_ref[...] = h.astype(BF16)
    logits_t = lax.dot_general(wrt_ref[...], h, (((1,), (1,)), ((), ())),
                               precision=lax.Precision.HIGHEST, preferred_element_type=F32)
    chosen, gates = _route_t(logits_t, br_ref[...], TM_IN)
    gates_ref[...] = gates
    ch = jnp.where(chosen, 1.0, 0.0)
    rank = _dot(ch.astype(BF16), tri_ref[...])
    rank_ref[...] = jnp.where(chosen, rank, -1.0)
    for j in range(TM_IN // B_SORT):
        cnt = jnp.sum(ch[:, j * B_SORT:(j + 1) * B_SORT], axis=1, keepdims=True)
        units = jnp.floor((cnt + (UNIT - 1)) * (1.0 / UNIT))
        pcu_ref[j] = jnp.broadcast_to(units, (N_EXP, LANES))


def _merge(li, oabp, oabs, ocp, ocs, xp, xs, modv, g_post, g_pre, w_out_bf, w_router_t, b_router_col, tri):
    def tok(w):
        return pl.BlockSpec((TM_IN, w), lambda i: (i, 0))

    def par(*shape):
        return pl.BlockSpec((None,) + shape, lambda i: (li,) + (0,) * len(shape))

    tok_t = pl.BlockSpec((N_EXP, TM_IN), lambda i: (0, i))
    nb = TM_IN // B_SORT
    return pl.pallas_call(
        _merge_body,
        grid=(T // TM_IN,),
        in_specs=[_ctx_tok(AB_W), _lat_tok(AB_W), _ctx_tok(POOL_W), _lat_tok(POOL_W), _ctx_tok(D), _lat_tok(D),
                  pl.BlockSpec((None, None, 6, D), lambda i: (li, _mod_row(i, TM_IN), 0, 0)),
                  par(1, D), par(1, D), par(D, D), par(N_EXP, D), par(N_EXP, 1),
                  pl.BlockSpec((TM_IN, TM_IN), lambda i: (0, 0))],
        out_specs=[tok(D), tok(D), tok_t, tok_t,
                   pl.BlockSpec((nb, N_EXP, LANES), lambda i: (i, 0, 0))],
        out_shape=[jax.ShapeDtypeStruct((T, D), F32),
                   jax.ShapeDtypeStruct((T, D), BF16),
                   jax.ShapeDtypeStruct((N_EXP, T), F32),
                   jax.ShapeDtypeStruct((N_EXP, T), F32),
                   jax.ShapeDtypeStruct((N_BLK, N_EXP, LANES), F32)],
        compiler_params=_cparams(("parallel",)),
        name=f"merge_l{li}",
    )(oabp, oabs, ocp, ocs, xp, xs, modv, g_post, g_pre, w_out_bf, w_router_t, b_router_col, tri)


def _rank_matrix():
    t = np.arange(TM_IN)
    m = (t[:, None] < t[None, :]) & (t[:, None] // B_SORT == t[None, :] // B_SORT)
    return jnp.asarray(m, BF16)


def _moe_tables(pcu_f):
    pcu = pcu_f[:, :, 0].astype(jnp.int32)
    incl = jnp.cumsum(pcu, axis=1)
    uo = incl - pcu
    tot = jnp.sum(pcu, axis=0)
    gt = (tot + UPT - 1) // UPT
    gstart_t = jnp.cumsum(gt) - gt
    gstart_u = gstart_t * UPT
    steps = (gt + TPS - 1) // TPS
    sincl = jnp.cumsum(steps)
    sstart = sincl - steps
    bstart = gstart_u[None, :] + jnp.cumsum(pcu, axis=0) - pcu
    u = jnp.arange(UPB, dtype=jnp.int32)
    eou = jnp.sum((u[None, :, None] >= incl[:, None, :]).astype(jnp.int32), axis=-1)
    valid = eou < N_EXP
    onehot = eou[:, :, None] == jnp.arange(N_EXP, dtype=jnp.int32)[None, None, :]
    run_u0 = jnp.sum(jnp.where(onehot, uo[:, None, :], 0), axis=-1)
    run_g0 = jnp.sum(jnp.where(onehot, bstart[:, None, :], 0), axis=-1)
    dst = run_g0 + u[None, :] - run_u0
    spare0 = SPARE_UNIT0 + (jnp.arange(N_BLK, dtype=jnp.int32) % 2)[:, None] * (UPB + ZPB)
    dst_sort = jnp.where(valid, dst, spare0 + u[None, :])
    dst_comb = jnp.where(valid, dst, ZERO_UNIT)
    loc = jnp.where(valid, (u[None, :] - run_u0) * UNIT, -(1 << 20))
    k = jnp.arange(UPT - 1, dtype=jnp.int32)
    ztail = gt * UPT - tot
    zdst = jnp.where(k[None, :] < ztail[:, None], (gstart_u + tot)[:, None] + k[None, :], -1).reshape(N_BLK, ZPB)
    zdst = jnp.where(zdst >= 0, zdst, spare0 + UPB + jnp.arange(ZPB, dtype=jnp.int32)[None, :])
    step = jnp.arange(NS_MAX, dtype=jnp.int32)
    te = jnp.minimum(jnp.sum((step[:, None] >= sincl[None, :]).astype(jnp.int32), axis=-1), N_EXP - 1)
    mine = te[:, None] == jnp.arange(N_EXP, dtype=jnp.int32)[None, :]
    pick = lambda v: jnp.sum(jnp.where(mine, v[None, :], 0), axis=-1)
    first = gt - TPS * (steps - 1)
    j = step - pick(sstart)
    ntile = jnp.where(j == 0, pick(first), TPS)
    tile0 = pick(gstart_t) + jnp.where(j == 0, 0, pick(first) + TPS * (j - 1))
    ns = sincl[-1]
    wslot = (jnp.cumsum((j == 0).astype(jnp.int32)) - 1) % 2
    nxt = pick(sincl)
    nexte = jnp.sum(jnp.where(step[None, :] == nxt[:, None], te[None, :], 0), axis=-1)
    nexte = jnp.where(j == 0, jnp.where(nxt < ns, nexte, -1), -2)
    flat = lambda a: a.reshape(-1)
    return (flat(jnp.minimum(eou, N_EXP - 1)), flat(loc), flat(dst_sort), flat(dst_comb), flat(zdst),
            incl[:, -1], te, ntile, tile0, wslot, nexte, sincl[-1:])


def _unit_rows(u):
    if isinstance(u, int):
        return pl.ds(u * UNIT, UNIT)
    return pl.ds(pl.multiple_of(u * UNIT, UNIT), UNIT)


def _for_live_groups(n_units, fn, groups=range(UPB // UPT)):
    for g in groups:
        if (g + 1) * UPT <= MIN_UNITS:
            fn(g)
        else:
            @pl.when(g * UPT < n_units)
            def _(g=g):
                fn(g)


def _sort_body(eou_ref, loc_ref, dst_ref, zdst_ref, nun_ref, h_ref, rank_ref, gates_ref, xs_hbm, p_ref,
               xs_scr, zero_scr, sem):
    b = pl.program_id(0)
    slot = b % 2

    def copy(src, d, s):
        return pltpu.make_async_copy(src, xs_hbm.at[_unit_rows(d)], sem.at[s])

    def wait_units(n, s):
        lax.fori_loop(0, n, lambda k, c: (copy(zero_scr, 0, s).wait(), c)[1], 0, unroll=True)

    def wait_step(blk, s):
        _for_live_groups(nun_ref[blk], lambda g: wait_units(UPT, s))
        wait_units(ZPB, s)

    @pl.when(b == 0)
    def _():
        zero_scr[...] = jnp.zeros((UNIT, XS_W), BF16)

    @pl.when(b >= 2)
    def _():
        wait_step(b - 2, slot)

    base = lax.broadcasted_iota(jnp.int32, (UNIT, B_SORT), 0)
    g = gates_ref[...]
    g_hi = g.astype(BF16)
    g_hl = jnp.concatenate([g_hi, (g - g_hi.astype(F32)).astype(BF16)], axis=0)
    h = h_ref[...]
    def start_group(g):
        for u in range(g * UPT, (g + 1) * UPT):
            copy(xs_scr.at[slot, _unit_rows(u)], dst_ref[b * UPB + u], slot).start()

    upc = SORT_CHUNK // UNIT
    for c in range(RB // SORT_CHUNK):
        for u in range(c * upc, (c + 1) * upc):
            match = rank_ref[pl.ds(eou_ref[b * UPB + u], 1), :] == (base + loc_ref[b * UPB + u]).astype(F32)
            p_ref[_unit_rows(u), :] = jnp.where(match, 1.0, 0.0).astype(BF16)
        rows = slice(c * SORT_CHUNK, (c + 1) * SORT_CHUNK)
        xs_scr[slot, rows, 0:D] = _dot(p_ref[rows, :], h).astype(BF16)
        xs_scr[slot, rows, D:XS_W] = _dot_nt(p_ref[rows, :], g_hl).astype(BF16)
        _for_live_groups(nun_ref[b], start_group, range(c * upc // UPT, (c + 1) * upc // UPT))
    for j in range(ZPB):
        copy(zero_scr, zdst_ref[b * ZPB + j], slot).start()

    @pl.when(b == N_BLK - 1)
    def _():
        wait_step(b - 1, 1 - slot)
        wait_step(b, slot)


def _moe_sort(li, eou, loc, dst, zdst, nun, h, rank_t, gates_t):
    blk_t = pl.BlockSpec((N_EXP, B_SORT), lambda b, *_: (0, b))
    return pl.pallas_call(
        _sort_body,
        grid_spec=pltpu.PrefetchScalarGridSpec(
            num_scalar_prefetch=5,
            grid=(N_BLK,),
            in_specs=[pl.BlockSpec((B_SORT, D), lambda b, *_: (b, 0)), blk_t, blk_t],
            out_specs=pl.BlockSpec(memory_space=pl.ANY),
            scratch_shapes=[pltpu.VMEM((RB, B_SORT), BF16),
                            pltpu.VMEM((2, RB, XS_W), BF16),
                            pltpu.VMEM((UNIT, XS_W), BF16),
                            pltpu.SemaphoreType.DMA((2,))]),
        out_shape=jax.ShapeDtypeStruct((R_SORT, XS_W), BF16),
        compiler_params=_cparams(("arbitrary",)),
        name=f"moe_sort_l{li}",
    )(eou, loc, dst, zdst, nun, h, rank_t, gates_t)


def _experts_body(te_ref, ntile_ref, tile0_ref, wslot_ref, nexte_ref, ns_ref,
                  xs_hbm, wg_hbm, wu_hbm, wd_hbm, os_hbm,
                  xs_buf, o_buf, wg_f, wu_f, wd_f, sem_in, sem_out, sem_w, *, li):
    ns = ns_ref[0]

    def tiles_of(step):
        return jnp.where((step >= 0) & (step < ns), ntile_ref[jnp.clip(step, 0, NS_MAX - 1)], 0)

    def tile_rows(step, t):
        return pl.ds(pl.multiple_of((tile0_ref[jnp.clip(step, 0, NS_MAX - 1)] + t) * TMG, TMG), TMG)

    def buf_rows(t):
        return pl.ds(pl.multiple_of(t * TMG, TMG), TMG)

    def copy_in(step, t, s):
        return pltpu.make_async_copy(xs_hbm.at[tile_rows(step, t)], xs_buf.at[s, buf_rows(t)], sem_in.at[s])

    def copy_out(step, t, s):
        return pltpu.make_async_copy(o_buf.at[s, buf_rows(t)], os_hbm.at[tile_rows(step, t)], sem_out.at[s])

    def for_tiles(step, fn):
        lax.fori_loop(0, tiles_of(step), lambda t, c: (fn(t), c)[1], 0)

    def weight_copies(e, s):
        return [pltpu.make_async_copy(hbm.at[li, e], buf.at[s], sem_w.at[s])
                for hbm, buf in ((wg_hbm, wg_f), (wu_hbm, wu_f), (wd_hbm, wd_f))]

    for c in weight_copies(te_ref[0], 0):
        c.start()
    for_tiles(0, lambda t: copy_in(0, t, 0).start())

    def step(i, carry):
        slot = i % 2
        for_tiles(i + 1, lambda t: copy_in(i + 1, t, 1 - slot).start())
        for_tiles(i - 2, lambda t: copy_out(i - 2, t, slot).wait())
        e = te_ref[i]
        ws = wslot_ref[i]

        @pl.when(nexte_ref[i] >= -1)
        def _():
            for c in weight_copies(e, ws):
                c.wait()

            @pl.when(nexte_ref[i] >= 0)
            def _():
                for c in weight_copies(nexte_ref[i], 1 - ws):
                    c.start()

        ntile = ntile_ref[i]
        for_tiles(i, lambda t: copy_in(i, t, slot).wait())
        for k in range(1, TPS + 1):
            rows = k * TMG

            @pl.when(ntile == k)
            def _():
                x = xs_buf[slot, 0:rows, 0:D]
                lane = lax.broadcasted_iota(jnp.int32, (rows, LANES), 1)
                mine = (lane == e) | (lane == e + N_EXP)
                gate = jnp.sum(jnp.where(mine, xs_buf[slot, 0:rows, D:XS_W].astype(F32), 0.0),
                               axis=1, keepdims=True)
                hid = (_silu(_dot(x, wg_f[ws].astype(BF16))) * _dot(x, wu_f[ws].astype(BF16))) * gate
                o_buf[slot, 0:rows, :] = _dot(hid.astype(BF16), wd_f[ws].astype(BF16)).astype(BF16)

        for_tiles(i, lambda t: copy_out(i, t, slot).start())
        return carry

    lax.fori_loop(0, ns, step, 0)
    for_tiles(ns - 2, lambda t: copy_out(ns - 2, t, ns % 2).wait())
    for_tiles(ns - 1, lambda t: copy_out(ns - 1, t, (ns - 1) % 2).wait())
    o_buf[0, 0:UNIT, :] = jnp.zeros((UNIT, D), BF16)
    zero = pltpu.make_async_copy(o_buf.at[0, 0:UNIT], os_hbm.at[_unit_rows(ZERO_UNIT)], sem_out.at[0])
    zero.start()
    zero.wait()


def _moe_experts(li, te, ntile, tile0, wslot, nexte, ns, xs, w_gate, w_up, w_down):
    return pl.pallas_call(
        functools.partial(_experts_body, li=li),
        grid_spec=pltpu.PrefetchScalarGridSpec(
            num_scalar_prefetch=6,
            grid=(1,),
            in_specs=[pl.BlockSpec(memory_space=pl.ANY)] * 4,
            out_specs=pl.BlockSpec(memory_space=pl.ANY),
            scratch_shapes=[pltpu.VMEM((2, STEP_ROWS, XS_W), BF16),
                            pltpu.VMEM((2, STEP_ROWS, D), BF16),
                            pltpu.VMEM((2, D, EXP_DIM), F32),
                            pltpu.VMEM((2, D, EXP_DIM), F32),
                            pltpu.VMEM((2, EXP_DIM, D), F32),
                            pltpu.SemaphoreType.DMA((2,)),
                            pltpu.SemaphoreType.DMA((2,)),
                            pltpu.SemaphoreType.DMA((2,))]),
        out_shape=jax.ShapeDtypeStruct((R_OUT, D), BF16),
        compiler_params=_cparams(("arbitrary",)),
        name=f"moe_experts_l{li}",
    )(te, ntile, tile0, wslot, nexte, ns, xs, w_gate, w_up, w_down)


def _combine_body(dst_ref, nun_ref, eou_ref, loc_ref, h_ref, rank_ref, y_ref, mod_ref, gpost_ref,
                  wsg_ref, wsu_ref, wsd_ref, os_hbm, op_ref, ol_ref, os_scr, p_ref, sem):
    b = pl.program_id(0)
    slot = b % 2

    def copy(d, u, s):
        return pltpu.make_async_copy(os_hbm.at[_unit_rows(d)], os_scr.at[s, _unit_rows(u)], sem.at[s])

    def fetch(blk, s):
        def group(g):
            lax.fori_loop(g * UPT, (g + 1) * UPT,
                          lambda u, c: (copy(dst_ref[blk * UPB + u], u, s).start(), c)[1], 0, unroll=UNROLL)

        _for_live_groups(nun_ref[blk], group)

    @pl.when(b == 0)
    def _():
        os_scr[:, MIN_UNITS * UNIT:RB, :] = jnp.zeros((2, RB - MIN_UNITS * UNIT, D), BF16)
        fetch(0, 0)

    @pl.when(b + 1 < N_BLK)
    def _():
        fetch(b + 1, 1 - slot)

    base = lax.broadcasted_iota(jnp.int32, (UNIT, B_SORT), 0)

    def onehot_unit(u, c):
        match = rank_ref[pl.ds(eou_ref[b * UPB + u], 1), :] == (base + loc_ref[b * UPB + u]).astype(F32)
        p_ref[_unit_rows(u), :] = jnp.where(match, 1.0, 0.0).astype(BF16)
        return c

    lax.fori_loop(0, UPB, onehot_unit, 0, unroll=UNROLL)
    h = h_ref[...]
    shared = _dot((_silu(_dot(h, wsg_ref[...])) * _dot(h, wsu_ref[...])).astype(BF16), wsd_ref[...])
    _for_live_groups(nun_ref[b], lambda g: lax.fori_loop(
        0, UPT, lambda k, c: (copy(0, 0, slot).wait(), c)[1], 0, unroll=True))
    routed = lax.dot_general(os_scr[slot], p_ref[...], (((0,), (0,)), ((), ())),
                             preferred_element_type=F32).T
    out = y_ref[...] + mod_ref[5:6, :] * _rms(routed + shared, gpost_ref[...])

    @pl.when(b < T_P // B_SORT)
    def _():
        op_ref[...] = out

    @pl.when(b >= T_P // B_SORT)
    def _():
        ol_ref[...] = out


def _moe_combine(li, dst, nun, eou, loc, h, rank_t, y, modv, g_post, wsg_bf, wsu_bf, wsd_bf, os):
    nbp = T_P // B_SORT
    def tok(w):
        return pl.BlockSpec((B_SORT, w), lambda b, *_: (b, 0))

    def par(*shape):
        return pl.BlockSpec((None,) + shape, lambda b, *_: (li,) + (0,) * len(shape))

    return pl.pallas_call(
        _combine_body,
        grid_spec=pltpu.PrefetchScalarGridSpec(
            num_scalar_prefetch=4,
            grid=(N_BLK,),
            in_specs=[tok(D), pl.BlockSpec((N_EXP, B_SORT), lambda b, *_: (0, b)), tok(D),
                      pl.BlockSpec((None, None, 6, D), lambda b, *_: (li, _mod_row(b, B_SORT), 0, 0)),
                      par(1, D), par(D, EXP_DIM), par(D, EXP_DIM), par(EXP_DIM, D),
                      pl.BlockSpec(memory_space=pl.ANY)],
            out_specs=[pl.BlockSpec((B_SORT, D), lambda b, *_: (jnp.minimum(b, nbp - 1), 0)),
                       pl.BlockSpec((B_SORT, D), lambda b, *_: (jnp.maximum(b - nbp, 0), 0))],
            scratch_shapes=[pltpu.VMEM((2, RB, D), BF16),
                            pltpu.VMEM((RB, B_SORT), BF16),
                            pltpu.SemaphoreType.DMA((2,))]),
        out_shape=[jax.ShapeDtypeStruct((T_P, D), F32), jax.ShapeDtypeStruct((T_S, D), F32)],
        compiler_params=_cparams(("arbitrary",)),
        name=f"moe_combine_l{li}",
    )(dst, nun, eou, loc, h, rank_t, y, modv, g_post, wsg_bf, wsu_bf, wsd_bf, os)


def _block_diag(w):
    out = jnp.zeros((DEPTH, POOL_W, POOL_W), F32)
    for g in range(len(POOL_WINDOWS)):
        out = out.at[:, g * POOL_CH:(g + 1) * POOL_CH, g * POOL_CH:(g + 1) * POOL_CH].set(w[:, g])
    return out


def kernel(x_prompt, x_sample, cache_diff_k, cache_diff_v, cache_na_k, cache_na_v, c, c_ctx,
           w_ada, b_ada, g_pre_mix, g_post_mix, g_pre_ffn, g_post_ffn, w_in, w_out,
           diff_lambda, g_diff, na_bias, pool_w, pool_scale, w_router, b_router,
           w_gate, w_up, w_down, ws_gate, ws_up, ws_down):
    xp, xs = x_prompt.reshape(T_P, D), x_sample.reshape(T_S, D)
    cvec = jnp.concatenate([c_ctx[None, :], c, jnp.zeros((3, D), F32)], axis=0)
    modv = _modulation(cvec, w_ada, b_ada)[:, :1 + NB_S].reshape(DEPTH, 1 + NB_S, 6, D)

    cos_t, sin_t = _rope_tables()
    w_in_bf = w_in.astype(BF16)
    w_out_bf = w_out.astype(BF16)
    wsg_bf, wsu_bf, wsd_bf = ws_gate.astype(BF16), ws_up.astype(BF16), ws_down.astype(BF16)
    pool_bd = _block_diag(pool_w)
    row = lambda a: a.reshape(DEPTH, 1, a.shape[-1])
    g_pre_mix, g_post_mix, g_pre_ffn, g_post_ffn = map(row, (g_pre_mix, g_post_mix, g_pre_ffn, g_post_ffn))
    g_diff, pool_scale = row(g_diff), row(pool_scale)
    w_router_t = jnp.swapaxes(w_router, 1, 2)
    b_router_col = b_router.reshape(DEPTH, N_EXP, 1)
    tri = _rank_matrix()
    ck = cache_diff_k.reshape(NB_S, DEPTH, PAST, DA_W)
    cv = cache_diff_v.reshape(NB_S, DEPTH, PAST, DA_W)
    cnk = cache_na_k.reshape(NB_S, DEPTH, PAST, NA_W)
    cnv = cache_na_v.reshape(NB_S, DEPTH, PAST, NA_W)

    strips = _na_bias_strips(na_bias)

    caches = ()
    for li in range(DEPTH):
        lam_init = 0.8 - 0.6 * math.exp(-0.3 * li)
        qa, qb, ocp, up, ka, va, kb, vb, *caches = _in_proj(li, xp, xs, modv, g_pre_mix, w_in_bf, cos_t, sin_t,
                                                           pool_bd, pool_scale, caches)
        ocs = _pool_lat(li, up, pool_bd, pool_scale)
        oabp = _attn_ctx(li, diff_lambda, g_diff, qa, caches[0], caches[1], qb, caches[2], caches[3], lam_init)
        oabs = _attn_lat(li, diff_lambda, g_diff, qa, ka, va, ck, cv, qb, kb, vb, cnk, cnv, strips, lam_init)
        y, h, gates_t, rank_t, pcu = _merge(li, oabp, oabs, ocp, ocs, xp, xs, modv, g_post_mix, g_pre_ffn,
                                            w_out_bf, w_router_t, b_router_col, tri)
        eou, loc, dst_sort, dst_comb, zdst, nun, te, ntile, tile0, wslot, nexte, ns = _moe_tables(pcu)
        srt = _moe_sort(li, eou, loc, dst_sort, zdst, nun, h, rank_t, gates_t)
        os = _moe_experts(li, te, ntile, tile0, wslot, nexte, ns, srt, w_gate, w_up, w_down)
        xp, xs = _moe_combine(li, dst_comb, nun, eou, loc, h, rank_t, y, modv, g_post_ffn,
                              wsg_bf, wsu_bf, wsd_bf, os)

    new_dk, new_dv, new_nk, new_nv = caches
    return (xp.reshape(NB_P, N_P, D), xs.reshape(NB_S, N_S, D),
            new_dk.reshape(NB_P, DEPTH, N_P, DA_H, 2 * DA_QK), new_dv.reshape(NB_P, DEPTH, N_P, DA_H, DA_V),
            new_nk.reshape(NB_P, DEPTH, N_P, NA_H, NA_D), new_nv.reshape(NB_P, DEPTH, N_P, NA_H, NA_D))
```
